```python
import jax, jax.numpy as jnp
from jax import lax
import numpy as np

D_MODEL = 1024
BATCH = 32
SEQ = 2048
DEPTH = 4
DEC_BATCH = 8
DEC_SEQ = 64
PAST_LEN = 4096

CHUNK = 64
Q_BLOCK = 128
H_FOX = 8
DH_FOX = 64
W_FOX = H_FOX * DH_FOX
FOX_F_BIAS = 3.0
H_ML = 4
DH_ML = 128
W_ML = H_ML * DH_ML
H_MLA = 8
NOPE = 128
ROPE = 64
DV_MLA = 128
Q_RANK = 384
KV_RANK = 256
ROPE_BASE = 10000.0
N_GROUPS = 4
EXP_PER_GROUP = 8
N_EXPERTS = N_GROUPS * EXP_PER_GROUP
TOP_K = 2
D_EXPERT = 256
N_AB = (DEPTH + 1) // 2
N_C = DEPTH // 2
ALPHA = (2 * DEPTH) ** 0.25
BETA = (8 * DEPTH) ** -0.25
LN_EPS = 1e-5
RMS_EPS = 1e-6
AB_IN = 3 * W_FOX + H_FOX + 3 * W_ML + 2 * H_ML + W_ML
MLA_DOWN = Q_RANK + KV_RANK + ROPE

kernel_name = 'streaming_fox_mlstm_mla_hmoe_step'


def layer_norm(x, g, b):
    xf = x.astype(jnp.float32)
    mu = jnp.mean(xf, -1, keepdims=True)
    var = jnp.mean(jnp.square(xf - mu), -1, keepdims=True)
    return ((xf - mu) * lax.rsqrt(var + LN_EPS) * g + b).astype(x.dtype)


def rms_norm(x, g):
    xf = x.astype(jnp.float32)
    return (xf * lax.rsqrt(jnp.mean(xf * xf, -1, keepdims=True) + RMS_EPS) * g).astype(x.dtype)


def apply_rope(x, pos):
    half = ROPE // 2
    inv_freq = ROPE_BASE ** (-jnp.arange(half, dtype=jnp.float32) / half)
    ang = pos.astype(jnp.float32)[:, None] * inv_freq[None, :]
    shape = (pos.shape[0],) + (1,) * (x.ndim - 3) + (half,)
    cos = jnp.cos(ang).reshape(shape)
    sin = jnp.sin(ang).reshape(shape)
    xf = x.astype(jnp.float32)
    x1, x2 = xf[..., :half], xf[..., half:]
    return jnp.concatenate([x1 * cos - x2 * sin, x2 * cos + x1 * sin], -1).astype(x.dtype)


def sweep_query_blocks(fn, q_arrays, qpos):
    S = qpos.shape[0]
    nb = S // Q_BLOCK

    def blocks(a):
        return jnp.moveaxis(a.reshape((a.shape[0], nb, Q_BLOCK) + a.shape[2:]), 1, 0)

    out = lax.map(lambda args: fn(*args),
                  tuple(blocks(a) for a in q_arrays) + (qpos.reshape(nb, Q_BLOCK),))
    out = jnp.moveaxis(out, 0, 1)
    return out.reshape((out.shape[0], S) + out.shape[3:])


def fox_block(q, k, v, cum_q, cum_k, qpos, kpos):
    s = jnp.einsum('bqhd,bkhd->bhqk', q, k, preferred_element_type=jnp.float32) * (DH_FOX ** -0.5)
    bias = jnp.transpose(cum_q, (0, 2, 1))[..., :, None] - jnp.transpose(cum_k, (0, 2, 1))[..., None, :]
    mask = kpos[None, :] <= qpos[:, None]
    p = jax.nn.softmax(jnp.where(mask, s + bias, -jnp.inf), axis=-1)
    return jnp.einsum('bhqk,bkhd->bqhd', p.astype(v.dtype), v)


def mlstm_chunk(carry, inp):
    c0, n0, m0 = carry
    q, k, v, ig, lf = inp
    L = q.shape[1]
    qf = q.astype(jnp.float32)
    kf = k.astype(jnp.float32) * (DH_ML ** -0.5)
    vf = v.astype(jnp.float32)
    bh = jnp.transpose(jnp.cumsum(lf, axis=1), (0, 2, 1))
    ih = jnp.transpose(ig, (0, 2, 1))
    causal = jnp.tril(jnp.ones((L, L), dtype=bool))
    logd = jnp.where(causal, bh[..., :, None] - bh[..., None, :] + ih[..., None, :], -jnp.inf)
    inter = bh + m0[..., None]
    m = jnp.maximum(inter, jnp.max(logd, -1))
    d = jnp.exp(logd - m[..., None])
    a = jnp.exp(inter - m)
    s = jnp.einsum('blhd,bshd->bhls', qf, kf) * d
    num = (jnp.einsum('bhls,bshd->blhd', s, vf)
           + jnp.einsum('bhvk,blhk->blhv', c0, qf) * jnp.transpose(a, (0, 2, 1))[..., None])
    den = jnp.sum(s, -1) + a * jnp.einsum('bhk,blhk->bhl', n0, qf)
    den = jnp.maximum(jnp.abs(den), jnp.exp(-m))
    h = num / jnp.transpose(den, (0, 2, 1))[..., None]
    m_l = m[..., -1]
    a_l = jnp.exp(bh[..., -1] + m0 - m_l)
    w_l = jnp.exp(bh[..., -1:] - bh + ih - m_l[..., None])
    c_l = a_l[..., None, None] * c0 + jnp.einsum('bhs,bshv,bshk->bhvk', w_l, vf, kf)
    n_l = a_l[..., None] * n0 + jnp.einsum('bhs,bshk->bhk', w_l, kf)
    return (c_l, n_l, m_l), h


def mlstm_sequence(q, k, v, ig, lf, init):
    B, S = q.shape[:2]
    nc = S // CHUNK

    def chunks(a):
        return jnp.moveaxis(a.reshape((B, nc, CHUNK) + a.shape[2:]), 1, 0)

    state, h = lax.scan(mlstm_chunk, init, tuple(chunks(a) for a in (q, k, v, ig, lf)))
    return state, jnp.moveaxis(h, 0, 1).reshape(B, S, H_ML, DH_ML)


def ab_project(x, w_in, b_fox_f, b_ml_i, b_ml_f):
    B, S, _ = x.shape
    z = jnp.einsum('bsd,de->bse', x, w_in)
    sizes = (W_FOX, W_FOX, W_FOX, H_FOX, W_ML, W_ML, W_ML, H_ML, H_ML, W_ML)
    idx = np.cumsum(sizes[:-1]).tolist()
    fq, fk, fv, ff, mq, mk, mv, mi, mf, mo = jnp.split(z, idx, axis=-1)
    f32 = jnp.float32
    return (fq.reshape(B, S, H_FOX, DH_FOX), fk.reshape(B, S, H_FOX, DH_FOX), fv.reshape(B, S, H_FOX, DH_FOX),
            jax.nn.log_sigmoid(ff.astype(f32) + b_fox_f),
            mq.reshape(B, S, H_ML, DH_ML), mk.reshape(B, S, H_ML, DH_ML), mv.reshape(B, S, H_ML, DH_ML),
            mi.astype(f32) + b_ml_i,
            jax.nn.log_sigmoid(mf.astype(f32) + b_ml_f),
            jax.nn.sigmoid(mo))


def ab_merge(fo, mh, og, g_ml, w_out):
    B, S = fo.shape[:2]
    mh = rms_norm(mh, g_ml.reshape(H_ML, DH_ML)).astype(og.dtype) * og.reshape(B, S, H_ML, DH_ML)
    cat = jnp.concatenate([fo.reshape(B, S, W_FOX).astype(og.dtype), mh.reshape(B, S, W_ML)], -1)
    return jnp.einsum('bse,ed->bsd', cat, w_out)


def ab_mixer_prompt(x, w_in, b_fox_f, b_ml_i, b_ml_f, g_ml, w_out):
    B, S, _ = x.shape
    pos = jnp.arange(S)
    fq, fk, fv, flf, mq, mk, mv, mig, mlf, og = ab_project(x, w_in, b_fox_f, b_ml_i, b_ml_f)
    cum = jnp.cumsum(flf, axis=1)
    fo = sweep_query_blocks(lambda q, cq, qp: fox_block(q, fk, fv, cq, cum, qp, pos), (fq, cum), pos)
    init = (jnp.zeros((B, H_ML, DH_ML, DH_ML), jnp.float32),
            jnp.zeros((B, H_ML, DH_ML), jnp.float32),
            jnp.zeros((B, H_ML), jnp.float32))
    (c, n, m), mh = mlstm_sequence(mq, mk, mv, mig, mlf, init)
    return ab_merge(fo, mh, og, g_ml, w_out), (fk, fv, flf, c, n, m)


def ab_mixer_sample(x, ck, cv, clf, c0, n0, m0, w_in, b_fox_f, b_ml_i, b_ml_f, g_ml, w_out):
    B, S, _ = x.shape
    past = ck.shape[1]
    qpos = past + jnp.arange(S)
    kpos = jnp.arange(past + S)
    fq, fk, fv, flf, mq, mk, mv, mig, mlf, og = ab_project(x, w_in, b_fox_f, b_ml_i, b_ml_f)
    k_all = jnp.concatenate([ck.astype(fk.dtype), fk], 1)
    v_all = jnp.concatenate([cv.astype(fv.dtype), fv], 1)
    cum = jnp.cumsum(jnp.concatenate([clf.astype(jnp.float32), flf], 1), axis=1)
    fo = fox_block(fq, k_all, v_all, cum[:, past:], cum, qpos, kpos)
    init = (c0.astype(jnp.float32), n0.astype(jnp.float32), m0.astype(jnp.float32))
    (c, n, m), mh = mlstm_chunk(init, (mq, mk, mv, mig, mlf))
    return ab_merge(fo, mh, og, g_ml, w_out), (fk, fv, flf, c, n, m)


def mla_project(x, pos, w_down, g_q, w_uq, g_kv, w_uk):
    B, S, _ = x.shape
    z = jnp.einsum('bsd,de->bse', x, w_down)
    cq, ckv, kr = jnp.split(z, [Q_RANK, Q_RANK + KV_RANK], axis=-1)
    cq = rms_norm(cq, g_q)
    ckv = rms_norm(ckv, g_kv)
    kr = apply_rope(kr, pos)
    q = jnp.einsum('bsr,re->bse', cq, w_uq).reshape(B, S, H_MLA, NOPE + ROPE)
    q_rope = apply_rope(q[..., NOPE:], pos)
    q_lat = jnp.einsum('bshn,chn->bshc', q[..., :NOPE], w_uk)
    return q_lat, q_rope, ckv, kr


def mla_block(q_lat, q_rope, ckv, kr, qpos, kpos):
    s = (jnp.einsum('bqhc,bkc->bhqk', q_lat, ckv, preferred_element_type=jnp.float32)
         + jnp.einsum('bqhr,bkr->bhqk', q_rope, kr, preferred_element_type=jnp.float32)) * ((NOPE + ROPE) ** -0.5)
    mask = (kpos // CHUNK)[None, :] <= (qpos // CHUNK)[:, None]
    p = jax.nn.softmax(jnp.where(mask, s, -jnp.inf), axis=-1)
    return jnp.einsum('bhqk,bkc->bqhc', p.astype(ckv.dtype), ckv)


def mla_out(o_lat, w_uv, w_out):
    B, S = o_lat.shape[:2]
    o = jnp.einsum('bqhc,chv->bqhv', o_lat, w_uv).reshape(B, S, H_MLA * DV_MLA)
    return jnp.einsum('bse,ed->bsd', o, w_out)


def mla_mixer_prompt(x, w_down, g_q, w_uq, g_kv, w_uk, w_uv, w_out):
    S = x.shape[1]
    pos = jnp.arange(S)
    q_lat, q_rope, ckv, kr = mla_project(x, pos, w_down, g_q, w_uq, g_kv, w_uk)
    o_lat = sweep_query_blocks(lambda ql, qr, qp: mla_block(ql, qr, ckv, kr, qp, pos), (q_lat, q_rope), pos)
    return mla_out(o_lat, w_uv, w_out), (ckv, kr)


def mla_mixer_sample(x, c_ckv, c_kr, w_down, g_q, w_uq, g_kv, w_uk, w_uv, w_out):
    S = x.shape[1]
    past = c_ckv.shape[1]
    qpos = past + jnp.arange(S)
    kpos = jnp.arange(past + S)
    q_lat, q_rope, ckv, kr = mla_project(x, qpos, w_down, g_q, w_uq, g_kv, w_uk)
    ckv_all = jnp.concatenate([c_ckv.astype(ckv.dtype), ckv], 1)
    kr_all = jnp.concatenate([c_kr.astype(kr.dtype), kr], 1)
    o_lat = mla_block(q_lat, q_rope, ckv_all, kr_all, qpos, kpos)
    return mla_out(o_lat, w_uv, w_out), (ckv, kr)


def hier_moe(x, w_grp, b_grp, w_rt, b_rt, w1, w3, w2):
    B, S, D = x.shape
    xt = x.reshape(-1, D)
    n_tok = xt.shape[0]
    g_logit = jnp.einsum('nd,dg->ng', xt, w_grp).astype(jnp.float32) + b_grp
    g_prob = jax.nn.softmax(g_logit, -1)
    g_idx = jnp.argmax(g_logit, -1)
    g_gate = jnp.take_along_axis(g_prob, g_idx[:, None], -1)
    e_logit = (jnp.einsum('nd,de->ne', xt, w_rt).astype(jnp.float32) + b_rt).reshape(n_tok, N_GROUPS, EXP_PER_GROUP)
    e_sel = jnp.take_along_axis(e_logit, g_idx[:, None, None], axis=1)[:, 0]
    top_v, top_i = lax.top_k(e_sel, TOP_K)
    top_w = jax.nn.softmax(top_v, -1) * g_gate
    expert_id = g_idx[:, None] * EXP_PER_GROUP + top_i
    gates = jnp.sum(jax.nn.one_hot(expert_id, N_EXPERTS, dtype=jnp.float32) * top_w[..., None], axis=1)

    def expert_step(acc, ew):
        w1e, w3e, w2e, ge = ew
        h = jax.nn.silu(xt @ w1e) * (xt @ w3e)
        return acc + ge[:, None] * (h @ w2e).astype(jnp.float32), None

    acc, _ = lax.scan(expert_step, jnp.zeros((n_tok, D), jnp.float32), (w1, w3, w2, gates.T))
    return acc.astype(x.dtype).reshape(B, S, D)


def post_norm_layer(x, mix, moe_w, g1, b1, g2, b2):
    x = layer_norm(ALPHA * x + mix, g1, b1)
    return layer_norm(ALPHA * x + hier_moe(x, *moe_w), g2, b2)


def setup_inputs(seed: int = 0) -> dict:
    key = jax.random.key(seed)
    ks = list(jax.random.split(key, 40))

    def nrm(shape, scale=1.0):
        return jax.random.normal(ks.pop(), shape, jnp.float32) * scale

    return {
        'x_prompt': nrm((BATCH, SEQ, D_MODEL)),
        'x_sample': nrm((DEC_BATCH, DEC_SEQ, D_MODEL)),
        'cache_fox_k': nrm((N_AB, DEC_BATCH, PAST_LEN, H_FOX, DH_FOX)),
        'cache_fox_v': nrm((N_AB, DEC_BATCH, PAST_LEN, H_FOX, DH_FOX)),
        'cache_fox_logf': jax.nn.log_sigmoid(FOX_F_BIAS + nrm((N_AB, DEC_BATCH, PAST_LEN, H_FOX), 0.5)),
        'state_mlstm_c': nrm((N_AB, DEC_BATCH, H_ML, DH_ML, DH_ML), 0.1),
        'state_mlstm_n': nrm((N_AB, DEC_BATCH, H_ML, DH_ML), 0.1),
        'state_mlstm_m': nrm((N_AB, DEC_BATCH, H_ML)),
        'cache_mla_ckv': nrm((N_C, DEC_BATCH, PAST_LEN, KV_RANK)),
        'cache_mla_krope': nrm((N_C, DEC_BATCH, PAST_LEN, ROPE)),
        'w_ab_in': nrm((N_AB, D_MODEL, AB_IN), D_MODEL ** -0.5),
        'b_fox_f': FOX_F_BIAS + nrm((N_AB, H_FOX), 0.5),
        'b_mlstm_i': nrm((N_AB, H_ML), 0.1),
        'b_mlstm_f': jnp.linspace(3.0, 6.0, H_ML) + nrm((N_AB, H_ML), 0.1),
        'g_mlstm_norm': 1.0 + nrm((N_AB, W_ML), 0.05),
        'w_ab_out': nrm((N_AB, W_FOX + W_ML, D_MODEL), BETA * (W_FOX + W_ML) ** -0.5),
        'w_mla_down': nrm((N_C, D_MODEL, MLA_DOWN), D_MODEL ** -0.5),
        'g_mla_q': 1.0 + nrm((N_C, Q_RANK), 0.05),
        'w_mla_uq': nrm((N_C, Q_RANK, H_MLA * (NOPE + ROPE)), Q_RANK ** -0.5),
        'g_mla_kv': 1.0 + nrm((N_C, KV_RANK), 0.05),
        'w_mla_uk': nrm((N_C, KV_RANK, H_MLA, NOPE), KV_RANK ** -0.5),
        'w_mla_uv': nrm((N_C, KV_RANK, H_MLA, DV_MLA), KV_RANK ** -0.5),
        'w_mla_out': nrm((N_C, H_MLA * DV_MLA, D_MODEL), BETA * (H_MLA * DV_MLA) ** -0.5),
        'ln1_g': 1.0 + nrm((DEPTH, D_MODEL), 0.05),
        'ln1_b': nrm((DEPTH, D_MODEL), 0.02),
        'ln2_g': 1.0 + nrm((DEPTH, D_MODEL), 0.05),
        'ln2_b': nrm((DEPTH, D_MODEL), 0.02),
        'w_moe_group': nrm((DEPTH, D_MODEL, N_GROUPS), D_MODEL ** -0.5),
        'b_moe_group': nrm((DEPTH, N_GROUPS), 0.01),
        'w_moe_router': nrm((DEPTH, D_MODEL, N_EXPERTS), D_MODEL ** -0.5),
        'b_moe_router': nrm((DEPTH, N_EXPERTS), 0.01),
        'w_exp_gate': nrm((DEPTH, N_EXPERTS, D_MODEL, D_EXPERT), D_MODEL ** -0.5),
        'w_exp_up': nrm((DEPTH, N_EXPERTS, D_MODEL, D_EXPERT), D_MODEL ** -0.5),
        'w_exp_down': nrm((DEPTH, N_EXPERTS, D_EXPERT, D_MODEL), BETA * D_EXPERT ** -0.5),
    }


def reference(x_prompt, x_sample, cache_fox_k, cache_fox_v, cache_fox_logf, state_mlstm_c, state_mlstm_n,
              state_mlstm_m, cache_mla_ckv, cache_mla_krope, w_ab_in, b_fox_f, b_mlstm_i, b_mlstm_f,
              g_mlstm_norm, w_ab_out, w_mla_down, g_mla_q, w_mla_uq, g_mla_kv, w_mla_uk, w_mla_uv, w_mla_out,
              ln1_g, ln1_b, ln2_g, ln2_b, w_moe_group, b_moe_group, w_moe_router, b_moe_router,
              w_exp_gate, w_exp_up, w_exp_down):
    xp, xs = x_prompt, x_sample
    ab_p, ab_s, c_p, c_s = [], [], [], []
    for l in range(DEPTH):
        j = l // 2
        if l % 2 == 0:
            wts = (w_ab_in[j], b_fox_f[j], b_mlstm_i[j], b_mlstm_f[j], g_mlstm_norm[j], w_ab_out[j])
            mix_p, st_p = ab_mixer_prompt(xp, *wts)
            mix_s, st_s = ab_mixer_sample(xs, cache_fox_k[j], cache_fox_v[j], cache_fox_logf[j],
                                          state_mlstm_c[j], state_mlstm_n[j], state_mlstm_m[j], *wts)
            ab_p.append(st_p)
            ab_s.append(st_s)
        else:
            wts = (w_mla_down[j], g_mla_q[j], w_mla_uq[j], g_mla_kv[j], w_mla_uk[j], w_mla_uv[j], w_mla_out[j])
            mix_p, st_p = mla_mixer_prompt(xp, *wts)
            mix_s, st_s = mla_mixer_sample(xs, cache_mla_ckv[j], cache_mla_krope[j], *wts)
            c_p.append(st_p)
            c_s.append(st_s)
        moe_w = (w_moe_group[l], b_moe_group[l], w_moe_router[l], b_moe_router[l],
                 w_exp_gate[l], w_exp_up[l], w_exp_down[l])
        xp = post_norm_layer(xp, mix_p, moe_w, ln1_g[l], ln1_b[l], ln2_g[l], ln2_b[l])
        xs = post_norm_layer(xs, mix_s, moe_w, ln1_g[l], ln1_b[l], ln2_g[l], ln2_b[l])

    def stack(groups, i):
        return jnp.stack([g[i] for g in groups])

    y_prompt, y_sample = xp, xs
    fox_k_p, fox_v_p, fox_logf_p = stack(ab_p, 0), stack(ab_p, 1), stack(ab_p, 2)
    mlstm_c_p, mlstm_n_p, mlstm_m_p = stack(ab_p, 3), stack(ab_p, 4), stack(ab_p, 5)
    mla_ckv_p, mla_krope_p = stack(c_p, 0), stack(c_p, 1)
    fox_k_s, fox_v_s, fox_logf_s = stack(ab_s, 0), stack(ab_s, 1), stack(ab_s, 2)
    mlstm_c_s, mlstm_n_s, mlstm_m_s = stack(ab_s, 3), stack(ab_s, 4), stack(ab_s, 5)
    mla_ckv_s, mla_krope_s = stack(c_s, 0), stack(c_s, 1)
    return (y_prompt, y_sample,
            fox_k_p, fox_v_p, fox_logf_p, mlstm_c_p, mlstm_n_p, mlstm_m_p, mla_ckv_p, mla_krope_p,
            fox_k_s, fox_v_s, fox_logf_s, mlstm_c_s, mlstm_n_s, mlstm_m_s, mla_ckv_s, mla_krope_s)
```

```python
import functools

import numpy as np
import jax
import jax.numpy as jnp
from jax import lax
from jax.experimental import pallas as pl
from jax.experimental.pallas import tpu as pltpu

F32 = jnp.float32
BF16 = jnp.bfloat16

LANE = 128
VMEM_LIMIT_BYTES = 48 * 1024 * 1024

H_FOX, DH_FOX = 8, 64
W_FOX = H_FOX * DH_FOX
H_ML, DH_ML = 4, 128
W_ML = H_ML * DH_ML
H_MLA, NOPE, ROPE, DV_MLA = 8, 128, 64, 128
Q_RANK, KV_RANK = 384, 256
ROPE_BASE = 10000.0
N_GROUPS, EXP_PER_GROUP, TOP_K = 4, 8, 2
N_EXPERTS = N_GROUPS * EXP_PER_GROUP
DEPTH = 4
ALPHA = (2 * DEPTH) ** 0.25
LN_EPS = 1e-5
RMS_EPS = 1e-6
CHUNK = 64
NEG_BIG = -1e30


def _params(*sem):
    return pltpu.CompilerParams(dimension_semantics=sem, vmem_limit_bytes=VMEM_LIMIT_BYTES)


def _dot(a, b):
    return jnp.dot(a, b, preferred_element_type=F32)


def _dot_nt(a, b):
    return lax.dot_general(a, b, (((1,), (1,)), ((), ())), preferred_element_type=F32)


def _split3(x):
    hi = x.astype(BF16)
    r = x - hi.astype(F32)
    mid = r.astype(BF16)
    lo = (r - mid.astype(F32)).astype(BF16)
    return hi, mid, lo


def _layer_norm(y, g, b):
    mu = jnp.mean(y, axis=1, keepdims=True)
    yc = y - mu
    var = jnp.mean(yc * yc, axis=1, keepdims=True)
    return yc * lax.rsqrt(var + LN_EPS) * g + b


def _sigmoid(x):
    return 1.0 / (1.0 + jnp.exp(-x))


def _row_tile(t):
    return min(512, t)


def _proj_kernel(x_ref, w_ref, *o_refs, col_starts):
    xb = x_ref[...].astype(BF16)
    for o_ref, c0 in zip(o_refs, col_starts):
        n = o_ref.shape[1]
        for a in range(0, n, 512):
            b = min(a + 512, n)
            o_ref[:, a:b] = _dot(xb, w_ref[:, c0 + a:c0 + b]).astype(o_ref.dtype)


def proj(x, w, outs):
    t, k = x.shape
    tm = _row_tile(t)
    starts = tuple(int(s) for s in np.cumsum([0] + [n for n, _ in outs])[:-1])
    return pl.pallas_call(
        functools.partial(_proj_kernel, col_starts=starts),
        grid=(t // tm,),
        in_specs=[pl.BlockSpec((tm, k), lambda i: (i, 0)), pl.BlockSpec(w.shape, lambda i: (0, 0))],
        out_specs=[pl.BlockSpec((tm, n), lambda i: (i, 0)) for n, _ in outs],
        out_shape=[jax.ShapeDtypeStruct((t, n), dt) for n, dt in outs],
        compiler_params=_params("parallel"),
        name="proj",
    )(x, w)


def _gate_kernel(x_ref, wh_ref, wl_ref, b_ref, mask_ref, o_ref):
    x = x_ref[...]
    xh = x.astype(BF16)
    xl = (x - xh.astype(F32)).astype(BF16)
    z = _dot(xh, wh_ref[...]) + _dot(xh, wl_ref[...]) + _dot(xl, wh_ref[...]) + b_ref[...]
    log_sig = jnp.minimum(z, 0.0) - jnp.log1p(jnp.exp(-jnp.abs(z)))
    o_ref[...] = jnp.where(mask_ref[...] > 0.0, log_sig, z)


def gate_proj(x, w, bias, log_sigmoid_cols):
    t, k = x.shape
    n = w.shape[1]
    tm = _row_tile(t)
    wp = jnp.zeros((k, LANE), F32).at[:, :n].set(w)
    wh = wp.astype(BF16)
    wl = (wp - wh.astype(F32)).astype(BF16)
    bp = jnp.zeros((1, LANE), F32).at[0, :n].set(bias)
    mask = jnp.zeros((1, LANE), F32).at[0, :n].set(jnp.asarray(log_sigmoid_cols, F32))
    row = pl.BlockSpec((1, LANE), lambda i: (0, 0))
    wspec = pl.BlockSpec((k, LANE), lambda i: (0, 0))
    return pl.pallas_call(
        _gate_kernel,
        grid=(t // tm,),
        in_specs=[pl.BlockSpec((tm, k), lambda i: (i, 0)), wspec, wspec, row, row],
        out_specs=pl.BlockSpec((tm, LANE), lambda i: (i, 0)),
        out_shape=jax.ShapeDtypeStruct((t, LANE), F32),
        compiler_params=_params("parallel"),
        name="gate_proj",
    )(x, wh, wl, bp, mask)


def _fox_kernel(q_ref, k_ref, v_ref, lf_ref, o_ref, ncum_ref, *, tq, tk, q_off, cblk):
    hp = pl.program_id(1)
    qi = pl.program_id(2)
    sub = cblk // LANE

    @pl.when(jnp.logical_and(hp == 0, qi == 0))
    def _():
        r = lax.broadcasted_iota(jnp.int32, (cblk, cblk), 0)
        c = lax.broadcasted_iota(jnp.int32, (cblk, cblk), 1)
        upper = jnp.where(r <= c, 1.0, 0.0).astype(BF16)
        carry = jnp.zeros((lf_ref.shape[1], 1), F32)
        for j in range(lf_ref.shape[2] // cblk):
            g1, g2, g3 = _split3(lf_ref[0, :, j * cblk:(j + 1) * cblk])
            cum = _dot(g1, upper) + _dot(g2, upper) + _dot(g3, upper) + carry
            carry = cum[:, cblk - 1:cblk]
            for h in range(H_FOX):
                for u in range(sub):
                    ncum_ref[h, j * sub + u] = -cum[h:h + 1, u * LANE:(u + 1) * LANE]

    lane = lax.broadcasted_iota(jnp.int32, (tq, LANE), 1)
    q = q_ref[0] * (DH_FOX ** -0.5)
    zero = jnp.zeros_like(q)
    q_heads = (jnp.where(lane < DH_FOX, q, zero), jnp.where(lane >= DH_FOX, q, zero))
    q_start = q_off + qi * tq

    def update(carry, kt, vt, bias_rows, mask):
        out = []
        for h in range(2):
            m, l, acc = carry[h]
            s = _dot_nt(q_heads[h], kt) + bias_rows[h]
            if mask is not None:
                s = jnp.where(mask, s, -jnp.inf)
            m_new = jnp.maximum(m, jnp.max(s, axis=1, keepdims=True))
            a = jnp.exp(m - m_new)
            p = jnp.exp(s - m_new)
            out.append((m_new, a * l + jnp.sum(p, axis=1, keepdims=True), a * acc + _dot(p.astype(BF16), vt)))
        return tuple(out)

    def full_tile(j, carry):
        off = pl.multiple_of(j * tk, tk)
        kt = k_ref[0, pl.ds(off, tk), :].astype(BF16)
        vt = v_ref[0, pl.ds(off, tk), :].astype(BF16)
        bias = [jnp.concatenate([ncum_ref[hp * 2 + h, j * (tk // LANE) + u] for u in range(tk // LANE)], axis=1)
                for h in range(2)]
        return update(carry, kt, vt, bias, None)

    init = tuple((jnp.full((tq, 1), NEG_BIG, F32), jnp.zeros((tq, 1), F32), jnp.zeros((tq, LANE), F32))
                 for _ in range(2))
    carry = lax.fori_loop(0, q_start // tk, full_tile, init)

    off = pl.multiple_of(q_start, tq)
    kt = k_ref[0, pl.ds(off, tq), :].astype(BF16)
    vt = v_ref[0, pl.ds(off, tq), :].astype(BF16)
    if tq >= LANE:
        bias = [jnp.concatenate([ncum_ref[hp * 2 + h, q_start // LANE + u] for u in range(tq // LANE)], axis=1)
                for h in range(2)]
    else:
        lo = q_off % LANE
        bias = [ncum_ref[hp * 2 + h, q_start // LANE][:, lo:lo + tq] for h in range(2)]
    r = lax.broadcasted_iota(jnp.int32, (tq, tq), 0)
    c = lax.broadcasted_iota(jnp.int32, (tq, tq), 1)
    (_, l0, acc0), (_, l1, acc1) = update(carry, kt, vt, bias, c <= r)
    o_ref[0] = jnp.where(lane < DH_FOX, acc0 / l0, acc1 / l1).astype(o_ref.dtype)


def fox_attention(q, k, v, logf_rows, *, tq, tk=256, cblk=256):
    b, lq, _ = q.shape
    lk = k.shape[1]
    lkp = logf_rows.shape[2]
    q_off = lk - lq
    nq = lq // tq
    assert q_off % tk == 0 and (tq % tk == 0 or lq == tq) and lkp % cblk == 0
    assert tq >= LANE or (nq == 1 and q_off % LANE + tq <= LANE)
    kern = functools.partial(_fox_kernel, tq=tq, tk=tk, q_off=q_off, cblk=cblk)
    return pl.pallas_call(
        kern,
        grid=(b, H_FOX // 2, nq),
        in_specs=[
            pl.BlockSpec((1, tq, LANE), lambda bi, hp, qi: (bi, qi, hp)),
            pl.BlockSpec((1, lk, LANE), lambda bi, hp, qi: (bi, 0, hp)),
            pl.BlockSpec((1, lk, LANE), lambda bi, hp, qi: (bi, 0, hp)),
            pl.BlockSpec((1, logf_rows.shape[1], lkp), lambda bi, hp, qi: (bi, 0, 0)),
        ],
        out_specs=pl.BlockSpec((1, tq, LANE), lambda bi, hp, qi: (bi, qi, hp)),
        out_shape=jax.ShapeDtypeStruct((b, lq, W_FOX), BF16),
        scratch_shapes=[pltpu.VMEM((H_FOX, lkp // LANE, 1, LANE), F32)],
        compiler_params=_params("parallel", "arbitrary", "arbitrary"),
        name="fox_attention",
    )(q, k, v, logf_rows)


def _mlstm_kernel(q_ref, k_ref, v_ref, g_ref, c0_ref, n0_ref, m0_ref,
                  h_ref, c_ref, n_ref, m_ref, c_s, n_s, m_s, *, chunk):
    ci = pl.program_id(1)
    L = chunk
    scale = DH_ML ** -0.5

    @pl.when(ci == 0)
    def _():
        c_s[...] = c0_ref[0]
        n_s[...] = n0_ref[0]
        m_s[...] = m0_ref[0]

    r = lax.broadcasted_iota(jnp.int32, (L, L), 0)
    c = lax.broadcasted_iota(jnp.int32, (L, L), 1)
    eye = r == c
    causal = c <= r
    upper = jnp.where(r <= c, 1.0, 0.0).astype(BF16)
    g = g_ref[0]
    g1, g2, g3 = _split3(g)
    cum = _dot(g1, upper) + _dot(g2, upper) + _dot(g3, upper)

    def to_col(row):
        return jnp.sum(jnp.where(eye, row, 0.0), axis=1, keepdims=True)

    for h in range(H_ML):
        sl = slice(h * DH_ML, (h + 1) * DH_ML)
        qh, kh, vh = q_ref[0, :, sl], k_ref[0, :, sl], v_ref[0, :, sl]
        ig_row = g[h:h + 1, :]
        bh_row = cum[H_ML + h:H_ML + h + 1, :]
        bh_col, ig_col = to_col(bh_row), to_col(ig_row)
        m0 = m_s[h]
        c0 = c_s[h]
        n0 = n_s[h]
        logd = jnp.where(causal, bh_col - bh_row + ig_row, -jnp.inf)
        inter = bh_col + m0
        m_col = jnp.maximum(inter, jnp.max(logd, axis=1, keepdims=True))
        d = jnp.exp(logd - m_col)
        a_col = jnp.exp(inter - m_col)
        s = _dot_nt(qh, kh) * scale * d
        num = _dot(s.astype(BF16), vh) + _dot_nt(qh, c0.astype(BF16)) * a_col
        qf = qh.astype(F32)
        den = jnp.sum(s, axis=1, keepdims=True) + a_col * jnp.sum(qf * n0, axis=1, keepdims=True)
        den = jnp.maximum(jnp.abs(den), jnp.exp(-m_col))
        h_ref[0, :, sl] = (num / den).astype(h_ref.dtype)

        m_l = m_col[L - 1:L, :]
        bh_l = bh_row[:, L - 1:L]
        a_l = jnp.exp(bh_l + m0 - m_l)
        w_col = jnp.exp(bh_l - bh_col + ig_col - m_l)
        vw_t = jnp.transpose(vh.astype(F32) * w_col).astype(BF16)
        c_s[h] = a_l * c0 + _dot(vw_t, kh) * scale
        n_s[h] = a_l * n0 + jnp.sum(kh.astype(F32) * w_col, axis=0, keepdims=True) * scale
        m_s[h] = m_l

    @pl.when(ci == pl.num_programs(1) - 1)
    def _():
        c_ref[0] = c_s[...]
        n_ref[0] = n_s[...]
        m_ref[0] = m_s[...]


def mlstm(q, k, v, gate_rows, c0, n0, m0, *, chunk):
    b, s, _ = q.shape
    nc = s // chunk
    seq = pl.BlockSpec((1, chunk, W_ML), lambda bi, ci: (bi, ci, 0))
    c_spec = pl.BlockSpec((1, H_ML, DH_ML, DH_ML), lambda bi, ci: (bi, 0, 0, 0))
    n_spec = pl.BlockSpec((1, H_ML, 1, DH_ML), lambda bi, ci: (bi, 0, 0, 0))
    m_spec = pl.BlockSpec((1, H_ML, 1, 1), lambda bi, ci: (bi, 0, 0, 0))
    return pl.pallas_call(
        functools.partial(_mlstm_kernel, chunk=chunk),
        grid=(b, nc),
        in_specs=[seq, seq, seq, pl.BlockSpec((1, gate_rows.shape[1], chunk), lambda bi, ci: (bi, 0, ci)),
                  c_spec, n_spec, m_spec],
        out_specs=[seq, c_spec, n_spec, m_spec],
        out_shape=[jax.ShapeDtypeStruct((b, s, W_ML), BF16),
                   jax.ShapeDtypeStruct(c0.shape, F32), jax.ShapeDtypeStruct(n0.shape, F32),
                   jax.ShapeDtypeStruct(m0.shape, F32)],
        scratch_shapes=[pltpu.VMEM((H_ML, DH_ML, DH_ML), F32), pltpu.VMEM((H_ML, 1, DH_ML), F32),
                        pltpu.VMEM((H_ML, 1, 1), F32)],
        compiler_params=_params("parallel", "arbitrary"),
        name="mlstm",
    )(q, k, v, gate_rows, c0, n0, m0)


def _ab_out_kernel(fo_ref, mh_ref, mo_ref, x_ref, gml_ref, w_ref, g_ref, b_ref, o_ref, obf_ref):
    parts = [fo_ref[...]]
    for h in range(H_ML):
        sl = slice(h * DH_ML, (h + 1) * DH_ML)
        mh = mh_ref[:, sl].astype(F32)
        ms = jnp.mean(mh * mh, axis=1, keepdims=True)
        nm = mh * lax.rsqrt(ms + RMS_EPS) * gml_ref[:, sl]
        parts.append((nm * _sigmoid(mo_ref[:, sl].astype(F32))).astype(BF16))
    cat = jnp.concatenate(parts, axis=1)
    y = ALPHA * x_ref[...] + _dot(cat, w_ref[...])
    out = _layer_norm(y, g_ref[...], b_ref[...])
    o_ref[...] = out
    obf_ref[...] = out.astype(BF16)


def _rows(tm, n):
    return pl.BlockSpec((tm, n), lambda i: (i, 0))


def _whole(shape):
    return pl.BlockSpec(shape, lambda i: (0,) * len(shape))


def ab_out(fo, mh, mo, x, g_ml, w_out, ln_g, ln_b):
    t, d = x.shape
    tm = _row_tile(t)
    return pl.pallas_call(
        _ab_out_kernel,
        grid=(t // tm,),
        in_specs=[_rows(tm, W_FOX), _rows(tm, W_ML), _rows(tm, W_ML), _rows(tm, d), _whole((1, W_ML)),
                  _whole(w_out.shape), _whole((1, d)), _whole((1, d))],
        out_specs=[_rows(tm, d), _rows(tm, d)],
        out_shape=[jax.ShapeDtypeStruct((t, d), F32), jax.ShapeDtypeStruct((t, d), BF16)],
        compiler_params=_params("parallel"),
        name="ab_out",
    )(fo, mh, mo, x, g_ml, w_out, ln_g, ln_b)


def _rms(z, g):
    return z * lax.rsqrt(jnp.mean(z * z, axis=1, keepdims=True) + RMS_EPS) * g


def _mla_down_kernel(x_ref, w_ref, gq_ref, gkv_ref, cc_ref, ss_ref, cq_ref, ckv_ref, kr_ref, ckvb_ref, kr2_ref):
    xb = x_ref[...].astype(BF16)
    cq_ref[...] = _rms(_dot(xb, w_ref[:, 0:Q_RANK]), gq_ref[...]).astype(BF16)
    ckv = _rms(_dot(xb, w_ref[:, Q_RANK:Q_RANK + KV_RANK]), gkv_ref[...])
    ckv_ref[...] = ckv
    ckvb_ref[...] = ckv.astype(BF16)
    c0 = Q_RANK + KV_RANK
    rope = (_dot(xb, w_ref[:, c0:c0 + LANE]) * cc_ref[...] + _dot(xb, w_ref[:, c0 + LANE:c0 + 2 * LANE]) * ss_ref[...])
    kr_ref[...] = rope[:, :ROPE]
    kr2_ref[...] = rope.astype(BF16)


def _table_spec(table, tm, s):
    if table.shape[0] == tm:
        return pl.BlockSpec((tm, LANE), lambda i: (0, 0))
    per = s // tm
    return pl.BlockSpec((tm, LANE), lambda i: (i % per, 0))


def mla_down(x, w, g_q, g_kv, cc, ss, s):
    t, d = x.shape
    tm = _row_tile(t)
    tab = _table_spec(cc, tm, s)
    return pl.pallas_call(
        _mla_down_kernel,
        grid=(t // tm,),
        in_specs=[_rows(tm, d), _whole(w.shape), _whole((1, Q_RANK)), _whole((1, KV_RANK)), tab, tab],
        out_specs=[_rows(tm, Q_RANK), _rows(tm, KV_RANK), _rows(tm, ROPE), _rows(tm, KV_RANK), _rows(tm, LANE)],
        out_shape=[jax.ShapeDtypeStruct((t, Q_RANK), BF16), jax.ShapeDtypeStruct((t, KV_RANK), F32),
                   jax.ShapeDtypeStruct((t, ROPE), F32), jax.ShapeDtypeStruct((t, KV_RANK), BF16),
                   jax.ShapeDtypeStruct((t, LANE), BF16)],
        compiler_params=_params("parallel"),
        name="mla_down",
    )(x, w, g_q, g_kv, cc, ss)


def _mla_uq_kernel(cq_ref, w_ref, wuk_ref, cc_ref, ss_ref, qlat_ref, qrope_ref):
    cq = cq_ref[...]
    scale = (NOPE + ROPE) ** -0.5
    for h in range(H_MLA):
        qn = _dot(cq, w_ref[:, h * NOPE:(h + 1) * NOPE]).astype(BF16)
        qlat_ref[:, h * KV_RANK:(h + 1) * KV_RANK] = (_dot(qn, wuk_ref[h]) * scale).astype(BF16)
    cc, ss = cc_ref[...] * scale, ss_ref[...] * scale
    r0 = H_MLA * NOPE
    r1 = r0 + H_MLA * ROPE
    for p in range(H_MLA // 2):
        qr = _dot(cq, w_ref[:, r0 + p * LANE:r0 + (p + 1) * LANE])
        qs = _dot(cq, w_ref[:, r1 + p * LANE:r1 + (p + 1) * LANE])
        qrope_ref[:, p * LANE:(p + 1) * LANE] = (qr * cc + qs * ss).astype(BF16)


def mla_uq(cq, w, wuk_t, cc, ss, s):
    t = cq.shape[0]
    tm = _row_tile(t)
    tab = _table_spec(cc, tm, s)
    return pl.pallas_call(
        _mla_uq_kernel,
        grid=(t // tm,),
        in_specs=[_rows(tm, Q_RANK), _whole(w.shape), _whole(wuk_t.shape), tab, tab],
        out_specs=[_rows(tm, H_MLA * KV_RANK), _rows(tm, H_MLA * ROPE)],
        out_shape=[jax.ShapeDtypeStruct((t, H_MLA * KV_RANK), BF16), jax.ShapeDtypeStruct((t, H_MLA * ROPE), BF16)],
        compiler_params=_params("parallel"),
        name="mla_uq",
    )(cq, w, wuk_t, cc, ss)


def _mla_attn_kernel(ql_ref, qr_ref, kc_ref, kr_ref, o_ref, qs_l, qs_r, m_s, l_s, acc_s, *, tq, tk, q_off):
    qi = pl.program_id(1)
    lane = lax.broadcasted_iota(jnp.int32, (tq, LANE), 1)
    for h in range(H_MLA):
        rows = slice(h * tq, (h + 1) * tq)
        qs_l[rows, :] = ql_ref[0, :, h * KV_RANK:(h + 1) * KV_RANK]
        qr = qr_ref[0, :, (h // 2) * LANE:(h // 2 + 1) * LANE]
        keep = (lane < ROPE) if h % 2 == 0 else (lane >= ROPE)
        qs_r[rows, :] = jnp.where(keep, qr, jnp.zeros_like(qr))
    m_s[...] = jnp.full(m_s.shape, NEG_BIG, F32)
    l_s[...] = jnp.zeros(l_s.shape, F32)
    acc_s[...] = jnp.zeros(acc_s.shape, F32)
    q_start = q_off + qi * tq

    def update(kc, kr, mask):
        s = _dot_nt(qs_l[...], kc) + _dot_nt(qs_r[...], kr)
        if mask is not None:
            s = jnp.where(mask, s, -jnp.inf)
        m_old = m_s[...]
        m_new = jnp.maximum(m_old, jnp.max(s, axis=1, keepdims=True))
        a = jnp.exp(m_old - m_new)
        p = jnp.exp(s - m_new)
        l_s[...] = a * l_s[...] + jnp.sum(p, axis=1, keepdims=True)
        acc_s[...] = a * acc_s[...] + _dot(p.astype(BF16), kc)
        m_s[...] = m_new

    def full_tile(j, carry):
        off = pl.multiple_of(j * tk, tk)
        update(kc_ref[0, pl.ds(off, tk), :], kr_ref[0, pl.ds(off, tk), :], None)
        return carry

    lax.fori_loop(0, q_start // tk, full_tile, 0)
    off = pl.multiple_of(q_start, tq)
    r = lax.broadcasted_iota(jnp.int32, (H_MLA * tq, tq), 0)
    c = lax.broadcasted_iota(jnp.int32, (H_MLA * tq, tq), 1)
    mask = (c // CHUNK) <= ((r & (tq - 1)) // CHUNK)
    update(kc_ref[0, pl.ds(off, tq), :], kr_ref[0, pl.ds(off, tq), :], mask)
    o = acc_s[...] / l_s[...]
    for h in range(H_MLA):
        o_ref[0, :, h * KV_RANK:(h + 1) * KV_RANK] = o[h * tq:(h + 1) * tq, :].astype(o_ref.dtype)


def mla_attention(q_lat, q_rope, kc, kr2, *, tq, tk):
    b, lq, _ = q_lat.shape
    lk = kc.shape[1]
    q_off = lk - lq
    assert q_off % tk == 0 and (tq % tk == 0 or lq == tq) and tq % CHUNK == 0 and (tq & (tq - 1)) == 0
    kern = functools.partial(_mla_attn_kernel, tq=tq, tk=tk, q_off=q_off)
    return pl.pallas_call(
        kern,
        grid=(b, lq // tq),
        in_specs=[
            pl.BlockSpec((1, tq, H_MLA * KV_RANK), lambda bi, qi: (bi, qi, 0)),
            pl.BlockSpec((1, tq, H_MLA * ROPE), lambda bi, qi: (bi, qi, 0)),
            pl.BlockSpec((1, lk, KV_RANK), lambda bi, qi: (bi, 0, 0)),
            pl.BlockSpec((1, lk, LANE), lambda bi, qi: (bi, 0, 0)),
        ],
        out_specs=pl.BlockSpec((1, tq, H_MLA * KV_RANK), lambda bi, qi: (bi, qi, 0)),
        out_shape=jax.ShapeDtypeStruct((b, lq, H_MLA * KV_RANK), BF16),
        scratch_shapes=[pltpu.VMEM((H_MLA * tq, KV_RANK), BF16), pltpu.VMEM((H_MLA * tq, LANE), BF16),
                        pltpu.VMEM((H_MLA * tq, 1), F32), pltpu.VMEM((H_MLA * tq, 1), F32),
                        pltpu.VMEM((H_MLA * tq, KV_RANK), F32)],
        compiler_params=_params("parallel", "arbitrary"),
        name="mla_attention",
    )(q_lat, q_rope, kc, kr2)


def _mla_out_kernel(ol_ref, wuv_ref, w_ref, x_ref, g_ref, b_ref, o_ref, obf_ref):
    parts = [_dot(ol_ref[:, h * KV_RANK:(h + 1) * KV_RANK], wuv_ref[h]).astype(BF16) for h in range(H_MLA)]
    y = ALPHA * x_ref[...] + _dot(jnp.concatenate(parts, axis=1), w_ref[...])
    out = _layer_norm(y, g_ref[...], b_ref[...])
    o_ref[...] = out
    obf_ref[...] = out.astype(BF16)


def mla_out(o_lat, wuv, w_out, x, ln_g, ln_b):
    t, d = x.shape
    tm = _row_tile(t)
    return pl.pallas_call(
        _mla_out_kernel,
        grid=(t // tm,),
        in_specs=[_rows(tm, H_MLA * KV_RANK), _whole(wuv.shape), _whole(w_out.shape), _rows(tm, d),
                  _whole((1, d)), _whole((1, d))],
        out_specs=[_rows(tm, d), _rows(tm, d)],
        out_shape=[jax.ShapeDtypeStruct((t, d), F32), jax.ShapeDtypeStruct((t, d), BF16)],
        compiler_params=_params("parallel"),
        name="mla_out",
    )(o_lat, wuv, w_out, x, ln_g, ln_b)


def _moe_kernel(te_ref, nu_ref, xs_ref, gate_ref, w1_ref, w3_ref, w2_ref, o_ref):
    t = pl.program_id(0)

    @pl.when(t < nu_ref[0])
    def _():
        xb = xs_ref[...]
        h1 = _dot(xb, w1_ref[0])
        h3 = _dot(xb, w3_ref[0])
        hidden = (h1 * _sigmoid(h1) * h3).astype(BF16)
        o_ref[...] = (gate_ref[...] * _dot(hidden, w2_ref[0])).astype(o_ref.dtype)

    @pl.when(t >= nu_ref[0])
    def _():
        o_ref[...] = jnp.zeros(o_ref.shape, o_ref.dtype)


def moe_experts(tile_expert, n_used, xs, gate, w1, w3, w2, tm):
    p, d = xs.shape
    de = w1.shape[2]
    grid_spec = pltpu.PrefetchScalarGridSpec(
        num_scalar_prefetch=2,
        grid=(p // tm,),
        in_specs=[
            pl.BlockSpec((tm, d), lambda t, te, nu: (t, 0)),
            pl.BlockSpec((tm, 1), lambda t, te, nu: (t, 0)),
            pl.BlockSpec((1, d, de), lambda t, te, nu: (te[t], 0, 0)),
            pl.BlockSpec((1, d, de), lambda t, te, nu: (te[t], 0, 0)),
            pl.BlockSpec((1, de, d), lambda t, te, nu: (te[t], 0, 0)),
        ],
        out_specs=pl.BlockSpec((tm, d), lambda t, te, nu: (t, 0)),
    )
    return pl.pallas_call(
        _moe_kernel,
        grid_spec=grid_spec,
        out_shape=jax.ShapeDtypeStruct((p, d), BF16),
        compiler_params=_params("arbitrary"),
        name="moe_experts",
    )(tile_expert, n_used, xs, gate, w1, w3, w2)


def _combine_kernel(x_ref, y0_ref, y1_ref, g_ref, b_ref, o_ref):
    y = ALPHA * x_ref[...] + (y0_ref[...].astype(F32) + y1_ref[...].astype(F32))
    o_ref[...] = _layer_norm(y, g_ref[...], b_ref[...])


def combine_ln(x, y0, y1, ln_g, ln_b):
    t, d = x.shape
    tm = _row_tile(t)
    return pl.pallas_call(
        _combine_kernel,
        grid=(t // tm,),
        in_specs=[_rows(tm, d), _rows(tm, d), _rows(tm, d), _whole((1, d)), _whole((1, d))],
        out_specs=_rows(tm, d),
        out_shape=jax.ShapeDtypeStruct((t, d), F32),
        compiler_params=_params("parallel"),
        name="combine_ln",
    )(x, y0, y1, ln_g, ln_b)


def _route(logits, tm):
    t = logits.shape[0]
    g_logit = logits[:, :N_GROUPS]
    g_prob = jax.nn.softmax(g_logit, -1)
    g_idx = jnp.argmax(g_logit, -1)
    g_gate = jnp.take_along_axis(g_prob, g_idx[:, None], -1)
    e_logit = logits[:, N_GROUPS:N_GROUPS + N_EXPERTS].reshape(t, N_GROUPS, EXP_PER_GROUP)
    e_sel = jnp.take_along_axis(e_logit, g_idx[:, None, None], axis=1)[:, 0]
    top_v, top_i = lax.top_k(e_sel, TOP_K)
    top_w = (jax.nn.softmax(top_v, -1) * g_gate).reshape(-1)
    flat_e = (g_idx[:, None] * EXP_PER_GROUP + top_i).astype(jnp.int32).reshape(-1)

    n_assign = flat_e.shape[0]
    n_rows = n_assign + N_EXPERTS * tm
    order = jnp.argsort(flat_e, stable=True).astype(jnp.int32)
    inv = jnp.argsort(order).astype(jnp.int32)
    counts = jnp.sum((flat_e[:, None] == jnp.arange(N_EXPERTS, dtype=jnp.int32)[None, :]).astype(jnp.int32), axis=0)
    padded = ((counts + tm - 1) // tm) * tm
    row_end = jnp.cumsum(padded)
    row_start = row_end - padded
    slot_start = jnp.cumsum(counts) - counts
    pos = row_start[flat_e] + inv - slot_start[flat_e]

    n_tiles = n_rows // tm
    tile_expert = jnp.minimum(jnp.searchsorted(row_end, jnp.arange(n_tiles, dtype=jnp.int32) * tm, side="right"),
                              N_EXPERTS - 1).astype(jnp.int32)
    n_used = (row_end[-1] // tm).astype(jnp.int32).reshape(1)
    rows = jnp.arange(n_rows, dtype=jnp.int32)
    row_e = jnp.repeat(tile_expert, tm)
    local = rows - row_start[row_e]
    valid = jnp.logical_and(local < counts[row_e], rows < row_end[-1])
    assign = order[jnp.clip(slot_start[row_e] + local, 0, n_assign - 1)]
    row_token = jnp.where(valid, assign // TOP_K, 0)
    row_gate = jnp.where(valid, top_w[assign], 0.0)[:, None]
    return tile_expert, n_used, row_token, row_gate, pos.reshape(t, TOP_K)


def hier_moe_ln(x1, x1_bf, moe_w, ln_g, ln_b):
    w_route, b_route, w1, w3, w2 = moe_w
    t = x1.shape[0]
    tm = 512 if t >= 8192 else 64
    logits = gate_proj(x1, w_route, b_route, np.zeros(w_route.shape[1]))
    tile_expert, n_used, row_token, row_gate, pos = _route(logits, tm)
    xs = jnp.take(x1_bf, row_token, axis=0)
    ys = moe_experts(tile_expert, n_used, xs, row_gate, w1, w3, w2, tm)
    y0 = jnp.take(ys, pos[:, 0], axis=0)
    y1 = jnp.take(ys, pos[:, 1], axis=0)
    return combine_ln(x1, y0, y1, ln_g, ln_b)


def _rope_tables(s, offset, tm):
    half = ROPE // 2
    inv_freq = ROPE_BASE ** (-jnp.arange(half, dtype=F32) / half)
    ang = (offset + jnp.arange(s)).astype(F32)[:, None] * inv_freq[None, :]
    cos, sin = jnp.cos(ang), jnp.sin(ang)
    cc = jnp.concatenate([cos, cos, cos, cos], -1)
    ss = jnp.concatenate([-sin, sin, -sin, sin], -1)
    if tm > s:
        cc, ss = jnp.tile(cc, (tm // s, 1)), jnp.tile(ss, (tm // s, 1))
    return cc, ss


def _round_up(n, m):
    return (n + m - 1) // m * m


def ab_layer(x, b, s, wts, cache):
    w_big, w_gate, b_gate, g_ml, w_out, ln_g, ln_b = wts
    t = b * s
    fq, mq, mk, mv, mo, fk, fv = proj(
        x, w_big, [(W_FOX, BF16), (W_ML, BF16), (W_ML, BF16), (W_ML, BF16), (W_ML, BF16), (W_FOX, F32), (W_FOX, F32)])
    ls_cols = np.array([1.0] * H_FOX + [0.0] * H_ML + [1.0] * H_ML)
    gates = gate_proj(x, w_gate, b_gate, ls_cols)
    flf = gates[:, :H_FOX].reshape(b, s, H_FOX)
    ml_rows = jnp.swapaxes(gates[:, H_FOX:H_FOX + 2 * H_ML].reshape(b, s, 2 * H_ML), 1, 2)
    ml_rows = jnp.pad(ml_rows, ((0, 0), (0, 16 - 2 * H_ML), (0, 0)))
    fk3, fv3 = fk.reshape(b, s, W_FOX), fv.reshape(b, s, W_FOX)
    if cache is None:
        k_all, v_all, lf_all = fk3, fv3, flf
        c0 = jnp.zeros((b, H_ML, DH_ML, DH_ML), F32)
        n0 = jnp.zeros((b, H_ML, 1, DH_ML), F32)
        m0 = jnp.zeros((b, H_ML, 1, 1), F32)
        tq, chunk = 256, 256
    else:
        ck, cv, clf, c0, n0, m0 = cache
        past = ck.shape[1]
        k_all = jnp.concatenate([ck.reshape(b, past, W_FOX), fk3], 1)
        v_all = jnp.concatenate([cv.reshape(b, past, W_FOX), fv3], 1)
        lf_all = jnp.concatenate([clf, flf], 1)
        n0 = n0.reshape(b, H_ML, 1, DH_ML)
        m0 = m0.reshape(b, H_ML, 1, 1)
        tq, chunk = s, s
    lk = k_all.shape[1]
    lf_rows = jnp.pad(jnp.swapaxes(lf_all, 1, 2), ((0, 0), (0, 16 - H_FOX), (0, _round_up(lk, 256) - lk)))
    fo = fox_attention(fq.reshape(b, s, W_FOX), k_all, v_all, lf_rows, tq=tq)
    mh, c_new, n_new, m_new = mlstm(mq.reshape(b, s, W_ML), mk.reshape(b, s, W_ML), mv.reshape(b, s, W_ML),
                                    ml_rows, c0, n0, m0, chunk=chunk)
    x1, x1_bf = ab_out(fo.reshape(t, W_FOX), mh.reshape(t, W_ML), mo, x, g_ml, w_out, ln_g, ln_b)
    state = (fk3.reshape(b, s, H_FOX, DH_FOX), fv3.reshape(b, s, H_FOX, DH_FOX), flf,
             c_new, n_new.reshape(b, H_ML, DH_ML), m_new.reshape(b, H_ML))
    return x1, x1_bf, state


def mla_layer(x, b, s, wts, cache):
    w_down, g_q, g_kv, w_uq, wuk_t, wuv, w_out, ln_g, ln_b = wts
    t = b * s
    tm = _row_tile(t)
    past = 0 if cache is None else cache[0].shape[1]
    cc, ss = _rope_tables(s, past, tm)
    cq, ckv, kr, ckv_bf, kr2_bf = mla_down(x, w_down, g_q, g_kv, cc, ss, s)
    q_lat, q_rope = mla_uq(cq, w_uq, wuk_t, cc, ss, s)
    kc = ckv_bf.reshape(b, s, KV_RANK)
    kr2 = kr2_bf.reshape(b, s, LANE)
    if cache is None:
        tq = 128
    else:
        c_ckv, c_kr = cache
        kc = jnp.concatenate([c_ckv.astype(BF16), kc], 1)
        c_kr_bf = c_kr.astype(BF16)
        kr2 = jnp.concatenate([jnp.concatenate([c_kr_bf, c_kr_bf], -1), kr2], 1)
        tq = s
    o_lat = mla_attention(q_lat.reshape(b, s, -1), q_rope.reshape(b, s, -1), kc, kr2, tq=tq,
                          tk=tq if cache is None else 256)
    x1, x1_bf = mla_out(o_lat.reshape(t, -1), wuv, w_out, x, ln_g, ln_b)
    return x1, x1_bf, (ckv.reshape(b, s, KV_RANK), kr.reshape(b, s, ROPE))


def _prep_ab_weights(w_in, b_fox_f, b_ml_i, b_ml_f, g_ml, w_out, ln_g, ln_b):
    sizes = (W_FOX, W_FOX, W_FOX, H_FOX, W_ML, W_ML, W_ML, H_ML, H_ML, W_ML)
    idx = np.cumsum(sizes[:-1]).tolist()
    fq, fk, fv, ff, mq, mk, mv, mi, mf, mo = jnp.split(w_in, idx, axis=1)
    w_big = jnp.concatenate([fq, mq, mk, mv, mo, fk, fv], 1).astype(BF16)
    w_gate = jnp.concatenate([ff, mi, mf], 1)
    b_gate = jnp.concatenate([b_fox_f, b_ml_i, b_ml_f])
    return (w_big, w_gate, b_gate, g_ml[None, :], w_out.astype(BF16), ln_g[None, :], ln_b[None, :])


def _prep_mla_weights(w_down, g_q, w_uq, g_kv, w_uk, w_uv, w_out, ln_g, ln_b):
    half = ROPE // 2
    cq_w, ckv_w, kr_w = jnp.split(w_down, [Q_RANK, Q_RANK + KV_RANK], axis=1)
    kr_sw = jnp.concatenate([kr_w[:, half:], kr_w[:, :half]], 1)
    w_down_p = jnp.concatenate([cq_w, ckv_w, kr_w, kr_w, kr_sw, kr_sw], 1).astype(BF16)
    uq = w_uq.reshape(Q_RANK, H_MLA, NOPE + ROPE)
    uq_nope = uq[:, :, :NOPE].reshape(Q_RANK, H_MLA * NOPE)
    uq_rope = uq[:, :, NOPE:]
    uq_rope_sw = jnp.concatenate([uq_rope[..., half:], uq_rope[..., :half]], -1)
    w_uq_p = jnp.concatenate([uq_nope, uq_rope.reshape(Q_RANK, -1), uq_rope_sw.reshape(Q_RANK, -1)], 1).astype(BF16)
    wuk_t = jnp.transpose(w_uk, (1, 2, 0)).astype(BF16)
    wuv = jnp.transpose(w_uv, (1, 0, 2)).astype(BF16)
    return (w_down_p, g_q[None, :], g_kv[None, :], w_uq_p, wuk_t, wuv, w_out.astype(BF16), ln_g[None, :], ln_b[None, :])


def kernel(x_prompt, x_sample, cache_fox_k, cache_fox_v, cache_fox_logf, state_mlstm_c, state_mlstm_n, state_mlstm_m, cache_mla_ckv, cache_mla_krope, w_ab_in, b_fox_f, b_mlstm_i, b_mlstm_f, g_mlstm_norm, w_ab_out, w_mla_down, g_mla_q, w_mla_uq, g_mla_kv, w_mla_uk, w_mla_uv, w_mla_out, ln1_g, ln1_b, ln2_g, ln2_b, w_moe_group, b_moe_group, w_moe_router, b_moe_router, w_exp_gate, w_exp_up, w_exp_down):
    bp, sp, d = x_prompt.shape
    bs, ss_, _ = x_sample.shape
    xp = x_prompt.reshape(bp * sp, d)
    xs = x_sample.reshape(bs * ss_, d)
    ab_p, ab_s, c_p, c_s = [], [], [], []
    for l in range(DEPTH):
        j = l // 2
        if l % 2 == 0:
            wts = _prep_ab_weights(w_ab_in[j], b_fox_f[j], b_mlstm_i[j], b_mlstm_f[j], g_mlstm_norm[j],
                                   w_ab_out[j], ln1_g[l], ln1_b[l])
            xp1, xp1_bf, st_p = ab_layer(xp, bp, sp, wts, None)
            xs1, xs1_bf, st_s = ab_layer(xs, bs, ss_, wts, (cache_fox_k[j], cache_fox_v[j], cache_fox_logf[j],
                                                            state_mlstm_c[j], state_mlstm_n[j], state_mlstm_m[j]))
            ab_p.append(st_p)
            ab_s.append(st_s)
        else:
            wts = _prep_mla_weights(w_mla_down[j], g_mla_q[j], w_mla_uq[j], g_mla_kv[j], w_mla_uk[j], w_mla_uv[j],
                                    w_mla_out[j], ln1_g[l], ln1_b[l])
            xp1, xp1_bf, st_p = mla_layer(xp, bp, sp, wts, None)
            xs1, xs1_bf, st_s = mla_layer(xs, bs, ss_, wts, (cache_mla_ckv[j], cache_mla_krope[j]))
            c_p.append(st_p)
            c_s.append(st_s)
        moe_w = (jnp.concatenate([w_moe_group[l], w_moe_router[l]], 1),
                 jnp.concatenate([b_moe_group[l], b_moe_router[l]]),
                 w_exp_gate[l].astype(BF16), w_exp_up[l].astype(BF16), w_exp_down[l].astype(BF16))
        xp = hier_moe_ln(xp1, xp1_bf, moe_w, ln2_g[l][None, :], ln2_b[l][None, :])
        xs = hier_moe_ln(xs1, xs1_bf, moe_w, ln2_g[l][None, :], ln2_b[l][None, :])

    def stack(groups, i):
        return jnp.stack([g[i] for g in groups])

    return (xp.reshape(bp, sp, d), xs.reshape(bs, ss_, d),
            stack(ab_p, 0), stack(ab_p, 1), stack(ab_p, 2), stack(ab_p, 3), stack(ab_p, 4), stack(ab_p, 5),
            stack(c_p, 0), stack(c_p, 1),
            stack(ab_s, 0), stack(ab_s, 1), stack(ab_s, 2), stack(ab_s, 3), stack(ab_s, 4), stack(ab_s, 5),
            stack(c_s, 0), stack(c_s, 1))
```

```python
import functools

import numpy as np
import jax
import jax.numpy as jnp
from jax import lax
from jax.experimental import pallas as pl
from jax.experimental.pallas import tpu as pltpu

F32 = jnp.float32
BF16 = jnp.bfloat16

LANE = 128
VMEM_LIMIT_BYTES = 48 * 1024 * 1024

H_FOX, DH_FOX = 8, 64
W_FOX = H_FOX * DH_FOX
H_ML, DH_ML = 4, 128
W_ML = H_ML * DH_ML
H_MLA, NOPE, ROPE, DV_MLA = 8, 128, 64, 128
Q_RANK, KV_RANK = 384, 256
ROPE_BASE = 10000.0
N_GROUPS, EXP_PER_GROUP, TOP_K = 4, 8, 2
N_EXPERTS = N_GROUPS * EXP_PER_GROUP
DEPTH = 4
ALPHA = (2 * DEPTH) ** 0.25
LN_EPS = 1e-5
RMS_EPS = 1e-6
CHUNK = 64
NEG_BIG = -1e30


def _params(*sem):
    return pltpu.CompilerParams(dimension_semantics=sem, vmem_limit_bytes=VMEM_LIMIT_BYTES)


def _dot(a, b):
    return jnp.dot(a, b, preferred_element_type=F32)


def _dot_nt(a, b):
    return lax.dot_general(a, b, (((1,), (1,)), ((), ())), preferred_element_type=F32)


def _split3(x):
    hi = x.astype(BF16)
    r = x - hi.astype(F32)
    mid = r.astype(BF16)
    lo = (r - mid.astype(F32)).astype(BF16)
    return hi, mid, lo


def _layer_norm(y, g, b):
    mu = jnp.mean(y, axis=1, keepdims=True)
    yc = y - mu
    var = jnp.mean(yc * yc, axis=1, keepdims=True)
    return yc * lax.rsqrt(var + LN_EPS) * g + b


def _sigmoid(x):
    return 1.0 / (1.0 + jnp.exp(-x))


def _row_tile(t):
    return min(512, t)


def _proj_kernel(x_ref, w_ref, *o_refs, groups):
    xb = x_ref[...].astype(BF16)
    refs = iter(o_refs)
    c0 = 0
    for n, dtypes in groups:
        group_refs = [next(refs) for _ in dtypes]
        for a in range(0, n, 512):
            b = min(a + 512, n)
            z = _dot(xb, w_ref[:, c0 + a:c0 + b])
            for o_ref in group_refs:
                o_ref[:, a:b] = z.astype(o_ref.dtype)
        c0 += n


def proj(x, w, groups):
    t, k = x.shape
    tm = _row_tile(t)
    flat = [(n, dt) for n, dts in groups for dt in dts]
    return pl.pallas_call(
        functools.partial(_proj_kernel, groups=tuple(groups)),
        grid=(t // tm,),
        in_specs=[pl.BlockSpec((tm, k), lambda i: (i, 0)), pl.BlockSpec(w.shape, lambda i: (0, 0))],
        out_specs=[pl.BlockSpec((tm, n), lambda i: (i, 0)) for n, _ in flat],
        out_shape=[jax.ShapeDtypeStruct((t, n), dt) for n, dt in flat],
        compiler_params=_params("parallel"),
        name="proj",
    )(x, w)


def _gate_kernel(x_ref, wh_ref, wl_ref, b_ref, mask_ref, o_ref):
    x = x_ref[...]
    xh = x.astype(BF16)
    xl = (x - xh.astype(F32)).astype(BF16)
    z = _dot(xh, wh_ref[...]) + _dot(xh, wl_ref[...]) + _dot(xl, wh_ref[...]) + b_ref[...]
    log_sig = jnp.minimum(z, 0.0) - jnp.log1p(jnp.exp(-jnp.abs(z)))
    o_ref[...] = jnp.where(mask_ref[...] > 0.0, log_sig, z)


def gate_proj(x, w, bias, log_sigmoid_cols):
    t, k = x.shape
    n = w.shape[1]
    tm = _row_tile(t)
    wp = jnp.zeros((k, LANE), F32).at[:, :n].set(w)
    wh = wp.astype(BF16)
    wl = (wp - wh.astype(F32)).astype(BF16)
    bp = jnp.zeros((1, LANE), F32).at[0, :n].set(bias)
    mask = jnp.zeros((1, LANE), F32).at[0, :n].set(jnp.asarray(log_sigmoid_cols, F32))
    row = pl.BlockSpec((1, LANE), lambda i: (0, 0))
    wspec = pl.BlockSpec((k, LANE), lambda i: (0, 0))
    return pl.pallas_call(
        _gate_kernel,
        grid=(t // tm,),
        in_specs=[pl.BlockSpec((tm, k), lambda i: (i, 0)), wspec, wspec, row, row],
        out_specs=pl.BlockSpec((tm, LANE), lambda i: (i, 0)),
        out_shape=jax.ShapeDtypeStruct((t, LANE), F32),
        compiler_params=_params("parallel"),
        name="gate_proj",
    )(x, wh, wl, bp, mask)


def _lanes(x, w):
    if w < LANE:
        return x[:, :w]
    return x if w == LANE else jnp.concatenate([x] * (w // LANE), axis=1)


def _softmax_step(s, h, m_s, l_s, acc_s, v):
    w = s.shape[1]
    m_prev = m_s[h]
    m_new = jnp.maximum(m_prev, jnp.max(s, axis=1, keepdims=True))
    p = jnp.exp(s - _lanes(m_new, w))
    a = jnp.exp(m_prev - m_new)
    l_s[h] = a * l_s[h] + jnp.sum(p, axis=1, keepdims=True)
    acc_s[h] = acc_s[h] * _lanes(a, acc_s.shape[2]) + _dot(p.astype(BF16), v)
    m_s[h] = m_new


def _fox_kernel(q_ref, k_ref, v_ref, lf_ref, o_ref, ncum_ref, qm_s, m_s, l_s, acc_s, *, tq, tk, q_off, cblk):
    qi = pl.program_id(1)
    sub = cblk // LANE
    n_pairs = H_FOX // 2

    @pl.when(qi == 0)
    def _():
        r = lax.broadcasted_iota(jnp.int32, (cblk, cblk), 0)
        c = lax.broadcasted_iota(jnp.int32, (cblk, cblk), 1)
        upper = jnp.where(r <= c, 1.0, 0.0).astype(BF16)
        carry = jnp.zeros((lf_ref.shape[1], 1), F32)
        for j in range(lf_ref.shape[2] // cblk):
            g1, g2, g3 = _split3(lf_ref[0, :, j * cblk:(j + 1) * cblk])
            cum = _dot(g1, upper) + _dot(g2, upper) + _dot(g3, upper) + carry
            carry = cum[:, cblk - 1:cblk]
            for h in range(H_FOX):
                for u in range(sub):
                    ncum_ref[h, j * sub + u] = -cum[h:h + 1, u * LANE:(u + 1) * LANE]

    lane = lax.broadcasted_iota(jnp.int32, (tq, LANE), 1)
    for hp in range(n_pairs):
        q = q_ref[0, :, hp * LANE:(hp + 1) * LANE] * (DH_FOX ** -0.5)
        zero = jnp.zeros_like(q)
        qm_s[2 * hp] = jnp.where(lane < DH_FOX, q, zero)
        qm_s[2 * hp + 1] = jnp.where(lane >= DH_FOX, q, zero)
    m_s[...] = jnp.full(m_s.shape, NEG_BIG, F32)
    l_s[...] = jnp.zeros(l_s.shape, F32)
    acc_s[...] = jnp.zeros(acc_s.shape, F32)
    q_start = q_off + qi * tq

    def update(off, w, bias_of_head, mask):
        for hp in range(n_pairs):
            kt = k_ref[0, pl.ds(off, w), hp * LANE:(hp + 1) * LANE]
            vt = v_ref[0, pl.ds(off, w), hp * LANE:(hp + 1) * LANE]
            for h in (2 * hp, 2 * hp + 1):
                s = _dot_nt(qm_s[h], kt) + bias_of_head(h)
                if mask is not None:
                    s = jnp.where(mask, s, -jnp.inf)
                _softmax_step(s, h, m_s, l_s, acc_s, vt)

    def full_tile(j, carry):
        def bias(h):
            return jnp.concatenate([ncum_ref[h, j * (tk // LANE) + u] for u in range(tk // LANE)], axis=1)
        update(pl.multiple_of(j * tk, tk), tk, bias, None)
        return carry

    lax.fori_loop(0, q_start // tk, full_tile, 0)

    def diag_bias(h):
        if tq >= LANE:
            return jnp.concatenate([ncum_ref[h, q_start // LANE + u] for u in range(tq // LANE)], axis=1)
        lo = q_off % LANE
        return ncum_ref[h, q_start // LANE][:, lo:lo + tq]

    r = lax.broadcasted_iota(jnp.int32, (tq, tq), 0)
    c = lax.broadcasted_iota(jnp.int32, (tq, tq), 1)
    update(pl.multiple_of(q_start, tq), tq, diag_bias, c <= r)
    for hp in range(n_pairs):
        o = jnp.where(lane < DH_FOX, acc_s[2 * hp] / l_s[2 * hp], acc_s[2 * hp + 1] / l_s[2 * hp + 1])
        o_ref[0, :, hp * LANE:(hp + 1) * LANE] = o.astype(o_ref.dtype)


def fox_attention(q, k, v, logf_rows, *, tq, tk=256, cblk=256):
    b, lq, _ = q.shape
    lk = k.shape[1]
    lkp = logf_rows.shape[2]
    q_off = lk - lq
    nq = lq // tq
    assert q_off % tk == 0 and (tq % tk == 0 or lq == tq) and lkp % cblk == 0
    assert tq >= LANE or (nq == 1 and q_off % LANE + tq <= LANE)
    kern = functools.partial(_fox_kernel, tq=tq, tk=tk, q_off=q_off, cblk=cblk)
    return pl.pallas_call(
        kern,
        grid=(b, nq),
        in_specs=[
            pl.BlockSpec((1, tq, W_FOX), lambda bi, qi: (bi, qi, 0)),
            pl.BlockSpec((1, lk, W_FOX), lambda bi, qi: (bi, 0, 0)),
            pl.BlockSpec((1, lk, W_FOX), lambda bi, qi: (bi, 0, 0)),
            pl.BlockSpec((1, logf_rows.shape[1], lkp), lambda bi, qi: (bi, 0, 0)),
        ],
        out_specs=pl.BlockSpec((1, tq, W_FOX), lambda bi, qi: (bi, qi, 0)),
        out_shape=jax.ShapeDtypeStruct((b, lq, W_FOX), BF16),
        scratch_shapes=[pltpu.VMEM((H_FOX, lkp // LANE, 1, LANE), F32), pltpu.VMEM((H_FOX, tq, LANE), BF16),
                        pltpu.VMEM((H_FOX, tq, LANE), F32), pltpu.VMEM((H_FOX, tq, LANE), F32),
                        pltpu.VMEM((H_FOX, tq, LANE), F32)],
        compiler_params=_params("parallel", "arbitrary"),
        name="fox_attention",
    )(q, k, v, logf_rows)


def _mlstm_kernel(q_ref, k_ref, v_ref, g_ref, c0_ref, n0_ref, m0_ref,
                  h_ref, c_ref, n_ref, m_ref, c_s, n_s, m_s, *, chunk):
    ci = pl.program_id(1)
    L = chunk
    scale = DH_ML ** -0.5

    @pl.when(ci == 0)
    def _():
        c_s[...] = c0_ref[0]
        n_s[...] = n0_ref[0]
        m_s[...] = m0_ref[0]

    r = lax.broadcasted_iota(jnp.int32, (L, L), 0)
    c = lax.broadcasted_iota(jnp.int32, (L, L), 1)
    eye = r == c
    causal = c <= r
    upper = jnp.where(r <= c, 1.0, 0.0).astype(BF16)
    g = g_ref[0]
    g1, g2, g3 = _split3(g)
    cum = _dot(g1, upper) + _dot(g2, upper) + _dot(g3, upper)

    def to_col(row):
        return jnp.sum(jnp.where(eye, row, 0.0), axis=1, keepdims=True)

    for h in range(H_ML):
        sl = slice(h * DH_ML, (h + 1) * DH_ML)
        qh, kh, vh = q_ref[0, :, sl], k_ref[0, :, sl], v_ref[0, :, sl]
        ig_row = g[h:h + 1, :]
        bh_row = cum[H_ML + h:H_ML + h + 1, :]
        bh_col, ig_col = to_col(bh_row), to_col(ig_row)
        m0 = m_s[h]
        c0 = c_s[h]
        n0 = n_s[h]
        logd = jnp.where(causal, bh_col - bh_row + ig_row, -jnp.inf)
        inter = bh_col + m0
        m_col = jnp.maximum(inter, jnp.max(logd, axis=1, keepdims=True))
        d = jnp.exp(logd - m_col)
        a_col = jnp.exp(inter - m_col)
        s = _dot_nt(qh, kh) * scale * d
        num = _dot(s.astype(BF16), vh) + _dot_nt(qh, c0.astype(BF16)) * a_col
        qf = qh.astype(F32)
        den = jnp.sum(s, axis=1, keepdims=True) + a_col * jnp.sum(qf * n0, axis=1, keepdims=True)
        den = jnp.maximum(jnp.abs(den), jnp.exp(-m_col))
        h_ref[0, :, sl] = (num / den).astype(h_ref.dtype)

        m_l = m_col[L - 1:L, :]
        bh_l = bh_row[:, L - 1:L]
        a_l = jnp.exp(bh_l + m0 - m_l)
        w_col = jnp.exp(bh_l - bh_col + ig_col - m_l)
        vw_t = jnp.transpose(vh.astype(F32) * w_col).astype(BF16)
        c_s[h] = a_l * c0 + _dot(vw_t, kh) * scale
        n_s[h] = a_l * n0 + jnp.sum(kh.astype(F32) * w_col, axis=0, keepdims=True) * scale
        m_s[h] = m_l

    @pl.when(ci == pl.num_programs(1) - 1)
    def _():
        c_ref[0] = c_s[...]
        n_ref[0] = n_s[...]
        m_ref[0] = m_s[...]


def mlstm(q, k, v, gate_rows, c0, n0, m0, *, chunk):
    b, s, _ = q.shape
    nc = s // chunk
    seq = pl.BlockSpec((1, chunk, W_ML), lambda bi, ci: (bi, ci, 0))
    c_spec = pl.BlockSpec((1, H_ML, DH_ML, DH_ML), lambda bi, ci: (bi, 0, 0, 0))
    n_spec = pl.BlockSpec((1, H_ML, 1, DH_ML), lambda bi, ci: (bi, 0, 0, 0))
    m_spec = pl.BlockSpec((1, H_ML, 1, 1), lambda bi, ci: (bi, 0, 0, 0))
    return pl.pallas_call(
        functools.partial(_mlstm_kernel, chunk=chunk),
        grid=(b, nc),
        in_specs=[seq, seq, seq, pl.BlockSpec((1, gate_rows.shape[1], chunk), lambda bi, ci: (bi, 0, ci)),
                  c_spec, n_spec, m_spec],
        out_specs=[seq, c_spec, n_spec, m_spec],
        out_shape=[jax.ShapeDtypeStruct((b, s, W_ML), BF16),
                   jax.ShapeDtypeStruct(c0.shape, F32), jax.ShapeDtypeStruct(n0.shape, F32),
                   jax.ShapeDtypeStruct(m0.shape, F32)],
        scratch_shapes=[pltpu.VMEM((H_ML, DH_ML, DH_ML), F32), pltpu.VMEM((H_ML, 1, DH_ML), F32),
                        pltpu.VMEM((H_ML, 1, 1), F32)],
        compiler_params=_params("parallel", "arbitrary"),
        name="mlstm",
    )(q, k, v, gate_rows, c0, n0, m0)


def _ab_out_kernel(fo_ref, mh_ref, mo_ref, x_ref, gml_ref, w_ref, g_ref, b_ref, o_ref, obf_ref):
    parts = [fo_ref[...]]
    for h in range(H_ML):
        sl = slice(h * DH_ML, (h + 1) * DH_ML)
        mh = mh_ref[:, sl].astype(F32)
        ms = jnp.mean(mh * mh, axis=1, keepdims=True)
        nm = mh * lax.rsqrt(ms + RMS_EPS) * gml_ref[:, sl]
        parts.append((nm * _sigmoid(mo_ref[:, sl].astype(F32))).astype(BF16))
    cat = jnp.concatenate(parts, axis=1)
    y = ALPHA * x_ref[...] + _dot(cat, w_ref[...])
    out = _layer_norm(y, g_ref[...], b_ref[...])
    o_ref[...] = out
    obf_ref[...] = out.astype(BF16)


def _rows(tm, n):
    return pl.BlockSpec((tm, n), lambda i: (i, 0))


def _whole(shape):
    return pl.BlockSpec(shape, lambda i: (0,) * len(shape))


def ab_out(fo, mh, mo, x, g_ml, w_out, ln_g, ln_b):
    t, d = x.shape
    tm = _row_tile(t)
    return pl.pallas_call(
        _ab_out_kernel,
        grid=(t // tm,),
        in_specs=[_rows(tm, W_FOX), _rows(tm, W_ML), _rows(tm, W_ML), _rows(tm, d), _whole((1, W_ML)),
                  _whole(w_out.shape), _whole((1, d)), _whole((1, d))],
        out_specs=[_rows(tm, d), _rows(tm, d)],
        out_shape=[jax.ShapeDtypeStruct((t, d), F32), jax.ShapeDtypeStruct((t, d), BF16)],
        compiler_params=_params("parallel"),
        name="ab_out",
    )(fo, mh, mo, x, g_ml, w_out, ln_g, ln_b)


def _rms(z, g):
    return z * lax.rsqrt(jnp.mean(z * z, axis=1, keepdims=True) + RMS_EPS) * g


def _mla_down_kernel(x_ref, w_ref, gq_ref, gkv_ref, cc_ref, ss_ref, cq_ref, ckv_ref, kr_ref, ckvb_ref, kr2_ref):
    xb = x_ref[...].astype(BF16)
    cq_ref[...] = _rms(_dot(xb, w_ref[:, 0:Q_RANK]), gq_ref[...]).astype(BF16)
    ckv = _rms(_dot(xb, w_ref[:, Q_RANK:Q_RANK + KV_RANK]), gkv_ref[...])
    ckv_ref[...] = ckv
    ckvb_ref[...] = ckv.astype(BF16)
    c0 = Q_RANK + KV_RANK
    rope = (_dot(xb, w_ref[:, c0:c0 + LANE]) * cc_ref[...] + _dot(xb, w_ref[:, c0 + LANE:c0 + 2 * LANE]) * ss_ref[...])
    kr_ref[...] = rope[:, :ROPE]
    kr2_ref[...] = rope.astype(BF16)


def _table_spec(table, tm, s):
    if table.shape[0] == tm:
        return pl.BlockSpec((tm, LANE), lambda i: (0, 0))
    per = s // tm
    return pl.BlockSpec((tm, LANE), lambda i: (i % per, 0))


def mla_down(x, w, g_q, g_kv, cc, ss, s):
    t, d = x.shape
    tm = _row_tile(t)
    tab = _table_spec(cc, tm, s)
    return pl.pallas_call(
        _mla_down_kernel,
        grid=(t // tm,),
        in_specs=[_rows(tm, d), _whole(w.shape), _whole((1, Q_RANK)), _whole((1, KV_RANK)), tab, tab],
        out_specs=[_rows(tm, Q_RANK), _rows(tm, KV_RANK), _rows(tm, ROPE), _rows(tm, KV_RANK), _rows(tm, LANE)],
        out_shape=[jax.ShapeDtypeStruct((t, Q_RANK), BF16), jax.ShapeDtypeStruct((t, KV_RANK), F32),
                   jax.ShapeDtypeStruct((t, ROPE), F32), jax.ShapeDtypeStruct((t, KV_RANK), BF16),
                   jax.ShapeDtypeStruct((t, LANE), BF16)],
        compiler_params=_params("parallel"),
        name="mla_down",
    )(x, w, g_q, g_kv, cc, ss)


def _mla_uq_kernel(cq_ref, w_ref, wuk_ref, cc_ref, ss_ref, qlat_ref, qrope_ref):
    cq = cq_ref[...]
    scale = (NOPE + ROPE) ** -0.5
    for h in range(H_MLA):
        qn = _dot(cq, w_ref[:, h * NOPE:(h + 1) * NOPE]).astype(BF16)
        qlat_ref[:, h * KV_RANK:(h + 1) * KV_RANK] = (_dot(qn, wuk_ref[h]) * scale).astype(BF16)
    cc, ss = cc_ref[...] * scale, ss_ref[...] * scale
    r0 = H_MLA * NOPE
    r1 = r0 + H_MLA * ROPE
    for p in range(H_MLA // 2):
        qr = _dot(cq, w_ref[:, r0 + p * LANE:r0 + (p + 1) * LANE])
        qs = _dot(cq, w_ref[:, r1 + p * LANE:r1 + (p + 1) * LANE])
        qrope_ref[:, p * LANE:(p + 1) * LANE] = (qr * cc + qs * ss).astype(BF16)


def mla_uq(cq, w, wuk_t, cc, ss, s):
    t = cq.shape[0]
    tm = _row_tile(t)
    tab = _table_spec(cc, tm, s)
    return pl.pallas_call(
        _mla_uq_kernel,
        grid=(t // tm,),
        in_specs=[_rows(tm, Q_RANK), _whole(w.shape), _whole(wuk_t.shape), tab, tab],
        out_specs=[_rows(tm, H_MLA * KV_RANK), _rows(tm, H_MLA * ROPE)],
        out_shape=[jax.ShapeDtypeStruct((t, H_MLA * KV_RANK), BF16), jax.ShapeDtypeStruct((t, H_MLA * ROPE), BF16)],
        compiler_params=_params("parallel"),
        name="mla_uq",
    )(cq, w, wuk_t, cc, ss)


def _mla_attn_kernel(ql_ref, qr_ref, kc_ref, kr_ref, o_ref, qr_s, m_s, l_s, acc_s, *, tq, tk, q_off):
    qi = pl.program_id(1)
    lane = lax.broadcasted_iota(jnp.int32, (tq, LANE), 1)
    for h in range(H_MLA):
        qr = qr_ref[0, :, (h // 2) * LANE:(h // 2 + 1) * LANE]
        keep = (lane < ROPE) if h % 2 == 0 else (lane >= ROPE)
        qr_s[h] = jnp.where(keep, qr, jnp.zeros_like(qr))
    m_s[...] = jnp.full(m_s.shape, NEG_BIG, F32)
    l_s[...] = jnp.zeros(l_s.shape, F32)
    acc_s[...] = jnp.zeros(acc_s.shape, F32)
    q_start = q_off + qi * tq

    def update(off, w, mask):
        kc = kc_ref[0, pl.ds(off, w), :]
        kr = kr_ref[0, pl.ds(off, w), :]
        for h in range(H_MLA):
            s = _dot_nt(ql_ref[0, :, h * KV_RANK:(h + 1) * KV_RANK], kc) + _dot_nt(qr_s[h], kr)
            if mask is not None:
                s = jnp.where(mask, s, -jnp.inf)
            _softmax_step(s, h, m_s, l_s, acc_s, kc)

    def full_tile(j, carry):
        update(pl.multiple_of(j * tk, tk), tk, None)
        return carry

    lax.fori_loop(0, q_start // tk, full_tile, 0)
    r = lax.broadcasted_iota(jnp.int32, (tq, tq), 0)
    c = lax.broadcasted_iota(jnp.int32, (tq, tq), 1)
    update(pl.multiple_of(q_start, tq), tq, (c // CHUNK) <= (r // CHUNK))
    for h in range(H_MLA):
        o = acc_s[h] / _lanes(l_s[h], KV_RANK)
        o_ref[0, :, h * KV_RANK:(h + 1) * KV_RANK] = o.astype(o_ref.dtype)


def mla_attention(q_lat, q_rope, kc, kr2, *, tq, tk):
    b, lq, _ = q_lat.shape
    lk = kc.shape[1]
    q_off = lk - lq
    assert q_off % tk == 0 and (tq % tk == 0 or lq == tq) and tq % CHUNK == 0 and (tq & (tq - 1)) == 0
    kern = functools.partial(_mla_attn_kernel, tq=tq, tk=tk, q_off=q_off)
    return pl.pallas_call(
        kern,
        grid=(b, lq // tq),
        in_specs=[
            pl.BlockSpec((1, tq, H_MLA * KV_RANK), lambda bi, qi: (bi, qi, 0)),
            pl.BlockSpec((1, tq, H_MLA * ROPE), lambda bi, qi: (bi, qi, 0)),
            pl.BlockSpec((1, lk, KV_RANK), lambda bi, qi: (bi, 0, 0)),
            pl.BlockSpec((1, lk, LANE), lambda bi, qi: (bi, 0, 0)),
        ],
        out_specs=pl.BlockSpec((1, tq, H_MLA * KV_RANK), lambda bi, qi: (bi, qi, 0)),
        out_shape=jax.ShapeDtypeStruct((b, lq, H_MLA * KV_RANK), BF16),
        scratch_shapes=[pltpu.VMEM((H_MLA, tq, LANE), BF16), pltpu.VMEM((H_MLA, tq, LANE), F32),
                        pltpu.VMEM((H_MLA, tq, LANE), F32), pltpu.VMEM((H_MLA, tq, KV_RANK), F32)],
        compiler_params=_params("parallel", "arbitrary"),
        name="mla_attention",
    )(q_lat, q_rope, kc, kr2)


def _mla_out_kernel(ol_ref, wuv_ref, w_ref, x_ref, g_ref, b_ref, o_ref, obf_ref):
    parts = [_dot(ol_ref[:, h * KV_RANK:(h + 1) * KV_RANK], wuv_ref[h]).astype(BF16) for h in range(H_MLA)]
    y = ALPHA * x_ref[...] + _dot(jnp.concatenate(parts, axis=1), w_ref[...])
    out = _layer_norm(y, g_ref[...], b_ref[...])
    o_ref[...] = out
    obf_ref[...] = out.astype(BF16)


def mla_out(o_lat, wuv, w_out, x, ln_g, ln_b):
    t, d = x.shape
    tm = _row_tile(t)
    return pl.pallas_call(
        _mla_out_kernel,
        grid=(t // tm,),
        in_specs=[_rows(tm, H_MLA * KV_RANK), _whole(wuv.shape), _whole(w_out.shape), _rows(tm, d),
                  _whole((1, d)), _whole((1, d))],
        out_specs=[_rows(tm, d), _rows(tm, d)],
        out_shape=[jax.ShapeDtypeStruct((t, d), F32), jax.ShapeDtypeStruct((t, d), BF16)],
        compiler_params=_params("parallel"),
        name="mla_out",
    )(o_lat, wuv, w_out, x, ln_g, ln_b)


def _moe_kernel(te_ref, nu_ref, xs_ref, gate_ref, w1_ref, w3_ref, w2_ref, o_ref):
    t = pl.program_id(0)

    @pl.when(t < nu_ref[0])
    def _():
        xb = xs_ref[...]
        h1 = _dot(xb, w1_ref[0])
        h3 = _dot(xb, w3_ref[0])
        hidden = (h1 * _sigmoid(h1) * h3).astype(BF16)
        o_ref[...] = (gate_ref[...] * _dot(hidden, w2_ref[0])).astype(o_ref.dtype)

    @pl.when(t >= nu_ref[0])
    def _():
        o_ref[...] = jnp.zeros(o_ref.shape, o_ref.dtype)


def moe_experts(tile_expert, n_used, xs, gate, w1, w3, w2, tm):
    p, d = xs.shape
    de = w1.shape[2]
    grid_spec = pltpu.PrefetchScalarGridSpec(
        num_scalar_prefetch=2,
        grid=(p // tm,),
        in_specs=[
            pl.BlockSpec((tm, d), lambda t, te, nu: (t, 0)),
            pl.BlockSpec((tm, 1), lambda t, te, nu: (t, 0)),
            pl.BlockSpec((1, d, de), lambda t, te, nu: (te[t], 0, 0)),
            pl.BlockSpec((1, d, de), lambda t, te, nu: (te[t], 0, 0)),
            pl.BlockSpec((1, de, d), lambda t, te, nu: (te[t], 0, 0)),
        ],
        out_specs=pl.BlockSpec((tm, d), lambda t, te, nu: (t, 0)),
    )
    return pl.pallas_call(
        _moe_kernel,
        grid_spec=grid_spec,
        out_shape=jax.ShapeDtypeStruct((p, d), BF16),
        compiler_params=_params("arbitrary"),
        name="moe_experts",
    )(tile_expert, n_used, xs, gate, w1, w3, w2)


def _combine_kernel(x_ref, y0_ref, y1_ref, g_ref, b_ref, o_ref):
    y = ALPHA * x_ref[...] + (y0_ref[...].astype(F32) + y1_ref[...].astype(F32))
    o_ref[...] = _layer_norm(y, g_ref[...], b_ref[...])


def combine_ln(x, y0, y1, ln_g, ln_b):
    t, d = x.shape
    tm = _row_tile(t)
    return pl.pallas_call(
        _combine_kernel,
        grid=(t // tm,),
        in_specs=[_rows(tm, d), _rows(tm, d), _rows(tm, d), _whole((1, d)), _whole((1, d))],
        out_specs=_rows(tm, d),
        out_shape=jax.ShapeDtypeStruct((t, d), F32),
        compiler_params=_params("parallel"),
        name="combine_ln",
    )(x, y0, y1, ln_g, ln_b)


def _route(logits, tm):
    t = logits.shape[0]
    g_logit = logits[:, :N_GROUPS]
    g_prob = jax.nn.softmax(g_logit, -1)
    g_idx = jnp.argmax(g_logit, -1)
    g_gate = jnp.take_along_axis(g_prob, g_idx[:, None], -1)
    e_logit = logits[:, N_GROUPS:N_GROUPS + N_EXPERTS].reshape(t, N_GROUPS, EXP_PER_GROUP)
    e_sel = jnp.take_along_axis(e_logit, g_idx[:, None, None], axis=1)[:, 0]
    top_v, top_i = lax.top_k(e_sel, TOP_K)
    top_w = (jax.nn.softmax(top_v, -1) * g_gate).reshape(-1)
    flat_e = (g_idx[:, None] * EXP_PER_GROUP + top_i).astype(jnp.int32).reshape(-1)

    n_assign = flat_e.shape[0]
    n_rows = n_assign + N_EXPERTS * tm
    order = jnp.argsort(flat_e, stable=True).astype(jnp.int32)
    inv = jnp.argsort(order).astype(jnp.int32)
    counts = jnp.sum((flat_e[:, None] == jnp.arange(N_EXPERTS, dtype=jnp.int32)[None, :]).astype(jnp.int32), axis=0)
    padded = ((counts + tm - 1) // tm) * tm
    row_end = jnp.cumsum(padded)
    row_start = row_end - padded
    slot_start = jnp.cumsum(counts) - counts
    pos = row_start[flat_e] + inv - slot_start[flat_e]

    n_tiles = n_rows // tm
    tile_expert = jnp.minimum(jnp.searchsorted(row_end, jnp.arange(n_tiles, dtype=jnp.int32) * tm, side="right"),
                              N_EXPERTS - 1).astype(jnp.int32)
    n_used = (row_end[-1] // tm).astype(jnp.int32).reshape(1)
    rows = jnp.arange(n_rows, dtype=jnp.int32)
    row_e = jnp.repeat(tile_expert, tm)
    local = rows - row_start[row_e]
    valid = jnp.logical_and(local < counts[row_e], rows < row_end[-1])
    assign = order[jnp.clip(slot_start[row_e] + local, 0, n_assign - 1)]
    row_token = jnp.where(valid, assign // TOP_K, 0)
    row_gate = jnp.where(valid, top_w[assign], 0.0)[:, None]
    return tile_expert, n_used, row_token, row_gate, pos.reshape(t, TOP_K)


def hier_moe_ln(x1, x1_bf, moe_w, ln_g, ln_b):
    w_route, b_route, w1, w3, w2 = moe_w
    t = x1.shape[0]
    tm = 512 if t >= 8192 else 64
    logits = gate_proj(x1, w_route, b_route, np.zeros(w_route.shape[1]))
    tile_expert, n_used, row_token, row_gate, pos = _route(logits, tm)
    xs = jnp.take(x1_bf, row_token, axis=0)
    ys = moe_experts(tile_expert, n_used, xs, row_gate, w1, w3, w2, tm)
    y0 = jnp.take(ys, pos[:, 0], axis=0)
    y1 = jnp.take(ys, pos[:, 1], axis=0)
    return combine_ln(x1, y0, y1, ln_g, ln_b)


def _rope_tables(s, offset, tm):
    half = ROPE // 2
    inv_freq = ROPE_BASE ** (-jnp.arange(half, dtype=F32) / half)
    ang = (offset + jnp.arange(s)).astype(F32)[:, None] * inv_freq[None, :]
    cos, sin = jnp.cos(ang), jnp.sin(ang)
    cc = jnp.concatenate([cos, cos, cos, cos], -1)
    ss = jnp.concatenate([-sin, sin, -sin, sin], -1)
    if tm > s:
        cc, ss = jnp.tile(cc, (tm // s, 1)), jnp.tile(ss, (tm // s, 1))
    return cc, ss


def _round_up(n, m):
    return (n + m - 1) // m * m


def ab_layer(x, b, s, wts, cache):
    w_big, w_gate, b_gate, g_ml, w_out, ln_g, ln_b = wts
    t = b * s
    fq, mq, mk, mv, mo, fk, fk_bf, fv, fv_bf = proj(
        x, w_big, [(W_FOX, (BF16,)), (W_ML, (BF16,)), (W_ML, (BF16,)), (W_ML, (BF16,)), (W_ML, (BF16,)),
                   (W_FOX, (F32, BF16)), (W_FOX, (F32, BF16))])
    ls_cols = np.array([1.0] * H_FOX + [0.0] * H_ML + [1.0] * H_ML)
    gates = gate_proj(x, w_gate, b_gate, ls_cols)
    flf = gates[:, :H_FOX].reshape(b, s, H_FOX)
    ml_rows = jnp.swapaxes(gates[:, H_FOX:H_FOX + 2 * H_ML].reshape(b, s, 2 * H_ML), 1, 2)
    ml_rows = jnp.pad(ml_rows, ((0, 0), (0, 16 - 2 * H_ML), (0, 0)))
    fk3, fv3 = fk.reshape(b, s, W_FOX), fv.reshape(b, s, W_FOX)
    if cache is None:
        k_all, v_all, lf_all = fk_bf.reshape(b, s, W_FOX), fv_bf.reshape(b, s, W_FOX), flf
        c0 = jnp.zeros((b, H_ML, DH_ML, DH_ML), F32)
        n0 = jnp.zeros((b, H_ML, 1, DH_ML), F32)
        m0 = jnp.zeros((b, H_ML, 1, 1), F32)
        tq, chunk = 256, 256
    else:
        ck, cv, clf, c0, n0, m0 = cache
        past = ck.shape[1]
        k_all = jnp.concatenate([ck.reshape(b, past, W_FOX).astype(BF16), fk_bf.reshape(b, s, W_FOX)], 1)
        v_all = jnp.concatenate([cv.reshape(b, past, W_FOX).astype(BF16), fv_bf.reshape(b, s, W_FOX)], 1)
        lf_all = jnp.concatenate([clf, flf], 1)
        n0 = n0.reshape(b, H_ML, 1, DH_ML)
        m0 = m0.reshape(b, H_ML, 1, 1)
        tq, chunk = s, s
    lk = k_all.shape[1]
    lf_rows = jnp.pad(jnp.swapaxes(lf_all, 1, 2), ((0, 0), (0, 16 - H_FOX), (0, _round_up(lk, 256) - lk)))
    fo = fox_attention(fq.reshape(b, s, W_FOX), k_all, v_all, lf_rows, tq=tq)
    mh, c_new, n_new, m_new = mlstm(mq.reshape(b, s, W_ML), mk.reshape(b, s, W_ML), mv.reshape(b, s, W_ML),
                                    ml_rows, c0, n0, m0, chunk=chunk)
    x1, x1_bf = ab_out(fo.reshape(t, W_FOX), mh.reshape(t, W_ML), mo, x, g_ml, w_out, ln_g, ln_b)
    state = (fk3.reshape(b, s, H_FOX, DH_FOX), fv3.reshape(b, s, H_FOX, DH_FOX), flf,
             c_new, n_new.reshape(b, H_ML, DH_ML), m_new.reshape(b, H_ML))
    return x1, x1_bf, state


def mla_layer(x, b, s, wts, cache):
    w_down, g_q, g_kv, w_uq, wuk_t, wuv, w_out, ln_g, ln_b = wts
    t = b * s
    tm = _row_tile(t)
    past = 0 if cache is None else cache[0].shape[1]
    cc, ss = _rope_tables(s, past, tm)
    cq, ckv, kr, ckv_bf, kr2_bf = mla_down(x, w_down, g_q, g_kv, cc, ss, s)
    q_lat, q_rope = mla_uq(cq, w_uq, wuk_t, cc, ss, s)
    kc = ckv_bf.reshape(b, s, KV_RANK)
    kr2 = kr2_bf.reshape(b, s, LANE)
    if cache is None:
        tq = 256
    else:
        c_ckv, c_kr = cache
        kc = jnp.concatenate([c_ckv.astype(BF16), kc], 1)
        c_kr_bf = c_kr.astype(BF16)
        kr2 = jnp.concatenate([jnp.concatenate([c_kr_bf, c_kr_bf], -1), kr2], 1)
        tq = s
    o_lat = mla_attention(q_lat.reshape(b, s, -1), q_rope.reshape(b, s, -1), kc, kr2, tq=tq,
                          tk=tq if cache is None else 256)
    x1, x1_bf = mla_out(o_lat.reshape(t, -1), wuv, w_out, x, ln_g, ln_b)
    return x1, x1_bf, (ckv.reshape(b, s, KV_RANK), kr.reshape(b, s, ROPE))


def _prep_ab_weights(w_in, b_fox_f, b_ml_i, b_ml_f, g_ml, w_out, ln_g, ln_b):
    sizes = (W_FOX, W_FOX, W_FOX, H_FOX, W_ML, W_ML, W_ML, H_ML, H_ML, W_ML)
    idx = np.cumsum(sizes[:-1]).tolist()
    fq, fk, fv, ff, mq, mk, mv, mi, mf, mo = jnp.split(w_in, idx, axis=1)
    w_big = jnp.concatenate([fq, mq, mk, mv, mo, fk, fv], 1).astype(BF16)
    w_gate = jnp.concatenate([ff, mi, mf], 1)
    b_gate = jnp.concatenate([b_fox_f, b_ml_i, b_ml_f])
    return (w_big, w_gate, b_gate, g_ml[None, :], w_out.astype(BF16), ln_g[None, :], ln_b[None, :])


def _prep_mla_weights(w_down, g_q, w_uq, g_kv, w_uk, w_uv, w_out, ln_g, ln_b):
    half = ROPE // 2
    cq_w, ckv_w, kr_w = jnp.split(w_down, [Q_RANK, Q_RANK + KV_RANK], axis=1)
    kr_sw = jnp.concatenate([kr_w[:, half:], kr_w[:, :half]], 1)
    w_down_p = jnp.concatenate([cq_w, ckv_w, kr_w, kr_w, kr_sw, kr_sw], 1).astype(BF16)
    uq = w_uq.reshape(Q_RANK, H_MLA, NOPE + ROPE)
    uq_nope = uq[:, :, :NOPE].reshape(Q_RANK, H_MLA * NOPE)
    uq_rope = uq[:, :, NOPE:]
    uq_rope_sw = jnp.concatenate([uq_rope[..., half:], uq_rope[..., :half]], -1)
    w_uq_p = jnp.concatenate([uq_nope, uq_rope.reshape(Q_RANK, -1), uq_rope_sw.reshape(Q_RANK, -1)], 1).astype(BF16)
    wuk_t = jnp.transpose(w_uk, (1, 2, 0)).astype(BF16)
    wuv = jnp.transpose(w_uv, (1, 0, 2)).astype(BF16)
    return (w_down_p, g_q[None, :], g_kv[None, :], w_uq_p, wuk_t, wuv, w_out.astype(BF16), ln_g[None, :], ln_b[None, :])


def kernel(x_prompt, x_sample, cache_fox_k, cache_fox_v, cache_fox_logf, state_mlstm_c, state_mlstm_n, state_mlstm_m, cache_mla_ckv, cache_mla_krope, w_ab_in, b_fox_f, b_mlstm_i, b_mlstm_f, g_mlstm_norm, w_ab_out, w_mla_down, g_mla_q, w_mla_uq, g_mla_kv, w_mla_uk, w_mla_uv, w_mla_out, ln1_g, ln1_b, ln2_g, ln2_b, w_moe_group, b_moe_group, w_moe_router, b_moe_router, w_exp_gate, w_exp_up, w_exp_down):
    bp, sp, d = x_prompt.shape
    bs, ss_, _ = x_sample.shape
    xp = x_prompt.reshape(bp * sp, d)
    xs = x_sample.reshape(bs * ss_, d)
    ab_p, ab_s, c_p, c_s = [], [], [], []
    for l in range(DEPTH):
        j = l // 2
        if l % 2 == 0:
            wts = _prep_ab_weights(w_ab_in[j], b_fox_f[j], b_mlstm_i[j], b_mlstm_f[j], g_mlstm_norm[j],
                                   w_ab_out[j], ln1_g[l], ln1_b[l])
            xp1, xp1_bf, st_p = ab_layer(xp, bp, sp, wts, None)
            xs1, xs1_bf, st_s = ab_layer(xs, bs, ss_, wts, (cache_fox_k[j], cache_fox_v[j], cache_fox_logf[j],
                                                            state_mlstm_c[j], state_mlstm_n[j], state_mlstm_m[j]))
            ab_p.append(st_p)
            ab_s.append(st_s)
        else:
            wts = _prep_mla_weights(w_mla_down[j], g_mla_q[j], w_mla_uq[j], g_mla_kv[j], w_mla_uk[j], w_mla_uv[j],
                                    w_mla_out[j], ln1_g[l], ln1_b[l])
            xp1, xp1_bf, st_p = mla_layer(xp, bp, sp, wts, None)
            xs1, xs1_bf, st_s = mla_layer(xs, bs, ss_, wts, (cache_mla_ckv[j], cache_mla_krope[j]))
            c_p.append(st_p)
            c_s.append(st_s)
        moe_w = (jnp.concatenate([w_moe_group[l], w_moe_router[l]], 1),
                 jnp.concatenate([b_moe_group[l], b_moe_router[l]]),
                 w_exp_gate[l].astype(BF16), w_exp_up[l].astype(BF16), w_exp_down[l].astype(BF16))
        xp = hier_moe_ln(xp1, xp1_bf, moe_w, ln2_g[l][None, :], ln2_b[l][None, :])
        xs = hier_moe_ln(xs1, xs1_bf, moe_w, ln2_g[l][None, :], ln2_b[l][None, :])

    def stack(groups, i):
        return jnp.stack([g[i] for g in groups])

    return (xp.reshape(bp, sp, d), xs.reshape(bs, ss_, d),
            stack(ab_p, 0), stack(ab_p, 1), stack(ab_p, 2), stack(ab_p, 3), stack(ab_p, 4), stack(ab_p, 5),
            stack(c_p, 0), stack(c_p, 1),
            stack(ab_s, 0), stack(ab_s, 1), stack(ab_s, 2), stack(ab_s, 3), stack(ab_s, 4), stack(ab_s, 5),
            stack(c_s, 0), stack(c_s, 1))
```

```python
import functools

import numpy as np
import jax
import jax.numpy as jnp
from jax import lax
from jax.experimental import pallas as pl
from jax.experimental.pallas import tpu as pltpu

F32 = jnp.float32
BF16 = jnp.bfloat16

LANE = 128
VMEM_LIMIT_BYTES = 48 * 1024 * 1024

H_FOX, DH_FOX = 8, 64
W_FOX = H_FOX * DH_FOX
H_ML, DH_ML = 4, 128
W_ML = H_ML * DH_ML
H_MLA, NOPE, ROPE, DV_MLA = 8, 128, 64, 128
Q_RANK, KV_RANK = 384, 256
ROPE_BASE = 10000.0
N_GROUPS, EXP_PER_GROUP, TOP_K = 4, 8, 2
N_EXPERTS = N_GROUPS * EXP_PER_GROUP
DEPTH = 4
ALPHA = (2 * DEPTH) ** 0.25
LN_EPS = 1e-5
RMS_EPS = 1e-6
CHUNK = 64
NEG_BIG = -1e30


def _params(*sem):
    return pltpu.CompilerParams(dimension_semantics=sem, vmem_limit_bytes=VMEM_LIMIT_BYTES)


def _dot(a, b):
    return jnp.dot(a, b, preferred_element_type=F32)


def _dot_nt(a, b):
    return lax.dot_general(a, b, (((1,), (1,)), ((), ())), preferred_element_type=F32)


def _split3(x):
    hi = x.astype(BF16)
    r = x - hi.astype(F32)
    mid = r.astype(BF16)
    lo = (r - mid.astype(F32)).astype(BF16)
    return hi, mid, lo


def _layer_norm(y, g, b):
    mu = jnp.mean(y, axis=1, keepdims=True)
    yc = y - mu
    var = jnp.mean(yc * yc, axis=1, keepdims=True)
    return yc * lax.rsqrt(var + LN_EPS) * g + b


def _sigmoid(x):
    return 1.0 / (1.0 + jnp.exp(-x))


def _row_tile(t):
    return min(512, t)


def _proj_kernel(x_ref, w_ref, *o_refs, groups):
    xb = x_ref[...].astype(BF16)
    refs = iter(o_refs)
    c0 = 0
    for n, dtypes in groups:
        group_refs = [next(refs) for _ in dtypes]
        for a in range(0, n, 512):
            b = min(a + 512, n)
            z = _dot(xb, w_ref[:, c0 + a:c0 + b])
            for o_ref in group_refs:
                o_ref[:, a:b] = z.astype(o_ref.dtype)
        c0 += n


def proj(x, w, groups):
    t, k = x.shape
    tm = _row_tile(t)
    flat = [(n, dt) for n, dts in groups for dt in dts]
    return pl.pallas_call(
        functools.partial(_proj_kernel, groups=tuple(groups)),
        grid=(t // tm,),
        in_specs=[pl.BlockSpec((tm, k), lambda i: (i, 0)), pl.BlockSpec(w.shape, lambda i: (0, 0))],
        out_specs=[pl.BlockSpec((tm, n), lambda i: (i, 0)) for n, _ in flat],
        out_shape=[jax.ShapeDtypeStruct((t, n), dt) for n, dt in flat],
        compiler_params=_params("parallel"),
        name="proj",
    )(x, w)


def _gate_kernel(x_ref, wh_ref, wl_ref, b_ref, mask_ref, o_ref):
    x = x_ref[...]
    xh = x.astype(BF16)
    xl = (x - xh.astype(F32)).astype(BF16)
    z = _dot(xh, wh_ref[...]) + _dot(xh, wl_ref[...]) + _dot(xl, wh_ref[...]) + b_ref[...]
    log_sig = jnp.minimum(z, 0.0) - jnp.log1p(jnp.exp(-jnp.abs(z)))
    o_ref[...] = jnp.where(mask_ref[...] > 0.0, log_sig, z)


def gate_proj(x, w, bias, log_sigmoid_cols):
    t, k = x.shape
    n = w.shape[1]
    tm = _row_tile(t)
    wp = jnp.zeros((k, LANE), F32).at[:, :n].set(w)
    wh = wp.astype(BF16)
    wl = (wp - wh.astype(F32)).astype(BF16)
    bp = jnp.zeros((1, LANE), F32).at[0, :n].set(bias)
    mask = jnp.zeros((1, LANE), F32).at[0, :n].set(jnp.asarray(log_sigmoid_cols, F32))
    row = pl.BlockSpec((1, LANE), lambda i: (0, 0))
    wspec = pl.BlockSpec((k, LANE), lambda i: (0, 0))
    return pl.pallas_call(
        _gate_kernel,
        grid=(t // tm,),
        in_specs=[pl.BlockSpec((tm, k), lambda i: (i, 0)), wspec, wspec, row, row],
        out_specs=pl.BlockSpec((tm, LANE), lambda i: (i, 0)),
        out_shape=jax.ShapeDtypeStruct((t, LANE), F32),
        compiler_params=_params("parallel"),
        name="gate_proj",
    )(x, wh, wl, bp, mask)


def _lanes(x, w):
    if w < LANE:
        return x[:, :w]
    return x if w == LANE else jnp.concatenate([x] * (w // LANE), axis=1)


def _softmax_step(s, h, m_s, l_s, acc_s, v):
    w = s.shape[1]
    m_prev = m_s[h]
    m_new = jnp.maximum(m_prev, jnp.max(s, axis=1, keepdims=True))
    p = jnp.exp(s - _lanes(m_new, w))
    a = jnp.exp(m_prev - m_new)
    l_s[h] = a * l_s[h] + jnp.sum(p, axis=1, keepdims=True)
    acc_s[h] = acc_s[h] * _lanes(a, acc_s.shape[2]) + _dot(p.astype(BF16), v)
    m_s[h] = m_new


def _fox_kernel(q_ref, k_ref, v_ref, lf_ref, o_ref, ncum_ref, qm_s, m_s, l_s, acc_s, *, tq, tk, q_off, cblk):
    qi = pl.program_id(1)
    sub = cblk // LANE
    n_pairs = H_FOX // 2

    @pl.when(qi == 0)
    def _():
        r = lax.broadcasted_iota(jnp.int32, (cblk, cblk), 0)
        c = lax.broadcasted_iota(jnp.int32, (cblk, cblk), 1)
        upper = jnp.where(r <= c, 1.0, 0.0).astype(BF16)
        carry = jnp.zeros((lf_ref.shape[1], 1), F32)
        for j in range(lf_ref.shape[2] // cblk):
            g1, g2, g3 = _split3(lf_ref[0, :, j * cblk:(j + 1) * cblk])
            cum = _dot(g1, upper) + _dot(g2, upper) + _dot(g3, upper) + carry
            carry = cum[:, cblk - 1:cblk]
            for h in range(H_FOX):
                for u in range(sub):
                    ncum_ref[h, j * sub + u] = -cum[h:h + 1, u * LANE:(u + 1) * LANE]

    lane = lax.broadcasted_iota(jnp.int32, (tq, LANE), 1)
    for hp in range(n_pairs):
        q = q_ref[0, :, hp * LANE:(hp + 1) * LANE] * (DH_FOX ** -0.5)
        zero = jnp.zeros_like(q)
        qm_s[2 * hp] = jnp.where(lane < DH_FOX, q, zero)
        qm_s[2 * hp + 1] = jnp.where(lane >= DH_FOX, q, zero)
    m_s[...] = jnp.full(m_s.shape, NEG_BIG, F32)
    l_s[...] = jnp.zeros(l_s.shape, F32)
    acc_s[...] = jnp.zeros(acc_s.shape, F32)
    q_start = q_off + qi * tq

    def update(off, w, bias_of_head, mask):
        for hp in range(n_pairs):
            kt = k_ref[0, pl.ds(off, w), hp * LANE:(hp + 1) * LANE]
            vt = v_ref[0, pl.ds(off, w), hp * LANE:(hp + 1) * LANE]
            for h in (2 * hp, 2 * hp + 1):
                s = _dot_nt(qm_s[h], kt) + bias_of_head(h)
                if mask is not None:
                    s = jnp.where(mask, s, -jnp.inf)
                _softmax_step(s, h, m_s, l_s, acc_s, vt)

    def full_tile(j, carry):
        def bias(h):
            return jnp.concatenate([ncum_ref[h, j * (tk // LANE) + u] for u in range(tk // LANE)], axis=1)
        update(pl.multiple_of(j * tk, tk), tk, bias, None)
        return carry

    lax.fori_loop(0, q_start // tk, full_tile, 0)

    def diag_bias(h):
        if tq >= LANE:
            return jnp.concatenate([ncum_ref[h, q_start // LANE + u] for u in range(tq // LANE)], axis=1)
        lo = q_off % LANE
        return ncum_ref[h, q_start // LANE][:, lo:lo + tq]

    r = lax.broadcasted_iota(jnp.int32, (tq, tq), 0)
    c = lax.broadcasted_iota(jnp.int32, (tq, tq), 1)
    update(pl.multiple_of(q_start, tq), tq, diag_bias, c <= r)
    for hp in range(n_pairs):
        o = jnp.where(lane < DH_FOX, acc_s[2 * hp] / l_s[2 * hp], acc_s[2 * hp + 1] / l_s[2 * hp + 1])
        o_ref[0, :, hp * LANE:(hp + 1) * LANE] = o.astype(o_ref.dtype)


def fox_attention(q, k, v, logf_rows, *, tq, tk=256, cblk=256):
    b, lq, _ = q.shape
    lk = k.shape[1]
    lkp = logf_rows.shape[2]
    q_off = lk - lq
    nq = lq // tq
    assert q_off % tk == 0 and (tq % tk == 0 or lq == tq) and lkp % cblk == 0
    assert tq >= LANE or (nq == 1 and q_off % LANE + tq <= LANE)
    kern = functools.partial(_fox_kernel, tq=tq, tk=tk, q_off=q_off, cblk=cblk)
    return pl.pallas_call(
        kern,
        grid=(b, nq),
        in_specs=[
            pl.BlockSpec((1, tq, W_FOX), lambda bi, qi: (bi, qi, 0)),
            pl.BlockSpec((1, lk, W_FOX), lambda bi, qi: (bi, 0, 0)),
            pl.BlockSpec((1, lk, W_FOX), lambda bi, qi: (bi, 0, 0)),
            pl.BlockSpec((1, logf_rows.shape[1], lkp), lambda bi, qi: (bi, 0, 0)),
        ],
        out_specs=pl.BlockSpec((1, tq, W_FOX), lambda bi, qi: (bi, qi, 0)),
        out_shape=jax.ShapeDtypeStruct((b, lq, W_FOX), BF16),
        scratch_shapes=[pltpu.VMEM((H_FOX, lkp // LANE, 1, LANE), F32), pltpu.VMEM((H_FOX, tq, LANE), BF16),
                        pltpu.VMEM((H_FOX, tq, LANE), F32), pltpu.VMEM((H_FOX, tq, LANE), F32),
                        pltpu.VMEM((H_FOX, tq, LANE), F32)],
        compiler_params=_params("parallel", "arbitrary"),
        name="fox_attention",
    )(q, k, v, logf_rows)


def _mlstm_kernel(q_ref, k_ref, v_ref, g_ref, c0_ref, n0_ref, m0_ref,
                  h_ref, c_ref, n_ref, m_ref, c_s, n_s, m_s, *, chunk):
    ci = pl.program_id(1)
    L = chunk
    scale = DH_ML ** -0.5

    @pl.when(ci == 0)
    def _():
        c_s[...] = c0_ref[0]
        n_s[...] = n0_ref[0]
        m_s[...] = m0_ref[0]

    r = lax.broadcasted_iota(jnp.int32, (L, L), 0)
    c = lax.broadcasted_iota(jnp.int32, (L, L), 1)
    eye = r == c
    causal = c <= r
    upper = jnp.where(r <= c, 1.0, 0.0).astype(BF16)
    g = g_ref[0]
    g1, g2, g3 = _split3(g)
    cum = _dot(g1, upper) + _dot(g2, upper) + _dot(g3, upper)

    def to_col(row):
        return jnp.sum(jnp.where(eye, row, 0.0), axis=1, keepdims=True)

    for h in range(H_ML):
        sl = slice(h * DH_ML, (h + 1) * DH_ML)
        qh, kh, vh = q_ref[0, :, sl], k_ref[0, :, sl], v_ref[0, :, sl]
        ig_row = g[h:h + 1, :]
        bh_row = cum[H_ML + h:H_ML + h + 1, :]
        bh_col, ig_col = to_col(bh_row), to_col(ig_row)
        m0 = m_s[h]
        c0 = c_s[h]
        n0 = n_s[h]
        logd = jnp.where(causal, bh_col - bh_row + ig_row, -jnp.inf)
        inter = bh_col + m0
        m_col = jnp.maximum(inter, jnp.max(logd, axis=1, keepdims=True))
        d = jnp.exp(logd - m_col)
        a_col = jnp.exp(inter - m_col)
        s = _dot_nt(qh, kh) * scale * d
        num = _dot(s.astype(BF16), vh) + _dot_nt(qh, c0.astype(BF16)) * a_col
        qf = qh.astype(F32)
        den = jnp.sum(s, axis=1, keepdims=True) + a_col * jnp.sum(qf * n0, axis=1, keepdims=True)
        den = jnp.maximum(jnp.abs(den), jnp.exp(-m_col))
        h_ref[0, :, sl] = (num / den).astype(h_ref.dtype)

        m_l = m_col[L - 1:L, :]
        bh_l = bh_row[:, L - 1:L]
        a_l = jnp.exp(bh_l + m0 - m_l)
        w_col = jnp.exp(bh_l - bh_col + ig_col - m_l)
        vw_t = jnp.transpose(vh.astype(F32) * w_col).astype(BF16)
        c_s[h] = a_l * c0 + _dot(vw_t, kh) * scale
        n_s[h] = a_l * n0 + jnp.sum(kh.astype(F32) * w_col, axis=0, keepdims=True) * scale
        m_s[h] = m_l

    @pl.when(ci == pl.num_programs(1) - 1)
    def _():
        c_ref[0] = c_s[...]
        n_ref[0] = n_s[...]
        m_ref[0] = m_s[...]


def mlstm(q, k, v, gate_rows, c0, n0, m0, *, chunk):
    b, s, _ = q.shape
    nc = s // chunk
    seq = pl.BlockSpec((1, chunk, W_ML), lambda bi, ci: (bi, ci, 0))
    c_spec = pl.BlockSpec((1, H_ML, DH_ML, DH_ML), lambda bi, ci: (bi, 0, 0, 0))
    n_spec = pl.BlockSpec((1, H_ML, 1, DH_ML), lambda bi, ci: (bi, 0, 0, 0))
    m_spec = pl.BlockSpec((1, H_ML, 1, 1), lambda bi, ci: (bi, 0, 0, 0))
    return pl.pallas_call(
        functools.partial(_mlstm_kernel, chunk=chunk),
        grid=(b, nc),
        in_specs=[seq, seq, seq, pl.BlockSpec((1, gate_rows.shape[1], chunk), lambda bi, ci: (bi, 0, ci)),
                  c_spec, n_spec, m_spec],
        out_specs=[seq, c_spec, n_spec, m_spec],
        out_shape=[jax.ShapeDtypeStruct((b, s, W_ML), BF16),
                   jax.ShapeDtypeStruct(c0.shape, F32), jax.ShapeDtypeStruct(n0.shape, F32),
                   jax.ShapeDtypeStruct(m0.shape, F32)],
        scratch_shapes=[pltpu.VMEM((H_ML, DH_ML, DH_ML), F32), pltpu.VMEM((H_ML, 1, DH_ML), F32),
                        pltpu.VMEM((H_ML, 1, 1), F32)],
        compiler_params=_params("parallel", "arbitrary"),
        name="mlstm",
    )(q, k, v, gate_rows, c0, n0, m0)


N_PAIRS = EXP_PER_GROUP * (EXP_PER_GROUP - 1) // 2
N_BUCKETS = N_GROUPS * N_PAIRS
assert N_BUCKETS <= LANE
EXT = LANE


def _route_store(out, wrh_ref, wrl_ref, br_ref, ltri_ref, o_ref, cnt_ref, cnt_s):
    tm, d = out.shape

    @pl.when(pl.program_id(0) == 0)
    def _():
        cnt_s[...] = jnp.zeros(cnt_s.shape, F32)

    xh = out.astype(BF16)
    xl = (out - xh.astype(F32)).astype(BF16)
    z = _dot(xh, wrh_ref[...]) + _dot(xh, wrl_ref[...]) + _dot(xl, wrh_ref[...]) + br_ref[...]
    lane = lax.broadcasted_iota(jnp.int32, (tm, LANE), 1).astype(F32)
    far = float(LANE)
    neg = -jnp.inf
    gl = jnp.where(lane < N_GROUPS, z, neg)
    gmax = jnp.max(gl, axis=1, keepdims=True)
    g_idx = jnp.min(jnp.where(gl == gmax, lane, far), axis=1, keepdims=True)
    g_gate = 1.0 / jnp.sum(jnp.exp(gl - gmax), axis=1, keepdims=True)
    lo = N_GROUPS + EXP_PER_GROUP * g_idx
    el = jnp.where(jnp.logical_and(lane >= lo, lane < lo + EXP_PER_GROUP), z, neg)
    v1 = jnp.max(el, axis=1, keepdims=True)
    i1 = jnp.min(jnp.where(el == v1, lane, far), axis=1, keepdims=True)
    el2 = jnp.where(lane == i1, neg, el)
    v2 = jnp.max(el2, axis=1, keepdims=True)
    i2 = jnp.min(jnp.where(el2 == v2, lane, far), axis=1, keepdims=True)
    e21 = jnp.exp(v2 - v1)
    w1 = g_gate / (1.0 + e21)
    w2 = w1 * e21
    j1, j2 = i1 - lo, i2 - lo
    ja, jb = jnp.minimum(j1, j2), jnp.maximum(j1, j2)
    pair = ja * (2 * EXP_PER_GROUP - 1 - ja) * 0.5 + (jb - ja - 1.0)
    bucket = g_idx * N_PAIRS + pair
    first_low = j1 < j2
    wa = jnp.where(first_low, w1, w2)
    wb = jnp.where(first_low, w2, w1)
    onehot = jnp.where(lane == bucket, 1.0, 0.0)
    earlier = _dot(ltri_ref[...], onehot.astype(BF16))
    cnt = cnt_s[...]
    rank = jnp.sum(onehot * (earlier + cnt), axis=1, keepdims=True)
    cnt_new = cnt + jnp.sum(onehot, axis=0, keepdims=True)
    cnt_s[...] = cnt_new
    cnt_ref[...] = cnt_new
    o_ref[:, :d] = out
    o_ref[:, d:] = jnp.where(lane == 0.0, bucket, jnp.where(lane == 1.0, rank, jnp.where(
        lane == 2.0, wa, jnp.where(lane == 3.0, wb, 0.0))))


def _route_operands(w_route, b_route, tm):
    n = w_route.shape[1]
    wp = jnp.zeros((w_route.shape[0], LANE), F32).at[:, :n].set(w_route)
    wh = wp.astype(BF16)
    wl = (wp - wh.astype(F32)).astype(BF16)
    bp = jnp.zeros((1, LANE), F32).at[0, :n].set(b_route)
    ltri = jnp.asarray(np.tril(np.ones((tm, tm), np.float32), -1), BF16)
    return wh, wl, bp, ltri


def _route_specs(k, tm):
    return [_whole((k, LANE)), _whole((k, LANE)), _whole((1, LANE)), _whole((tm, tm))]


def _ab_out_kernel(fo_ref, mh_ref, mo_ref, x_ref, gml_ref, w_ref, g_ref, b_ref, wrh_ref, wrl_ref, br_ref, ltri_ref,
                   o_ref, cnt_ref, cnt_s):
    parts = [fo_ref[...]]
    for h in range(H_ML):
        sl = slice(h * DH_ML, (h + 1) * DH_ML)
        mh = mh_ref[:, sl].astype(F32)
        ms = jnp.mean(mh * mh, axis=1, keepdims=True)
        nm = mh * lax.rsqrt(ms + RMS_EPS) * gml_ref[:, sl]
        parts.append((nm * _sigmoid(mo_ref[:, sl].astype(F32))).astype(BF16))
    cat = jnp.concatenate(parts, axis=1)
    y = ALPHA * x_ref[...] + _dot(cat, w_ref[...])
    out = _layer_norm(y, g_ref[...], b_ref[...])
    _route_store(out, wrh_ref, wrl_ref, br_ref, ltri_ref, o_ref, cnt_ref, cnt_s)


def _rows(tm, n):
    return pl.BlockSpec((tm, n), lambda i: (i, 0))


def _whole(shape):
    return pl.BlockSpec(shape, lambda i: (0,) * len(shape))


def _routed_out_call(kern, name, t, d, tm, in_specs, operands, route_w):
    w_route, b_route = route_w
    return pl.pallas_call(
        kern,
        grid=(t // tm,),
        in_specs=in_specs + _route_specs(d, tm),
        out_specs=[_rows(tm, d + EXT), _whole((1, LANE))],
        out_shape=[jax.ShapeDtypeStruct((t, d + EXT), F32), jax.ShapeDtypeStruct((1, LANE), F32)],
        scratch_shapes=[pltpu.VMEM((1, LANE), F32)],
        compiler_params=_params("arbitrary"),
        name=name,
    )(*operands, *_route_operands(w_route, b_route, tm))


def ab_out(fo, mh, mo, x, g_ml, w_out, ln_g, ln_b, route_w):
    t, d = x.shape
    tm = _row_tile(t)
    in_specs = [_rows(tm, W_FOX), _rows(tm, W_ML), _rows(tm, W_ML), _rows(tm, d), _whole((1, W_ML)),
                _whole(w_out.shape), _whole((1, d)), _whole((1, d))]
    return _routed_out_call(_ab_out_kernel, "ab_out", t, d, tm, in_specs,
                            (fo, mh, mo, x, g_ml, w_out, ln_g, ln_b), route_w)


def _rms(z, g):
    return z * lax.rsqrt(jnp.mean(z * z, axis=1, keepdims=True) + RMS_EPS) * g


def _mla_down_kernel(x_ref, w_ref, gq_ref, gkv_ref, cc_ref, ss_ref, cq_ref, ckv_ref, kr_ref, ckvb_ref, kr2_ref):
    xb = x_ref[...].astype(BF16)
    cq_ref[...] = _rms(_dot(xb, w_ref[:, 0:Q_RANK]), gq_ref[...]).astype(BF16)
    ckv = _rms(_dot(xb, w_ref[:, Q_RANK:Q_RANK + KV_RANK]), gkv_ref[...])
    ckv_ref[...] = ckv
    ckvb_ref[...] = ckv.astype(BF16)
    c0 = Q_RANK + KV_RANK
    rope = (_dot(xb, w_ref[:, c0:c0 + LANE]) * cc_ref[...] + _dot(xb, w_ref[:, c0 + LANE:c0 + 2 * LANE]) * ss_ref[...])
    kr_ref[...] = rope[:, :ROPE]
    kr2_ref[...] = rope.astype(BF16)


def _table_spec(table, tm, s):
    if table.shape[0] == tm:
        return pl.BlockSpec((tm, LANE), lambda i: (0, 0))
    per = s // tm
    return pl.BlockSpec((tm, LANE), lambda i: (i % per, 0))


def mla_down(x, w, g_q, g_kv, cc, ss, s):
    t, d = x.shape
    tm = _row_tile(t)
    tab = _table_spec(cc, tm, s)
    return pl.pallas_call(
        _mla_down_kernel,
        grid=(t // tm,),
        in_specs=[_rows(tm, d), _whole(w.shape), _whole((1, Q_RANK)), _whole((1, KV_RANK)), tab, tab],
        out_specs=[_rows(tm, Q_RANK), _rows(tm, KV_RANK), _rows(tm, ROPE), _rows(tm, KV_RANK), _rows(tm, LANE)],
        out_shape=[jax.ShapeDtypeStruct((t, Q_RANK), BF16), jax.ShapeDtypeStruct((t, KV_RANK), F32),
                   jax.ShapeDtypeStruct((t, ROPE), F32), jax.ShapeDtypeStruct((t, KV_RANK), BF16),
                   jax.ShapeDtypeStruct((t, LANE), BF16)],
        compiler_params=_params("parallel"),
        name="mla_down",
    )(x, w, g_q, g_kv, cc, ss)


def _mla_uq_kernel(cq_ref, w_ref, wuk_ref, cc_ref, ss_ref, qlat_ref, qrope_ref):
    cq = cq_ref[...]
    scale = (NOPE + ROPE) ** -0.5
    for h in range(H_MLA):
        qn = _dot(cq, w_ref[:, h * NOPE:(h + 1) * NOPE]).astype(BF16)
        qlat_ref[:, h * KV_RANK:(h + 1) * KV_RANK] = (_dot(qn, wuk_ref[h]) * scale).astype(BF16)
    cc, ss = cc_ref[...] * scale, ss_ref[...] * scale
    r0 = H_MLA * NOPE
    r1 = r0 + H_MLA * ROPE
    for p in range(H_MLA // 2):
        qr = _dot(cq, w_ref[:, r0 + p * LANE:r0 + (p + 1) * LANE])
        qs = _dot(cq, w_ref[:, r1 + p * LANE:r1 + (p + 1) * LANE])
        qrope_ref[:, p * LANE:(p + 1) * LANE] = (qr * cc + qs * ss).astype(BF16)


def mla_uq(cq, w, wuk_t, cc, ss, s):
    t = cq.shape[0]
    tm = _row_tile(t)
    tab = _table_spec(cc, tm, s)
    return pl.pallas_call(
        _mla_uq_kernel,
        grid=(t // tm,),
        in_specs=[_rows(tm, Q_RANK), _whole(w.shape), _whole(wuk_t.shape), tab, tab],
        out_specs=[_rows(tm, H_MLA * KV_RANK), _rows(tm, H_MLA * ROPE)],
        out_shape=[jax.ShapeDtypeStruct((t, H_MLA * KV_RANK), BF16), jax.ShapeDtypeStruct((t, H_MLA * ROPE), BF16)],
        compiler_params=_params("parallel"),
        name="mla_uq",
    )(cq, w, wuk_t, cc, ss)


def _mla_attn_kernel(ql_ref, qr_ref, kc_ref, kr_ref, o_ref, qr_s, m_s, l_s, acc_s, *, tq, tk, q_off):
    qi = pl.program_id(1)
    lane = lax.broadcasted_iota(jnp.int32, (tq, LANE), 1)
    for h in range(H_MLA):
        qr = qr_ref[0, :, (h // 2) * LANE:(h // 2 + 1) * LANE]
        keep = (lane < ROPE) if h % 2 == 0 else (lane >= ROPE)
        qr_s[h] = jnp.where(keep, qr, jnp.zeros_like(qr))
    m_s[...] = jnp.full(m_s.shape, NEG_BIG, F32)
    l_s[...] = jnp.zeros(l_s.shape, F32)
    acc_s[...] = jnp.zeros(acc_s.shape, F32)
    q_start = q_off + qi * tq

    def update(off, w, mask):
        kc = kc_ref[0, pl.ds(off, w), :]
        kr = kr_ref[0, pl.ds(off, w), :]
        for h in range(H_MLA):
            s = _dot_nt(ql_ref[0, :, h * KV_RANK:(h + 1) * KV_RANK], kc) + _dot_nt(qr_s[h], kr)
            if mask is not None:
                s = jnp.where(mask, s, -jnp.inf)
            _softmax_step(s, h, m_s, l_s, acc_s, kc)

    def full_tile(j, carry):
        update(pl.multiple_of(j * tk, tk), tk, None)
        return carry

    lax.fori_loop(0, q_start // tk, full_tile, 0)
    r = lax.broadcasted_iota(jnp.int32, (tq, tq), 0)
    c = lax.broadcasted_iota(jnp.int32, (tq, tq), 1)
    update(pl.multiple_of(q_start, tq), tq, (c // CHUNK) <= (r // CHUNK))
    for h in range(H_MLA):
        o = acc_s[h] / _lanes(l_s[h], KV_RANK)
        o_ref[0, :, h * KV_RANK:(h + 1) * KV_RANK] = o.astype(o_ref.dtype)


def mla_attention(q_lat, q_rope, kc, kr2, *, tq, tk):
    b, lq, _ = q_lat.shape
    lk = kc.shape[1]
    q_off = lk - lq
    assert q_off % tk == 0 and (tq % tk == 0 or lq == tq) and tq % CHUNK == 0 and (tq & (tq - 1)) == 0
    kern = functools.partial(_mla_attn_kernel, tq=tq, tk=tk, q_off=q_off)
    return pl.pallas_call(
        kern,
        grid=(b, lq // tq),
        in_specs=[
            pl.BlockSpec((1, tq, H_MLA * KV_RANK), lambda bi, qi: (bi, qi, 0)),
            pl.BlockSpec((1, tq, H_MLA * ROPE), lambda bi, qi: (bi, qi, 0)),
            pl.BlockSpec((1, lk, KV_RANK), lambda bi, qi: (bi, 0, 0)),
            pl.BlockSpec((1, lk, LANE), lambda bi, qi: (bi, 0, 0)),
        ],
        out_specs=pl.BlockSpec((1, tq, H_MLA * KV_RANK), lambda bi, qi: (bi, qi, 0)),
        out_shape=jax.ShapeDtypeStruct((b, lq, H_MLA * KV_RANK), BF16),
        scratch_shapes=[pltpu.VMEM((H_MLA, tq, LANE), BF16), pltpu.VMEM((H_MLA, tq, LANE), F32),
                        pltpu.VMEM((H_MLA, tq, LANE), F32), pltpu.VMEM((H_MLA, tq, KV_RANK), F32)],
        compiler_params=_params("parallel", "arbitrary"),
        name="mla_attention",
    )(q_lat, q_rope, kc, kr2)


def _mla_out_kernel(ol_ref, wuv_ref, w_ref, x_ref, g_ref, b_ref, wrh_ref, wrl_ref, br_ref, ltri_ref,
                    o_ref, cnt_ref, cnt_s):
    parts = [_dot(ol_ref[:, h * KV_RANK:(h + 1) * KV_RANK], wuv_ref[h]).astype(BF16) for h in range(H_MLA)]
    y = ALPHA * x_ref[...] + _dot(jnp.concatenate(parts, axis=1), w_ref[...])
    out = _layer_norm(y, g_ref[...], b_ref[...])
    _route_store(out, wrh_ref, wrl_ref, br_ref, ltri_ref, o_ref, cnt_ref, cnt_s)


def mla_out(o_lat, wuv, w_out, x, ln_g, ln_b, route_w):
    t, d = x.shape
    tm = _row_tile(t)
    in_specs = [_rows(tm, H_MLA * KV_RANK), _whole(wuv.shape), _whole(w_out.shape), _rows(tm, d),
                _whole((1, d)), _whole((1, d))]
    return _routed_out_call(_mla_out_kernel, "mla_out", t, d, tm, in_specs,
                            (o_lat, wuv, w_out, x, ln_g, ln_b), route_w)


def _row_move_kernel(idx_ref, src_ref, *rest, rows, scatter):
    dst_ref, sem = rest[-2], rest[-1]
    base = pl.program_id(0) * rows

    def row_copy(src_row, dst_row, n):
        return pltpu.make_async_copy(src_ref.at[pl.ds(src_row, n)], dst_ref.at[pl.ds(dst_row, n)], sem)

    def issue(r, carry):
        tok = base + r
        row = idx_ref[tok]
        if scatter:
            row_copy(tok, row, 1).start()
        else:
            row_copy(row, tok, 1).start()
        return carry

    lax.fori_loop(0, rows, issue, 0, unroll=8)
    row_copy(0, 0, rows).wait()


def row_move(idx, src, dst_rows, *, scatter, dst_init=None):
    t = idx.shape[0]
    width = src.shape[1]
    rows = min(2048, t)
    any_spec = pl.BlockSpec(memory_space=pl.ANY)
    operands = [idx, src] + ([dst_init] if dst_init is not None else [])
    grid_spec = pltpu.PrefetchScalarGridSpec(
        num_scalar_prefetch=1, grid=(t // rows,), in_specs=[any_spec] * (len(operands) - 1), out_specs=any_spec,
        scratch_shapes=[pltpu.SemaphoreType.DMA(())])
    return pl.pallas_call(
        functools.partial(_row_move_kernel, rows=rows, scatter=scatter),
        grid_spec=grid_spec,
        out_shape=jax.ShapeDtypeStruct((dst_rows, width), src.dtype),
        input_output_aliases={2: 0} if dst_init is not None else {},
        compiler_params=_params("arbitrary"),
        name="row_scatter" if scatter else "row_gather",
    )(*operands)


def _moe_kernel(ta_ref, tb_ref, nu_ref, xs_ref, w1a_ref, w3a_ref, w2a_ref, w1b_ref, w3b_ref, w2b_ref,
                g_ref, b_ref, o_ref):
    t = pl.program_id(0)
    d = o_ref.shape[1]

    @pl.when(t < nu_ref[0])
    def _():
        x = xs_ref[:, :d]
        ext = xs_ref[:, d:]
        xb = x.astype(BF16)

        def ffn(w1_ref, w3_ref, w2_ref):
            h1 = _dot(xb, w1_ref[0])
            h3 = _dot(xb, w3_ref[0])
            return _dot((h1 * _sigmoid(h1) * h3).astype(BF16), w2_ref[0])

        y = ext[:, 2:3] * ffn(w1a_ref, w3a_ref, w2a_ref) + ext[:, 3:4] * ffn(w1b_ref, w3b_ref, w2b_ref)
        o_ref[...] = _layer_norm(ALPHA * x + y, g_ref[...], b_ref[...])

    @pl.when(t >= nu_ref[0])
    def _():
        o_ref[...] = jnp.zeros(o_ref.shape, o_ref.dtype)


def moe_ln(tile_ea, tile_eb, n_used, xs, w1, w3, w2, ln_g, ln_b, tm):
    p, wide = xs.shape
    d = wide - EXT
    de = w1.shape[2]

    def w_spec(shape, which):
        return pl.BlockSpec(shape, lambda t, ta, tb, nu: ((ta, tb)[which][t], 0, 0))

    grid_spec = pltpu.PrefetchScalarGridSpec(
        num_scalar_prefetch=3,
        grid=(p // tm,),
        in_specs=[pl.BlockSpec((tm, wide), lambda t, ta, tb, nu: (t, 0)),
                  w_spec((1, d, de), 0), w_spec((1, d, de), 0), w_spec((1, de, d), 0),
                  w_spec((1, d, de), 1), w_spec((1, d, de), 1), w_spec((1, de, d), 1),
                  pl.BlockSpec((1, d), lambda t, ta, tb, nu: (0, 0)),
                  pl.BlockSpec((1, d), lambda t, ta, tb, nu: (0, 0))],
        out_specs=pl.BlockSpec((tm, d), lambda t, ta, tb, nu: (t, 0)),
    )
    return pl.pallas_call(
        _moe_kernel,
        grid_spec=grid_spec,
        out_shape=jax.ShapeDtypeStruct((p, d), F32),
        compiler_params=_params("arbitrary"),
        name="moe_ln",
    )(tile_ea, tile_eb, n_used, xs, w1, w3, w2, w1, w3, w2, ln_g, ln_b)


def _bucket_experts():
    ea, eb = [], []
    for g in range(N_GROUPS):
        for ja in range(EXP_PER_GROUP):
            for jb in range(ja + 1, EXP_PER_GROUP):
                ea.append(g * EXP_PER_GROUP + ja)
                eb.append(g * EXP_PER_GROUP + jb)
    return np.asarray(ea, np.int32), np.asarray(eb, np.int32)


def _bucket_layout(x1ext, counts, tm):
    t, wide = x1ext.shape
    d = wide - EXT
    bucket = x1ext[:, d].astype(jnp.int32)
    rank = x1ext[:, d + 1].astype(jnp.int32)
    counts = counts[0, :N_BUCKETS].astype(jnp.int32)
    padded = ((counts + tm - 1) // tm) * tm
    row_end = jnp.cumsum(padded)
    row_start = row_end - padded
    ids = jnp.arange(N_BUCKETS, dtype=jnp.int32)
    pos = rank + jnp.sum(jnp.where(bucket[:, None] == ids[None, :], row_start[None, :], 0), axis=1)
    n_rows = _round_up(t, tm) + N_BUCKETS * tm
    n_tiles = n_rows // tm
    n_used = (row_end[-1] // tm).astype(jnp.int32)
    tile_bucket = jnp.searchsorted(row_end, jnp.arange(n_tiles, dtype=jnp.int32) * tm, side="right")
    last_bucket = jnp.searchsorted(row_end, (n_used - 1) * tm, side="right")
    tile_bucket = jnp.minimum(tile_bucket, last_bucket).astype(jnp.int32)
    ea, eb = _bucket_experts()
    return pos.astype(jnp.int32), jnp.asarray(ea)[tile_bucket], jnp.asarray(eb)[tile_bucket], n_used.reshape(1), n_rows


def hier_moe_ln(x1ext, counts, moe_w, ln_g, ln_b):
    w1, w3, w2 = moe_w
    t, wide = x1ext.shape
    tm = 256 if t >= 8192 else 32
    pos, tile_ea, tile_eb, n_used, n_rows = _bucket_layout(x1ext, counts, tm)
    xs = row_move(pos, x1ext, n_rows, scatter=True, dst_init=jnp.zeros((n_rows, wide), F32))
    ys = moe_ln(tile_ea, tile_eb, n_used, xs, w1, w3, w2, ln_g, ln_b, tm)
    return row_move(pos, ys, t, scatter=False)


def _rope_tables(s, offset, tm):
    half = ROPE // 2
    inv_freq = ROPE_BASE ** (-jnp.arange(half, dtype=F32) / half)
    ang = (offset + jnp.arange(s)).astype(F32)[:, None] * inv_freq[None, :]
    cos, sin = jnp.cos(ang), jnp.sin(ang)
    cc = jnp.concatenate([cos, cos, cos, cos], -1)
    ss = jnp.concatenate([-sin, sin, -sin, sin], -1)
    if tm > s:
        cc, ss = jnp.tile(cc, (tm // s, 1)), jnp.tile(ss, (tm // s, 1))
    return cc, ss


def _round_up(n, m):
    return (n + m - 1) // m * m


def ab_layer(x, b, s, wts, route_w, cache):
    w_big, w_gate, b_gate, g_ml, w_out, ln_g, ln_b = wts
    t = b * s
    fq, mq, mk, mv, mo, fk, fk_bf, fv, fv_bf = proj(
        x, w_big, [(W_FOX, (BF16,)), (W_ML, (BF16,)), (W_ML, (BF16,)), (W_ML, (BF16,)), (W_ML, (BF16,)),
                   (W_FOX, (F32, BF16)), (W_FOX, (F32, BF16))])
    ls_cols = np.array([1.0] * H_FOX + [0.0] * H_ML + [1.0] * H_ML)
    gates = gate_proj(x, w_gate, b_gate, ls_cols)
    flf = gates[:, :H_FOX].reshape(b, s, H_FOX)
    ml_rows = jnp.swapaxes(gates[:, H_FOX:H_FOX + 2 * H_ML].reshape(b, s, 2 * H_ML), 1, 2)
    ml_rows = jnp.pad(ml_rows, ((0, 0), (0, 16 - 2 * H_ML), (0, 0)))
    fk3, fv3 = fk.reshape(b, s, W_FOX), fv.reshape(b, s, W_FOX)
    if cache is None:
        k_all, v_all, lf_all = fk_bf.reshape(b, s, W_FOX), fv_bf.reshape(b, s, W_FOX), flf
        c0 = jnp.zeros((b, H_ML, DH_ML, DH_ML), F32)
        n0 = jnp.zeros((b, H_ML, 1, DH_ML), F32)
        m0 = jnp.zeros((b, H_ML, 1, 1), F32)
        tq, chunk = 256, 256
    else:
        ck, cv, clf, c0, n0, m0 = cache
        past = ck.shape[1]
        k_all = jnp.concatenate([ck.reshape(b, past, W_FOX).astype(BF16), fk_bf.reshape(b, s, W_FOX)], 1)
        v_all = jnp.concatenate([cv.reshape(b, past, W_FOX).astype(BF16), fv_bf.reshape(b, s, W_FOX)], 1)
        lf_all = jnp.concatenate([clf, flf], 1)
        n0 = n0.reshape(b, H_ML, 1, DH_ML)
        m0 = m0.reshape(b, H_ML, 1, 1)
        tq, chunk = s, s
    lk = k_all.shape[1]
    lf_rows = jnp.pad(jnp.swapaxes(lf_all, 1, 2), ((0, 0), (0, 16 - H_FOX), (0, _round_up(lk, 256) - lk)))
    fo = fox_attention(fq.reshape(b, s, W_FOX), k_all, v_all, lf_rows, tq=tq)
    mh, c_new, n_new, m_new = mlstm(mq.reshape(b, s, W_ML), mk.reshape(b, s, W_ML), mv.reshape(b, s, W_ML),
                                    ml_rows, c0, n0, m0, chunk=chunk)
    x1ext, counts = ab_out(fo.reshape(t, W_FOX), mh.reshape(t, W_ML), mo, x, g_ml, w_out, ln_g, ln_b, route_w)
    state = (fk3.reshape(b, s, H_FOX, DH_FOX), fv3.reshape(b, s, H_FOX, DH_FOX), flf,
             c_new, n_new.reshape(b, H_ML, DH_ML), m_new.reshape(b, H_ML))
    return x1ext, counts, state


def mla_layer(x, b, s, wts, route_w, cache):
    w_down, g_q, g_kv, w_uq, wuk_t, wuv, w_out, ln_g, ln_b = wts
    t = b * s
    tm = _row_tile(t)
    past = 0 if cache is None else cache[0].shape[1]
    cc, ss = _rope_tables(s, past, tm)
    cq, ckv, kr, ckv_bf, kr2_bf = mla_down(x, w_down, g_q, g_kv, cc, ss, s)
    q_lat, q_rope = mla_uq(cq, w_uq, wuk_t, cc, ss, s)
    kc = ckv_bf.reshape(b, s, KV_RANK)
    kr2 = kr2_bf.reshape(b, s, LANE)
    if cache is None:
        tq = 256
    else:
        c_ckv, c_kr = cache
        kc = jnp.concatenate([c_ckv.astype(BF16), kc], 1)
        c_kr_bf = c_kr.astype(BF16)
        kr2 = jnp.concatenate([jnp.concatenate([c_kr_bf, c_kr_bf], -1), kr2], 1)
        tq = s
    o_lat = mla_attention(q_lat.reshape(b, s, -1), q_rope.reshape(b, s, -1), kc, kr2, tq=tq,
                          tk=tq if cache is None else 256)
    x1ext, counts = mla_out(o_lat.reshape(t, -1), wuv, w_out, x, ln_g, ln_b, route_w)
    return x1ext, counts, (ckv.reshape(b, s, KV_RANK), kr.reshape(b, s, ROPE))


def _prep_ab_weights(w_in, b_fox_f, b_ml_i, b_ml_f, g_ml, w_out, ln_g, ln_b):
    sizes = (W_FOX, W_FOX, W_FOX, H_FOX, W_ML, W_ML, W_ML, H_ML, H_ML, W_ML)
    idx = np.cumsum(sizes[:-1]).tolist()
    fq, fk, fv, ff, mq, mk, mv, mi, mf, mo = jnp.split(w_in, idx, axis=1)
    w_big = jnp.concatenate([fq, mq, mk, mv, mo, fk, fv], 1).astype(BF16)
    w_gate = jnp.concatenate([ff, mi, mf], 1)
    b_gate = jnp.concatenate([b_fox_f, b_ml_i, b_ml_f])
    return (w_big, w_gate, b_gate, g_ml[None, :], w_out.astype(BF16), ln_g[None, :], ln_b[None, :])


def _prep_mla_weights(w_down, g_q, w_uq, g_kv, w_uk, w_uv, w_out, ln_g, ln_b):
    half = ROPE // 2
    cq_w, ckv_w, kr_w = jnp.split(w_down, [Q_RANK, Q_RANK + KV_RANK], axis=1)
    kr_sw = jnp.concatenate([kr_w[:, half:], kr_w[:, :half]], 1)
    w_down_p = jnp.concatenate([cq_w, ckv_w, kr_w, kr_w, kr_sw, kr_sw], 1).astype(BF16)
    uq = w_uq.reshape(Q_RANK, H_MLA, NOPE + ROPE)
    uq_nope = uq[:, :, :NOPE].reshape(Q_RANK, H_MLA * NOPE)
    uq_rope = uq[:, :, NOPE:]
    uq_rope_sw = jnp.concatenate([uq_rope[..., half:], uq_rope[..., :half]], -1)
    w_uq_p = jnp.concatenate([uq_nope, uq_rope.reshape(Q_RANK, -1), uq_rope_sw.reshape(Q_RANK, -1)], 1).astype(BF16)
    wuk_t = jnp.transpose(w_uk, (1, 2, 0)).astype(BF16)
    wuv = jnp.transpose(w_uv, (1, 0, 2)).astype(BF16)
    return (w_down_p, g_q[None, :], g_kv[None, :], w_uq_p, wuk_t, wuv, w_out.astype(BF16), ln_g[None, :], ln_b[None, :])


def kernel(x_prompt, x_sample, cache_fox_k, cache_fox_v, cache_fox_logf, state_mlstm_c, state_mlstm_n, state_mlstm_m, cache_mla_ckv, cache_mla_krope, w_ab_in, b_fox_f, b_mlstm_i, b_mlstm_f, g_mlstm_norm, w_ab_out, w_mla_down, g_mla_q, w_mla_uq, g_mla_kv, w_mla_uk, w_mla_uv, w_mla_out, ln1_g, ln1_b, ln2_g, ln2_b, w_moe_group, b_moe_group, w_moe_router, b_moe_router, w_exp_gate, w_exp_up, w_exp_down):
    bp, sp, d = x_prompt.shape
    bs, ss_, _ = x_sample.shape
    xp = x_prompt.reshape(bp * sp, d)
    xs = x_sample.reshape(bs * ss_, d)
    ab_p, ab_s, c_p, c_s = [], [], [], []
    for l in range(DEPTH):
        j = l // 2
        route_w = (jnp.concatenate([w_moe_group[l], w_moe_router[l]], 1),
                   jnp.concatenate([b_moe_group[l], b_moe_router[l]]))
        if l % 2 == 0:
            wts = _prep_ab_weights(w_ab_in[j], b_fox_f[j], b_mlstm_i[j], b_mlstm_f[j], g_mlstm_norm[j],
                                   w_ab_out[j], ln1_g[l], ln1_b[l])
            xp1, cnt_p, st_p = ab_layer(xp, bp, sp, wts, route_w, None)
            xs1, cnt_s, st_s = ab_layer(xs, bs, ss_, wts, route_w,
                                        (cache_fox_k[j], cache_fox_v[j], cache_fox_logf[j],
                                         state_mlstm_c[j], state_mlstm_n[j], state_mlstm_m[j]))
            ab_p.append(st_p)
            ab_s.append(st_s)
        else:
            wts = _prep_mla_weights(w_mla_down[j], g_mla_q[j], w_mla_uq[j], g_mla_kv[j], w_mla_uk[j], w_mla_uv[j],
                                    w_mla_out[j], ln1_g[l], ln1_b[l])
            xp1, cnt_p, st_p = mla_layer(xp, bp, sp, wts, route_w, None)
            xs1, cnt_s, st_s = mla_layer(xs, bs, ss_, wts, route_w, (cache_mla_ckv[j], cache_mla_krope[j]))
            c_p.append(st_p)
            c_s.append(st_s)
        moe_w = (w_exp_gate[l].astype(BF16), w_exp_up[l].astype(BF16), w_exp_down[l].astype(BF16))
        xp = hier_moe_ln(xp1, cnt_p, moe_w, ln2_g[l][None, :], ln2_b[l][None, :])
        xs = hier_moe_ln(xs1, cnt_s, moe_w, ln2_g[l][None, :], ln2_b[l][None, :])

    def stack(groups, i):
        return jnp.stack([g[i] for g in groups])

    return (xp.reshape(bp, sp, d), xs.reshape(bs, ss_, d),
            stack(ab_p, 0), stack(ab_p, 1), stack(ab_p, 2), stack(ab_p, 3), stack(ab_p, 4), stack(ab_p, 5),
            stack(c_p, 0), stack(c_p, 1),
            stack(ab_s, 0), stack(ab_s, 1), stack(ab_s, 2), stack(ab_s, 3), stack(ab_s, 4), stack(ab_s, 5),
            stack(c_s, 0), stack(c_s, 1))
```

```python
import functools

import numpy as np
import jax
import jax.numpy as jnp
from jax import lax
from jax.experimental import pallas as pl
from jax.experimental.pallas import tpu as pltpu

F32 = jnp.float32
BF16 = jnp.bfloat16

LANE = 128
VMEM_LIMIT_BYTES = 48 * 1024 * 1024

H_FOX, DH_FOX = 8, 64
W_FOX = H_FOX * DH_FOX
H_ML, DH_ML = 4, 128
W_ML = H_ML * DH_ML
H_MLA, NOPE, ROPE, DV_MLA = 8, 128, 64, 128
Q_RANK, KV_RANK = 384, 256
ROPE_BASE = 10000.0
N_GROUPS, EXP_PER_GROUP, TOP_K = 4, 8, 2
N_EXPERTS = N_GROUPS * EXP_PER_GROUP
DEPTH = 4
ALPHA = (2 * DEPTH) ** 0.25
LN_EPS = 1e-5
RMS_EPS = 1e-6
CHUNK = 64
NEG_BIG = -1e30


def _params(*sem):
    return pltpu.CompilerParams(dimension_semantics=sem, vmem_limit_bytes=VMEM_LIMIT_BYTES)


def _dot(a, b):
    return jnp.dot(a, b, preferred_element_type=F32)


def _dot_nt(a, b):
    return lax.dot_general(a, b, (((1,), (1,)), ((), ())), preferred_element_type=F32)


def _split3(x):
    hi = x.astype(BF16)
    r = x - hi.astype(F32)
    mid = r.astype(BF16)
    lo = (r - mid.astype(F32)).astype(BF16)
    return hi, mid, lo


def _layer_norm(y, g, b):
    mu = jnp.mean(y, axis=1, keepdims=True)
    yc = y - mu
    var = jnp.mean(yc * yc, axis=1, keepdims=True)
    return yc * lax.rsqrt(var + LN_EPS) * g + b


def _sigmoid(x):
    return 1.0 / (1.0 + jnp.exp(-x))


def _row_tile(t):
    return min(512, t)


def _proj_kernel(x_ref, w_ref, *o_refs, groups):
    xb = x_ref[...].astype(BF16)
    refs = iter(o_refs)
    c0 = 0
    for n, dtypes in groups:
        group_refs = [next(refs) for _ in dtypes]
        for a in range(0, n, 512):
            b = min(a + 512, n)
            z = _dot(xb, w_ref[:, c0 + a:c0 + b])
            for o_ref in group_refs:
                o_ref[:, a:b] = z.astype(o_ref.dtype)
        c0 += n


def proj(x, w, groups):
    t, k = x.shape
    tm = _row_tile(t)
    flat = [(n, dt) for n, dts in groups for dt in dts]
    return pl.pallas_call(
        functools.partial(_proj_kernel, groups=tuple(groups)),
        grid=(t // tm,),
        in_specs=[pl.BlockSpec((tm, k), lambda i: (i, 0)), pl.BlockSpec(w.shape, lambda i: (0, 0))],
        out_specs=[pl.BlockSpec((tm, n), lambda i: (i, 0)) for n, _ in flat],
        out_shape=[jax.ShapeDtypeStruct((t, n), dt) for n, dt in flat],
        compiler_params=_params("parallel"),
        name="proj",
    )(x, w)


def _gate_kernel(x_ref, wh_ref, wl_ref, b_ref, mask_ref, o_ref):
    x = x_ref[...]
    xh = x.astype(BF16)
    xl = (x - xh.astype(F32)).astype(BF16)
    z = _dot(xh, wh_ref[...]) + _dot(xh, wl_ref[...]) + _dot(xl, wh_ref[...]) + b_ref[...]
    log_sig = jnp.minimum(z, 0.0) - jnp.log1p(jnp.exp(-jnp.abs(z)))
    o_ref[...] = jnp.where(mask_ref[...] > 0.0, log_sig, z)


def gate_proj(x, w, bias, log_sigmoid_cols):
    t, k = x.shape
    n = w.shape[1]
    tm = _row_tile(t)
    wp = jnp.zeros((k, LANE), F32).at[:, :n].set(w)
    wh = wp.astype(BF16)
    wl = (wp - wh.astype(F32)).astype(BF16)
    bp = jnp.zeros((1, LANE), F32).at[0, :n].set(bias)
    mask = jnp.zeros((1, LANE), F32).at[0, :n].set(jnp.asarray(log_sigmoid_cols, F32))
    row = pl.BlockSpec((1, LANE), lambda i: (0, 0))
    wspec = pl.BlockSpec((k, LANE), lambda i: (0, 0))
    return pl.pallas_call(
        _gate_kernel,
        grid=(t // tm,),
        in_specs=[pl.BlockSpec((tm, k), lambda i: (i, 0)), wspec, wspec, row, row],
        out_specs=pl.BlockSpec((tm, LANE), lambda i: (i, 0)),
        out_shape=jax.ShapeDtypeStruct((t, LANE), F32),
        compiler_params=_params("parallel"),
        name="gate_proj",
    )(x, wh, wl, bp, mask)


def _lanes(x, w):
    if w < LANE:
        return x[:, :w]
    return x if w == LANE else jnp.concatenate([x] * (w // LANE), axis=1)


def _softmax_step(s, h, m_s, l_s, acc_s, v):
    w = s.shape[1]
    m_prev = m_s[h]
    m_new = jnp.maximum(m_prev, jnp.max(s, axis=1, keepdims=True))
    p = jnp.exp(s - _lanes(m_new, w))
    a = jnp.exp(m_prev - m_new)
    l_s[h] = a * l_s[h] + jnp.sum(p, axis=1, keepdims=True)
    acc_s[h] = acc_s[h] * _lanes(a, acc_s.shape[2]) + _dot(p.astype(BF16), v)
    m_s[h] = m_new


def _fox_kernel(q_ref, k_ref, v_ref, lf_ref, o_ref, ncum_ref, qm_s, m_s, l_s, acc_s, *, tq, tk, q_off, cblk):
    qi = pl.program_id(1)
    sub = cblk // LANE
    n_pairs = H_FOX // 2

    @pl.when(qi == 0)
    def _():
        r = lax.broadcasted_iota(jnp.int32, (cblk, cblk), 0)
        c = lax.broadcasted_iota(jnp.int32, (cblk, cblk), 1)
        upper = jnp.where(r <= c, 1.0, 0.0).astype(BF16)
        carry = jnp.zeros((lf_ref.shape[1], 1), F32)
        for j in range(lf_ref.shape[2] // cblk):
            g1, g2, g3 = _split3(lf_ref[0, :, j * cblk:(j + 1) * cblk])
            cum = _dot(g1, upper) + _dot(g2, upper) + _dot(g3, upper) + carry
            carry = cum[:, cblk - 1:cblk]
            for h in range(H_FOX):
                for u in range(sub):
                    ncum_ref[h, j * sub + u] = -cum[h:h + 1, u * LANE:(u + 1) * LANE]

    lane = lax.broadcasted_iota(jnp.int32, (tq, LANE), 1)
    for hp in range(n_pairs):
        q = q_ref[0, :, hp * LANE:(hp + 1) * LANE] * (DH_FOX ** -0.5)
        zero = jnp.zeros_like(q)
        qm_s[2 * hp] = jnp.where(lane < DH_FOX, q, zero)
        qm_s[2 * hp + 1] = jnp.where(lane >= DH_FOX, q, zero)
    m_s[...] = jnp.full(m_s.shape, NEG_BIG, F32)
    l_s[...] = jnp.zeros(l_s.shape, F32)
    acc_s[...] = jnp.zeros(acc_s.shape, F32)
    q_start = q_off + qi * tq

    def update(off, w, bias_of_head, mask):
        for hp in range(n_pairs):
            kt = k_ref[0, pl.ds(off, w), hp * LANE:(hp + 1) * LANE]
            vt = v_ref[0, pl.ds(off, w), hp * LANE:(hp + 1) * LANE]
            for h in (2 * hp, 2 * hp + 1):
                s = _dot_nt(qm_s[h], kt) + bias_of_head(h)
                if mask is not None:
                    s = jnp.where(mask, s, -jnp.inf)
                _softmax_step(s, h, m_s, l_s, acc_s, vt)

    def full_tile(j, carry):
        def bias(h):
            return jnp.concatenate([ncum_ref[h, j * (tk // LANE) + u] for u in range(tk // LANE)], axis=1)
        update(pl.multiple_of(j * tk, tk), tk, bias, None)
        return carry

    lax.fori_loop(0, q_start // tk, full_tile, 0)

    def diag_bias(h):
        if tq >= LANE:
            return jnp.concatenate([ncum_ref[h, q_start // LANE + u] for u in range(tq // LANE)], axis=1)
        lo = q_off % LANE
        return ncum_ref[h, q_start // LANE][:, lo:lo + tq]

    r = lax.broadcasted_iota(jnp.int32, (tq, tq), 0)
    c = lax.broadcasted_iota(jnp.int32, (tq, tq), 1)
    update(pl.multiple_of(q_start, tq), tq, diag_bias, c <= r)
    for hp in range(n_pairs):
        o = jnp.where(lane < DH_FOX, acc_s[2 * hp] / l_s[2 * hp], acc_s[2 * hp + 1] / l_s[2 * hp + 1])
        o_ref[0, :, hp * LANE:(hp + 1) * LANE] = o.astype(o_ref.dtype)


def fox_attention(q, k, v, logf_rows, *, tq, tk=256, cblk=256):
    b, lq, _ = q.shape
    lk = k.shape[1]
    lkp = logf_rows.shape[2]
    q_off = lk - lq
    nq = lq // tq
    assert q_off % tk == 0 and (tq % tk == 0 or lq == tq) and lkp % cblk == 0
    assert tq >= LANE or (nq == 1 and q_off % LANE + tq <= LANE)
    kern = functools.partial(_fox_kernel, tq=tq, tk=tk, q_off=q_off, cblk=cblk)
    return pl.pallas_call(
        kern,
        grid=(b, nq),
        in_specs=[
            pl.BlockSpec((1, tq, W_FOX), lambda bi, qi: (bi, qi, 0)),
            pl.BlockSpec((1, lk, W_FOX), lambda bi, qi: (bi, 0, 0)),
            pl.BlockSpec((1, lk, W_FOX), lambda bi, qi: (bi, 0, 0)),
            pl.BlockSpec((1, logf_rows.shape[1], lkp), lambda bi, qi: (bi, 0, 0)),
        ],
        out_specs=pl.BlockSpec((1, tq, W_FOX), lambda bi, qi: (bi, qi, 0)),
        out_shape=jax.ShapeDtypeStruct((b, lq, W_FOX), BF16),
        scratch_shapes=[pltpu.VMEM((H_FOX, lkp // LANE, 1, LANE), F32), pltpu.VMEM((H_FOX, tq, LANE), BF16),
                        pltpu.VMEM((H_FOX, tq, LANE), F32), pltpu.VMEM((H_FOX, tq, LANE), F32),
                        pltpu.VMEM((H_FOX, tq, LANE), F32)],
        compiler_params=_params("parallel", "arbitrary"),
        name="fox_attention",
    )(q, k, v, logf_rows)


def _mlstm_kernel(q_ref, k_ref, v_ref, g_ref, c0_ref, n0_ref, m0_ref,
                  h_ref, c_ref, n_ref, m_ref, c_s, n_s, m_s, *, chunk):
    ci = pl.program_id(1)
    L = chunk
    scale = DH_ML ** -0.5

    @pl.when(ci == 0)
    def _():
        c_s[...] = c0_ref[0]
        n_s[...] = n0_ref[0]
        m_s[...] = m0_ref[0]

    r = lax.broadcasted_iota(jnp.int32, (L, L), 0)
    c = lax.broadcasted_iota(jnp.int32, (L, L), 1)
    eye = r == c
    causal = c <= r
    upper = jnp.where(r <= c, 1.0, 0.0).astype(BF16)
    g = g_ref[0]
    g1, g2, g3 = _split3(g)
    cum = _dot(g1, upper) + _dot(g2, upper) + _dot(g3, upper)

    def to_col(row):
        return jnp.sum(jnp.where(eye, row, 0.0), axis=1, keepdims=True)

    for h in range(H_ML):
        sl = slice(h * DH_ML, (h + 1) * DH_ML)
        qh, kh, vh = q_ref[0, :, sl], k_ref[0, :, sl], v_ref[0, :, sl]
        ig_row = g[h:h + 1, :]
        bh_row = cum[H_ML + h:H_ML + h + 1, :]
        bh_col, ig_col = to_col(bh_row), to_col(ig_row)
        m0 = m_s[h]
        c0 = c_s[h]
        n0 = n_s[h]
        logd = jnp.where(causal, bh_col - bh_row + ig_row, -jnp.inf)
        inter = bh_col + m0
        m_col = jnp.maximum(inter, jnp.max(logd, axis=1, keepdims=True))
        d = jnp.exp(logd - m_col)
        a_col = jnp.exp(inter - m_col)
        s = _dot_nt(qh, kh) * scale * d
        num = _dot(s.astype(BF16), vh) + _dot_nt(qh, c0.astype(BF16)) * a_col
        qf = qh.astype(F32)
        den = jnp.sum(s, axis=1, keepdims=True) + a_col * jnp.sum(qf * n0, axis=1, keepdims=True)
        den = jnp.maximum(jnp.abs(den), jnp.exp(-m_col))
        h_ref[0, :, sl] = (num / den).astype(h_ref.dtype)

        m_l = m_col[L - 1:L, :]
        bh_l = bh_row[:, L - 1:L]
        a_l = jnp.exp(bh_l + m0 - m_l)
        w_col = jnp.exp(bh_l - bh_col + ig_col - m_l)
        vw_t = jnp.transpose(vh.astype(F32) * w_col).astype(BF16)
        c_s[h] = a_l * c0 + _dot(vw_t, kh) * scale
        n_s[h] = a_l * n0 + jnp.sum(kh.astype(F32) * w_col, axis=0, keepdims=True) * scale
        m_s[h] = m_l

    @pl.when(ci == pl.num_programs(1) - 1)
    def _():
        c_ref[0] = c_s[...]
        n_ref[0] = n_s[...]
        m_ref[0] = m_s[...]


def mlstm(q, k, v, gate_rows, c0, n0, m0, *, chunk):
    b, s, _ = q.shape
    nc = s // chunk
    seq = pl.BlockSpec((1, chunk, W_ML), lambda bi, ci: (bi, ci, 0))
    c_spec = pl.BlockSpec((1, H_ML, DH_ML, DH_ML), lambda bi, ci: (bi, 0, 0, 0))
    n_spec = pl.BlockSpec((1, H_ML, 1, DH_ML), lambda bi, ci: (bi, 0, 0, 0))
    m_spec = pl.BlockSpec((1, H_ML, 1, 1), lambda bi, ci: (bi, 0, 0, 0))
    return pl.pallas_call(
        functools.partial(_mlstm_kernel, chunk=chunk),
        grid=(b, nc),
        in_specs=[seq, seq, seq, pl.BlockSpec((1, gate_rows.shape[1], chunk), lambda bi, ci: (bi, 0, ci)),
                  c_spec, n_spec, m_spec],
        out_specs=[seq, c_spec, n_spec, m_spec],
        out_shape=[jax.ShapeDtypeStruct((b, s, W_ML), BF16),
                   jax.ShapeDtypeStruct(c0.shape, F32), jax.ShapeDtypeStruct(n0.shape, F32),
                   jax.ShapeDtypeStruct(m0.shape, F32)],
        scratch_shapes=[pltpu.VMEM((H_ML, DH_ML, DH_ML), F32), pltpu.VMEM((H_ML, 1, DH_ML), F32),
                        pltpu.VMEM((H_ML, 1, 1), F32)],
        compiler_params=_params("parallel", "arbitrary"),
        name="mlstm",
    )(q, k, v, gate_rows, c0, n0, m0)


N_PAIRS = EXP_PER_GROUP * (EXP_PER_GROUP - 1) // 2
N_BUCKETS = N_GROUPS * N_PAIRS
assert N_BUCKETS <= LANE
EXT = LANE


def _route_store(out, wrh_ref, wrl_ref, br_ref, ltri_ref, o_ref, cnt_ref, cnt_s):
    tm, d = out.shape

    @pl.when(pl.program_id(0) == 0)
    def _():
        cnt_s[...] = jnp.zeros(cnt_s.shape, F32)

    xh = out.astype(BF16)
    xl = (out - xh.astype(F32)).astype(BF16)
    z = _dot(xh, wrh_ref[...]) + _dot(xh, wrl_ref[...]) + _dot(xl, wrh_ref[...]) + br_ref[...]
    lane = lax.broadcasted_iota(jnp.int32, (tm, LANE), 1).astype(F32)
    far = float(LANE)
    neg = -jnp.inf
    gl = jnp.where(lane < N_GROUPS, z, neg)
    gmax = jnp.max(gl, axis=1, keepdims=True)
    g_idx = jnp.min(jnp.where(gl == gmax, lane, far), axis=1, keepdims=True)
    g_gate = 1.0 / jnp.sum(jnp.exp(gl - gmax), axis=1, keepdims=True)
    lo = N_GROUPS + EXP_PER_GROUP * g_idx
    el = jnp.where(jnp.logical_and(lane >= lo, lane < lo + EXP_PER_GROUP), z, neg)
    v1 = jnp.max(el, axis=1, keepdims=True)
    i1 = jnp.min(jnp.where(el == v1, lane, far), axis=1, keepdims=True)
    el2 = jnp.where(lane == i1, neg, el)
    v2 = jnp.max(el2, axis=1, keepdims=True)
    i2 = jnp.min(jnp.where(el2 == v2, lane, far), axis=1, keepdims=True)
    e21 = jnp.exp(v2 - v1)
    w1 = g_gate / (1.0 + e21)
    w2 = w1 * e21
    j1, j2 = i1 - lo, i2 - lo
    ja, jb = jnp.minimum(j1, j2), jnp.maximum(j1, j2)
    pair = ja * (2 * EXP_PER_GROUP - 1 - ja) * 0.5 + (jb - ja - 1.0)
    bucket = g_idx * N_PAIRS + pair
    first_low = j1 < j2
    wa = jnp.where(first_low, w1, w2)
    wb = jnp.where(first_low, w2, w1)
    onehot = jnp.where(lane == bucket, 1.0, 0.0)
    earlier = _dot(ltri_ref[...], onehot.astype(BF16))
    cnt = cnt_s[...]
    rank = jnp.sum(onehot * (earlier + cnt), axis=1, keepdims=True)
    cnt_new = cnt + jnp.sum(onehot, axis=0, keepdims=True)
    cnt_s[...] = cnt_new
    cnt_ref[...] = cnt_new
    o_ref[:, :d] = out
    o_ref[:, d:] = jnp.where(lane == 0.0, bucket, jnp.where(lane == 1.0, rank, jnp.where(
        lane == 2.0, wa, jnp.where(lane == 3.0, wb, 0.0))))


def _route_operands(w_route, b_route, tm):
    n = w_route.shape[1]
    wp = jnp.zeros((w_route.shape[0], LANE), F32).at[:, :n].set(w_route)
    wh = wp.astype(BF16)
    wl = (wp - wh.astype(F32)).astype(BF16)
    bp = jnp.zeros((1, LANE), F32).at[0, :n].set(b_route)
    ltri = jnp.asarray(np.tril(np.ones((tm, tm), np.float32), -1), BF16)
    return wh, wl, bp, ltri


def _route_specs(k, tm):
    return [_whole((k, LANE)), _whole((k, LANE)), _whole((1, LANE)), _whole((tm, tm))]


def _ab_out_kernel(fo_ref, mh_ref, mo_ref, x_ref, gml_ref, w_ref, g_ref, b_ref, wrh_ref, wrl_ref, br_ref, ltri_ref,
                   o_ref, cnt_ref, cnt_s):
    parts = [fo_ref[...]]
    for h in range(H_ML):
        sl = slice(h * DH_ML, (h + 1) * DH_ML)
        mh = mh_ref[:, sl].astype(F32)
        ms = jnp.mean(mh * mh, axis=1, keepdims=True)
        nm = mh * lax.rsqrt(ms + RMS_EPS) * gml_ref[:, sl]
        parts.append((nm * _sigmoid(mo_ref[:, sl].astype(F32))).astype(BF16))
    cat = jnp.concatenate(parts, axis=1)
    y = ALPHA * x_ref[...] + _dot(cat, w_ref[...])
    out = _layer_norm(y, g_ref[...], b_ref[...])
    _route_store(out, wrh_ref, wrl_ref, br_ref, ltri_ref, o_ref, cnt_ref, cnt_s)


def _rows(tm, n):
    return pl.BlockSpec((tm, n), lambda i: (i, 0))


def _whole(shape):
    return pl.BlockSpec(shape, lambda i: (0,) * len(shape))


def _routed_out_call(kern, name, t, d, tm, in_specs, operands, route_w):
    w_route, b_route = route_w
    return pl.pallas_call(
        kern,
        grid=(t // tm,),
        in_specs=in_specs + _route_specs(d, tm),
        out_specs=[_rows(tm, d + EXT), _whole((1, LANE))],
        out_shape=[jax.ShapeDtypeStruct((t, d + EXT), F32), jax.ShapeDtypeStruct((1, LANE), F32)],
        scratch_shapes=[pltpu.VMEM((1, LANE), F32)],
        compiler_params=_params("arbitrary"),
        name=name,
    )(*operands, *_route_operands(w_route, b_route, tm))


def ab_out(fo, mh, mo, x, g_ml, w_out, ln_g, ln_b, route_w):
    t, d = x.shape
    tm = _row_tile(t)
    in_specs = [_rows(tm, W_FOX), _rows(tm, W_ML), _rows(tm, W_ML), _rows(tm, d), _whole((1, W_ML)),
                _whole(w_out.shape), _whole((1, d)), _whole((1, d))]
    return _routed_out_call(_ab_out_kernel, "ab_out", t, d, tm, in_specs,
                            (fo, mh, mo, x, g_ml, w_out, ln_g, ln_b), route_w)


def _rms(z, g):
    return z * lax.rsqrt(jnp.mean(z * z, axis=1, keepdims=True) + RMS_EPS) * g


def _mla_down_kernel(x_ref, w_ref, gq_ref, gkv_ref, cc_ref, ss_ref, cq_ref, ckv_ref, kr_ref, ckvb_ref, kr2_ref):
    xb = x_ref[...].astype(BF16)
    cq_ref[...] = _rms(_dot(xb, w_ref[:, 0:Q_RANK]), gq_ref[...]).astype(BF16)
    ckv = _rms(_dot(xb, w_ref[:, Q_RANK:Q_RANK + KV_RANK]), gkv_ref[...])
    ckv_ref[...] = ckv
    ckvb_ref[...] = ckv.astype(BF16)
    c0 = Q_RANK + KV_RANK
    rope = (_dot(xb, w_ref[:, c0:c0 + LANE]) * cc_ref[...] + _dot(xb, w_ref[:, c0 + LANE:c0 + 2 * LANE]) * ss_ref[...])
    kr_ref[...] = rope[:, :ROPE]
    kr2_ref[...] = rope.astype(BF16)


def _table_spec(table, tm, s):
    if table.shape[0] == tm:
        return pl.BlockSpec((tm, LANE), lambda i: (0, 0))
    per = s // tm
    return pl.BlockSpec((tm, LANE), lambda i: (i % per, 0))


def mla_down(x, w, g_q, g_kv, cc, ss, s):
    t, d = x.shape
    tm = _row_tile(t)
    tab = _table_spec(cc, tm, s)
    return pl.pallas_call(
        _mla_down_kernel,
        grid=(t // tm,),
        in_specs=[_rows(tm, d), _whole(w.shape), _whole((1, Q_RANK)), _whole((1, KV_RANK)), tab, tab],
        out_specs=[_rows(tm, Q_RANK), _rows(tm, KV_RANK), _rows(tm, ROPE), _rows(tm, KV_RANK), _rows(tm, LANE)],
        out_shape=[jax.ShapeDtypeStruct((t, Q_RANK), BF16), jax.ShapeDtypeStruct((t, KV_RANK), F32),
                   jax.ShapeDtypeStruct((t, ROPE), F32), jax.ShapeDtypeStruct((t, KV_RANK), BF16),
                   jax.ShapeDtypeStruct((t, LANE), BF16)],
        compiler_params=_params("parallel"),
        name="mla_down",
    )(x, w, g_q, g_kv, cc, ss)


def _mla_uq_kernel(cq_ref, w_ref, wuk_ref, cc_ref, ss_ref, qlat_ref, qrope_ref):
    cq = cq_ref[...]
    scale = (NOPE + ROPE) ** -0.5
    for h in range(H_MLA):
        qn = _dot(cq, w_ref[:, h * NOPE:(h + 1) * NOPE]).astype(BF16)
        qlat_ref[:, h * KV_RANK:(h + 1) * KV_RANK] = (_dot(qn, wuk_ref[h]) * scale).astype(BF16)
    cc, ss = cc_ref[...] * scale, ss_ref[...] * scale
    r0 = H_MLA * NOPE
    r1 = r0 + H_MLA * ROPE
    for p in range(H_MLA // 2):
        qr = _dot(cq, w_ref[:, r0 + p * LANE:r0 + (p + 1) * LANE])
        qs = _dot(cq, w_ref[:, r1 + p * LANE:r1 + (p + 1) * LANE])
        qrope_ref[:, p * LANE:(p + 1) * LANE] = (qr * cc + qs * ss).astype(BF16)


def mla_uq(cq, w, wuk_t, cc, ss, s):
    t = cq.shape[0]
    tm = _row_tile(t)
    tab = _table_spec(cc, tm, s)
    return pl.pallas_call(
        _mla_uq_kernel,
        grid=(t // tm,),
        in_specs=[_rows(tm, Q_RANK), _whole(w.shape), _whole(wuk_t.shape), tab, tab],
        out_specs=[_rows(tm, H_MLA * KV_RANK), _rows(tm, H_MLA * ROPE)],
        out_shape=[jax.ShapeDtypeStruct((t, H_MLA * KV_RANK), BF16), jax.ShapeDtypeStruct((t, H_MLA * ROPE), BF16)],
        compiler_params=_params("parallel"),
        name="mla_uq",
    )(cq, w, wuk_t, cc, ss)


def _mla_attn_kernel(ql_ref, qr_ref, kc_ref, kr_ref, o_ref, qr_s, m_s, l_s, acc_s, *, tq, tk, q_off):
    qi = pl.program_id(1)
    lane = lax.broadcasted_iota(jnp.int32, (tq, LANE), 1)
    for h in range(H_MLA):
        qr = qr_ref[0, :, (h // 2) * LANE:(h // 2 + 1) * LANE]
        keep = (lane < ROPE) if h % 2 == 0 else (lane >= ROPE)
        qr_s[h] = jnp.where(keep, qr, jnp.zeros_like(qr))
    m_s[...] = jnp.full(m_s.shape, NEG_BIG, F32)
    l_s[...] = jnp.zeros(l_s.shape, F32)
    acc_s[...] = jnp.zeros(acc_s.shape, F32)
    q_start = q_off + qi * tq

    def update(off, w, mask):
        kc = kc_ref[0, pl.ds(off, w), :]
        kr = kr_ref[0, pl.ds(off, w), :]
        for h in range(H_MLA):
            s = _dot_nt(ql_ref[0, :, h * KV_RANK:(h + 1) * KV_RANK], kc) + _dot_nt(qr_s[h], kr)
            if mask is not None:
                s = jnp.where(mask, s, -jnp.inf)
            _softmax_step(s, h, m_s, l_s, acc_s, kc)

    def full_tile(j, carry):
        update(pl.multiple_of(j * tk, tk), tk, None)
        return carry

    lax.fori_loop(0, q_start // tk, full_tile, 0)
    r = lax.broadcasted_iota(jnp.int32, (tq, tq), 0)
    c = lax.broadcasted_iota(jnp.int32, (tq, tq), 1)
    update(pl.multiple_of(q_start, tq), tq, (c // CHUNK) <= (r // CHUNK))
    for h in range(H_MLA):
        o = acc_s[h] / _lanes(l_s[h], KV_RANK)
        o_ref[0, :, h * KV_RANK:(h + 1) * KV_RANK] = o.astype(o_ref.dtype)


def mla_attention(q_lat, q_rope, kc, kr2, *, tq, tk):
    b, lq, _ = q_lat.shape
    lk = kc.shape[1]
    q_off = lk - lq
    assert q_off % tk == 0 and (tq % tk == 0 or lq == tq) and tq % CHUNK == 0 and (tq & (tq - 1)) == 0
    kern = functools.partial(_mla_attn_kernel, tq=tq, tk=tk, q_off=q_off)
    return pl.pallas_call(
        kern,
        grid=(b, lq // tq),
        in_specs=[
            pl.BlockSpec((1, tq, H_MLA * KV_RANK), lambda bi, qi: (bi, qi, 0)),
            pl.BlockSpec((1, tq, H_MLA * ROPE), lambda bi, qi: (bi, qi, 0)),
            pl.BlockSpec((1, lk, KV_RANK), lambda bi, qi: (bi, 0, 0)),
            pl.BlockSpec((1, lk, LANE), lambda bi, qi: (bi, 0, 0)),
        ],
        out_specs=pl.BlockSpec((1, tq, H_MLA * KV_RANK), lambda bi, qi: (bi, qi, 0)),
        out_shape=jax.ShapeDtypeStruct((b, lq, H_MLA * KV_RANK), BF16),
        scratch_shapes=[pltpu.VMEM((H_MLA, tq, LANE), BF16), pltpu.VMEM((H_MLA, tq, LANE), F32),
                        pltpu.VMEM((H_MLA, tq, LANE), F32), pltpu.VMEM((H_MLA, tq, KV_RANK), F32)],
        compiler_params=_params("parallel", "arbitrary"),
        name="mla_attention",
    )(q_lat, q_rope, kc, kr2)


def _mla_out_kernel(ol_ref, wuv_ref, w_ref, x_ref, g_ref, b_ref, wrh_ref, wrl_ref, br_ref, ltri_ref,
                    o_ref, cnt_ref, cnt_s):
    parts = [_dot(ol_ref[:, h * KV_RANK:(h + 1) * KV_RANK], wuv_ref[h]).astype(BF16) for h in range(H_MLA)]
    y = ALPHA * x_ref[...] + _dot(jnp.concatenate(parts, axis=1), w_ref[...])
    out = _layer_norm(y, g_ref[...], b_ref[...])
    _route_store(out, wrh_ref, wrl_ref, br_ref, ltri_ref, o_ref, cnt_ref, cnt_s)


def mla_out(o_lat, wuv, w_out, x, ln_g, ln_b, route_w):
    t, d = x.shape
    tm = _row_tile(t)
    in_specs = [_rows(tm, H_MLA * KV_RANK), _whole(wuv.shape), _whole(w_out.shape), _rows(tm, d),
                _whole((1, d)), _whole((1, d))]
    return _routed_out_call(_mla_out_kernel, "mla_out", t, d, tm, in_specs,
                            (o_lat, wuv, w_out, x, ln_g, ln_b), route_w)


ROW_SLOTS = 3


def _row_scatter_kernel(idx_ref, src_ref, dst_init_ref, dst_ref, buf, in_sem, out_sem, *, rows, n_steps):
    del dst_init_ref

    def load(i, slot):
        return pltpu.make_async_copy(src_ref.at[pl.ds(pl.multiple_of(i * rows, rows), rows)], buf.at[slot],
                                     in_sem.at[slot])

    def rows_done(slot):
        return pltpu.make_async_copy(buf.at[slot], dst_ref.at[pl.ds(0, rows)], out_sem.at[slot])

    i = pl.program_id(0)
    slot = i % ROW_SLOTS
    nxt = (i + 1) % ROW_SLOTS

    @pl.when(i == 0)
    def _():
        load(0, 0).start()

    @pl.when(i + 1 < n_steps)
    def _():
        @pl.when(i + 1 >= ROW_SLOTS)
        def _():
            rows_done(nxt).wait()
        load(i + 1, nxt).start()

    load(i, slot).wait()
    base = i * rows

    def issue(r, c):
        pltpu.make_async_copy(buf.at[slot, pl.ds(r, 1)], dst_ref.at[pl.ds(idx_ref[base + r], 1)],
                              out_sem.at[slot]).start()
        return c

    lax.fori_loop(0, rows, issue, 0, unroll=8)

    @pl.when(i == n_steps - 1)
    def _():
        for k in range(min(ROW_SLOTS, n_steps)):
            rows_done((n_steps - 1 - k) % ROW_SLOTS).wait()


def _row_gather_kernel(idx_ref, src_ref, dst_ref, buf, row_sem, out_sem, *, rows, n_steps):
    def rows_done(slot):
        return pltpu.make_async_copy(src_ref.at[pl.ds(0, rows)], buf.at[slot], row_sem.at[slot])

    def store(i, slot):
        return pltpu.make_async_copy(buf.at[slot], dst_ref.at[pl.ds(pl.multiple_of(i * rows, rows), rows)],
                                     out_sem.at[slot])

    i = pl.program_id(0)
    slot = i % ROW_SLOTS

    @pl.when(i >= ROW_SLOTS)
    def _():
        store(i - ROW_SLOTS, slot).wait()

    base = i * rows

    def issue(r, c):
        pltpu.make_async_copy(src_ref.at[pl.ds(idx_ref[base + r], 1)], buf.at[slot, pl.ds(r, 1)],
                              row_sem.at[slot]).start()
        return c

    lax.fori_loop(0, rows, issue, 0, unroll=8)

    @pl.when(i >= 1)
    def _():
        prev = (i + ROW_SLOTS - 1) % ROW_SLOTS
        rows_done(prev).wait()
        store(i - 1, prev).start()

    @pl.when(i == n_steps - 1)
    def _():
        last = (n_steps - 1) % ROW_SLOTS
        rows_done(last).wait()
        store(n_steps - 1, last).start()
        for k in range(min(ROW_SLOTS, n_steps)):
            store(n_steps - 1 - k, (n_steps - 1 - k) % ROW_SLOTS).wait()


def row_move(idx, src, dst_rows, *, scatter, dst_init=None):
    t = idx.shape[0]
    width = src.shape[1]
    rows = min(512, t)
    any_spec = pl.BlockSpec(memory_space=pl.ANY)
    operands = [idx, src] + ([dst_init] if scatter else [])
    grid_spec = pltpu.PrefetchScalarGridSpec(
        num_scalar_prefetch=1, grid=(t // rows,), in_specs=[any_spec] * (len(operands) - 1), out_specs=any_spec,
        scratch_shapes=[pltpu.VMEM((ROW_SLOTS, rows, width), src.dtype),
                        pltpu.SemaphoreType.DMA((ROW_SLOTS,)), pltpu.SemaphoreType.DMA((ROW_SLOTS,))])
    kern = _row_scatter_kernel if scatter else _row_gather_kernel
    return pl.pallas_call(
        functools.partial(kern, rows=rows, n_steps=t // rows),
        grid_spec=grid_spec,
        out_shape=jax.ShapeDtypeStruct((dst_rows, width), src.dtype),
        input_output_aliases={2: 0} if scatter else {},
        compiler_params=_params("arbitrary"),
        name="row_scatter" if scatter else "row_gather",
    )(*operands)


def _moe_kernel(ta_ref, tb_ref, nu_ref, xs_ref, w1a_ref, w3a_ref, w2a_ref, w1b_ref, w3b_ref, w2b_ref,
                g_ref, b_ref, o_ref):
    t = pl.program_id(0)
    d = o_ref.shape[1]

    @pl.when(t < nu_ref[0])
    def _():
        x = xs_ref[:, :d]
        ext = xs_ref[:, d:]
        xb = x.astype(BF16)

        def ffn(w1_ref, w3_ref, w2_ref):
            h1 = _dot(xb, w1_ref[0])
            h3 = _dot(xb, w3_ref[0])
            return _dot((h1 * _sigmoid(h1) * h3).astype(BF16), w2_ref[0])

        y = ext[:, 2:3] * ffn(w1a_ref, w3a_ref, w2a_ref) + ext[:, 3:4] * ffn(w1b_ref, w3b_ref, w2b_ref)
        o_ref[...] = _layer_norm(ALPHA * x + y, g_ref[...], b_ref[...])

    @pl.when(t >= nu_ref[0])
    def _():
        o_ref[...] = jnp.zeros(o_ref.shape, o_ref.dtype)


def moe_ln(tile_ea, tile_eb, n_used, xs, w1, w3, w2, ln_g, ln_b, tm):
    p, wide = xs.shape
    d = wide - EXT
    de = w1.shape[2]

    def w_spec(shape, which):
        return pl.BlockSpec(shape, lambda t, ta, tb, nu: ((ta, tb)[which][t], 0, 0))

    grid_spec = pltpu.PrefetchScalarGridSpec(
        num_scalar_prefetch=3,
        grid=(p // tm,),
        in_specs=[pl.BlockSpec((tm, wide), lambda t, ta, tb, nu: (t, 0)),
                  w_spec((1, d, de), 0), w_spec((1, d, de), 0), w_spec((1, de, d), 0),
                  w_spec((1, d, de), 1), w_spec((1, d, de), 1), w_spec((1, de, d), 1),
                  pl.BlockSpec((1, d), lambda t, ta, tb, nu: (0, 0)),
                  pl.BlockSpec((1, d), lambda t, ta, tb, nu: (0, 0))],
        out_specs=pl.BlockSpec((tm, d), lambda t, ta, tb, nu: (t, 0)),
    )
    return pl.pallas_call(
        _moe_kernel,
        grid_spec=grid_spec,
        out_shape=jax.ShapeDtypeStruct((p, d), F32),
        compiler_params=_params("arbitrary"),
        name="moe_ln",
    )(tile_ea, tile_eb, n_used, xs, w1, w3, w2, w1, w3, w2, ln_g, ln_b)


def _bucket_experts():
    ea, eb = [], []
    for g in range(N_GROUPS):
        for ja in range(EXP_PER_GROUP):
            for jb in range(ja + 1, EXP_PER_GROUP):
                ea.append(g * EXP_PER_GROUP + ja)
                eb.append(g * EXP_PER_GROUP + jb)
    return np.asarray(ea, np.int32), np.asarray(eb, np.int32)


def _bucket_layout(x1ext, counts, tm):
    t, wide = x1ext.shape
    d = wide - EXT
    bucket = x1ext[:, d].astype(jnp.int32)
    rank = x1ext[:, d + 1].astype(jnp.int32)
    counts = counts[0, :N_BUCKETS].astype(jnp.int32)
    padded = ((counts + tm - 1) // tm) * tm
    row_end = jnp.cumsum(padded)
    row_start = row_end - padded
    ids = jnp.arange(N_BUCKETS, dtype=jnp.int32)
    pos = rank + jnp.sum(jnp.where(bucket[:, None] == ids[None, :], row_start[None, :], 0), axis=1)
    n_rows = _round_up(t, tm) + N_BUCKETS * tm
    n_tiles = n_rows // tm
    n_used = (row_end[-1] // tm).astype(jnp.int32)
    tile_bucket = jnp.searchsorted(row_end, jnp.arange(n_tiles, dtype=jnp.int32) * tm, side="right")
    last_bucket = jnp.searchsorted(row_end, (n_used - 1) * tm, side="right")
    tile_bucket = jnp.minimum(tile_bucket, last_bucket).astype(jnp.int32)
    ea, eb = _bucket_experts()
    return pos.astype(jnp.int32), jnp.asarray(ea)[tile_bucket], jnp.asarray(eb)[tile_bucket], n_used.reshape(1), n_rows


def hier_moe_ln(x1ext, counts, moe_w, ln_g, ln_b):
    w1, w3, w2 = moe_w
    t, wide = x1ext.shape
    tm = 256 if t >= 8192 else 32
    pos, tile_ea, tile_eb, n_used, n_rows = _bucket_layout(x1ext, counts, tm)
    xs = row_move(pos, x1ext, n_rows, scatter=True, dst_init=jnp.zeros((n_rows, wide), F32))
    ys = moe_ln(tile_ea, tile_eb, n_used, xs, w1, w3, w2, ln_g, ln_b, tm)
    return row_move(pos, ys, t, scatter=False)


def _rope_tables(s, offset, tm):
    half = ROPE // 2
    inv_freq = ROPE_BASE ** (-jnp.arange(half, dtype=F32) / half)
    ang = (offset + jnp.arange(s)).astype(F32)[:, None] * inv_freq[None, :]
    cos, sin = jnp.cos(ang), jnp.sin(ang)
    cc = jnp.concatenate([cos, cos, cos, cos], -1)
    ss = jnp.concatenate([-sin, sin, -sin, sin], -1)
    if tm > s:
        cc, ss = jnp.tile(cc, (tm // s, 1)), jnp.tile(ss, (tm // s, 1))
    return cc, ss


def _round_up(n, m):
    return (n + m - 1) // m * m


def ab_layer(x, b, s, wts, route_w, cache):
    w_big, w_gate, b_gate, g_ml, w_out, ln_g, ln_b = wts
    t = b * s
    fq, mq, mk, mv, mo, fk, fk_bf, fv, fv_bf = proj(
        x, w_big, [(W_FOX, (BF16,)), (W_ML, (BF16,)), (W_ML, (BF16,)), (W_ML, (BF16,)), (W_ML, (BF16,)),
                   (W_FOX, (F32, BF16)), (W_FOX, (F32, BF16))])
    ls_cols = np.array([1.0] * H_FOX + [0.0] * H_ML + [1.0] * H_ML)
    gates = gate_proj(x, w_gate, b_gate, ls_cols)
    flf = gates[:, :H_FOX].reshape(b, s, H_FOX)
    ml_rows = jnp.swapaxes(gates[:, H_FOX:H_FOX + 2 * H_ML].reshape(b, s, 2 * H_ML), 1, 2)
    ml_rows = jnp.pad(ml_rows, ((0, 0), (0, 16 - 2 * H_ML), (0, 0)))
    fk3, fv3 = fk.reshape(b, s, W_FOX), fv.reshape(b, s, W_FOX)
    if cache is None:
        k_all, v_all, lf_all = fk_bf.reshape(b, s, W_FOX), fv_bf.reshape(b, s, W_FOX), flf
        c0 = jnp.zeros((b, H_ML, DH_ML, DH_ML), F32)
        n0 = jnp.zeros((b, H_ML, 1, DH_ML), F32)
        m0 = jnp.zeros((b, H_ML, 1, 1), F32)
        tq, chunk = 256, 256
    else:
        ck, cv, clf, c0, n0, m0 = cache
        past = ck.shape[1]
        k_all = jnp.concatenate([ck.reshape(b, past, W_FOX).astype(BF16), fk_bf.reshape(b, s, W_FOX)], 1)
        v_all = jnp.concatenate([cv.reshape(b, past, W_FOX).astype(BF16), fv_bf.reshape(b, s, W_FOX)], 1)
        lf_all = jnp.concatenate([clf, flf], 1)
        n0 = n0.reshape(b, H_ML, 1, DH_ML)
        m0 = m0.reshape(b, H_ML, 1, 1)
        tq, chunk = s, s
    lk = k_all.shape[1]
    lf_rows = jnp.pad(jnp.swapaxes(lf_all, 1, 2), ((0, 0), (0, 16 - H_FOX), (0, _round_up(lk, 256) - lk)))
    fo = fox_attention(fq.reshape(b, s, W_FOX), k_all, v_all, lf_rows, tq=tq)
    mh, c_new, n_new, m_new = mlstm(mq.reshape(b, s, W_ML), mk.reshape(b, s, W_ML), mv.reshape(b, s, W_ML),
                                    ml_rows, c0, n0, m0, chunk=chunk)
    x1ext, counts = ab_out(fo.reshape(t, W_FOX), mh.reshape(t, W_ML), mo, x, g_ml, w_out, ln_g, ln_b, route_w)
    state = (fk3.reshape(b, s, H_FOX, DH_FOX), fv3.reshape(b, s, H_FOX, DH_FOX), flf,
             c_new, n_new.reshape(b, H_ML, DH_ML), m_new.reshape(b, H_ML))
    return x1ext, counts, state


def mla_layer(x, b, s, wts, route_w, cache):
    w_down, g_q, g_kv, w_uq, wuk_t, wuv, w_out, ln_g, ln_b = wts
    t = b * s
    tm = _row_tile(t)
    past = 0 if cache is None else cache[0].shape[1]
    cc, ss = _rope_tables(s, past, tm)
    cq, ckv, kr, ckv_bf, kr2_bf = mla_down(x, w_down, g_q, g_kv, cc, ss, s)
    q_lat, q_rope = mla_uq(cq, w_uq, wuk_t, cc, ss, s)
    kc = ckv_bf.reshape(b, s, KV_RANK)
    kr2 = kr2_bf.reshape(b, s, LANE)
    if cache is None:
        tq = 256
    else:
        c_ckv, c_kr = cache
        kc = jnp.concatenate([c_ckv.astype(BF16), kc], 1)
        c_kr_bf = c_kr.astype(BF16)
        kr2 = jnp.concatenate([jnp.concatenate([c_kr_bf, c_kr_bf], -1), kr2], 1)
        tq = s
    o_lat = mla_attention(q_lat.reshape(b, s, -1), q_rope.reshape(b, s, -1), kc, kr2, tq=tq,
                          tk=tq if cache is None else 256)
    x1ext, counts = mla_out(o_lat.reshape(t, -1), wuv, w_out, x, ln_g, ln_b, route_w)
    return x1ext, counts, (ckv.reshape(b, s, KV_RANK), kr.reshape(b, s, ROPE))


def _prep_ab_weights(w_in, b_fox_f, b_ml_i, b_ml_f, g_ml, w_out, ln_g, ln_b):
    sizes = (W_FOX, W_FOX, W_FOX, H_FOX, W_ML, W_ML, W_ML, H_ML, H_ML, W_ML)
    idx = np.cumsum(sizes[:-1]).tolist()
    fq, fk, fv, ff, mq, mk, mv, mi, mf, mo = jnp.split(w_in, idx, axis=1)
    w_big = jnp.concatenate([fq, mq, mk, mv, mo, fk, fv], 1).astype(BF16)
    w_gate = jnp.concatenate([ff, mi, mf], 1)
    b_gate = jnp.concatenate([b_fox_f, b_ml_i, b_ml_f])
    return (w_big, w_gate, b_gate, g_ml[None, :], w_out.astype(BF16), ln_g[None, :], ln_b[None, :])


def _prep_mla_weights(w_down, g_q, w_uq, g_kv, w_uk, w_uv, w_out, ln_g, ln_b):
    half = ROPE // 2
    cq_w, ckv_w, kr_w = jnp.split(w_down, [Q_RANK, Q_RANK + KV_RANK], axis=1)
    kr_sw = jnp.concatenate([kr_w[:, half:], kr_w[:, :half]], 1)
    w_down_p = jnp.concatenate([cq_w, ckv_w, kr_w, kr_w, kr_sw, kr_sw], 1).astype(BF16)
    uq = w_uq.reshape(Q_RANK, H_MLA, NOPE + ROPE)
    uq_nope = uq[:, :, :NOPE].reshape(Q_RANK, H_MLA * NOPE)
    uq_rope = uq[:, :, NOPE:]
    uq_rope_sw = jnp.concatenate([uq_rope[..., half:], uq_rope[..., :half]], -1)
    w_uq_p = jnp.concatenate([uq_nope, uq_rope.reshape(Q_RANK, -1), uq_rope_sw.reshape(Q_RANK, -1)], 1).astype(BF16)
    wuk_t = jnp.transpose(w_uk, (1, 2, 0)).astype(BF16)
    wuv = jnp.transpose(w_uv, (1, 0, 2)).astype(BF16)
    return (w_down_p, g_q[None, :], g_kv[None, :], w_uq_p, wuk_t, wuv, w_out.astype(BF16), ln_g[None, :], ln_b[None, :])


def kernel(x_prompt, x_sample, cache_fox_k, cache_fox_v, cache_fox_logf, state_mlstm_c, state_mlstm_n, state_mlstm_m, cache_mla_ckv, cache_mla_krope, w_ab_in, b_fox_f, b_mlstm_i, b_mlstm_f, g_mlstm_norm, w_ab_out, w_mla_down, g_mla_q, w_mla_uq, g_mla_kv, w_mla_uk, w_mla_uv, w_mla_out, ln1_g, ln1_b, ln2_g, ln2_b, w_moe_group, b_moe_group, w_moe_router, b_moe_router, w_exp_gate, w_exp_up, w_exp_down):
    bp, sp, d = x_prompt.shape
    bs, ss_, _ = x_sample.shape
    xp = x_prompt.reshape(bp * sp, d)
    xs = x_sample.reshape(bs * ss_, d)
    ab_p, ab_s, c_p, c_s = [], [], [], []
    for l in range(DEPTH):
        j = l // 2
        route_w = (jnp.concatenate([w_moe_group[l], w_moe_router[l]], 1),
                   jnp.concatenate([b_moe_group[l], b_moe_router[l]]))
        if l % 2 == 0:
            wts = _prep_ab_weights(w_ab_in[j], b_fox_f[j], b_mlstm_i[j], b_mlstm_f[j], g_mlstm_norm[j],
                                   w_ab_out[j], ln1_g[l], ln1_b[l])
            xp1, cnt_p, st_p = ab_layer(xp, bp, sp, wts, route_w, None)
            xs1, cnt_s, st_s = ab_layer(xs, bs, ss_, wts, route_w,
                                        (cache_fox_k[j], cache_fox_v[j], cache_fox_logf[j],
                                         state_mlstm_c[j], state_mlstm_n[j], state_mlstm_m[j]))
            ab_p.append(st_p)
            ab_s.append(st_s)
        else:
            wts = _prep_mla_weights(w_mla_down[j], g_mla_q[j], w_mla_uq[j], g_mla_kv[j], w_mla_uk[j], w_mla_uv[j],
                                    w_mla_out[j], ln1_g[l], ln1_b[l])
            xp1, cnt_p, st_p = mla_layer(xp, bp, sp, wts, route_w, None)
            xs1, cnt_s, st_s = mla_layer(xs, bs, ss_, wts, route_w, (cache_mla_ckv[j], cache_mla_krope[j]))
            c_p.append(st_p)
            c_s.append(st_s)
        moe_w = (w_exp_gate[l].astype(BF16), w_exp_up[l].astype(BF16), w_exp_down[l].astype(BF16))
        xp = hier_moe_ln(xp1, cnt_p, moe_w, ln2_g[l][None, :], ln2_b[l][None, :])
        xs = hier_moe_ln(xs1, cnt_s, moe_w, ln2_g[l][None, :], ln2_b[l][None, :])

    def stack(groups, i):
        return jnp.stack([g[i] for g in groups])

    return (xp.reshape(bp, sp, d), xs.reshape(bs, ss_, d),
            stack(ab_p, 0), stack(ab_p, 1), stack(ab_p, 2), stack(ab_p, 3), stack(ab_p, 4), stack(ab_p, 5),
            stack(c_p, 0), stack(c_p, 1),
            stack(ab_s, 0), stack(ab_s, 1), stack(ab_s, 2), stack(ab_s, 3), stack(ab_s, 4), stack(ab_s, 5),
            stack(c_s, 0), stack(c_s, 1))
```

```python
import functools

import numpy as np
import jax
import jax.numpy as jnp
from jax import lax
from jax.experimental import pallas as pl
from jax.experimental.pallas import tpu as pltpu

F32 = jnp.float32
BF16 = jnp.bfloat16

LANE = 128
VMEM_LIMIT_BYTES = 48 * 1024 * 1024

H_FOX, DH_FOX = 8, 64
W_FOX = H_FOX * DH_FOX
H_ML, DH_ML = 4, 128
W_ML = H_ML * DH_ML
H_MLA, NOPE, ROPE, DV_MLA = 8, 128, 64, 128
Q_RANK, KV_RANK = 384, 256
ROPE_BASE = 10000.0
N_GROUPS, EXP_PER_GROUP, TOP_K = 4, 8, 2
N_EXPERTS = N_GROUPS * EXP_PER_GROUP
DEPTH = 4
ALPHA = (2 * DEPTH) ** 0.25
LN_EPS = 1e-5
RMS_EPS = 1e-6
CHUNK = 64
NEG_BIG = -1e30


def _params(*sem):
    return pltpu.CompilerParams(dimension_semantics=sem, vmem_limit_bytes=VMEM_LIMIT_BYTES)


def _dot(a, b):
    return jnp.dot(a, b, preferred_element_type=F32)


def _dot_nt(a, b):
    return lax.dot_general(a, b, (((1,), (1,)), ((), ())), preferred_element_type=F32)


def _split3(x):
    hi = x.astype(BF16)
    r = x - hi.astype(F32)
    mid = r.astype(BF16)
    lo = (r - mid.astype(F32)).astype(BF16)
    return hi, mid, lo


def _layer_norm(y, g, b):
    mu = jnp.mean(y, axis=1, keepdims=True)
    yc = y - mu
    var = jnp.mean(yc * yc, axis=1, keepdims=True)
    return yc * lax.rsqrt(var + LN_EPS) * g + b


def _sigmoid(x):
    return 1.0 / (1.0 + jnp.exp(-x))


def _row_tile(t):
    return min(512, t)


def _proj_kernel(x_ref, w_ref, *o_refs, groups):
    xb = x_ref[...].astype(BF16)
    refs = iter(o_refs)
    c0 = 0
    for n, dtypes in groups:
        group_refs = [next(refs) for _ in dtypes]
        for a in range(0, n, 512):
            b = min(a + 512, n)
            z = _dot(xb, w_ref[:, c0 + a:c0 + b])
            for o_ref in group_refs:
                o_ref[:, a:b] = z.astype(o_ref.dtype)
        c0 += n


def proj(x, w, groups):
    t, k = x.shape
    tm = _row_tile(t)
    flat = [(n, dt) for n, dts in groups for dt in dts]
    return pl.pallas_call(
        functools.partial(_proj_kernel, groups=tuple(groups)),
        grid=(t // tm,),
        in_specs=[pl.BlockSpec((tm, k), lambda i: (i, 0)), pl.BlockSpec(w.shape, lambda i: (0, 0))],
        out_specs=[pl.BlockSpec((tm, n), lambda i: (i, 0)) for n, _ in flat],
        out_shape=[jax.ShapeDtypeStruct((t, n), dt) for n, dt in flat],
        compiler_params=_params("parallel"),
        name="proj",
    )(x, w)


def _gate_kernel(x_ref, wh_ref, wl_ref, b_ref, mask_ref, o_ref):
    x = x_ref[...]
    xh = x.astype(BF16)
    xl = (x - xh.astype(F32)).astype(BF16)
    z = _dot(xh, wh_ref[...]) + _dot(xh, wl_ref[...]) + _dot(xl, wh_ref[...]) + b_ref[...]
    log_sig = jnp.minimum(z, 0.0) - jnp.log1p(jnp.exp(-jnp.abs(z)))
    o_ref[...] = jnp.where(mask_ref[...] > 0.0, log_sig, z)


def gate_proj(x, w, bias, log_sigmoid_cols):
    t, k = x.shape
    n = w.shape[1]
    tm = _row_tile(t)
    wp = jnp.zeros((k, LANE), F32).at[:, :n].set(w)
    wh = wp.astype(BF16)
    wl = (wp - wh.astype(F32)).astype(BF16)
    bp = jnp.zeros((1, LANE), F32).at[0, :n].set(bias)
    mask = jnp.zeros((1, LANE), F32).at[0, :n].set(jnp.asarray(log_sigmoid_cols, F32))
    row = pl.BlockSpec((1, LANE), lambda i: (0, 0))
    wspec = pl.BlockSpec((k, LANE), lambda i: (0, 0))
    return pl.pallas_call(
        _gate_kernel,
        grid=(t // tm,),
        in_specs=[pl.BlockSpec((tm, k), lambda i: (i, 0)), wspec, wspec, row, row],
        out_specs=pl.BlockSpec((tm, LANE), lambda i: (i, 0)),
        out_shape=jax.ShapeDtypeStruct((t, LANE), F32),
        compiler_params=_params("parallel"),
        name="gate_proj",
    )(x, wh, wl, bp, mask)


def _lanes(x, w):
    if w < LANE:
        return x[:, :w]
    return x if w == LANE else jnp.concatenate([x] * (w // LANE), axis=1)


def _softmax_step(s, at, m_s, l_s, acc_s, v):
    w = s.shape[1]
    m_prev = m_s[at]
    m_new = jnp.maximum(m_prev, jnp.max(s, axis=1, keepdims=True))
    p = jnp.exp(s - _lanes(m_new, w))
    a = jnp.exp(m_prev - m_new)
    l_s[at] = a * l_s[at] + jnp.sum(p, axis=1, keepdims=True)
    acc_s[at] = acc_s[at] * _lanes(a, acc_s.shape[-1]) + _dot(p.astype(BF16), v)
    m_s[at] = m_new


def _fox_kernel(q_ref, k_ref, v_ref, lf_ref, o_ref, ncum_ref, qm_s, m_s, l_s, acc_s, *, tq, tk, q_off, cblk):
    qi = pl.program_id(1)
    sub = cblk // LANE
    n_pairs = H_FOX // 2

    @pl.when(qi == 0)
    def _():
        r = lax.broadcasted_iota(jnp.int32, (cblk, cblk), 0)
        c = lax.broadcasted_iota(jnp.int32, (cblk, cblk), 1)
        upper = jnp.where(r <= c, 1.0, 0.0).astype(BF16)
        carry = jnp.zeros((lf_ref.shape[1], 1), F32)
        for j in range(lf_ref.shape[2] // cblk):
            g1, g2, g3 = _split3(lf_ref[0, :, j * cblk:(j + 1) * cblk])
            cum = _dot(g1, upper) + _dot(g2, upper) + _dot(g3, upper) + carry
            carry = cum[:, cblk - 1:cblk]
            for h in range(H_FOX):
                for u in range(sub):
                    ncum_ref[h, j * sub + u] = -cum[h:h + 1, u * LANE:(u + 1) * LANE]

    lane = lax.broadcasted_iota(jnp.int32, (tq, LANE), 1)
    for hp in range(n_pairs):
        q = q_ref[0, :, hp * LANE:(hp + 1) * LANE] * (DH_FOX ** -0.5)
        zero = jnp.zeros_like(q)
        qm_s[hp, 0:tq, :] = jnp.where(lane < DH_FOX, q, zero)
        qm_s[hp, tq:2 * tq, :] = jnp.where(lane >= DH_FOX, q, zero)
    m_s[...] = jnp.full(m_s.shape, NEG_BIG, F32)
    l_s[...] = jnp.zeros(l_s.shape, F32)
    acc_s[...] = jnp.zeros(acc_s.shape, F32)
    q_start = q_off + qi * tq

    def update(off, w, bias_of_head, mask):
        for hp in range(n_pairs):
            kt = k_ref[0, pl.ds(off, w), hp * LANE:(hp + 1) * LANE]
            vt = v_ref[0, pl.ds(off, w), hp * LANE:(hp + 1) * LANE]
            s = _dot_nt(qm_s[hp], kt)
            s = jnp.concatenate([s[:tq] + bias_of_head(2 * hp), s[tq:] + bias_of_head(2 * hp + 1)], axis=0)
            if mask is not None:
                s = jnp.where(mask, s, -jnp.inf)
            _softmax_step(s, hp, m_s, l_s, acc_s, vt)

    def full_tile(j, carry):
        def bias(h):
            return jnp.concatenate([ncum_ref[h, j * (tk // LANE) + u] for u in range(tk // LANE)], axis=1)
        update(pl.multiple_of(j * tk, tk), tk, bias, None)
        return carry

    lax.fori_loop(0, q_start // tk, full_tile, 0)

    def diag_bias(h):
        if tq >= LANE:
            return jnp.concatenate([ncum_ref[h, q_start // LANE + u] for u in range(tq // LANE)], axis=1)
        lo = q_off % LANE
        return ncum_ref[h, q_start // LANE][:, lo:lo + tq]

    r = lax.broadcasted_iota(jnp.int32, (2 * tq, tq), 0)
    c = lax.broadcasted_iota(jnp.int32, (2 * tq, tq), 1)
    update(pl.multiple_of(q_start, tq), tq, diag_bias, c <= jnp.where(r >= tq, r - tq, r))
    for hp in range(n_pairs):
        o = acc_s[hp] / l_s[hp]
        o_ref[0, :, hp * LANE:(hp + 1) * LANE] = jnp.where(lane < DH_FOX, o[:tq], o[tq:]).astype(o_ref.dtype)


def fox_attention(q, k, v, logf_rows, *, tq, tk=256, cblk=256):
    b, lq, _ = q.shape
    lk = k.shape[1]
    lkp = logf_rows.shape[2]
    q_off = lk - lq
    nq = lq // tq
    assert q_off % tk == 0 and (tq % tk == 0 or lq == tq) and lkp % cblk == 0
    assert tq >= LANE or (nq == 1 and q_off % LANE + tq <= LANE)
    kern = functools.partial(_fox_kernel, tq=tq, tk=tk, q_off=q_off, cblk=cblk)
    return pl.pallas_call(
        kern,
        grid=(b, nq),
        in_specs=[
            pl.BlockSpec((1, tq, W_FOX), lambda bi, qi: (bi, qi, 0)),
            pl.BlockSpec((1, lk, W_FOX), lambda bi, qi: (bi, 0, 0)),
            pl.BlockSpec((1, lk, W_FOX), lambda bi, qi: (bi, 0, 0)),
            pl.BlockSpec((1, logf_rows.shape[1], lkp), lambda bi, qi: (bi, 0, 0)),
        ],
        out_specs=pl.BlockSpec((1, tq, W_FOX), lambda bi, qi: (bi, qi, 0)),
        out_shape=jax.ShapeDtypeStruct((b, lq, W_FOX), BF16),
        scratch_shapes=[pltpu.VMEM((H_FOX, lkp // LANE, 1, LANE), F32), pltpu.VMEM((H_FOX // 2, 2 * tq, LANE), BF16),
                        pltpu.VMEM((H_FOX // 2, 2 * tq, LANE), F32), pltpu.VMEM((H_FOX // 2, 2 * tq, LANE), F32),
                        pltpu.VMEM((H_FOX // 2, 2 * tq, LANE), F32)],
        compiler_params=_params("parallel", "arbitrary"),
        name="fox_attention",
    )(q, k, v, logf_rows)


def _mlstm_kernel(q_ref, k_ref, v_ref, g_ref, c0_ref, n0_ref, m0_ref,
                  h_ref, c_ref, n_ref, m_ref, c_s, n_s, m_s, *, chunk):
    ci = pl.program_id(1)
    L = chunk
    scale = DH_ML ** -0.5

    @pl.when(ci == 0)
    def _():
        c_s[...] = c0_ref[0]
        n_s[...] = n0_ref[0]
        m_s[...] = m0_ref[0]

    r = lax.broadcasted_iota(jnp.int32, (L, L), 0)
    c = lax.broadcasted_iota(jnp.int32, (L, L), 1)
    eye = r == c
    causal = c <= r
    upper = jnp.where(r <= c, 1.0, 0.0).astype(BF16)
    g = g_ref[0]
    g1, g2, g3 = _split3(g)
    cum = _dot(g1, upper) + _dot(g2, upper) + _dot(g3, upper)

    def to_col(row):
        return jnp.sum(jnp.where(eye, row, 0.0), axis=1, keepdims=True)

    for h in range(H_ML):
        sl = slice(h * DH_ML, (h + 1) * DH_ML)
        qh, kh, vh = q_ref[0, :, sl], k_ref[0, :, sl], v_ref[0, :, sl]
        ig_row = g[h:h + 1, :]
        bh_row = cum[H_ML + h:H_ML + h + 1, :]
        bh_col, ig_col = to_col(bh_row), to_col(ig_row)
        m0 = m_s[h]
        c0 = c_s[h]
        n0 = n_s[h]
        logd = jnp.where(causal, bh_col - bh_row + ig_row, -jnp.inf)
        inter = bh_col + m0
        m_col = jnp.maximum(inter, jnp.max(logd, axis=1, keepdims=True))
        d = jnp.exp(logd - m_col)
        a_col = jnp.exp(inter - m_col)
        s = _dot_nt(qh, kh) * scale * d
        num = _dot(s.astype(BF16), vh) + _dot_nt(qh, c0.astype(BF16)) * a_col
        qf = qh.astype(F32)
        den = jnp.sum(s, axis=1, keepdims=True) + a_col * jnp.sum(qf * n0, axis=1, keepdims=True)
        den = jnp.maximum(jnp.abs(den), jnp.exp(-m_col))
        h_ref[0, :, sl] = (num / den).astype(h_ref.dtype)

        m_l = m_col[L - 1:L, :]
        bh_l = bh_row[:, L - 1:L]
        a_l = jnp.exp(bh_l + m0 - m_l)
        w_col = jnp.exp(bh_l - bh_col + ig_col - m_l)
        vw_t = jnp.transpose(vh.astype(F32) * w_col).astype(BF16)
        c_s[h] = a_l * c0 + _dot(vw_t, kh) * scale
        n_s[h] = a_l * n0 + jnp.sum(kh.astype(F32) * w_col, axis=0, keepdims=True) * scale
        m_s[h] = m_l

    @pl.when(ci == pl.num_programs(1) - 1)
    def _():
        c_ref[0] = c_s[...]
        n_ref[0] = n_s[...]
        m_ref[0] = m_s[...]


def mlstm(q, k, v, gate_rows, c0, n0, m0, *, chunk):
    b, s, _ = q.shape
    nc = s // chunk
    seq = pl.BlockSpec((1, chunk, W_ML), lambda bi, ci: (bi, ci, 0))
    c_spec = pl.BlockSpec((1, H_ML, DH_ML, DH_ML), lambda bi, ci: (bi, 0, 0, 0))
    n_spec = pl.BlockSpec((1, H_ML, 1, DH_ML), lambda bi, ci: (bi, 0, 0, 0))
    m_spec = pl.BlockSpec((1, H_ML, 1, 1), lambda bi, ci: (bi, 0, 0, 0))
    return pl.pallas_call(
        functools.partial(_mlstm_kernel, chunk=chunk),
        grid=(b, nc),
        in_specs=[seq, seq, seq, pl.BlockSpec((1, gate_rows.shape[1], chunk), lambda bi, ci: (bi, 0, ci)),
                  c_spec, n_spec, m_spec],
        out_specs=[seq, c_spec, n_spec, m_spec],
        out_shape=[jax.ShapeDtypeStruct((b, s, W_ML), BF16),
                   jax.ShapeDtypeStruct(c0.shape, F32), jax.ShapeDtypeStruct(n0.shape, F32),
                   jax.ShapeDtypeStruct(m0.shape, F32)],
        scratch_shapes=[pltpu.VMEM((H_ML, DH_ML, DH_ML), F32), pltpu.VMEM((H_ML, 1, DH_ML), F32),
                        pltpu.VMEM((H_ML, 1, 1), F32)],
        compiler_params=_params("parallel", "arbitrary"),
        name="mlstm",
    )(q, k, v, gate_rows, c0, n0, m0)


N_PAIRS = EXP_PER_GROUP * (EXP_PER_GROUP - 1) // 2
N_BUCKETS = N_GROUPS * N_PAIRS
assert N_BUCKETS <= LANE
EXT = LANE
FEW_TOKENS = 1024


def _route_store(out, wrh_ref, wrl_ref, br_ref, ltri_ref, o_ref, cnt_ref, cnt_s):
    tm, d = out.shape

    @pl.when(pl.program_id(0) == 0)
    def _():
        cnt_s[...] = jnp.zeros(cnt_s.shape, F32)

    xh = out.astype(BF16)
    xl = (out - xh.astype(F32)).astype(BF16)
    z = _dot(xh, wrh_ref[...]) + _dot(xh, wrl_ref[...]) + _dot(xl, wrh_ref[...]) + br_ref[...]
    lane = lax.broadcasted_iota(jnp.int32, (tm, LANE), 1).astype(F32)
    far = float(LANE)
    neg = -jnp.inf
    gl = jnp.where(lane < N_GROUPS, z, neg)
    gmax = jnp.max(gl, axis=1, keepdims=True)
    g_idx = jnp.min(jnp.where(gl == gmax, lane, far), axis=1, keepdims=True)
    g_gate = 1.0 / jnp.sum(jnp.exp(gl - gmax), axis=1, keepdims=True)
    lo = N_GROUPS + EXP_PER_GROUP * g_idx
    el = jnp.where(jnp.logical_and(lane >= lo, lane < lo + EXP_PER_GROUP), z, neg)
    v1 = jnp.max(el, axis=1, keepdims=True)
    i1 = jnp.min(jnp.where(el == v1, lane, far), axis=1, keepdims=True)
    el2 = jnp.where(lane == i1, neg, el)
    v2 = jnp.max(el2, axis=1, keepdims=True)
    i2 = jnp.min(jnp.where(el2 == v2, lane, far), axis=1, keepdims=True)
    e21 = jnp.exp(v2 - v1)
    w1 = g_gate / (1.0 + e21)
    w2 = w1 * e21
    j1, j2 = i1 - lo, i2 - lo
    ja, jb = jnp.minimum(j1, j2), jnp.maximum(j1, j2)
    pair = ja * (2 * EXP_PER_GROUP - 1 - ja) * 0.5 + (jb - ja - 1.0)
    bucket = g_idx * N_PAIRS + pair
    first_low = j1 < j2
    wa = jnp.where(first_low, w1, w2)
    wb = jnp.where(first_low, w2, w1)
    onehot = jnp.where(lane == bucket, 1.0, 0.0)
    earlier = _dot(ltri_ref[...], onehot.astype(BF16))
    cnt = cnt_s[...]
    rank = jnp.sum(onehot * (earlier + cnt), axis=1, keepdims=True)
    cnt_new = cnt + jnp.sum(onehot, axis=0, keepdims=True)
    cnt_s[...] = cnt_new
    cnt_ref[...] = cnt_new
    o_ref[:, :d] = out
    e_lo = lo - N_GROUPS + ja
    e_hi = lo - N_GROUPS + jb
    record = (bucket, rank, wa, wb, e_lo, e_hi)
    ext = jnp.zeros((tm, LANE), F32)
    for k, col in enumerate(record):
        ext = jnp.where(lane == float(k), col, ext)
    o_ref[:, d:] = ext


def _route_operands(w_route, b_route, tm):
    n = w_route.shape[1]
    wp = jnp.zeros((w_route.shape[0], LANE), F32).at[:, :n].set(w_route)
    wh = wp.astype(BF16)
    wl = (wp - wh.astype(F32)).astype(BF16)
    bp = jnp.zeros((1, LANE), F32).at[0, :n].set(b_route)
    ltri = jnp.asarray(np.tril(np.ones((tm, tm), np.float32), -1), BF16)
    return wh, wl, bp, ltri


def _route_specs(k, tm):
    return [_whole((k, LANE)), _whole((k, LANE)), _whole((1, LANE)), _whole((tm, tm))]


def _ab_out_kernel(fo_ref, mh_ref, mo_ref, x_ref, gml_ref, w_ref, g_ref, b_ref, wrh_ref, wrl_ref, br_ref, ltri_ref,
                   o_ref, cnt_ref, cnt_s):
    parts = [fo_ref[...]]
    for h in range(H_ML):
        sl = slice(h * DH_ML, (h + 1) * DH_ML)
        mh = mh_ref[:, sl].astype(F32)
        ms = jnp.mean(mh * mh, axis=1, keepdims=True)
        nm = mh * lax.rsqrt(ms + RMS_EPS) * gml_ref[:, sl]
        parts.append((nm * _sigmoid(mo_ref[:, sl].astype(F32))).astype(BF16))
    cat = jnp.concatenate(parts, axis=1)
    y = ALPHA * x_ref[...] + _dot(cat, w_ref[...])
    out = _layer_norm(y, g_ref[...], b_ref[...])
    _route_store(out, wrh_ref, wrl_ref, br_ref, ltri_ref, o_ref, cnt_ref, cnt_s)


def _rows(tm, n):
    return pl.BlockSpec((tm, n), lambda i: (i, 0))


def _whole(shape):
    return pl.BlockSpec(shape, lambda i: (0,) * len(shape))


def _routed_out_call(kern, name, t, d, tm, in_specs, operands, route_w):
    w_route, b_route = route_w
    return pl.pallas_call(
        kern,
        grid=(t // tm,),
        in_specs=in_specs + _route_specs(d, tm),
        out_specs=[_rows(tm, d + EXT), _whole((1, LANE))],
        out_shape=[jax.ShapeDtypeStruct((t, d + EXT), F32), jax.ShapeDtypeStruct((1, LANE), F32)],
        scratch_shapes=[pltpu.VMEM((1, LANE), F32)],
        compiler_params=_params("arbitrary"),
        name=name,
    )(*operands, *_route_operands(w_route, b_route, tm))


def ab_out(fo, mh, mo, x, g_ml, w_out, ln_g, ln_b, route_w):
    t, d = x.shape
    tm = _row_tile(t)
    in_specs = [_rows(tm, W_FOX), _rows(tm, W_ML), _rows(tm, W_ML), _rows(tm, d), _whole((1, W_ML)),
                _whole(w_out.shape), _whole((1, d)), _whole((1, d))]
    return _routed_out_call(_ab_out_kernel, "ab_out", t, d, tm, in_specs,
                            (fo, mh, mo, x, g_ml, w_out, ln_g, ln_b), route_w)


def _rms(z, g):
    return z * lax.rsqrt(jnp.mean(z * z, axis=1, keepdims=True) + RMS_EPS) * g


def _mla_down_kernel(x_ref, w_ref, gq_ref, gkv_ref, cc_ref, ss_ref, cq_ref, ckv_ref, kr_ref, ckvb_ref, kr2_ref):
    xb = x_ref[...].astype(BF16)
    cq_ref[...] = _rms(_dot(xb, w_ref[:, 0:Q_RANK]), gq_ref[...]).astype(BF16)
    ckv = _rms(_dot(xb, w_ref[:, Q_RANK:Q_RANK + KV_RANK]), gkv_ref[...])
    ckv_ref[...] = ckv
    ckvb_ref[...] = ckv.astype(BF16)
    c0 = Q_RANK + KV_RANK
    rope = (_dot(xb, w_ref[:, c0:c0 + LANE]) * cc_ref[...] + _dot(xb, w_ref[:, c0 + LANE:c0 + 2 * LANE]) * ss_ref[...])
    kr_ref[...] = rope[:, :ROPE]
    kr2_ref[...] = rope.astype(BF16)


def _table_spec(table, tm, s):
    if table.shape[0] == tm:
        return pl.BlockSpec((tm, LANE), lambda i: (0, 0))
    per = s // tm
    return pl.BlockSpec((tm, LANE), lambda i: (i % per, 0))


def mla_down(x, w, g_q, g_kv, cc, ss, s):
    t, d = x.shape
    tm = _row_tile(t)
    tab = _table_spec(cc, tm, s)
    return pl.pallas_call(
        _mla_down_kernel,
        grid=(t // tm,),
        in_specs=[_rows(tm, d), _whole(w.shape), _whole((1, Q_RANK)), _whole((1, KV_RANK)), tab, tab],
        out_specs=[_rows(tm, Q_RANK), _rows(tm, KV_RANK), _rows(tm, ROPE), _rows(tm, KV_RANK), _rows(tm, LANE)],
        out_shape=[jax.ShapeDtypeStruct((t, Q_RANK), BF16), jax.ShapeDtypeStruct((t, KV_RANK), F32),
                   jax.ShapeDtypeStruct((t, ROPE), F32), jax.ShapeDtypeStruct((t, KV_RANK), BF16),
                   jax.ShapeDtypeStruct((t, LANE), BF16)],
        compiler_params=_params("parallel"),
        name="mla_down",
    )(x, w, g_q, g_kv, cc, ss)


def _mla_uq_kernel(cq_ref, w_ref, wuk_ref, cc_ref, ss_ref, qlat_ref, qrope_ref):
    cq = cq_ref[...]
    scale = (NOPE + ROPE) ** -0.5
    r0 = H_MLA * NOPE
    r1 = r0 + H_MLA * ROPE
    qn = _dot(cq, w_ref[:, :r0]).astype(BF16)
    for h in range(H_MLA):
        qlat_ref[:, h * KV_RANK:(h + 1) * KV_RANK] = (
            _dot(qn[:, h * NOPE:(h + 1) * NOPE], wuk_ref[h]) * scale).astype(BF16)
    qr = _dot(cq, w_ref[:, r0:])
    cc, ss = cc_ref[...] * scale, ss_ref[...] * scale
    for p in range(H_MLA // 2):
        a, b = p * LANE, (r1 - r0) + p * LANE
        qrope_ref[:, p * LANE:(p + 1) * LANE] = (qr[:, a:a + LANE] * cc + qr[:, b:b + LANE] * ss).astype(BF16)


def mla_uq(cq, w, wuk_t, cc, ss, s):
    t = cq.shape[0]
    tm = _row_tile(t)
    tab = _table_spec(cc, tm, s)
    return pl.pallas_call(
        _mla_uq_kernel,
        grid=(t // tm,),
        in_specs=[_rows(tm, Q_RANK), _whole(w.shape), _whole(wuk_t.shape), tab, tab],
        out_specs=[_rows(tm, H_MLA * KV_RANK), _rows(tm, H_MLA * ROPE)],
        out_shape=[jax.ShapeDtypeStruct((t, H_MLA * KV_RANK), BF16), jax.ShapeDtypeStruct((t, H_MLA * ROPE), BF16)],
        compiler_params=_params("parallel"),
        name="mla_uq",
    )(cq, w, wuk_t, cc, ss)


def _mla_attn_kernel(ql_ref, qr_ref, kc_ref, kr_ref, o_ref, ql_s, qr_s, m_s, l_s, acc_s, *, tq, tk, q_off):
    qi = pl.program_id(1)
    lane = lax.broadcasted_iota(jnp.int32, (tq, LANE), 1)
    for h in range(H_MLA):
        rows = slice(h * tq, (h + 1) * tq)
        ql_s[rows, :] = ql_ref[0, :, h * KV_RANK:(h + 1) * KV_RANK]
        qr = qr_ref[0, :, (h // 2) * LANE:(h // 2 + 1) * LANE]
        keep = (lane < ROPE) if h % 2 == 0 else (lane >= ROPE)
        qr_s[rows, :] = jnp.where(keep, qr, jnp.zeros_like(qr))
    m_s[...] = jnp.full(m_s.shape, NEG_BIG, F32)
    l_s[...] = jnp.zeros(l_s.shape, F32)
    acc_s[...] = jnp.zeros(acc_s.shape, F32)
    q_start = q_off + qi * tq

    def update(off, w, mask):
        kc = kc_ref[0, pl.ds(off, w), :]
        kr = kr_ref[0, pl.ds(off, w), :]
        s = _dot_nt(ql_s[...], kc) + _dot_nt(qr_s[...], kr)
        if mask is not None:
            s = jnp.where(mask, s, -jnp.inf)
        _softmax_step(s, Ellipsis, m_s, l_s, acc_s, kc)

    def full_tile(j, carry):
        update(pl.multiple_of(j * tk, tk), tk, None)
        return carry

    lax.fori_loop(0, q_start // tk, full_tile, 0)
    r = lax.broadcasted_iota(jnp.int32, (H_MLA * tq, tq), 0)
    c = lax.broadcasted_iota(jnp.int32, (H_MLA * tq, tq), 1)
    update(pl.multiple_of(q_start, tq), tq, (c // CHUNK) <= ((r & (tq - 1)) // CHUNK))
    for h in range(H_MLA):
        rows = slice(h * tq, (h + 1) * tq)
        o = acc_s[rows, :] / _lanes(l_s[rows, :], KV_RANK)
        o_ref[0, :, h * KV_RANK:(h + 1) * KV_RANK] = o.astype(o_ref.dtype)


def mla_attention(q_lat, q_rope, kc, kr2, *, tq, tk):
    b, lq, _ = q_lat.shape
    lk = kc.shape[1]
    q_off = lk - lq
    assert q_off % tk == 0 and (tq % tk == 0 or lq == tq) and tq % CHUNK == 0 and (tq & (tq - 1)) == 0
    kern = functools.partial(_mla_attn_kernel, tq=tq, tk=tk, q_off=q_off)
    return pl.pallas_call(
        kern,
        grid=(b, lq // tq),
        in_specs=[
            pl.BlockSpec((1, tq, H_MLA * KV_RANK), lambda bi, qi: (bi, qi, 0)),
            pl.BlockSpec((1, tq, H_MLA * ROPE), lambda bi, qi: (bi, qi, 0)),
            pl.BlockSpec((1, lk, KV_RANK), lambda bi, qi: (bi, 0, 0)),
            pl.BlockSpec((1, lk, LANE), lambda bi, qi: (bi, 0, 0)),
        ],
        out_specs=pl.BlockSpec((1, tq, H_MLA * KV_RANK), lambda bi, qi: (bi, qi, 0)),
        out_shape=jax.ShapeDtypeStruct((b, lq, H_MLA * KV_RANK), BF16),
        scratch_shapes=[pltpu.VMEM((H_MLA * tq, KV_RANK), BF16), pltpu.VMEM((H_MLA * tq, LANE), BF16),
                        pltpu.VMEM((H_MLA * tq, LANE), F32), pltpu.VMEM((H_MLA * tq, LANE), F32),
                        pltpu.VMEM((H_MLA * tq, KV_RANK), F32)],
        compiler_params=_params("parallel", "arbitrary"),
        name="mla_attention",
    )(q_lat, q_rope, kc, kr2)


def _mla_out_kernel(ol_ref, wuv_ref, w_ref, x_ref, g_ref, b_ref, wrh_ref, wrl_ref, br_ref, ltri_ref,
                    o_ref, cnt_ref, cnt_s):
    parts = [_dot(ol_ref[:, h * KV_RANK:(h + 1) * KV_RANK], wuv_ref[h]).astype(BF16) for h in range(H_MLA)]
    y = ALPHA * x_ref[...] + _dot(jnp.concatenate(parts, axis=1), w_ref[...])
    out = _layer_norm(y, g_ref[...], b_ref[...])
    _route_store(out, wrh_ref, wrl_ref, br_ref, ltri_ref, o_ref, cnt_ref, cnt_s)


def mla_out(o_lat, wuv, w_out, x, ln_g, ln_b, route_w):
    t, d = x.shape
    tm = _row_tile(t)
    in_specs = [_rows(tm, H_MLA * KV_RANK), _whole(wuv.shape), _whole(w_out.shape), _rows(tm, d),
                _whole((1, d)), _whole((1, d))]
    return _routed_out_call(_mla_out_kernel, "mla_out", t, d, tm, in_specs,
                            (o_lat, wuv, w_out, x, ln_g, ln_b), route_w)


ROW_SLOTS = 3


def _row_scatter_kernel(idx_ref, src_ref, dst_init_ref, dst_ref, buf, in_sem, out_sem, *, rows, n_steps):
    del dst_init_ref

    def load(i, slot):
        return pltpu.make_async_copy(src_ref.at[pl.ds(pl.multiple_of(i * rows, rows), rows)], buf.at[slot],
                                     in_sem.at[slot])

    def rows_done(slot):
        return pltpu.make_async_copy(buf.at[slot], dst_ref.at[pl.ds(0, rows)], out_sem.at[slot])

    i = pl.program_id(0)
    slot = i % ROW_SLOTS
    nxt = (i + 1) % ROW_SLOTS

    @pl.when(i == 0)
    def _():
        load(0, 0).start()

    @pl.when(i + 1 < n_steps)
    def _():
        @pl.when(i + 1 >= ROW_SLOTS)
        def _():
            rows_done(nxt).wait()
        load(i + 1, nxt).start()

    load(i, slot).wait()
    base = i * rows

    def issue(r, c):
        pltpu.make_async_copy(buf.at[slot, pl.ds(r, 1)], dst_ref.at[pl.ds(idx_ref[base + r], 1)],
                              out_sem.at[slot]).start()
        return c

    lax.fori_loop(0, rows, issue, 0, unroll=8)

    @pl.when(i == n_steps - 1)
    def _():
        for k in range(min(ROW_SLOTS, n_steps)):
            rows_done((n_steps - 1 - k) % ROW_SLOTS).wait()


def _row_gather_kernel(idx_ref, src_ref, dst_ref, buf, row_sem, out_sem, *, rows, n_steps):
    def rows_done(slot):
        return pltpu.make_async_copy(src_ref.at[pl.ds(0, rows)], buf.at[slot], row_sem.at[slot])

    def store(i, slot):
        return pltpu.make_async_copy(buf.at[slot], dst_ref.at[pl.ds(pl.multiple_of(i * rows, rows), rows)],
                                     out_sem.at[slot])

    i = pl.program_id(0)
    slot = i % ROW_SLOTS

    @pl.when(i >= ROW_SLOTS)
    def _():
        store(i - ROW_SLOTS, slot).wait()

    base = i * rows

    def issue(r, c):
        pltpu.make_async_copy(src_ref.at[pl.ds(idx_ref[base + r], 1)], buf.at[slot, pl.ds(r, 1)],
                              row_sem.at[slot]).start()
        return c

    lax.fori_loop(0, rows, issue, 0, unroll=8)

    @pl.when(i >= 1)
    def _():
        prev = (i + ROW_SLOTS - 1) % ROW_SLOTS
        rows_done(prev).wait()
        store(i - 1, prev).start()

    @pl.when(i == n_steps - 1)
    def _():
        last = (n_steps - 1) % ROW_SLOTS
        rows_done(last).wait()
        store(n_steps - 1, last).start()
        for k in range(min(ROW_SLOTS, n_steps)):
            store(n_steps - 1 - k, (n_steps - 1 - k) % ROW_SLOTS).wait()


def row_move(idx, src, dst_rows, *, scatter, dst_init=None):
    t = idx.shape[0]
    width = src.shape[1]
    rows = min(512, t)
    any_spec = pl.BlockSpec(memory_space=pl.ANY)
    operands = [idx, src] + ([dst_init] if scatter else [])
    grid_spec = pltpu.PrefetchScalarGridSpec(
        num_scalar_prefetch=1, grid=(t // rows,), in_specs=[any_spec] * (len(operands) - 1), out_specs=any_spec,
        scratch_shapes=[pltpu.VMEM((ROW_SLOTS, rows, width), src.dtype),
                        pltpu.SemaphoreType.DMA((ROW_SLOTS,)), pltpu.SemaphoreType.DMA((ROW_SLOTS,))])
    kern = _row_scatter_kernel if scatter else _row_gather_kernel
    return pl.pallas_call(
        functools.partial(kern, rows=rows, n_steps=t // rows),
        grid_spec=grid_spec,
        out_shape=jax.ShapeDtypeStruct((dst_rows, width), src.dtype),
        input_output_aliases={2: 0} if scatter else {},
        compiler_params=_params("arbitrary"),
        name="row_scatter" if scatter else "row_gather",
    )(*operands)


def _moe_kernel(ta_ref, tb_ref, nu_ref, xs_ref, w1a_ref, w3a_ref, w2a_ref, w1b_ref, w3b_ref, w2b_ref,
                g_ref, b_ref, o_ref):
    t = pl.program_id(0)
    d = o_ref.shape[1]

    @pl.when(t < nu_ref[0])
    def _():
        x = xs_ref[:, :d]
        ext = xs_ref[:, d:]
        xb = x.astype(BF16)

        def ffn(w1_ref, w3_ref, w2_ref):
            h1 = _dot(xb, w1_ref[0])
            h3 = _dot(xb, w3_ref[0])
            return _dot((h1 * _sigmoid(h1) * h3).astype(BF16), w2_ref[0])

        y = ext[:, 2:3] * ffn(w1a_ref, w3a_ref, w2a_ref) + ext[:, 3:4] * ffn(w1b_ref, w3b_ref, w2b_ref)
        o_ref[...] = _layer_norm(ALPHA * x + y, g_ref[...], b_ref[...])

    @pl.when(t >= nu_ref[0])
    def _():
        o_ref[...] = jnp.zeros(o_ref.shape, o_ref.dtype)


def moe_ln(tile_ea, tile_eb, n_used, xs, w1, w3, w2, ln_g, ln_b, tm):
    p, wide = xs.shape
    d = wide - EXT
    de = w1.shape[2]

    def w_spec(shape, which):
        return pl.BlockSpec(shape, lambda t, ta, tb, nu: ((ta, tb)[which][t], 0, 0))

    grid_spec = pltpu.PrefetchScalarGridSpec(
        num_scalar_prefetch=3,
        grid=(p // tm,),
        in_specs=[pl.BlockSpec((tm, wide), lambda t, ta, tb, nu: (t, 0)),
                  w_spec((1, d, de), 0), w_spec((1, d, de), 0), w_spec((1, de, d), 0),
                  w_spec((1, d, de), 1), w_spec((1, d, de), 1), w_spec((1, de, d), 1),
                  pl.BlockSpec((1, d), lambda t, ta, tb, nu: (0, 0)),
                  pl.BlockSpec((1, d), lambda t, ta, tb, nu: (0, 0))],
        out_specs=pl.BlockSpec((tm, d), lambda t, ta, tb, nu: (t, 0)),
    )
    return pl.pallas_call(
        _moe_kernel,
        grid_spec=grid_spec,
        out_shape=jax.ShapeDtypeStruct((p, d), F32),
        compiler_params=_params("arbitrary"),
        name="moe_ln",
    )(tile_ea, tile_eb, n_used, xs, w1, w3, w2, w1, w3, w2, ln_g, ln_b)


def _moe_few_kernel(x_ref, w1_ref, w3_ref, w2_ref, g_ref, b_ref, o_ref, acc_s):
    e = pl.program_id(0)
    d = o_ref.shape[1]

    @pl.when(e == 0)
    def _():
        acc_s[...] = jnp.zeros(acc_s.shape, F32)

    x = x_ref[:, :d]
    ext = x_ref[:, d:]
    xb = x.astype(BF16)
    h1 = _dot(xb, w1_ref[0])
    h3 = _dot(xb, w3_ref[0])
    y = _dot((h1 * _sigmoid(h1) * h3).astype(BF16), w2_ref[0])
    ef = e.astype(F32)
    gate = jnp.where(ext[:, 4:5] == ef, ext[:, 2:3], 0.0) + jnp.where(ext[:, 5:6] == ef, ext[:, 3:4], 0.0)
    acc_s[...] += gate * y

    @pl.when(e == pl.num_programs(0) - 1)
    def _():
        o_ref[...] = _layer_norm(ALPHA * x + acc_s[...], g_ref[...], b_ref[...])


def moe_ln_few(x1ext, w1, w3, w2, ln_g, ln_b):
    t, wide = x1ext.shape
    d = wide - EXT
    de = w1.shape[2]
    return pl.pallas_call(
        _moe_few_kernel,
        grid=(w1.shape[0],),
        in_specs=[_whole((t, wide)), pl.BlockSpec((1, d, de), lambda e: (e, 0, 0)),
                  pl.BlockSpec((1, d, de), lambda e: (e, 0, 0)), pl.BlockSpec((1, de, d), lambda e: (e, 0, 0)),
                  _whole((1, d)), _whole((1, d))],
        out_specs=_whole((t, d)),
        out_shape=jax.ShapeDtypeStruct((t, d), F32),
        scratch_shapes=[pltpu.VMEM((t, d), F32)],
        compiler_params=_params("arbitrary"),
        name="moe_ln_few",
    )(x1ext, w1, w3, w2, ln_g, ln_b)


def _bucket_experts():
    ea, eb = [], []
    for g in range(N_GROUPS):
        for ja in range(EXP_PER_GROUP):
            for jb in range(ja + 1, EXP_PER_GROUP):
                ea.append(g * EXP_PER_GROUP + ja)
                eb.append(g * EXP_PER_GROUP + jb)
    return np.asarray(ea, np.int32), np.asarray(eb, np.int32)


def _bucket_layout(x1ext, counts, tm):
    t, wide = x1ext.shape
    d = wide - EXT
    bucket = x1ext[:, d].astype(jnp.int32)
    rank = x1ext[:, d + 1].astype(jnp.int32)
    counts = counts[0, :N_BUCKETS].astype(jnp.int32)
    padded = ((counts + tm - 1) // tm) * tm
    row_end = jnp.cumsum(padded)
    row_start = row_end - padded
    ids = jnp.arange(N_BUCKETS, dtype=jnp.int32)
    pos = rank + jnp.sum(jnp.where(bucket[:, None] == ids[None, :], row_start[None, :], 0), axis=1)
    n_rows = _round_up(t, tm) + N_BUCKETS * tm
    n_tiles = n_rows // tm
    n_used = (row_end[-1] // tm).astype(jnp.int32)
    tile_bucket = jnp.searchsorted(row_end, jnp.arange(n_tiles, dtype=jnp.int32) * tm, side="right")
    last_bucket = jnp.searchsorted(row_end, (n_used - 1) * tm, side="right")
    tile_bucket = jnp.minimum(tile_bucket, last_bucket).astype(jnp.int32)
    ea, eb = _bucket_experts()
    return pos.astype(jnp.int32), jnp.asarray(ea)[tile_bucket], jnp.asarray(eb)[tile_bucket], n_used.reshape(1), n_rows


def hier_moe_ln(x1ext, counts, moe_w, ln_g, ln_b):
    w1, w3, w2 = moe_w
    t, wide = x1ext.shape
    if t <= FEW_TOKENS:
        return moe_ln_few(x1ext, w1, w3, w2, ln_g, ln_b)
    tm = 256
    pos, tile_ea, tile_eb, n_used, n_rows = _bucket_layout(x1ext, counts, tm)
    xs = row_move(pos, x1ext, n_rows, scatter=True, dst_init=jnp.zeros((n_rows, wide), F32))
    ys = moe_ln(tile_ea, tile_eb, n_used, xs, w1, w3, w2, ln_g, ln_b, tm)
    return row_move(pos, ys, t, scatter=False)


def _rope_tables(s, offset, tm):
    half = ROPE // 2
    inv_freq = ROPE_BASE ** (-jnp.arange(half, dtype=F32) / half)
    ang = (offset + jnp.arange(s)).astype(F32)[:, None] * inv_freq[None, :]
    cos, sin = jnp.cos(ang), jnp.sin(ang)
    cc = jnp.concatenate([cos, cos, cos, cos], -1)
    ss = jnp.concatenate([-sin, sin, -sin, sin], -1)
    if tm > s:
        cc, ss = jnp.tile(cc, (tm // s, 1)), jnp.tile(ss, (tm // s, 1))
    return cc, ss


def _round_up(n, m):
    return (n + m - 1) // m * m


def ab_layer(x, b, s, wts, route_w, cache):
    w_big, w_gate, b_gate, g_ml, w_out, ln_g, ln_b = wts
    t = b * s
    fq, mq, mk, mv, mo, fk, fk_bf, fv, fv_bf = proj(
        x, w_big, [(W_FOX, (BF16,)), (W_ML, (BF16,)), (W_ML, (BF16,)), (W_ML, (BF16,)), (W_ML, (BF16,)),
                   (W_FOX, (F32, BF16)), (W_FOX, (F32, BF16))])
    ls_cols = np.array([1.0] * H_FOX + [0.0] * H_ML + [1.0] * H_ML)
    gates = gate_proj(x, w_gate, b_gate, ls_cols)
    flf = gates[:, :H_FOX].reshape(b, s, H_FOX)
    ml_rows = jnp.swapaxes(gates[:, H_FOX:H_FOX + 2 * H_ML].reshape(b, s, 2 * H_ML), 1, 2)
    ml_rows = jnp.pad(ml_rows, ((0, 0), (0, 16 - 2 * H_ML), (0, 0)))
    fk3, fv3 = fk.reshape(b, s, W_FOX), fv.reshape(b, s, W_FOX)
    if cache is None:
        k_all, v_all, lf_all = fk_bf.reshape(b, s, W_FOX), fv_bf.reshape(b, s, W_FOX), flf
        c0 = jnp.zeros((b, H_ML, DH_ML, DH_ML), F32)
        n0 = jnp.zeros((b, H_ML, 1, DH_ML), F32)
        m0 = jnp.zeros((b, H_ML, 1, 1), F32)
        tq, chunk = 256, 256
    else:
        ck, cv, clf, c0, n0, m0 = cache
        past = ck.shape[1]
        k_all = jnp.concatenate([ck.reshape(b, past, W_FOX).astype(BF16), fk_bf.reshape(b, s, W_FOX)], 1)
        v_all = jnp.concatenate([cv.reshape(b, past, W_FOX).astype(BF16), fv_bf.reshape(b, s, W_FOX)], 1)
        lf_all = jnp.concatenate([clf, flf], 1)
        n0 = n0.reshape(b, H_ML, 1, DH_ML)
        m0 = m0.reshape(b, H_ML, 1, 1)
        tq, chunk = s, s
    lk = k_all.shape[1]
    lf_rows = jnp.pad(jnp.swapaxes(lf_all, 1, 2), ((0, 0), (0, 16 - H_FOX), (0, _round_up(lk, 256) - lk)))
    fo = fox_attention(fq.reshape(b, s, W_FOX), k_all, v_all, lf_rows, tq=tq)
    mh, c_new, n_new, m_new = mlstm(mq.reshape(b, s, W_ML), mk.reshape(b, s, W_ML), mv.reshape(b, s, W_ML),
                                    ml_rows, c0, n0, m0, chunk=chunk)
    x1ext, counts = ab_out(fo.reshape(t, W_FOX), mh.reshape(t, W_ML), mo, x, g_ml, w_out, ln_g, ln_b, route_w)
    state = (fk3.reshape(b, s, H_FOX, DH_FOX), fv3.reshape(b, s, H_FOX, DH_FOX), flf,
             c_new, n_new.reshape(b, H_ML, DH_ML), m_new.reshape(b, H_ML))
    return x1ext, counts, state


def mla_layer(x, b, s, wts, route_w, cache):
    w_down, g_q, g_kv, w_uq, wuk_t, wuv, w_out, ln_g, ln_b = wts
    t = b * s
    tm = _row_tile(t)
    past = 0 if cache is None else cache[0].shape[1]
    cc, ss = _rope_tables(s, past, tm)
    cq, ckv, kr, ckv_bf, kr2_bf = mla_down(x, w_down, g_q, g_kv, cc, ss, s)
    q_lat, q_rope = mla_uq(cq, w_uq, wuk_t, cc, ss, s)
    kc = ckv_bf.reshape(b, s, KV_RANK)
    kr2 = kr2_bf.reshape(b, s, LANE)
    if cache is None:
        tq = 256
    else:
        c_ckv, c_kr = cache
        kc = jnp.concatenate([c_ckv.astype(BF16), kc], 1)
        c_kr_bf = c_kr.astype(BF16)
        kr2 = jnp.concatenate([jnp.concatenate([c_kr_bf, c_kr_bf], -1), kr2], 1)
        tq = s
    o_lat = mla_attention(q_lat.reshape(b, s, -1), q_rope.reshape(b, s, -1), kc, kr2, tq=tq,
                          tk=tq if cache is None else 256)
    x1ext, counts = mla_out(o_lat.reshape(t, -1), wuv, w_out, x, ln_g, ln_b, route_w)
    return x1ext, counts, (ckv.reshape(b, s, KV_RANK), kr.reshape(b, s, ROPE))


def _prep_ab_weights(w_in, b_fox_f, b_ml_i, b_ml_f, g_ml, w_out, ln_g, ln_b):
    sizes = (W_FOX, W_FOX, W_FOX, H_FOX, W_ML, W_ML, W_ML, H_ML, H_ML, W_ML)
    idx = np.cumsum(sizes[:-1]).tolist()
    fq, fk, fv, ff, mq, mk, mv, mi, mf, mo = jnp.split(w_in, idx, axis=1)
    w_big = jnp.concatenate([fq, mq, mk, mv, mo, fk, fv], 1).astype(BF16)
    w_gate = jnp.concatenate([ff, mi, mf], 1)
    b_gate = jnp.concatenate([b_fox_f, b_ml_i, b_ml_f])
    return (w_big, w_gate, b_gate, g_ml[None, :], w_out.astype(BF16), ln_g[None, :], ln_b[None, :])


def _prep_mla_weights(w_down, g_q, w_uq, g_kv, w_uk, w_uv, w_out, ln_g, ln_b):
    half = ROPE // 2
    cq_w, ckv_w, kr_w = jnp.split(w_down, [Q_RANK, Q_RANK + KV_RANK], axis=1)
    kr_sw = jnp.concatenate([kr_w[:, half:], kr_w[:, :half]], 1)
    w_down_p = jnp.concatenate([cq_w, ckv_w, kr_w, kr_w, kr_sw, kr_sw], 1).astype(BF16)
    uq = w_uq.reshape(Q_RANK, H_MLA, NOPE + ROPE)
    uq_nope = uq[:, :, :NOPE].reshape(Q_RANK, H_MLA * NOPE)
    uq_rope = uq[:, :, NOPE:]
    uq_rope_sw = jnp.concatenate([uq_rope[..., half:], uq_rope[..., :half]], -1)
    w_uq_p = jnp.concatenate([uq_nope, uq_rope.reshape(Q_RANK, -1), uq_rope_sw.reshape(Q_RANK, -1)], 1).astype(BF16)
    wuk_t = jnp.transpose(w_uk, (1, 2, 0)).astype(BF16)
    wuv = jnp.transpose(w_uv, (1, 0, 2)).astype(BF16)
    return (w_down_p, g_q[None, :], g_kv[None, :], w_uq_p, wuk_t, wuv, w_out.astype(BF16), ln_g[None, :], ln_b[None, :])


def kernel(x_prompt, x_sample, cache_fox_k, cache_fox_v, cache_fox_logf, state_mlstm_c, state_mlstm_n, state_mlstm_m, cache_mla_ckv, cache_mla_krope, w_ab_in, b_fox_f, b_mlstm_i, b_mlstm_f, g_mlstm_norm, w_ab_out, w_mla_down, g_mla_q, w_mla_uq, g_mla_kv, w_mla_uk, w_mla_uv, w_mla_out, ln1_g, ln1_b, ln2_g, ln2_b, w_moe_group, b_moe_group, w_moe_router, b_moe_router, w_exp_gate, w_exp_up, w_exp_down):
    bp, sp, d = x_prompt.shape
    bs, ss_, _ = x_sample.shape
    xp = x_prompt.reshape(bp * sp, d)
    xs = x_sample.reshape(bs * ss_, d)
    ab_p, ab_s, c_p, c_s = [], [], [], []
    for l in range(DEPTH):
        j = l // 2
        route_w = (jnp.concatenate([w_moe_group[l], w_moe_router[l]], 1),
                   jnp.concatenate([b_moe_group[l], b_moe_router[l]]))
        if l % 2 == 0:
            wts = _prep_ab_weights(w_ab_in[j], b_fox_f[j], b_mlstm_i[j], b_mlstm_f[j], g_mlstm_norm[j],
                                   w_ab_out[j], ln1_g[l], ln1_b[l])
            xp1, cnt_p, st_p = ab_layer(xp, bp, sp, wts, route_w, None)
            xs1, cnt_s, st_s = ab_layer(xs, bs, ss_, wts, route_w,
                                        (cache_fox_k[j], cache_fox_v[j], cache_fox_logf[j],
                                         state_mlstm_c[j], state_mlstm_n[j], state_mlstm_m[j]))
            ab_p.append(st_p)
            ab_s.append(st_s)
        else:
            wts = _prep_mla_weights(w_mla_down[j], g_mla_q[j], w_mla_uq[j], g_mla_kv[j], w_mla_uk[j], w_mla_uv[j],
                                    w_mla_out[j], ln1_g[l], ln1_b[l])
            xp1, cnt_p, st_p = mla_layer(xp, bp, sp, wts, route_w, None)
            xs1, cnt_s, st_s = mla_layer(xs, bs, ss_, wts, route_w, (cache_mla_ckv[j], cache_mla_krope[j]))
            c_p.append(st_p)
            c_s.append(st_s)
        moe_w = (w_exp_gate[l].astype(BF16), w_exp_up[l].astype(BF16), w_exp_down[l].astype(BF16))
        xp = hier_moe_ln(xp1, cnt_p, moe_w, ln2_g[l][None, :], ln2_b[l][None, :])
        xs = hier_moe_ln(xs1, cnt_s, moe_w, ln2_g[l][None, :], ln2_b[l][None, :])

    def stack(groups, i):
        return jnp.stack([g[i] for g in groups])

    return (xp.reshape(bp, sp, d), xs.reshape(bs, ss_, d),
            stack(ab_p, 0), stack(ab_p, 1), stack(ab_p, 2), stack(ab_p, 3), stack(ab_p, 4), stack(ab_p, 5),
            stack(c_p, 0), stack(c_p, 1),
            stack(ab_s, 0), stack(ab_s, 1), stack(ab_s, 2), stack(ab_s, 3), stack(ab_s, 4), stack(ab_s, 5),
            stack(c_s, 0), stack(c_s, 1))
```

```python
import functools

import numpy as np
import jax
import jax.numpy as jnp
from jax import lax
from jax.experimental import pallas as pl
from jax.experimental.pallas import tpu as pltpu

F32 = jnp.float32
BF16 = jnp.bfloat16

LANE = 128
VMEM_LIMIT_BYTES = 48 * 1024 * 1024

H_FOX, DH_FOX = 8, 64
W_FOX = H_FOX * DH_FOX
H_ML, DH_ML = 4, 128
W_ML = H_ML * DH_ML
H_MLA, NOPE, ROPE, DV_MLA = 8, 128, 64, 128
Q_RANK, KV_RANK = 384, 256
ROPE_BASE = 10000.0
N_GROUPS, EXP_PER_GROUP, TOP_K = 4, 8, 2
N_EXPERTS = N_GROUPS * EXP_PER_GROUP
DEPTH = 4
ALPHA = (2 * DEPTH) ** 0.25
LN_EPS = 1e-5
RMS_EPS = 1e-6
CHUNK = 64
NEG_BIG = -1e30


def _params(*sem):
    return pltpu.CompilerParams(dimension_semantics=sem, vmem_limit_bytes=VMEM_LIMIT_BYTES)


def _dot(a, b):
    return jnp.dot(a, b, preferred_element_type=F32)


def _dot_nt(a, b):
    return lax.dot_general(a, b, (((1,), (1,)), ((), ())), preferred_element_type=F32)


def _split3(x):
    hi = x.astype(BF16)
    r = x - hi.astype(F32)
    mid = r.astype(BF16)
    lo = (r - mid.astype(F32)).astype(BF16)
    return hi, mid, lo


def _layer_norm(y, g, b):
    mu = jnp.mean(y, axis=1, keepdims=True)
    yc = y - mu
    var = jnp.mean(yc * yc, axis=1, keepdims=True)
    return yc * lax.rsqrt(var + LN_EPS) * g + b


def _sigmoid(x):
    return 1.0 / (1.0 + jnp.exp(-x))


def _row_tile(t):
    return min(512, t)


def _proj_kernel(x_ref, w_ref, *o_refs, groups):
    xb = x_ref[...].astype(BF16)
    refs = iter(o_refs)
    c0 = 0
    for n, dtypes in groups:
        group_refs = [next(refs) for _ in dtypes]
        for a in range(0, n, 512):
            b = min(a + 512, n)
            z = _dot(xb, w_ref[:, c0 + a:c0 + b])
            for o_ref in group_refs:
                o_ref[:, a:b] = z.astype(o_ref.dtype)
        c0 += n


def proj(x, w, groups):
    t, k = x.shape
    tm = _row_tile(t)
    flat = [(n, dt) for n, dts in groups for dt in dts]
    return pl.pallas_call(
        functools.partial(_proj_kernel, groups=tuple(groups)),
        grid=(t // tm,),
        in_specs=[pl.BlockSpec((tm, k), lambda i: (i, 0)), pl.BlockSpec(w.shape, lambda i: (0, 0))],
        out_specs=[pl.BlockSpec((tm, n), lambda i: (i, 0)) for n, _ in flat],
        out_shape=[jax.ShapeDtypeStruct((t, n), dt) for n, dt in flat],
        compiler_params=_params("parallel"),
        name="proj",
    )(x, w)


def _gate_kernel(x_ref, wh_ref, wl_ref, b_ref, mask_ref, o_ref):
    x = x_ref[...]
    xh = x.astype(BF16)
    xl = (x - xh.astype(F32)).astype(BF16)
    z = _dot(xh, wh_ref[...]) + _dot(xh, wl_ref[...]) + _dot(xl, wh_ref[...]) + b_ref[...]
    log_sig = jnp.minimum(z, 0.0) - jnp.log1p(jnp.exp(-jnp.abs(z)))
    o_ref[...] = jnp.where(mask_ref[...] > 0.0, log_sig, z)


def gate_proj(x, w, bias, log_sigmoid_cols):
    t, k = x.shape
    n = w.shape[1]
    tm = _row_tile(t)
    wp = jnp.zeros((k, LANE), F32).at[:, :n].set(w)
    wh = wp.astype(BF16)
    wl = (wp - wh.astype(F32)).astype(BF16)
    bp = jnp.zeros((1, LANE), F32).at[0, :n].set(bias)
    mask = jnp.zeros((1, LANE), F32).at[0, :n].set(jnp.asarray(log_sigmoid_cols, F32))
    row = pl.BlockSpec((1, LANE), lambda i: (0, 0))
    wspec = pl.BlockSpec((k, LANE), lambda i: (0, 0))
    return pl.pallas_call(
        _gate_kernel,
        grid=(t // tm,),
        in_specs=[pl.BlockSpec((tm, k), lambda i: (i, 0)), wspec, wspec, row, row],
        out_specs=pl.BlockSpec((tm, LANE), lambda i: (i, 0)),
        out_shape=jax.ShapeDtypeStruct((t, LANE), F32),
        compiler_params=_params("parallel"),
        name="gate_proj",
    )(x, wh, wl, bp, mask)


def _lanes(x, w):
    if w < LANE:
        return x[:, :w]
    return x if w == LANE else jnp.concatenate([x] * (w // LANE), axis=1)


def _softmax_step(s, at, m_s, l_s, acc_s, v):
    w = s.shape[1]
    m_prev = m_s[at]
    m_new = jnp.maximum(m_prev, jnp.max(s, axis=1, keepdims=True))
    p = jnp.exp(s - _lanes(m_new, w))
    a = jnp.exp(m_prev - m_new)
    l_s[at] = a * l_s[at] + jnp.sum(p, axis=1, keepdims=True)
    acc_s[at] = acc_s[at] * _lanes(a, acc_s.shape[-1]) + _dot(p.astype(BF16), v)
    m_s[at] = m_new


def _fox_kernel(q_ref, k_ref, v_ref, *rest, tq, tk, q_off, cblk):
    pk_ref, pv_ref = rest[:2] if len(rest) == 9 else (k_ref, v_ref)
    lf_ref, o_ref, ncum_ref, qm_s, m_s, l_s, acc_s = rest[-7:]
    qi = pl.program_id(1)
    sub = cblk // LANE
    n_pairs = H_FOX // 2

    @pl.when(qi == 0)
    def _():
        r = lax.broadcasted_iota(jnp.int32, (cblk, cblk), 0)
        c = lax.broadcasted_iota(jnp.int32, (cblk, cblk), 1)
        upper = jnp.where(r <= c, 1.0, 0.0).astype(BF16)
        carry = jnp.zeros((lf_ref.shape[1], 1), F32)
        for j in range(lf_ref.shape[2] // cblk):
            g1, g2, g3 = _split3(lf_ref[0, :, j * cblk:(j + 1) * cblk])
            cum = _dot(g1, upper) + _dot(g2, upper) + _dot(g3, upper) + carry
            carry = cum[:, cblk - 1:cblk]
            for h in range(H_FOX):
                for u in range(sub):
                    ncum_ref[h, j * sub + u] = -cum[h:h + 1, u * LANE:(u + 1) * LANE]

    lane = lax.broadcasted_iota(jnp.int32, (tq, LANE), 1)
    for hp in range(n_pairs):
        q = q_ref[0, :, hp * LANE:(hp + 1) * LANE] * (DH_FOX ** -0.5)
        zero = jnp.zeros_like(q)
        qm_s[hp, 0:tq, :] = jnp.where(lane < DH_FOX, q, zero)
        qm_s[hp, tq:2 * tq, :] = jnp.where(lane >= DH_FOX, q, zero)
    m_s[...] = jnp.full(m_s.shape, NEG_BIG, F32)
    l_s[...] = jnp.zeros(l_s.shape, F32)
    acc_s[...] = jnp.zeros(acc_s.shape, F32)
    q_start = q_off + qi * tq

    def update(k_src, v_src, off, w, bias_of_head, mask):
        for hp in range(n_pairs):
            kt = k_src[0, pl.ds(off, w), hp * LANE:(hp + 1) * LANE]
            vt = v_src[0, pl.ds(off, w), hp * LANE:(hp + 1) * LANE]
            s = _dot_nt(qm_s[hp], kt)
            s = jnp.concatenate([s[:tq] + bias_of_head(2 * hp), s[tq:] + bias_of_head(2 * hp + 1)], axis=0)
            if mask is not None:
                s = jnp.where(mask, s, -jnp.inf)
            _softmax_step(s, hp, m_s, l_s, acc_s, vt)

    def full_tile(j, carry):
        def bias(h):
            return jnp.concatenate([ncum_ref[h, j * (tk // LANE) + u] for u in range(tk // LANE)], axis=1)
        update(pk_ref, pv_ref, pl.multiple_of(j * tk, tk), tk, bias, None)
        return carry

    lax.fori_loop(0, q_start // tk, full_tile, 0)

    def diag_bias(h):
        if tq >= LANE:
            return jnp.concatenate([ncum_ref[h, q_start // LANE + u] for u in range(tq // LANE)], axis=1)
        lo = q_off % LANE
        return ncum_ref[h, q_start // LANE][:, lo:lo + tq]

    r = lax.broadcasted_iota(jnp.int32, (2 * tq, tq), 0)
    c = lax.broadcasted_iota(jnp.int32, (2 * tq, tq), 1)
    update(k_ref, v_ref, pl.multiple_of(qi * tq, tq), tq, diag_bias, c <= jnp.where(r >= tq, r - tq, r))
    for hp in range(n_pairs):
        o = acc_s[hp] / l_s[hp]
        o_ref[0, :, hp * LANE:(hp + 1) * LANE] = jnp.where(lane < DH_FOX, o[:tq], o[tq:]).astype(o_ref.dtype)


def fox_attention(q, k, v, logf_rows, *, tq, past=None, tk=256, cblk=256):
    b, lq, _ = q.shape
    lkp = logf_rows.shape[2]
    q_off = 0 if past is None else past[0].shape[1]
    nq = lq // tq
    assert k.shape[1] == lq and q_off % tk == 0 and (tq % tk == 0 or lq == tq) and lkp % cblk == 0
    assert tq >= LANE or (nq == 1 and q_off % LANE + tq <= LANE)
    kern = functools.partial(_fox_kernel, tq=tq, tk=tk, q_off=q_off, cblk=cblk)
    whole_seq = [pl.BlockSpec((1, a.shape[1], W_FOX), lambda bi, qi: (bi, 0, 0)) for a in (k, v) + tuple(past or ())]
    return pl.pallas_call(
        kern,
        grid=(b, nq),
        in_specs=[pl.BlockSpec((1, tq, W_FOX), lambda bi, qi: (bi, qi, 0))] + whole_seq + [
            pl.BlockSpec((1, logf_rows.shape[1], lkp), lambda bi, qi: (bi, 0, 0)),
        ],
        out_specs=pl.BlockSpec((1, tq, W_FOX), lambda bi, qi: (bi, qi, 0)),
        out_shape=jax.ShapeDtypeStruct((b, lq, W_FOX), BF16),
        scratch_shapes=[pltpu.VMEM((H_FOX, lkp // LANE, 1, LANE), F32), pltpu.VMEM((H_FOX // 2, 2 * tq, LANE), BF16),
                        pltpu.VMEM((H_FOX // 2, 2 * tq, LANE), F32), pltpu.VMEM((H_FOX // 2, 2 * tq, LANE), F32),
                        pltpu.VMEM((H_FOX // 2, 2 * tq, LANE), F32)],
        compiler_params=_params("parallel", "arbitrary"),
        name="fox_attention",
    )(q, k, v, *(past or ()), logf_rows)


def _mlstm_kernel(q_ref, k_ref, v_ref, g_ref, c0_ref, n0_ref, m0_ref,
                  h_ref, c_ref, n_ref, m_ref, c_s, n_s, m_s, *, chunk):
    ci = pl.program_id(1)
    L = chunk
    scale = DH_ML ** -0.5

    @pl.when(ci == 0)
    def _():
        c_s[...] = c0_ref[0]
        n_s[...] = n0_ref[0]
        m_s[...] = m0_ref[0]

    r = lax.broadcasted_iota(jnp.int32, (L, L), 0)
    c = lax.broadcasted_iota(jnp.int32, (L, L), 1)
    eye = r == c
    causal = c <= r
    upper = jnp.where(r <= c, 1.0, 0.0).astype(BF16)
    g = g_ref[0]
    g1, g2, g3 = _split3(g)
    cum = _dot(g1, upper) + _dot(g2, upper) + _dot(g3, upper)

    def to_col(row):
        return jnp.sum(jnp.where(eye, row, 0.0), axis=1, keepdims=True)

    for h in range(H_ML):
        sl = slice(h * DH_ML, (h + 1) * DH_ML)
        qh, kh, vh = q_ref[0, :, sl], k_ref[0, :, sl], v_ref[0, :, sl]
        ig_row = g[h:h + 1, :]
        bh_row = cum[H_ML + h:H_ML + h + 1, :]
        bh_col, ig_col = to_col(bh_row), to_col(ig_row)
        m0 = m_s[h]
        c0 = c_s[h]
        n0 = n_s[h]
        logd = jnp.where(causal, bh_col - bh_row + ig_row, -jnp.inf)
        inter = bh_col + m0
        m_col = jnp.maximum(inter, jnp.max(logd, axis=1, keepdims=True))
        d = jnp.exp(logd - m_col)
        a_col = jnp.exp(inter - m_col)
        s = _dot_nt(qh, kh) * scale * d
        num = _dot(s.astype(BF16), vh) + _dot_nt(qh, c0.astype(BF16)) * a_col
        qf = qh.astype(F32)
        den = jnp.sum(s, axis=1, keepdims=True) + a_col * jnp.sum(qf * n0, axis=1, keepdims=True)
        den = jnp.maximum(jnp.abs(den), jnp.exp(-m_col))
        h_ref[0, :, sl] = (num / den).astype(h_ref.dtype)

        m_l = m_col[L - 1:L, :]
        bh_l = bh_row[:, L - 1:L]
        a_l = jnp.exp(bh_l + m0 - m_l)
        w_col = jnp.exp(bh_l - bh_col + ig_col - m_l)
        vw_t = jnp.transpose(vh.astype(F32) * w_col).astype(BF16)
        c_s[h] = a_l * c0 + _dot(vw_t, kh) * scale
        n_s[h] = a_l * n0 + jnp.sum(kh.astype(F32) * w_col, axis=0, keepdims=True) * scale
        m_s[h] = m_l

    @pl.when(ci == pl.num_programs(1) - 1)
    def _():
        c_ref[0] = c_s[...]
        n_ref[0] = n_s[...]
        m_ref[0] = m_s[...]


def mlstm(q, k, v, gate_rows, c0, n0, m0, *, chunk):
    b, s, _ = q.shape
    nc = s // chunk
    seq = pl.BlockSpec((1, chunk, W_ML), lambda bi, ci: (bi, ci, 0))
    c_spec = pl.BlockSpec((1, H_ML, DH_ML, DH_ML), lambda bi, ci: (bi, 0, 0, 0))
    n_spec = pl.BlockSpec((1, H_ML, 1, DH_ML), lambda bi, ci: (bi, 0, 0, 0))
    m_spec = pl.BlockSpec((1, H_ML, 1, 1), lambda bi, ci: (bi, 0, 0, 0))
    return pl.pallas_call(
        functools.partial(_mlstm_kernel, chunk=chunk),
        grid=(b, nc),
        in_specs=[seq, seq, seq, pl.BlockSpec((1, gate_rows.shape[1], chunk), lambda bi, ci: (bi, 0, ci)),
                  c_spec, n_spec, m_spec],
        out_specs=[seq, c_spec, n_spec, m_spec],
        out_shape=[jax.ShapeDtypeStruct((b, s, W_ML), BF16),
                   jax.ShapeDtypeStruct(c0.shape, F32), jax.ShapeDtypeStruct(n0.shape, F32),
                   jax.ShapeDtypeStruct(m0.shape, F32)],
        scratch_shapes=[pltpu.VMEM((H_ML, DH_ML, DH_ML), F32), pltpu.VMEM((H_ML, 1, DH_ML), F32),
                        pltpu.VMEM((H_ML, 1, 1), F32)],
        compiler_params=_params("parallel", "arbitrary"),
        name="mlstm",
    )(q, k, v, gate_rows, c0, n0, m0)


N_PAIRS = EXP_PER_GROUP * (EXP_PER_GROUP - 1) // 2
N_BUCKETS = N_GROUPS * N_PAIRS
assert N_BUCKETS <= LANE
EXT = LANE
FEW_TOKENS = 1024


def _route_store(out, wrh_ref, wrl_ref, br_ref, ltri_ref, o_ref, cnt_ref, cnt_s):
    tm, d = out.shape

    @pl.when(pl.program_id(0) == 0)
    def _():
        cnt_s[...] = jnp.zeros(cnt_s.shape, F32)

    xh = out.astype(BF16)
    xl = (out - xh.astype(F32)).astype(BF16)
    z = _dot(xh, wrh_ref[...]) + _dot(xh, wrl_ref[...]) + _dot(xl, wrh_ref[...]) + br_ref[...]
    lane = lax.broadcasted_iota(jnp.int32, (tm, LANE), 1).astype(F32)
    far = float(LANE)
    neg = -jnp.inf
    gl = jnp.where(lane < N_GROUPS, z, neg)
    gmax = jnp.max(gl, axis=1, keepdims=True)
    g_idx = jnp.min(jnp.where(gl == gmax, lane, far), axis=1, keepdims=True)
    g_gate = 1.0 / jnp.sum(jnp.exp(gl - gmax), axis=1, keepdims=True)
    lo = N_GROUPS + EXP_PER_GROUP * g_idx
    el = jnp.where(jnp.logical_and(lane >= lo, lane < lo + EXP_PER_GROUP), z, neg)
    v1 = jnp.max(el, axis=1, keepdims=True)
    i1 = jnp.min(jnp.where(el == v1, lane, far), axis=1, keepdims=True)
    el2 = jnp.where(lane == i1, neg, el)
    v2 = jnp.max(el2, axis=1, keepdims=True)
    i2 = jnp.min(jnp.where(el2 == v2, lane, far), axis=1, keepdims=True)
    e21 = jnp.exp(v2 - v1)
    w1 = g_gate / (1.0 + e21)
    w2 = w1 * e21
    j1, j2 = i1 - lo, i2 - lo
    ja, jb = jnp.minimum(j1, j2), jnp.maximum(j1, j2)
    pair = ja * (2 * EXP_PER_GROUP - 1 - ja) * 0.5 + (jb - ja - 1.0)
    bucket = g_idx * N_PAIRS + pair
    first_low = j1 < j2
    wa = jnp.where(first_low, w1, w2)
    wb = jnp.where(first_low, w2, w1)
    onehot = jnp.where(lane == bucket, 1.0, 0.0)
    earlier = _dot(ltri_ref[...], onehot.astype(BF16))
    cnt = cnt_s[...]
    rank = jnp.sum(onehot * (earlier + cnt), axis=1, keepdims=True)
    cnt_new = cnt + jnp.sum(onehot, axis=0, keepdims=True)
    cnt_s[...] = cnt_new
    cnt_ref[...] = cnt_new
    o_ref[:, :d] = out
    e_lo = lo - N_GROUPS + ja
    e_hi = lo - N_GROUPS + jb
    record = (bucket, rank, wa, wb, e_lo, e_hi)
    ext = jnp.zeros((tm, LANE), F32)
    for k, col in enumerate(record):
        ext = jnp.where(lane == float(k), col, ext)
    o_ref[:, d:] = ext


def _route_operands(w_route, b_route, tm):
    n = w_route.shape[1]
    wp = jnp.zeros((w_route.shape[0], LANE), F32).at[:, :n].set(w_route)
    wh = wp.astype(BF16)
    wl = (wp - wh.astype(F32)).astype(BF16)
    bp = jnp.zeros((1, LANE), F32).at[0, :n].set(b_route)
    ltri = jnp.asarray(np.tril(np.ones((tm, tm), np.float32), -1), BF16)
    return wh, wl, bp, ltri


def _route_specs(k, tm):
    return [_whole((k, LANE)), _whole((k, LANE)), _whole((1, LANE)), _whole((tm, tm))]


def _ab_out_kernel(fo_ref, mh_ref, mo_ref, x_ref, gml_ref, w_ref, g_ref, b_ref, wrh_ref, wrl_ref, br_ref, ltri_ref,
                   o_ref, cnt_ref, cnt_s):
    parts = [fo_ref[...]]
    for h in range(H_ML):
        sl = slice(h * DH_ML, (h + 1) * DH_ML)
        mh = mh_ref[:, sl].astype(F32)
        ms = jnp.mean(mh * mh, axis=1, keepdims=True)
        nm = mh * lax.rsqrt(ms + RMS_EPS) * gml_ref[:, sl]
        parts.append((nm * _sigmoid(mo_ref[:, sl].astype(F32))).astype(BF16))
    cat = jnp.concatenate(parts, axis=1)
    y = ALPHA * x_ref[...] + _dot(cat, w_ref[...])
    out = _layer_norm(y, g_ref[...], b_ref[...])
    _route_store(out, wrh_ref, wrl_ref, br_ref, ltri_ref, o_ref, cnt_ref, cnt_s)


def _rows(tm, n):
    return pl.BlockSpec((tm, n), lambda i: (i, 0))


def _whole(shape):
    return pl.BlockSpec(shape, lambda i: (0,) * len(shape))


def _routed_out_call(kern, name, t, d, tm, in_specs, operands, route_w):
    w_route, b_route = route_w
    return pl.pallas_call(
        kern,
        grid=(t // tm,),
        in_specs=in_specs + _route_specs(d, tm),
        out_specs=[_rows(tm, d + EXT), _whole((1, LANE))],
        out_shape=[jax.ShapeDtypeStruct((t, d + EXT), F32), jax.ShapeDtypeStruct((1, LANE), F32)],
        scratch_shapes=[pltpu.VMEM((1, LANE), F32)],
        compiler_params=_params("arbitrary"),
        name=name,
    )(*operands, *_route_operands(w_route, b_route, tm))


def ab_out(fo, mh, mo, x, g_ml, w_out, ln_g, ln_b, route_w):
    t, d = x.shape
    tm = _row_tile(t)
    in_specs = [_rows(tm, W_FOX), _rows(tm, W_ML), _rows(tm, W_ML), _rows(tm, d), _whole((1, W_ML)),
                _whole(w_out.shape), _whole((1, d)), _whole((1, d))]
    return _routed_out_call(_ab_out_kernel, "ab_out", t, d, tm, in_specs,
                            (fo, mh, mo, x, g_ml, w_out, ln_g, ln_b), route_w)


def _rms(z, g):
    return z * lax.rsqrt(jnp.mean(z * z, axis=1, keepdims=True) + RMS_EPS) * g


def _mla_down_kernel(x_ref, w_ref, gq_ref, gkv_ref, cc_ref, ss_ref, cq_ref, ckv_ref, kr_ref, ckvb_ref, kr2_ref):
    xb = x_ref[...].astype(BF16)
    cq_ref[...] = _rms(_dot(xb, w_ref[:, 0:Q_RANK]), gq_ref[...]).astype(BF16)
    ckv = _rms(_dot(xb, w_ref[:, Q_RANK:Q_RANK + KV_RANK]), gkv_ref[...])
    ckv_ref[...] = ckv
    ckvb_ref[...] = ckv.astype(BF16)
    c0 = Q_RANK + KV_RANK
    rope = (_dot(xb, w_ref[:, c0:c0 + LANE]) * cc_ref[...] + _dot(xb, w_ref[:, c0 + LANE:c0 + 2 * LANE]) * ss_ref[...])
    kr_ref[...] = rope[:, :ROPE]
    kr2_ref[...] = rope.astype(BF16)


def _table_spec(table, tm, s):
    if table.shape[0] == tm:
        return pl.BlockSpec((tm, LANE), lambda i: (0, 0))
    per = s // tm
    return pl.BlockSpec((tm, LANE), lambda i: (i % per, 0))


def mla_down(x, w, g_q, g_kv, cc, ss, s):
    t, d = x.shape
    tm = _row_tile(t)
    tab = _table_spec(cc, tm, s)
    return pl.pallas_call(
        _mla_down_kernel,
        grid=(t // tm,),
        in_specs=[_rows(tm, d), _whole(w.shape), _whole((1, Q_RANK)), _whole((1, KV_RANK)), tab, tab],
        out_specs=[_rows(tm, Q_RANK), _rows(tm, KV_RANK), _rows(tm, ROPE), _rows(tm, KV_RANK), _rows(tm, LANE)],
        out_shape=[jax.ShapeDtypeStruct((t, Q_RANK), BF16), jax.ShapeDtypeStruct((t, KV_RANK), F32),
                   jax.ShapeDtypeStruct((t, ROPE), F32), jax.ShapeDtypeStruct((t, KV_RANK), BF16),
                   jax.ShapeDtypeStruct((t, LANE), BF16)],
        compiler_params=_params("parallel"),
        name="mla_down",
    )(x, w, g_q, g_kv, cc, ss)


def _mla_uq_kernel(cq_ref, w_ref, wuk_ref, cc_ref, ss_ref, qlat_ref, qrope_ref):
    cq = cq_ref[...]
    scale = (NOPE + ROPE) ** -0.5
    r0 = H_MLA * NOPE
    r1 = r0 + H_MLA * ROPE
    qn = _dot(cq, w_ref[:, :r0]).astype(BF16)
    for h in range(H_MLA):
        qlat_ref[:, h * KV_RANK:(h + 1) * KV_RANK] = (
            _dot(qn[:, h * NOPE:(h + 1) * NOPE], wuk_ref[h]) * scale).astype(BF16)
    qr = _dot(cq, w_ref[:, r0:])
    cc, ss = cc_ref[...] * scale, ss_ref[...] * scale
    for p in range(H_MLA // 2):
        a, b = p * LANE, (r1 - r0) + p * LANE
        qrope_ref[:, p * LANE:(p + 1) * LANE] = (qr[:, a:a + LANE] * cc + qr[:, b:b + LANE] * ss).astype(BF16)


def mla_uq(cq, w, wuk_t, cc, ss, s):
    t = cq.shape[0]
    tm = _row_tile(t)
    tab = _table_spec(cc, tm, s)
    return pl.pallas_call(
        _mla_uq_kernel,
        grid=(t // tm,),
        in_specs=[_rows(tm, Q_RANK), _whole(w.shape), _whole(wuk_t.shape), tab, tab],
        out_specs=[_rows(tm, H_MLA * KV_RANK), _rows(tm, H_MLA * ROPE)],
        out_shape=[jax.ShapeDtypeStruct((t, H_MLA * KV_RANK), BF16), jax.ShapeDtypeStruct((t, H_MLA * ROPE), BF16)],
        compiler_params=_params("parallel"),
        name="mla_uq",
    )(cq, w, wuk_t, cc, ss)


def _mla_attn_kernel(ql_ref, qr_ref, kc_ref, kr_ref, *rest, tq, tk, q_off):
    pc_ref, pr_ref = rest[:2] if len(rest) == 8 else (kc_ref, kr_ref)
    o_ref, ql_s, qr_s, m_s, l_s, acc_s = rest[-6:]
    qi = pl.program_id(1)
    lane = lax.broadcasted_iota(jnp.int32, (tq, LANE), 1)
    for h in range(H_MLA):
        rows = slice(h * tq, (h + 1) * tq)
        ql_s[rows, :] = ql_ref[0, :, h * KV_RANK:(h + 1) * KV_RANK]
        qr = qr_ref[0, :, (h // 2) * LANE:(h // 2 + 1) * LANE]
        keep = (lane < ROPE) if h % 2 == 0 else (lane >= ROPE)
        qr_s[rows, :] = jnp.where(keep, qr, jnp.zeros_like(qr))
    m_s[...] = jnp.full(m_s.shape, NEG_BIG, F32)
    l_s[...] = jnp.zeros(l_s.shape, F32)
    acc_s[...] = jnp.zeros(acc_s.shape, F32)
    q_start = q_off + qi * tq

    def update(c_src, r_src, off, w, mask):
        kc = c_src[0, pl.ds(off, w), :]
        kr = r_src[0, pl.ds(off, w), :]
        s = _dot_nt(ql_s[...], kc) + _dot_nt(qr_s[...], kr)
        if mask is not None:
            s = jnp.where(mask, s, -jnp.inf)
        _softmax_step(s, Ellipsis, m_s, l_s, acc_s, kc)

    def full_tile(j, carry):
        update(pc_ref, pr_ref, pl.multiple_of(j * tk, tk), tk, None)
        return carry

    lax.fori_loop(0, q_start // tk, full_tile, 0)
    r = lax.broadcasted_iota(jnp.int32, (H_MLA * tq, tq), 0)
    c = lax.broadcasted_iota(jnp.int32, (H_MLA * tq, tq), 1)
    update(kc_ref, kr_ref, pl.multiple_of(qi * tq, tq), tq, (c // CHUNK) <= ((r & (tq - 1)) // CHUNK))
    for h in range(H_MLA):
        rows = slice(h * tq, (h + 1) * tq)
        o = acc_s[rows, :] / _lanes(l_s[rows, :], KV_RANK)
        o_ref[0, :, h * KV_RANK:(h + 1) * KV_RANK] = o.astype(o_ref.dtype)


def mla_attention(q_lat, q_rope, kc, kr2, *, tq, tk, past=None):
    b, lq, _ = q_lat.shape
    q_off = 0 if past is None else past[0].shape[1]
    assert kc.shape[1] == lq and q_off % tk == 0 and (tq % tk == 0 or lq == tq)
    assert tq % CHUNK == 0 and (tq & (tq - 1)) == 0
    kern = functools.partial(_mla_attn_kernel, tq=tq, tk=tk, q_off=q_off)
    whole_seq = [pl.BlockSpec((1,) + a.shape[1:], lambda bi, qi: (bi, 0, 0)) for a in (kc, kr2) + tuple(past or ())]
    return pl.pallas_call(
        kern,
        grid=(b, lq // tq),
        in_specs=[
            pl.BlockSpec((1, tq, H_MLA * KV_RANK), lambda bi, qi: (bi, qi, 0)),
            pl.BlockSpec((1, tq, H_MLA * ROPE), lambda bi, qi: (bi, qi, 0)),
        ] + whole_seq,
        out_specs=pl.BlockSpec((1, tq, H_MLA * KV_RANK), lambda bi, qi: (bi, qi, 0)),
        out_shape=jax.ShapeDtypeStruct((b, lq, H_MLA * KV_RANK), BF16),
        scratch_shapes=[pltpu.VMEM((H_MLA * tq, KV_RANK), BF16), pltpu.VMEM((H_MLA * tq, LANE), BF16),
                        pltpu.VMEM((H_MLA * tq, LANE), F32), pltpu.VMEM((H_MLA * tq, LANE), F32),
                        pltpu.VMEM((H_MLA * tq, KV_RANK), F32)],
        compiler_params=_params("parallel", "arbitrary"),
        name="mla_attention",
    )(q_lat, q_rope, kc, kr2, *(past or ()))


def _mla_out_kernel(ol_ref, wuv_ref, w_ref, x_ref, g_ref, b_ref, wrh_ref, wrl_ref, br_ref, ltri_ref,
                    o_ref, cnt_ref, cnt_s):
    parts = [_dot(ol_ref[:, h * KV_RANK:(h + 1) * KV_RANK], wuv_ref[h]).astype(BF16) for h in range(H_MLA)]
    y = ALPHA * x_ref[...] + _dot(jnp.concatenate(parts, axis=1), w_ref[...])
    out = _layer_norm(y, g_ref[...], b_ref[...])
    _route_store(out, wrh_ref, wrl_ref, br_ref, ltri_ref, o_ref, cnt_ref, cnt_s)


def mla_out(o_lat, wuv, w_out, x, ln_g, ln_b, route_w):
    t, d = x.shape
    tm = _row_tile(t)
    in_specs = [_rows(tm, H_MLA * KV_RANK), _whole(wuv.shape), _whole(w_out.shape), _rows(tm, d),
                _whole((1, d)), _whole((1, d))]
    return _routed_out_call(_mla_out_kernel, "mla_out", t, d, tm, in_specs,
                            (o_lat, wuv, w_out, x, ln_g, ln_b), route_w)


ROW_SLOTS = 3


def _row_scatter_kernel(idx_ref, src_ref, dst_init_ref, dst_ref, buf, in_sem, out_sem, *, rows, n_steps):
    del dst_init_ref

    def load(i, slot):
        return pltpu.make_async_copy(src_ref.at[pl.ds(pl.multiple_of(i * rows, rows), rows)], buf.at[slot],
                                     in_sem.at[slot])

    def rows_done(slot):
        return pltpu.make_async_copy(buf.at[slot], dst_ref.at[pl.ds(0, rows)], out_sem.at[slot])

    i = pl.program_id(0)
    slot = i % ROW_SLOTS
    nxt = (i + 1) % ROW_SLOTS

    @pl.when(i == 0)
    def _():
        load(0, 0).start()

    @pl.when(i + 1 < n_steps)
    def _():
        @pl.when(i + 1 >= ROW_SLOTS)
        def _():
            rows_done(nxt).wait()
        load(i + 1, nxt).start()

    load(i, slot).wait()
    base = i * rows

    for r in range(rows):
        pltpu.make_async_copy(buf.at[slot, pl.ds(r, 1)], dst_ref.at[pl.ds(idx_ref[base + r], 1)],
                              out_sem.at[slot]).start()

    @pl.when(i == n_steps - 1)
    def _():
        for k in range(min(ROW_SLOTS, n_steps)):
            rows_done((n_steps - 1 - k) % ROW_SLOTS).wait()


def _row_gather_kernel(idx_ref, src_ref, dst_ref, buf, row_sem, out_sem, *, rows, n_steps):
    def rows_done(slot):
        return pltpu.make_async_copy(src_ref.at[pl.ds(0, rows)], buf.at[slot], row_sem.at[slot])

    def store(i, slot):
        return pltpu.make_async_copy(buf.at[slot], dst_ref.at[pl.ds(pl.multiple_of(i * rows, rows), rows)],
                                     out_sem.at[slot])

    i = pl.program_id(0)
    slot = i % ROW_SLOTS

    @pl.when(i >= ROW_SLOTS)
    def _():
        store(i - ROW_SLOTS, slot).wait()

    base = i * rows

    for r in range(rows):
        pltpu.make_async_copy(src_ref.at[pl.ds(idx_ref[base + r], 1)], buf.at[slot, pl.ds(r, 1)],
                              row_sem.at[slot]).start()

    @pl.when(i >= 1)
    def _():
        prev = (i + ROW_SLOTS - 1) % ROW_SLOTS
        rows_done(prev).wait()
        store(i - 1, prev).start()

    @pl.when(i == n_steps - 1)
    def _():
        last = (n_steps - 1) % ROW_SLOTS
        rows_done(last).wait()
        store(n_steps - 1, last).start()
        for k in range(min(ROW_SLOTS, n_steps)):
            store(n_steps - 1 - k, (n_steps - 1 - k) % ROW_SLOTS).wait()


def row_move(idx, src, dst_rows, *, scatter, dst_init=None):
    t = idx.shape[0]
    width = src.shape[1]
    rows = min(512, t)
    any_spec = pl.BlockSpec(memory_space=pl.ANY)
    operands = [idx, src] + ([dst_init] if scatter else [])
    grid_spec = pltpu.PrefetchScalarGridSpec(
        num_scalar_prefetch=1, grid=(t // rows,), in_specs=[any_spec] * (len(operands) - 1), out_specs=any_spec,
        scratch_shapes=[pltpu.VMEM((ROW_SLOTS, rows, width), src.dtype),
                        pltpu.SemaphoreType.DMA((ROW_SLOTS,)), pltpu.SemaphoreType.DMA((ROW_SLOTS,))])
    kern = _row_scatter_kernel if scatter else _row_gather_kernel
    return pl.pallas_call(
        functools.partial(kern, rows=rows, n_steps=t // rows),
        grid_spec=grid_spec,
        out_shape=jax.ShapeDtypeStruct((dst_rows, width), src.dtype),
        input_output_aliases={2: 0} if scatter else {},
        compiler_params=_params("arbitrary"),
        name="row_scatter" if scatter else "row_gather",
    )(*operands)


def _moe_kernel(ta_ref, tb_ref, nu_ref, xs_ref, w1a_ref, w3a_ref, w2a_ref, w1b_ref, w3b_ref, w2b_ref,
                g_ref, b_ref, o_ref):
    t = pl.program_id(0)
    d = o_ref.shape[1]

    @pl.when(t < nu_ref[0])
    def _():
        x = xs_ref[:, :d]
        ext = xs_ref[:, d:]
        xb = x.astype(BF16)

        def ffn(w1_ref, w3_ref, w2_ref):
            h1 = _dot(xb, w1_ref[0])
            h3 = _dot(xb, w3_ref[0])
            return _dot((h1 * _sigmoid(h1) * h3).astype(BF16), w2_ref[0])

        y = ext[:, 2:3] * ffn(w1a_ref, w3a_ref, w2a_ref) + ext[:, 3:4] * ffn(w1b_ref, w3b_ref, w2b_ref)
        o_ref[...] = _layer_norm(ALPHA * x + y, g_ref[...], b_ref[...])

    @pl.when(t >= nu_ref[0])
    def _():
        o_ref[...] = jnp.zeros(o_ref.shape, o_ref.dtype)


def moe_ln(tile_ea, tile_eb, n_used, xs, w1, w3, w2, ln_g, ln_b, tm):
    p, wide = xs.shape
    d = wide - EXT
    de = w1.shape[2]

    def w_spec(shape, which):
        return pl.BlockSpec(shape, lambda t, ta, tb, nu: ((ta, tb)[which][t], 0, 0))

    grid_spec = pltpu.PrefetchScalarGridSpec(
        num_scalar_prefetch=3,
        grid=(p // tm,),
        in_specs=[pl.BlockSpec((tm, wide), lambda t, ta, tb, nu: (t, 0)),
                  w_spec((1, d, de), 0), w_spec((1, d, de), 0), w_spec((1, de, d), 0),
                  w_spec((1, d, de), 1), w_spec((1, d, de), 1), w_spec((1, de, d), 1),
                  pl.BlockSpec((1, d), lambda t, ta, tb, nu: (0, 0)),
                  pl.BlockSpec((1, d), lambda t, ta, tb, nu: (0, 0))],
        out_specs=pl.BlockSpec((tm, d), lambda t, ta, tb, nu: (t, 0)),
    )
    return pl.pallas_call(
        _moe_kernel,
        grid_spec=grid_spec,
        out_shape=jax.ShapeDtypeStruct((p, d), F32),
        compiler_params=_params("arbitrary"),
        name="moe_ln",
    )(tile_ea, tile_eb, n_used, xs, w1, w3, w2, w1, w3, w2, ln_g, ln_b)


def _moe_few_kernel(x_ref, w1_ref, w3_ref, w2_ref, g_ref, b_ref, o_ref, acc_s):
    e = pl.program_id(0)
    d = o_ref.shape[1]

    @pl.when(e == 0)
    def _():
        acc_s[...] = jnp.zeros(acc_s.shape, F32)

    x = x_ref[:, :d]
    ext = x_ref[:, d:]
    xb = x.astype(BF16)
    h1 = _dot(xb, w1_ref[0])
    h3 = _dot(xb, w3_ref[0])
    y = _dot((h1 * _sigmoid(h1) * h3).astype(BF16), w2_ref[0])
    ef = e.astype(F32)
    gate = jnp.where(ext[:, 4:5] == ef, ext[:, 2:3], 0.0) + jnp.where(ext[:, 5:6] == ef, ext[:, 3:4], 0.0)
    acc_s[...] += gate * y

    @pl.when(e == pl.num_programs(0) - 1)
    def _():
        o_ref[...] = _layer_norm(ALPHA * x + acc_s[...], g_ref[...], b_ref[...])


def moe_ln_few(x1ext, w1, w3, w2, ln_g, ln_b):
    t, wide = x1ext.shape
    d = wide - EXT
    de = w1.shape[2]
    return pl.pallas_call(
        _moe_few_kernel,
        grid=(w1.shape[0],),
        in_specs=[_whole((t, wide)), pl.BlockSpec((1, d, de), lambda e: (e, 0, 0)),
                  pl.BlockSpec((1, d, de), lambda e: (e, 0, 0)), pl.BlockSpec((1, de, d), lambda e: (e, 0, 0)),
                  _whole((1, d)), _whole((1, d))],
        out_specs=_whole((t, d)),
        out_shape=jax.ShapeDtypeStruct((t, d), F32),
        scratch_shapes=[pltpu.VMEM((t, d), F32)],
        compiler_params=_params("arbitrary"),
        name="moe_ln_few",
    )(x1ext, w1, w3, w2, ln_g, ln_b)


def _bucket_experts():
    ea, eb = [], []
    for g in range(N_GROUPS):
        for ja in range(EXP_PER_GROUP):
            for jb in range(ja + 1, EXP_PER_GROUP):
                ea.append(g * EXP_PER_GROUP + ja)
                eb.append(g * EXP_PER_GROUP + jb)
    return np.asarray(ea, np.int32), np.asarray(eb, np.int32)


def _bucket_layout(x1ext, counts, tm):
    t, wide = x1ext.shape
    d = wide - EXT
    bucket = x1ext[:, d].astype(jnp.int32)
    rank = x1ext[:, d + 1].astype(jnp.int32)
    counts = counts[0, :N_BUCKETS].astype(jnp.int32)
    padded = ((counts + tm - 1) // tm) * tm
    row_end = jnp.cumsum(padded)
    row_start = row_end - padded
    ids = jnp.arange(N_BUCKETS, dtype=jnp.int32)
    pos = rank + jnp.sum(jnp.where(bucket[:, None] == ids[None, :], row_start[None, :], 0), axis=1)
    n_rows = _round_up(t, tm) + N_BUCKETS * tm
    n_tiles = n_rows // tm
    n_used = (row_end[-1] // tm).astype(jnp.int32)
    tile_row = jnp.minimum(jnp.arange(n_tiles, dtype=jnp.int32), n_used - 1) * tm
    tile_bucket = jnp.sum((row_end[None, :] <= tile_row[:, None]).astype(jnp.int32), axis=1)
    ea, eb = _bucket_experts()
    return pos.astype(jnp.int32), jnp.asarray(ea)[tile_bucket], jnp.asarray(eb)[tile_bucket], n_used.reshape(1), n_rows


def hier_moe_ln(x1ext, counts, moe_w, ln_g, ln_b, sorted_buf=None):
    w1, w3, w2 = moe_w
    t, wide = x1ext.shape
    if t <= FEW_TOKENS:
        return moe_ln_few(x1ext, w1, w3, w2, ln_g, ln_b), None
    tm = 256
    pos, tile_ea, tile_eb, n_used, n_rows = _bucket_layout(x1ext, counts, tm)
    if sorted_buf is None:
        sorted_buf = jnp.zeros((n_rows, wide), F32)
    xs = row_move(pos, x1ext, n_rows, scatter=True, dst_init=sorted_buf)
    ys = moe_ln(tile_ea, tile_eb, n_used, xs, w1, w3, w2, ln_g, ln_b, tm)
    return row_move(pos, ys, t, scatter=False), xs


def _rope_tables(s, offset, tm):
    half = ROPE // 2
    inv_freq = ROPE_BASE ** (-jnp.arange(half, dtype=F32) / half)
    ang = (offset + jnp.arange(s)).astype(F32)[:, None] * inv_freq[None, :]
    cos, sin = jnp.cos(ang), jnp.sin(ang)
    cc = jnp.concatenate([cos, cos, cos, cos], -1)
    ss = jnp.concatenate([-sin, sin, -sin, sin], -1)
    if tm > s:
        cc, ss = jnp.tile(cc, (tm // s, 1)), jnp.tile(ss, (tm // s, 1))
    return cc, ss


def _round_up(n, m):
    return (n + m - 1) // m * m


def ab_layer(x, b, s, wts, route_w, cache):
    w_big, w_gate, b_gate, g_ml, w_out, ln_g, ln_b = wts
    t = b * s
    fq, mq, mk, mv, mo, fk, fk_bf, fv, fv_bf = proj(
        x, w_big, [(W_FOX, (BF16,)), (W_ML, (BF16,)), (W_ML, (BF16,)), (W_ML, (BF16,)), (W_ML, (BF16,)),
                   (W_FOX, (F32, BF16)), (W_FOX, (F32, BF16))])
    ls_cols = np.array([1.0] * H_FOX + [0.0] * H_ML + [1.0] * H_ML)
    gates = gate_proj(x, w_gate, b_gate, ls_cols)
    flf = gates[:, :H_FOX].reshape(b, s, H_FOX)
    ml_rows = jnp.swapaxes(gates[:, H_FOX:H_FOX + 2 * H_ML].reshape(b, s, 2 * H_ML), 1, 2)
    ml_rows = jnp.pad(ml_rows, ((0, 0), (0, 16 - 2 * H_ML), (0, 0)))
    fk3, fv3 = fk.reshape(b, s, W_FOX), fv.reshape(b, s, W_FOX)
    if cache is None:
        past_kv, lf_all = None, flf
        c0 = jnp.zeros((b, H_ML, DH_ML, DH_ML), F32)
        n0 = jnp.zeros((b, H_ML, 1, DH_ML), F32)
        m0 = jnp.zeros((b, H_ML, 1, 1), F32)
        tq, chunk = 256, 256
    else:
        ck, cv, clf, c0, n0, m0 = cache
        past = ck.shape[1]
        past_kv = (ck.reshape(b, past, W_FOX).astype(BF16), cv.reshape(b, past, W_FOX).astype(BF16))
        lf_all = jnp.concatenate([clf, flf], 1)
        n0 = n0.reshape(b, H_ML, 1, DH_ML)
        m0 = m0.reshape(b, H_ML, 1, 1)
        tq, chunk = s, s
    lk = lf_all.shape[1]
    lf_rows = jnp.pad(jnp.swapaxes(lf_all, 1, 2), ((0, 0), (0, 16 - H_FOX), (0, _round_up(lk, 256) - lk)))
    fo = fox_attention(fq.reshape(b, s, W_FOX), fk_bf.reshape(b, s, W_FOX), fv_bf.reshape(b, s, W_FOX), lf_rows,
                       tq=tq, past=past_kv)
    mh, c_new, n_new, m_new = mlstm(mq.reshape(b, s, W_ML), mk.reshape(b, s, W_ML), mv.reshape(b, s, W_ML),
                                    ml_rows, c0, n0, m0, chunk=chunk)
    x1ext, counts = ab_out(fo.reshape(t, W_FOX), mh.reshape(t, W_ML), mo, x, g_ml, w_out, ln_g, ln_b, route_w)
    state = (fk3.reshape(b, s, H_FOX, DH_FOX), fv3.reshape(b, s, H_FOX, DH_FOX), flf,
             c_new, n_new.reshape(b, H_ML, DH_ML), m_new.reshape(b, H_ML))
    return x1ext, counts, state


def mla_layer(x, b, s, wts, route_w, cache):
    w_down, g_q, g_kv, w_uq, wuk_t, wuv, w_out, ln_g, ln_b = wts
    t = b * s
    tm = _row_tile(t)
    past = 0 if cache is None else cache[0].shape[1]
    cc, ss = _rope_tables(s, past, tm)
    cq, ckv, kr, ckv_bf, kr2_bf = mla_down(x, w_down, g_q, g_kv, cc, ss, s)
    q_lat, q_rope = mla_uq(cq, w_uq, wuk_t, cc, ss, s)
    kc = ckv_bf.reshape(b, s, KV_RANK)
    kr2 = kr2_bf.reshape(b, s, LANE)
    if cache is None:
        tq, past_k = 256, None
    else:
        c_ckv, c_kr = cache
        c_kr_bf = c_kr.astype(BF16)
        tq, past_k = s, (c_ckv.astype(BF16), jnp.concatenate([c_kr_bf, c_kr_bf], -1))
    o_lat = mla_attention(q_lat.reshape(b, s, -1), q_rope.reshape(b, s, -1), kc, kr2, tq=tq,
                          tk=tq if cache is None else 256, past=past_k)
    x1ext, counts = mla_out(o_lat.reshape(t, -1), wuv, w_out, x, ln_g, ln_b, route_w)
    return x1ext, counts, (ckv.reshape(b, s, KV_RANK), kr.reshape(b, s, ROPE))


def _prep_ab_weights(w_in, b_fox_f, b_ml_i, b_ml_f, g_ml, w_out, ln_g, ln_b):
    sizes = (W_FOX, W_FOX, W_FOX, H_FOX, W_ML, W_ML, W_ML, H_ML, H_ML, W_ML)
    idx = np.cumsum(sizes[:-1]).tolist()
    fq, fk, fv, ff, mq, mk, mv, mi, mf, mo = jnp.split(w_in, idx, axis=1)
    w_big = jnp.concatenate([fq, mq, mk, mv, mo, fk, fv], 1).astype(BF16)
    w_gate = jnp.concatenate([ff, mi, mf], 1)
    b_gate = jnp.concatenate([b_fox_f, b_ml_i, b_ml_f])
    return (w_big, w_gate, b_gate, g_ml[None, :], w_out.astype(BF16), ln_g[None, :], ln_b[None, :])


def _prep_mla_weights(w_down, g_q, w_uq, g_kv, w_uk, w_uv, w_out, ln_g, ln_b):
    half = ROPE // 2
    cq_w, ckv_w, kr_w = jnp.split(w_down, [Q_RANK, Q_RANK + KV_RANK], axis=1)
    kr_sw = jnp.concatenate([kr_w[:, half:], kr_w[:, :half]], 1)
    w_down_p = jnp.concatenate([cq_w, ckv_w, kr_w, kr_w, kr_sw, kr_sw], 1).astype(BF16)
    uq = w_uq.reshape(Q_RANK, H_MLA, NOPE + ROPE)
    uq_nope = uq[:, :, :NOPE].reshape(Q_RANK, H_MLA * NOPE)
    uq_rope = uq[:, :, NOPE:]
    uq_rope_sw = jnp.concatenate([uq_rope[..., half:], uq_rope[..., :half]], -1)
    w_uq_p = jnp.concatenate([uq_nope, uq_rope.reshape(Q_RANK, -1), uq_rope_sw.reshape(Q_RANK, -1)], 1).astype(BF16)
    wuk_t = jnp.transpose(w_uk, (1, 2, 0)).astype(BF16)
    wuv = jnp.transpose(w_uv, (1, 0, 2)).astype(BF16)
    return (w_down_p, g_q[None, :], g_kv[None, :], w_uq_p, wuk_t, wuv, w_out.astype(BF16), ln_g[None, :], ln_b[None, :])


def kernel(x_prompt, x_sample, cache_fox_k, cache_fox_v, cache_fox_logf, state_mlstm_c, state_mlstm_n, state_mlstm_m, cache_mla_ckv, cache_mla_krope, w_ab_in, b_fox_f, b_mlstm_i, b_mlstm_f, g_mlstm_norm, w_ab_out, w_mla_down, g_mla_q, w_mla_uq, g_mla_kv, w_mla_uk, w_mla_uv, w_mla_out, ln1_g, ln1_b, ln2_g, ln2_b, w_moe_group, b_moe_group, w_moe_router, b_moe_router, w_exp_gate, w_exp_up, w_exp_down):
    bp, sp, d = x_prompt.shape
    bs, ss_, _ = x_sample.shape
    xp = x_prompt.reshape(bp * sp, d)
    xs = x_sample.reshape(bs * ss_, d)
    ab_p, ab_s, c_p, c_s = [], [], [], []
    buf_p = buf_s = None
    for l in range(DEPTH):
        j = l // 2
        route_w = (jnp.concatenate([w_moe_group[l], w_moe_router[l]], 1),
                   jnp.concatenate([b_moe_group[l], b_moe_router[l]]))
        if l % 2 == 0:
            wts = _prep_ab_weights(w_ab_in[j], b_fox_f[j], b_mlstm_i[j], b_mlstm_f[j], g_mlstm_norm[j],
                                   w_ab_out[j], ln1_g[l], ln1_b[l])
            xp1, cnt_p, st_p = ab_layer(xp, bp, sp, wts, route_w, None)
            xs1, cnt_s, st_s = ab_layer(xs, bs, ss_, wts, route_w,
                                        (cache_fox_k[j], cache_fox_v[j], cache_fox_logf[j],
                                         state_mlstm_c[j], state_mlstm_n[j], state_mlstm_m[j]))
            ab_p.append(st_p)
            ab_s.append(st_s)
        else:
            wts = _prep_mla_weights(w_mla_down[j], g_mla_q[j], w_mla_uq[j], g_mla_kv[j], w_mla_uk[j], w_mla_uv[j],
                                    w_mla_out[j], ln1_g[l], ln1_b[l])
            xp1, cnt_p, st_p = mla_layer(xp, bp, sp, wts, route_w, None)
            xs1, cnt_s, st_s = mla_layer(xs, bs, ss_, wts, route_w, (cache_mla_ckv[j], cache_mla_krope[j]))
            c_p.append(st_p)
            c_s.append(st_s)
        moe_w = (w_exp_gate[l].astype(BF16), w_exp_up[l].astype(BF16), w_exp_down[l].astype(BF16))
        xp, buf_p = hier_moe_ln(xp1, cnt_p, moe_w, ln2_g[l][None, :], ln2_b[l][None, :], buf_p)
        xs, buf_s = hier_moe_ln(xs1, cnt_s, moe_w, ln2_g[l][None, :], ln2_b[l][None, :], buf_s)

    def stack(groups, i):
        return jnp.stack([g[i] for g in groups])

    return (xp.reshape(bp, sp, d), xs.reshape(bs, ss_, d),
            stack(ab_p, 0), stack(ab_p, 1), stack(ab_p, 2), stack(ab_p, 3), stack(ab_p, 4), stack(ab_p, 5),
            stack(c_p, 0), stack(c_p, 1),
            stack(ab_s, 0), stack(ab_s, 1), stack(ab_s, 2), stack(ab_s, 3), stack(ab_s, 4), stack(ab_s, 5),
            stack(c_s, 0), stack(c_s, 1))
```

```python
import functools

import numpy as np
import jax
import jax.numpy as jnp
from jax import lax
from jax.experimental import pallas as pl
from jax.experimental.pallas import tpu as pltpu

F32 = jnp.float32
BF16 = jnp.bfloat16

LANE = 128
VMEM_LIMIT_BYTES = 48 * 1024 * 1024

H_FOX, DH_FOX = 8, 64
W_FOX = H_FOX * DH_FOX
H_ML, DH_ML = 4, 128
W_ML = H_ML * DH_ML
H_MLA, NOPE, ROPE, DV_MLA = 8, 128, 64, 128
Q_RANK, KV_RANK = 384, 256
ROPE_BASE = 10000.0
N_GROUPS, EXP_PER_GROUP, TOP_K = 4, 8, 2
N_EXPERTS = N_GROUPS * EXP_PER_GROUP
DEPTH = 4
ALPHA = (2 * DEPTH) ** 0.25
LN_EPS = 1e-5
RMS_EPS = 1e-6
CHUNK = 64
NEG_BIG = -1e30


def _params(*sem):
    return pltpu.CompilerParams(dimension_semantics=sem, vmem_limit_bytes=VMEM_LIMIT_BYTES)


def _dot(a, b):
    return jnp.dot(a, b, preferred_element_type=F32)


def _dot_nt(a, b):
    return lax.dot_general(a, b, (((1,), (1,)), ((), ())), preferred_element_type=F32)


def _split3(x):
    hi = x.astype(BF16)
    r = x - hi.astype(F32)
    mid = r.astype(BF16)
    lo = (r - mid.astype(F32)).astype(BF16)
    return hi, mid, lo


def _layer_norm(y, g, b):
    mu = jnp.mean(y, axis=1, keepdims=True)
    yc = y - mu
    var = jnp.mean(yc * yc, axis=1, keepdims=True)
    return yc * lax.rsqrt(var + LN_EPS) * g + b


def _sigmoid(x):
    return 1.0 / (1.0 + jnp.exp(-x))


def _row_tile(t):
    return min(512, t)


def _ab_proj_kernel(x_ref, w_ref, gwh_ref, gwl_ref, gb_ref, gmask_ref, *rest, n_prev):
    prev_k, prev_v = rest[:n_prev], rest[n_prev:2 * n_prev]
    fq_ref, mq_ref, mk_ref, mv_ref, mo_ref, kbf_ref, vbf_ref, gates_ref, k_ref, v_ref = rest[2 * n_prev:]
    x = x_ref[...]
    xh = x.astype(BF16)
    c0 = 0
    for o_ref in (fq_ref, mq_ref, mk_ref, mv_ref, mo_ref):
        n = o_ref.shape[1]
        o_ref[...] = _dot(xh, w_ref[:, c0:c0 + n]).astype(BF16)
        c0 += n
    for bf_ref, state_ref, prev in ((kbf_ref, k_ref, prev_k), (vbf_ref, v_ref, prev_v)):
        z = _dot(xh, w_ref[:, c0:c0 + W_FOX])
        c0 += W_FOX
        bf_ref[...] = z.astype(BF16)
        if n_prev:
            for j, p in enumerate(prev):
                state_ref[j] = p[...]
            state_ref[n_prev] = z
        else:
            state_ref[...] = z
    xl = (x - xh.astype(F32)).astype(BF16)
    g = _dot(xh, gwh_ref[...]) + _dot(xh, gwl_ref[...]) + _dot(xl, gwh_ref[...]) + gb_ref[...]
    log_sig = jnp.minimum(g, 0.0) - jnp.log1p(jnp.exp(-jnp.abs(g)))
    gates_ref[...] = jnp.where(gmask_ref[...] > 0.0, log_sig, g)


def ab_proj(x, w_big, w_gate, b_gate, log_sigmoid_cols, prev_k=(), prev_v=()):
    t, k = x.shape
    tm = _row_tile(t)
    n_gate = w_gate.shape[1]
    wp = jnp.zeros((k, LANE), F32).at[:, :n_gate].set(w_gate)
    wh = wp.astype(BF16)
    wl = (wp - wh.astype(F32)).astype(BF16)
    bp = jnp.zeros((1, LANE), F32).at[0, :n_gate].set(b_gate)
    mask = jnp.zeros((1, LANE), F32).at[0, :n_gate].set(jnp.asarray(log_sigmoid_cols, F32))
    n_prev = len(prev_k)
    if n_prev:
        state_spec = pl.BlockSpec((n_prev + 1, tm, W_FOX), lambda i: (0, i, 0))
        state_shape = jax.ShapeDtypeStruct((n_prev + 1, t, W_FOX), F32)
    else:
        state_spec, state_shape = _rows(tm, W_FOX), jax.ShapeDtypeStruct((t, W_FOX), F32)
    bf = lambda n: jax.ShapeDtypeStruct((t, n), BF16)
    return pl.pallas_call(
        functools.partial(_ab_proj_kernel, n_prev=n_prev),
        grid=(t // tm,),
        in_specs=[_rows(tm, k), _whole(w_big.shape), _whole((k, LANE)), _whole((k, LANE)), _whole((1, LANE)),
                  _whole((1, LANE))] + [_rows(tm, W_FOX)] * (2 * n_prev),
        out_specs=[_rows(tm, W_FOX)] + [_rows(tm, W_ML)] * 4 + [_rows(tm, W_FOX)] * 2 + [_rows(tm, LANE)]
        + [state_spec] * 2,
        out_shape=[bf(W_FOX)] + [bf(W_ML)] * 4 + [bf(W_FOX)] * 2 + [jax.ShapeDtypeStruct((t, LANE), F32)]
        + [state_shape] * 2,
        compiler_params=_params("parallel"),
        name="ab_proj",
    )(x, w_big, wh, wl, bp, mask, *prev_k, *prev_v)


def _lanes(x, w):
    if w < LANE:
        return x[:, :w]
    return x if w == LANE else jnp.concatenate([x] * (w // LANE), axis=1)


def _softmax_step(s, at, m_s, l_s, acc_s, v):
    w = s.shape[1]
    m_prev = m_s[at]
    m_new = jnp.maximum(m_prev, jnp.max(s, axis=1, keepdims=True))
    p = jnp.exp(s - _lanes(m_new, w))
    a = jnp.exp(m_prev - m_new)
    l_s[at] = a * l_s[at] + jnp.sum(p, axis=1, keepdims=True)
    acc_s[at] = acc_s[at] * _lanes(a, acc_s.shape[-1]) + _dot(p.astype(BF16), v)
    m_s[at] = m_new


def _fox_kernel(q_ref, k_ref, v_ref, *rest, tq, tk, q_off, cblk):
    pk_ref, pv_ref = rest[:2] if len(rest) == 9 else (k_ref, v_ref)
    lf_ref, o_ref, ncum_ref, qm_s, m_s, l_s, acc_s = rest[-7:]
    qi = pl.program_id(1)
    sub = cblk // LANE
    n_pairs = H_FOX // 2

    @pl.when(qi == 0)
    def _():
        r = lax.broadcasted_iota(jnp.int32, (cblk, cblk), 0)
        c = lax.broadcasted_iota(jnp.int32, (cblk, cblk), 1)
        upper = jnp.where(r <= c, 1.0, 0.0).astype(BF16)
        carry = jnp.zeros((lf_ref.shape[1], 1), F32)
        for j in range(lf_ref.shape[2] // cblk):
            g1, g2, g3 = _split3(lf_ref[0, :, j * cblk:(j + 1) * cblk])
            cum = _dot(g1, upper) + _dot(g2, upper) + _dot(g3, upper) + carry
            carry = cum[:, cblk - 1:cblk]
            for h in range(H_FOX):
                for u in range(sub):
                    ncum_ref[h, j * sub + u] = -cum[h:h + 1, u * LANE:(u + 1) * LANE]

    lane = lax.broadcasted_iota(jnp.int32, (tq, LANE), 1)
    for hp in range(n_pairs):
        q = q_ref[0, :, hp * LANE:(hp + 1) * LANE] * (DH_FOX ** -0.5)
        zero = jnp.zeros_like(q)
        qm_s[hp, 0:tq, :] = jnp.where(lane < DH_FOX, q, zero)
        qm_s[hp, tq:2 * tq, :] = jnp.where(lane >= DH_FOX, q, zero)
    m_s[...] = jnp.full(m_s.shape, NEG_BIG, F32)
    l_s[...] = jnp.zeros(l_s.shape, F32)
    acc_s[...] = jnp.zeros(acc_s.shape, F32)
    q_start = q_off + qi * tq

    def update(k_src, v_src, off, w, bias_of_head, mask):
        for hp in range(n_pairs):
            kt = k_src[0, pl.ds(off, w), hp * LANE:(hp + 1) * LANE]
            vt = v_src[0, pl.ds(off, w), hp * LANE:(hp + 1) * LANE]
            s = _dot_nt(qm_s[hp], kt)
            s = jnp.concatenate([s[:tq] + bias_of_head(2 * hp), s[tq:] + bias_of_head(2 * hp + 1)], axis=0)
            if mask is not None:
                s = jnp.where(mask, s, -jnp.inf)
            _softmax_step(s, hp, m_s, l_s, acc_s, vt)

    def full_tile(j, carry):
        def bias(h):
            return jnp.concatenate([ncum_ref[h, j * (tk // LANE) + u] for u in range(tk // LANE)], axis=1)
        update(pk_ref, pv_ref, pl.multiple_of(j * tk, tk), tk, bias, None)
        return carry

    lax.fori_loop(0, q_start // tk, full_tile, 0)

    def diag_bias(h):
        if tq >= LANE:
            return jnp.concatenate([ncum_ref[h, q_start // LANE + u] for u in range(tq // LANE)], axis=1)
        lo = q_off % LANE
        return ncum_ref[h, q_start // LANE][:, lo:lo + tq]

    r = lax.broadcasted_iota(jnp.int32, (2 * tq, tq), 0)
    c = lax.broadcasted_iota(jnp.int32, (2 * tq, tq), 1)
    update(k_ref, v_ref, pl.multiple_of(qi * tq, tq), tq, diag_bias, c <= jnp.where(r >= tq, r - tq, r))
    for hp in range(n_pairs):
        o = acc_s[hp] / l_s[hp]
        o_ref[0, :, hp * LANE:(hp + 1) * LANE] = jnp.where(lane < DH_FOX, o[:tq], o[tq:]).astype(o_ref.dtype)


def fox_attention(q, k, v, logf_rows, *, tq, past=None, tk=256, cblk=256):
    b, lq, _ = q.shape
    lkp = logf_rows.shape[2]
    q_off = 0 if past is None else past[0].shape[1]
    nq = lq // tq
    assert k.shape[1] == lq and q_off % tk == 0 and (tq % tk == 0 or lq == tq) and lkp % cblk == 0
    assert tq >= LANE or (nq == 1 and q_off % LANE + tq <= LANE)
    kern = functools.partial(_fox_kernel, tq=tq, tk=tk, q_off=q_off, cblk=cblk)
    whole_seq = [pl.BlockSpec((1, a.shape[1], W_FOX), lambda bi, qi: (bi, 0, 0)) for a in (k, v) + tuple(past or ())]
    return pl.pallas_call(
        kern,
        grid=(b, nq),
        in_specs=[pl.BlockSpec((1, tq, W_FOX), lambda bi, qi: (bi, qi, 0))] + whole_seq + [
            pl.BlockSpec((1, logf_rows.shape[1], lkp), lambda bi, qi: (bi, 0, 0)),
        ],
        out_specs=pl.BlockSpec((1, tq, W_FOX), lambda bi, qi: (bi, qi, 0)),
        out_shape=jax.ShapeDtypeStruct((b, lq, W_FOX), BF16),
        scratch_shapes=[pltpu.VMEM((H_FOX, lkp // LANE, 1, LANE), F32), pltpu.VMEM((H_FOX // 2, 2 * tq, LANE), BF16),
                        pltpu.VMEM((H_FOX // 2, 2 * tq, LANE), F32), pltpu.VMEM((H_FOX // 2, 2 * tq, LANE), F32),
                        pltpu.VMEM((H_FOX // 2, 2 * tq, LANE), F32)],
        compiler_params=_params("parallel", "arbitrary"),
        name="fox_attention",
    )(q, k, v, *(past or ()), logf_rows)


def _mlstm_kernel(q_ref, k_ref, v_ref, g_ref, c0_ref, n0_ref, m0_ref,
                  h_ref, c_ref, n_ref, m_ref, c_s, n_s, m_s, *, chunk):
    ci = pl.program_id(1)
    L = chunk
    scale = DH_ML ** -0.5

    @pl.when(ci == 0)
    def _():
        c_s[...] = c0_ref[0]
        n_s[...] = n0_ref[0]
        m_s[...] = m0_ref[0]

    r = lax.broadcasted_iota(jnp.int32, (L, L), 0)
    c = lax.broadcasted_iota(jnp.int32, (L, L), 1)
    eye = r == c
    causal = c <= r
    upper = jnp.where(r <= c, 1.0, 0.0).astype(BF16)
    g = g_ref[0]
    g1, g2, g3 = _split3(g)
    cum = _dot(g1, upper) + _dot(g2, upper) + _dot(g3, upper)

    def to_col(row):
        return jnp.sum(jnp.where(eye, row, 0.0), axis=1, keepdims=True)

    for h in range(H_ML):
        sl = slice(h * DH_ML, (h + 1) * DH_ML)
        qh, kh, vh = q_ref[0, :, sl], k_ref[0, :, sl], v_ref[0, :, sl]
        ig_row = g[h:h + 1, :]
        bh_row = cum[H_ML + h:H_ML + h + 1, :]
        bh_col, ig_col = to_col(bh_row), to_col(ig_row)
        m0 = m_s[h]
        c0 = c_s[h]
        n0 = n_s[h]
        logd = jnp.where(causal, bh_col - bh_row + ig_row, -jnp.inf)
        inter = bh_col + m0
        m_col = jnp.maximum(inter, jnp.max(logd, axis=1, keepdims=True))
        d = jnp.exp(logd - m_col)
        a_col = jnp.exp(inter - m_col)
        s = _dot_nt(qh, kh) * scale * d
        num = _dot(s.astype(BF16), vh) + _dot_nt(qh, c0.astype(BF16)) * a_col
        qf = qh.astype(F32)
        den = jnp.sum(s, axis=1, keepdims=True) + a_col * jnp.sum(qf * n0, axis=1, keepdims=True)
        den = jnp.maximum(jnp.abs(den), jnp.exp(-m_col))
        h_ref[0, :, sl] = (num / den).astype(h_ref.dtype)

        m_l = m_col[L - 1:L, :]
        bh_l = bh_row[:, L - 1:L]
        a_l = jnp.exp(bh_l + m0 - m_l)
        w_col = jnp.exp(bh_l - bh_col + ig_col - m_l)
        vw_t = jnp.transpose(vh.astype(F32) * w_col).astype(BF16)
        c_s[h] = a_l * c0 + _dot(vw_t, kh) * scale
        n_s[h] = a_l * n0 + jnp.sum(kh.astype(F32) * w_col, axis=0, keepdims=True) * scale
        m_s[h] = m_l

    @pl.when(ci == pl.num_programs(1) - 1)
    def _():
        c_ref[0] = c_s[...]
        n_ref[0] = n_s[...]
        m_ref[0] = m_s[...]


def mlstm(q, k, v, gate_rows, c0, n0, m0, *, chunk):
    b, s, _ = q.shape
    nc = s // chunk
    seq = pl.BlockSpec((1, chunk, W_ML), lambda bi, ci: (bi, ci, 0))
    c_spec = pl.BlockSpec((1, H_ML, DH_ML, DH_ML), lambda bi, ci: (bi, 0, 0, 0))
    n_spec = pl.BlockSpec((1, H_ML, 1, DH_ML), lambda bi, ci: (bi, 0, 0, 0))
    m_spec = pl.BlockSpec((1, H_ML, 1, 1), lambda bi, ci: (bi, 0, 0, 0))
    return pl.pallas_call(
        functools.partial(_mlstm_kernel, chunk=chunk),
        grid=(b, nc),
        in_specs=[seq, seq, seq, pl.BlockSpec((1, gate_rows.shape[1], chunk), lambda bi, ci: (bi, 0, ci)),
                  c_spec, n_spec, m_spec],
        out_specs=[seq, c_spec, n_spec, m_spec],
        out_shape=[jax.ShapeDtypeStruct((b, s, W_ML), BF16),
                   jax.ShapeDtypeStruct(c0.shape, F32), jax.ShapeDtypeStruct(n0.shape, F32),
                   jax.ShapeDtypeStruct(m0.shape, F32)],
        scratch_shapes=[pltpu.VMEM((H_ML, DH_ML, DH_ML), F32), pltpu.VMEM((H_ML, 1, DH_ML), F32),
                        pltpu.VMEM((H_ML, 1, 1), F32)],
        compiler_params=_params("parallel", "arbitrary"),
        name="mlstm",
    )(q, k, v, gate_rows, c0, n0, m0)


N_PAIRS = EXP_PER_GROUP * (EXP_PER_GROUP - 1) // 2
N_BUCKETS = N_GROUPS * N_PAIRS
assert N_BUCKETS <= LANE
EXT = LANE
FEW_TOKENS = 1024


def _route_store(out, wrh_ref, wrl_ref, br_ref, ltri_ref, o_ref, cnt_ref, cnt_s):
    tm, d = out.shape

    @pl.when(pl.program_id(0) == 0)
    def _():
        cnt_s[...] = jnp.zeros(cnt_s.shape, F32)

    xh = out.astype(BF16)
    xl = (out - xh.astype(F32)).astype(BF16)
    z = _dot(xh, wrh_ref[...]) + _dot(xh, wrl_ref[...]) + _dot(xl, wrh_ref[...]) + br_ref[...]
    lane = lax.broadcasted_iota(jnp.int32, (tm, LANE), 1).astype(F32)
    far = float(LANE)
    neg = -jnp.inf
    gl = jnp.where(lane < N_GROUPS, z, neg)
    gmax = jnp.max(gl, axis=1, keepdims=True)
    g_idx = jnp.min(jnp.where(gl == gmax, lane, far), axis=1, keepdims=True)
    g_gate = 1.0 / jnp.sum(jnp.exp(gl - gmax), axis=1, keepdims=True)
    lo = N_GROUPS + EXP_PER_GROUP * g_idx
    el = jnp.where(jnp.logical_and(lane >= lo, lane < lo + EXP_PER_GROUP), z, neg)
    v1 = jnp.max(el, axis=1, keepdims=True)
    i1 = jnp.min(jnp.where(el == v1, lane, far), axis=1, keepdims=True)
    el2 = jnp.where(lane == i1, neg, el)
    v2 = jnp.max(el2, axis=1, keepdims=True)
    i2 = jnp.min(jnp.where(el2 == v2, lane, far), axis=1, keepdims=True)
    e21 = jnp.exp(v2 - v1)
    w1 = g_gate / (1.0 + e21)
    w2 = w1 * e21
    j1, j2 = i1 - lo, i2 - lo
    ja, jb = jnp.minimum(j1, j2), jnp.maximum(j1, j2)
    pair = ja * (2 * EXP_PER_GROUP - 1 - ja) * 0.5 + (jb - ja - 1.0)
    bucket = g_idx * N_PAIRS + pair
    first_low = j1 < j2
    wa = jnp.where(first_low, w1, w2)
    wb = jnp.where(first_low, w2, w1)
    onehot = jnp.where(lane == bucket, 1.0, 0.0)
    earlier = _dot(ltri_ref[...], onehot.astype(BF16))
    cnt = cnt_s[...]
    rank = jnp.sum(onehot * (earlier + cnt), axis=1, keepdims=True)
    cnt_new = cnt + jnp.sum(onehot, axis=0, keepdims=True)
    cnt_s[...] = cnt_new
    cnt_ref[...] = cnt_new
    o_ref[:, :d] = out
    e_lo = lo - N_GROUPS + ja
    e_hi = lo - N_GROUPS + jb
    record = (bucket, rank, wa, wb, e_lo, e_hi)
    ext = jnp.zeros((tm, LANE), F32)
    for k, col in enumerate(record):
        ext = jnp.where(lane == float(k), col, ext)
    o_ref[:, d:] = ext


def _route_operands(w_route, b_route, tm):
    n = w_route.shape[1]
    wp = jnp.zeros((w_route.shape[0], LANE), F32).at[:, :n].set(w_route)
    wh = wp.astype(BF16)
    wl = (wp - wh.astype(F32)).astype(BF16)
    bp = jnp.zeros((1, LANE), F32).at[0, :n].set(b_route)
    ltri = jnp.asarray(np.tril(np.ones((tm, tm), np.float32), -1), BF16)
    return wh, wl, bp, ltri


def _route_specs(k, tm):
    return [_whole((k, LANE)), _whole((k, LANE)), _whole((1, LANE)), _whole((tm, tm))]


def _ab_out_kernel(fo_ref, mh_ref, mo_ref, x_ref, gml_ref, w_ref, g_ref, b_ref, wrh_ref, wrl_ref, br_ref, ltri_ref,
                   o_ref, cnt_ref, cnt_s):
    parts = [fo_ref[...]]
    for h in range(H_ML):
        sl = slice(h * DH_ML, (h + 1) * DH_ML)
        mh = mh_ref[:, sl].astype(F32)
        ms = jnp.mean(mh * mh, axis=1, keepdims=True)
        nm = mh * lax.rsqrt(ms + RMS_EPS) * gml_ref[:, sl]
        parts.append((nm * _sigmoid(mo_ref[:, sl].astype(F32))).astype(BF16))
    cat = jnp.concatenate(parts, axis=1)
    y = ALPHA * x_ref[...] + _dot(cat, w_ref[...])
    out = _layer_norm(y, g_ref[...], b_ref[...])
    _route_store(out, wrh_ref, wrl_ref, br_ref, ltri_ref, o_ref, cnt_ref, cnt_s)


def _rows(tm, n):
    return pl.BlockSpec((tm, n), lambda i: (i, 0))


def _whole(shape):
    return pl.BlockSpec(shape, lambda i: (0,) * len(shape))


def _routed_out_call(kern, name, t, d, tm, in_specs, operands, route_w):
    w_route, b_route = route_w
    return pl.pallas_call(
        kern,
        grid=(t // tm,),
        in_specs=in_specs + _route_specs(d, tm),
        out_specs=[_rows(tm, d + EXT), _whole((1, LANE))],
        out_shape=[jax.ShapeDtypeStruct((t, d + EXT), F32), jax.ShapeDtypeStruct((1, LANE), F32)],
        scratch_shapes=[pltpu.VMEM((1, LANE), F32)],
        compiler_params=_params("arbitrary"),
        name=name,
    )(*operands, *_route_operands(w_route, b_route, tm))


def ab_out(fo, mh, mo, x, g_ml, w_out, ln_g, ln_b, route_w):
    t, d = x.shape
    tm = _row_tile(t)
    in_specs = [_rows(tm, W_FOX), _rows(tm, W_ML), _rows(tm, W_ML), _rows(tm, d), _whole((1, W_ML)),
                _whole(w_out.shape), _whole((1, d)), _whole((1, d))]
    return _routed_out_call(_ab_out_kernel, "ab_out", t, d, tm, in_specs,
                            (fo, mh, mo, x, g_ml, w_out, ln_g, ln_b), route_w)


def _rms(z, g):
    return z * lax.rsqrt(jnp.mean(z * z, axis=1, keepdims=True) + RMS_EPS) * g


def _mla_down_kernel(x_ref, w_ref, gq_ref, gkv_ref, cc_ref, ss_ref, *rest, n_prev):
    prev_c, prev_r = rest[:n_prev], rest[n_prev:2 * n_prev]
    cq_ref, ckv_ref, kr_ref, ckvb_ref, kr2_ref = rest[2 * n_prev:]

    def put(state_ref, prev, value):
        if n_prev:
            for j, p in enumerate(prev):
                state_ref[j] = p[...]
            state_ref[n_prev] = value
        else:
            state_ref[...] = value

    xb = x_ref[...].astype(BF16)
    cq_ref[...] = _rms(_dot(xb, w_ref[:, 0:Q_RANK]), gq_ref[...]).astype(BF16)
    ckv = _rms(_dot(xb, w_ref[:, Q_RANK:Q_RANK + KV_RANK]), gkv_ref[...])
    put(ckv_ref, prev_c, ckv)
    ckvb_ref[...] = ckv.astype(BF16)
    c0 = Q_RANK + KV_RANK
    rope = (_dot(xb, w_ref[:, c0:c0 + LANE]) * cc_ref[...] + _dot(xb, w_ref[:, c0 + LANE:c0 + 2 * LANE]) * ss_ref[...])
    put(kr_ref, prev_r, rope[:, :ROPE])
    kr2_ref[...] = rope.astype(BF16)


def _table_spec(table, tm, s):
    if table.shape[0] == tm:
        return pl.BlockSpec((tm, LANE), lambda i: (0, 0))
    per = s // tm
    return pl.BlockSpec((tm, LANE), lambda i: (i % per, 0))


def mla_down(x, w, g_q, g_kv, cc, ss, s, prev_c=(), prev_r=()):
    t, d = x.shape
    tm = _row_tile(t)
    tab = _table_spec(cc, tm, s)
    n_prev = len(prev_c)

    def state(width):
        if n_prev:
            return (pl.BlockSpec((n_prev + 1, tm, width), lambda i: (0, i, 0)),
                    jax.ShapeDtypeStruct((n_prev + 1, t, width), F32))
        return _rows(tm, width), jax.ShapeDtypeStruct((t, width), F32)

    (c_spec, c_shape), (r_spec, r_shape) = state(KV_RANK), state(ROPE)
    return pl.pallas_call(
        functools.partial(_mla_down_kernel, n_prev=n_prev),
        grid=(t // tm,),
        in_specs=[_rows(tm, d), _whole(w.shape), _whole((1, Q_RANK)), _whole((1, KV_RANK)), tab, tab]
        + [_rows(tm, KV_RANK)] * n_prev + [_rows(tm, ROPE)] * n_prev,
        out_specs=[_rows(tm, Q_RANK), c_spec, r_spec, _rows(tm, KV_RANK), _rows(tm, LANE)],
        out_shape=[jax.ShapeDtypeStruct((t, Q_RANK), BF16), c_shape, r_shape,
                   jax.ShapeDtypeStruct((t, KV_RANK), BF16), jax.ShapeDtypeStruct((t, LANE), BF16)],
        compiler_params=_params("parallel"),
        name="mla_down",
    )(x, w, g_q, g_kv, cc, ss, *prev_c, *prev_r)


def _mla_uq_kernel(cq_ref, w_ref, wuk_ref, cc_ref, ss_ref, qlat_ref, qrope_ref):
    cq = cq_ref[...]
    scale = (NOPE + ROPE) ** -0.5
    r0 = H_MLA * NOPE
    r1 = r0 + H_MLA * ROPE
    qn = _dot(cq, w_ref[:, :r0]).astype(BF16)
    for h in range(H_MLA):
        qlat_ref[:, h * KV_RANK:(h + 1) * KV_RANK] = (
            _dot(qn[:, h * NOPE:(h + 1) * NOPE], wuk_ref[h]) * scale).astype(BF16)
    qr = _dot(cq, w_ref[:, r0:])
    cc, ss = cc_ref[...] * scale, ss_ref[...] * scale
    for p in range(H_MLA // 2):
        a, b = p * LANE, (r1 - r0) + p * LANE
        qrope_ref[:, p * LANE:(p + 1) * LANE] = (qr[:, a:a + LANE] * cc + qr[:, b:b + LANE] * ss).astype(BF16)


def mla_uq(cq, w, wuk_t, cc, ss, s):
    t = cq.shape[0]
    tm = _row_tile(t)
    tab = _table_spec(cc, tm, s)
    return pl.pallas_call(
        _mla_uq_kernel,
        grid=(t // tm,),
        in_specs=[_rows(tm, Q_RANK), _whole(w.shape), _whole(wuk_t.shape), tab, tab],
        out_specs=[_rows(tm, H_MLA * KV_RANK), _rows(tm, H_MLA * ROPE)],
        out_shape=[jax.ShapeDtypeStruct((t, H_MLA * KV_RANK), BF16), jax.ShapeDtypeStruct((t, H_MLA * ROPE), BF16)],
        compiler_params=_params("parallel"),
        name="mla_uq",
    )(cq, w, wuk_t, cc, ss)


def _mla_attn_kernel(ql_ref, qr_ref, kc_ref, kr_ref, *rest, tq, tk, q_off):
    pc_ref, pr_ref = rest[:2] if len(rest) == 8 else (kc_ref, kr_ref)
    o_ref, ql_s, qr_s, m_s, l_s, acc_s = rest[-6:]
    qi = pl.program_id(1)
    lane = lax.broadcasted_iota(jnp.int32, (tq, LANE), 1)
    for h in range(H_MLA):
        rows = slice(h * tq, (h + 1) * tq)
        ql_s[rows, :] = ql_ref[0, :, h * KV_RANK:(h + 1) * KV_RANK]
        qr = qr_ref[0, :, (h // 2) * LANE:(h // 2 + 1) * LANE]
        keep = (lane < ROPE) if h % 2 == 0 else (lane >= ROPE)
        qr_s[rows, :] = jnp.where(keep, qr, jnp.zeros_like(qr))
    m_s[...] = jnp.full(m_s.shape, NEG_BIG, F32)
    l_s[...] = jnp.zeros(l_s.shape, F32)
    acc_s[...] = jnp.zeros(acc_s.shape, F32)
    q_start = q_off + qi * tq

    def update(c_src, r_src, off, w, mask):
        kc = c_src[0, pl.ds(off, w), :]
        kr = r_src[0, pl.ds(off, w), :]
        s = _dot_nt(ql_s[...], kc) + _dot_nt(qr_s[...], kr)
        if mask is not None:
            s = jnp.where(mask, s, -jnp.inf)
        _softmax_step(s, Ellipsis, m_s, l_s, acc_s, kc)

    def full_tile(j, carry):
        update(pc_ref, pr_ref, pl.multiple_of(j * tk, tk), tk, None)
        return carry

    lax.fori_loop(0, q_start // tk, full_tile, 0)
    r = lax.broadcasted_iota(jnp.int32, (H_MLA * tq, tq), 0)
    c = lax.broadcasted_iota(jnp.int32, (H_MLA * tq, tq), 1)
    update(kc_ref, kr_ref, pl.multiple_of(qi * tq, tq), tq, (c // CHUNK) <= ((r & (tq - 1)) // CHUNK))
    for h in range(H_MLA):
        rows = slice(h * tq, (h + 1) * tq)
        o = acc_s[rows, :] / _lanes(l_s[rows, :], KV_RANK)
        o_ref[0, :, h * KV_RANK:(h + 1) * KV_RANK] = o.astype(o_ref.dtype)


def mla_attention(q_lat, q_rope, kc, kr2, *, tq, tk, past=None):
    b, lq, _ = q_lat.shape
    q_off = 0 if past is None else past[0].shape[1]
    assert kc.shape[1] == lq and q_off % tk == 0 and (tq % tk == 0 or lq == tq)
    assert tq % CHUNK == 0 and (tq & (tq - 1)) == 0
    kern = functools.partial(_mla_attn_kernel, tq=tq, tk=tk, q_off=q_off)
    whole_seq = [pl.BlockSpec((1,) + a.shape[1:], lambda bi, qi: (bi, 0, 0)) for a in (kc, kr2) + tuple(past or ())]
    return pl.pallas_call(
        kern,
        grid=(b, lq // tq),
        in_specs=[
            pl.BlockSpec((1, tq, H_MLA * KV_RANK), lambda bi, qi: (bi, qi, 0)),
            pl.BlockSpec((1, tq, H_MLA * ROPE), lambda bi, qi: (bi, qi, 0)),
        ] + whole_seq,
        out_specs=pl.BlockSpec((1, tq, H_MLA * KV_RANK), lambda bi, qi: (bi, qi, 0)),
        out_shape=jax.ShapeDtypeStruct((b, lq, H_MLA * KV_RANK), BF16),
        scratch_shapes=[pltpu.VMEM((H_MLA * tq, KV_RANK), BF16), pltpu.VMEM((H_MLA * tq, LANE), BF16),
                        pltpu.VMEM((H_MLA * tq, LANE), F32), pltpu.VMEM((H_MLA * tq, LANE), F32),
                        pltpu.VMEM((H_MLA * tq, KV_RANK), F32)],
        compiler_params=_params("parallel", "arbitrary"),
        name="mla_attention",
    )(q_lat, q_rope, kc, kr2, *(past or ()))


def _mla_out_kernel(ol_ref, wuv_ref, w_ref, x_ref, g_ref, b_ref, wrh_ref, wrl_ref, br_ref, ltri_ref,
                    o_ref, cnt_ref, cnt_s):
    parts = [_dot(ol_ref[:, h * KV_RANK:(h + 1) * KV_RANK], wuv_ref[h]).astype(BF16) for h in range(H_MLA)]
    y = ALPHA * x_ref[...] + _dot(jnp.concatenate(parts, axis=1), w_ref[...])
    out = _layer_norm(y, g_ref[...], b_ref[...])
    _route_store(out, wrh_ref, wrl_ref, br_ref, ltri_ref, o_ref, cnt_ref, cnt_s)


def mla_out(o_lat, wuv, w_out, x, ln_g, ln_b, route_w):
    t, d = x.shape
    tm = _row_tile(t)
    in_specs = [_rows(tm, H_MLA * KV_RANK), _whole(wuv.shape), _whole(w_out.shape), _rows(tm, d),
                _whole((1, d)), _whole((1, d))]
    return _routed_out_call(_mla_out_kernel, "mla_out", t, d, tm, in_specs,
                            (o_lat, wuv, w_out, x, ln_g, ln_b), route_w)


ROW_SLOTS = 3


def _row_scatter_kernel(idx_ref, src_ref, dst_init_ref, dst_ref, buf, in_sem, out_sem, *, rows, n_steps):
    del dst_init_ref

    def load(i, slot):
        return pltpu.make_async_copy(src_ref.at[pl.ds(pl.multiple_of(i * rows, rows), rows)], buf.at[slot],
                                     in_sem.at[slot])

    def rows_done(slot):
        return pltpu.make_async_copy(buf.at[slot], dst_ref.at[pl.ds(0, rows)], out_sem.at[slot])

    i = pl.program_id(0)
    slot = i % ROW_SLOTS
    nxt = (i + 1) % ROW_SLOTS

    @pl.when(i == 0)
    def _():
        load(0, 0).start()

    @pl.when(i + 1 < n_steps)
    def _():
        @pl.when(i + 1 >= ROW_SLOTS)
        def _():
            rows_done(nxt).wait()
        load(i + 1, nxt).start()

    load(i, slot).wait()
    base = i * rows

    for r in range(rows):
        pltpu.make_async_copy(buf.at[slot, pl.ds(r, 1)], dst_ref.at[pl.ds(idx_ref[base + r], 1)],
                              out_sem.at[slot]).start()

    @pl.when(i == n_steps - 1)
    def _():
        for k in range(min(ROW_SLOTS, n_steps)):
            rows_done((n_steps - 1 - k) % ROW_SLOTS).wait()


def _row_gather_kernel(idx_ref, src_ref, dst_ref, buf, row_sem, out_sem, *, rows, n_steps):
    def rows_done(slot):
        return pltpu.make_async_copy(src_ref.at[pl.ds(0, rows)], buf.at[slot], row_sem.at[slot])

    def store(i, slot):
        return pltpu.make_async_copy(buf.at[slot], dst_ref.at[pl.ds(pl.multiple_of(i * rows, rows), rows)],
                                     out_sem.at[slot])

    i = pl.program_id(0)
    slot = i % ROW_SLOTS

    @pl.when(i >= ROW_SLOTS)
    def _():
        store(i - ROW_SLOTS, slot).wait()

    base = i * rows

    for r in range(rows):
        pltpu.make_async_copy(src_ref.at[pl.ds(idx_ref[base + r], 1)], buf.at[slot, pl.ds(r, 1)],
                              row_sem.at[slot]).start()

    @pl.when(i >= 1)
    def _():
        prev = (i + ROW_SLOTS - 1) % ROW_SLOTS
        rows_done(prev).wait()
        store(i - 1, prev).start()

    @pl.when(i == n_steps - 1)
    def _():
        last = (n_steps - 1) % ROW_SLOTS
        rows_done(last).wait()
        store(n_steps - 1, last).start()
        for k in range(min(ROW_SLOTS, n_steps)):
            store(n_steps - 1 - k, (n_steps - 1 - k) % ROW_SLOTS).wait()


def row_move(idx, src, dst_rows, *, scatter, dst_init=None):
    t = idx.shape[0]
    width = src.shape[1]
    rows = min(512, t)
    any_spec = pl.BlockSpec(memory_space=pl.ANY)
    operands = [idx, src] + ([dst_init] if scatter else [])
    grid_spec = pltpu.PrefetchScalarGridSpec(
        num_scalar_prefetch=1, grid=(t // rows,), in_specs=[any_spec] * (len(operands) - 1), out_specs=any_spec,
        scratch_shapes=[pltpu.VMEM((ROW_SLOTS, rows, width), src.dtype),
                        pltpu.SemaphoreType.DMA((ROW_SLOTS,)), pltpu.SemaphoreType.DMA((ROW_SLOTS,))])
    kern = _row_scatter_kernel if scatter else _row_gather_kernel
    return pl.pallas_call(
        functools.partial(kern, rows=rows, n_steps=t // rows),
        grid_spec=grid_spec,
        out_shape=jax.ShapeDtypeStruct((dst_rows, width), src.dtype),
        input_output_aliases={2: 0} if scatter else {},
        compiler_params=_params("arbitrary"),
        name="row_scatter" if scatter else "row_gather",
    )(*operands)


def _moe_kernel(ta_ref, tb_ref, nu_ref, xs_ref, w1a_ref, w3a_ref, w2a_ref, w1b_ref, w3b_ref, w2b_ref,
                g_ref, b_ref, o_ref, pend):
    t = pl.program_id(0)
    d = o_ref.shape[1]
    n_used = nu_ref[0]

    @pl.when(t == 0)
    def _():
        pend[...] = jnp.zeros(pend.shape, F32)

    @pl.when(t < n_used)
    def _():
        o_ref[...] = _layer_norm(pend[...], g_ref[...], b_ref[...])
        x = xs_ref[:, :d]
        ext = xs_ref[:, d:]
        xb = x.astype(BF16)

        def ffn(w1_ref, w3_ref, w2_ref):
            h1 = _dot(xb, w1_ref[0])
            h3 = _dot(xb, w3_ref[0])
            return _dot((h1 * _sigmoid(h1) * h3).astype(BF16), w2_ref[0])

        y = ext[:, 2:3] * ffn(w1a_ref, w3a_ref, w2a_ref) + ext[:, 3:4] * ffn(w1b_ref, w3b_ref, w2b_ref)
        pend[...] = ALPHA * x + y

    @pl.when(t == n_used)
    def _():
        o_ref[...] = _layer_norm(pend[...], g_ref[...], b_ref[...])

    @pl.when(t > n_used)
    def _():
        o_ref[...] = jnp.zeros(o_ref.shape, o_ref.dtype)


def moe_ln(tile_ea, tile_eb, n_used, xs, w1, w3, w2, ln_g, ln_b, tm):
    p, wide = xs.shape
    d = wide - EXT
    de = w1.shape[2]
    n = p // tm

    def w_spec(shape, which):
        return pl.BlockSpec(shape, lambda t, ta, tb, nu: ((ta, tb)[which][jnp.minimum(t, n - 1)], 0, 0))

    grid_spec = pltpu.PrefetchScalarGridSpec(
        num_scalar_prefetch=3,
        grid=(n + 1,),
        in_specs=[pl.BlockSpec((tm, wide), lambda t, ta, tb, nu: (jnp.minimum(t, n - 1), 0)),
                  w_spec((1, d, de), 0), w_spec((1, d, de), 0), w_spec((1, de, d), 0),
                  w_spec((1, d, de), 1), w_spec((1, d, de), 1), w_spec((1, de, d), 1),
                  pl.BlockSpec((1, d), lambda t, ta, tb, nu: (0, 0)),
                  pl.BlockSpec((1, d), lambda t, ta, tb, nu: (0, 0))],
        out_specs=pl.BlockSpec((tm, d), lambda t, ta, tb, nu: (jnp.maximum(t - 1, 0), 0)),
        scratch_shapes=[pltpu.VMEM((tm, d), F32)],
    )
    return pl.pallas_call(
        _moe_kernel,
        grid_spec=grid_spec,
        out_shape=jax.ShapeDtypeStruct((p, d), F32),
        compiler_params=_params("arbitrary"),
        name="moe_ln",
    )(tile_ea, tile_eb, n_used, xs, w1, w3, w2, w1, w3, w2, ln_g, ln_b)


def _moe_few_kernel(x_ref, w1_ref, w3_ref, w2_ref, g_ref, b_ref, o_ref, acc_s):
    e = pl.program_id(0)
    d = o_ref.shape[1]

    @pl.when(e == 0)
    def _():
        acc_s[...] = jnp.zeros(acc_s.shape, F32)

    x = x_ref[:, :d]
    ext = x_ref[:, d:]
    xb = x.astype(BF16)
    h1 = _dot(xb, w1_ref[0])
    h3 = _dot(xb, w3_ref[0])
    y = _dot((h1 * _sigmoid(h1) * h3).astype(BF16), w2_ref[0])
    ef = e.astype(F32)
    gate = jnp.where(ext[:, 4:5] == ef, ext[:, 2:3], 0.0) + jnp.where(ext[:, 5:6] == ef, ext[:, 3:4], 0.0)
    acc_s[...] += gate * y

    @pl.when(e == pl.num_programs(0) - 1)
    def _():
        o_ref[...] = _layer_norm(ALPHA * x + acc_s[...], g_ref[...], b_ref[...])


def moe_ln_few(x1ext, w1, w3, w2, ln_g, ln_b):
    t, wide = x1ext.shape
    d = wide - EXT
    de = w1.shape[2]
    return pl.pallas_call(
        _moe_few_kernel,
        grid=(w1.shape[0],),
        in_specs=[_whole((t, wide)), pl.BlockSpec((1, d, de), lambda e: (e, 0, 0)),
                  pl.BlockSpec((1, d, de), lambda e: (e, 0, 0)), pl.BlockSpec((1, de, d), lambda e: (e, 0, 0)),
                  _whole((1, d)), _whole((1, d))],
        out_specs=_whole((t, d)),
        out_shape=jax.ShapeDtypeStruct((t, d), F32),
        scratch_shapes=[pltpu.VMEM((t, d), F32)],
        compiler_params=_params("arbitrary"),
        name="moe_ln_few",
    )(x1ext, w1, w3, w2, ln_g, ln_b)


def _bucket_experts():
    ea, eb = [], []
    for g in range(N_GROUPS):
        for ja in range(EXP_PER_GROUP):
            for jb in range(ja + 1, EXP_PER_GROUP):
                ea.append(g * EXP_PER_GROUP + ja)
                eb.append(g * EXP_PER_GROUP + jb)
    return np.asarray(ea, np.int32), np.asarray(eb, np.int32)


def _bucket_layout(x1ext, counts, tm):
    t, wide = x1ext.shape
    d = wide - EXT
    bucket = x1ext[:, d].astype(jnp.int32)
    rank = x1ext[:, d + 1].astype(jnp.int32)
    counts = counts[0, :N_BUCKETS].astype(jnp.int32)
    padded = ((counts + tm - 1) // tm) * tm
    row_end = jnp.cumsum(padded)
    row_start = row_end - padded
    ids = jnp.arange(N_BUCKETS, dtype=jnp.int32)
    pos = rank + jnp.sum(jnp.where(bucket[:, None] == ids[None, :], row_start[None, :], 0), axis=1)
    n_rows = _round_up(t, tm) + N_BUCKETS * tm
    n_tiles = n_rows // tm
    n_used = (row_end[-1] // tm).astype(jnp.int32)
    tile_row = jnp.minimum(jnp.arange(n_tiles, dtype=jnp.int32), n_used - 1) * tm
    tile_bucket = jnp.sum((row_end[None, :] <= tile_row[:, None]).astype(jnp.int32), axis=1)
    ea, eb = _bucket_experts()
    in_bucket = tile_bucket[:, None] == ids[None, :]
    tile_ea = jnp.sum(jnp.where(in_bucket, jnp.asarray(ea)[None, :], 0), axis=1)
    tile_eb = jnp.sum(jnp.where(in_bucket, jnp.asarray(eb)[None, :], 0), axis=1)
    return pos.astype(jnp.int32), tile_ea, tile_eb, n_used.reshape(1), n_rows


def hier_moe_ln(x1ext, counts, moe_w, ln_g, ln_b, sorted_buf=None):
    w1, w3, w2 = moe_w
    t, wide = x1ext.shape
    if t <= FEW_TOKENS:
        return moe_ln_few(x1ext, w1, w3, w2, ln_g, ln_b), None
    tm = 256
    pos, tile_ea, tile_eb, n_used, n_rows = _bucket_layout(x1ext, counts, tm)
    if sorted_buf is None:
        sorted_buf = jnp.zeros((n_rows, wide), F32)
    xs = row_move(pos, x1ext, n_rows, scatter=True, dst_init=sorted_buf)
    ys = moe_ln(tile_ea, tile_eb, n_used, xs, w1, w3, w2, ln_g, ln_b, tm)
    return row_move(pos, ys, t, scatter=False), xs


def _rope_tables(s, offset, tm):
    half = ROPE // 2
    inv_freq = ROPE_BASE ** (-jnp.arange(half, dtype=F32) / half)
    ang = (offset + jnp.arange(s)).astype(F32)[:, None] * inv_freq[None, :]
    cos, sin = jnp.cos(ang), jnp.sin(ang)
    cc = jnp.concatenate([cos, cos, cos, cos], -1)
    ss = jnp.concatenate([-sin, sin, -sin, sin], -1)
    if tm > s:
        cc, ss = jnp.tile(cc, (tm // s, 1)), jnp.tile(ss, (tm // s, 1))
    return cc, ss


def _round_up(n, m):
    return (n + m - 1) // m * m


def ab_layer(x, b, s, wts, route_w, cache, prev_k=(), prev_v=()):
    w_big, w_gate, b_gate, g_ml, w_out, ln_g, ln_b = wts
    t = b * s
    ls_cols = np.array([1.0] * H_FOX + [0.0] * H_ML + [1.0] * H_ML)
    fq, mq, mk, mv, mo, fk_bf, fv_bf, gates, fk, fv = ab_proj(x, w_big, w_gate, b_gate, ls_cols, prev_k, prev_v)
    flf = gates[:, :H_FOX].reshape(b, s, H_FOX)
    ml_rows = jnp.swapaxes(gates[:, H_FOX:H_FOX + 2 * H_ML].reshape(b, s, 2 * H_ML), 1, 2)
    ml_rows = jnp.pad(ml_rows, ((0, 0), (0, 16 - 2 * H_ML), (0, 0)))
    if cache is None:
        past_kv, lf_all = None, flf
        c0 = jnp.zeros((b, H_ML, DH_ML, DH_ML), F32)
        n0 = jnp.zeros((b, H_ML, 1, DH_ML), F32)
        m0 = jnp.zeros((b, H_ML, 1, 1), F32)
        tq, chunk = 256, 256
    else:
        ck, cv, clf, c0, n0, m0 = cache
        past = ck.shape[1]
        past_kv = (ck.reshape(b, past, W_FOX).astype(BF16), cv.reshape(b, past, W_FOX).astype(BF16))
        lf_all = jnp.concatenate([clf, flf], 1)
        n0 = n0.reshape(b, H_ML, 1, DH_ML)
        m0 = m0.reshape(b, H_ML, 1, 1)
        tq, chunk = s, s
    lk = lf_all.shape[1]
    lf_rows = jnp.pad(jnp.swapaxes(lf_all, 1, 2), ((0, 0), (0, 16 - H_FOX), (0, _round_up(lk, 256) - lk)))
    fo = fox_attention(fq.reshape(b, s, W_FOX), fk_bf.reshape(b, s, W_FOX), fv_bf.reshape(b, s, W_FOX), lf_rows,
                       tq=tq, past=past_kv)
    mh, c_new, n_new, m_new = mlstm(mq.reshape(b, s, W_ML), mk.reshape(b, s, W_ML), mv.reshape(b, s, W_ML),
                                    ml_rows, c0, n0, m0, chunk=chunk)
    x1ext, counts = ab_out(fo.reshape(t, W_FOX), mh.reshape(t, W_ML), mo, x, g_ml, w_out, ln_g, ln_b, route_w)
    state = (fk, fv, flf, c_new, n_new.reshape(b, H_ML, DH_ML), m_new.reshape(b, H_ML))
    return x1ext, counts, state


def mla_layer(x, b, s, wts, route_w, cache, prev_c=(), prev_r=()):
    w_down, g_q, g_kv, w_uq, wuk_t, wuv, w_out, ln_g, ln_b = wts
    t = b * s
    tm = _row_tile(t)
    past = 0 if cache is None else cache[0].shape[1]
    cc, ss = _rope_tables(s, past, tm)
    cq, ckv, kr, ckv_bf, kr2_bf = mla_down(x, w_down, g_q, g_kv, cc, ss, s, prev_c, prev_r)
    q_lat, q_rope = mla_uq(cq, w_uq, wuk_t, cc, ss, s)
    kc = ckv_bf.reshape(b, s, KV_RANK)
    kr2 = kr2_bf.reshape(b, s, LANE)
    if cache is None:
        tq, past_k = 256, None
    else:
        c_ckv, c_kr = cache
        c_kr_bf = c_kr.astype(BF16)
        tq, past_k = s, (c_ckv.astype(BF16), jnp.concatenate([c_kr_bf, c_kr_bf], -1))
    o_lat = mla_attention(q_lat.reshape(b, s, -1), q_rope.reshape(b, s, -1), kc, kr2, tq=tq,
                          tk=tq if cache is None else 256, past=past_k)
    x1ext, counts = mla_out(o_lat.reshape(t, -1), wuv, w_out, x, ln_g, ln_b, route_w)
    return x1ext, counts, (ckv, kr)


def _prep_ab_weights(w_in, b_fox_f, b_ml_i, b_ml_f, g_ml, w_out, ln_g, ln_b):
    sizes = (W_FOX, W_FOX, W_FOX, H_FOX, W_ML, W_ML, W_ML, H_ML, H_ML, W_ML)
    idx = np.cumsum(sizes[:-1]).tolist()
    fq, fk, fv, ff, mq, mk, mv, mi, mf, mo = jnp.split(w_in, idx, axis=1)
    w_big = jnp.concatenate([fq, mq, mk, mv, mo, fk, fv], 1).astype(BF16)
    w_gate = jnp.concatenate([ff, mi, mf], 1)
    b_gate = jnp.concatenate([b_fox_f, b_ml_i, b_ml_f])
    return (w_big, w_gate, b_gate, g_ml[None, :], w_out.astype(BF16), ln_g[None, :], ln_b[None, :])


def _prep_mla_weights(w_down, g_q, w_uq, g_kv, w_uk, w_uv, w_out, ln_g, ln_b):
    half = ROPE // 2
    cq_w, ckv_w, kr_w = jnp.split(w_down, [Q_RANK, Q_RANK + KV_RANK], axis=1)
    kr_sw = jnp.concatenate([kr_w[:, half:], kr_w[:, :half]], 1)
    w_down_p = jnp.concatenate([cq_w, ckv_w, kr_w, kr_w, kr_sw, kr_sw], 1).astype(BF16)
    uq = w_uq.reshape(Q_RANK, H_MLA, NOPE + ROPE)
    uq_nope = uq[:, :, :NOPE].reshape(Q_RANK, H_MLA * NOPE)
    uq_rope = uq[:, :, NOPE:]
    uq_rope_sw = jnp.concatenate([uq_rope[..., half:], uq_rope[..., :half]], -1)
    w_uq_p = jnp.concatenate([uq_nope, uq_rope.reshape(Q_RANK, -1), uq_rope_sw.reshape(Q_RANK, -1)], 1).astype(BF16)
    wuk_t = jnp.transpose(w_uk, (1, 2, 0)).astype(BF16)
    wuv = jnp.transpose(w_uv, (1, 0, 2)).astype(BF16)
    return (w_down_p, g_q[None, :], g_kv[None, :], w_uq_p, wuk_t, wuv, w_out.astype(BF16), ln_g[None, :], ln_b[None, :])


def kernel(x_prompt, x_sample, cache_fox_k, cache_fox_v, cache_fox_logf, state_mlstm_c, state_mlstm_n, state_mlstm_m, cache_mla_ckv, cache_mla_krope, w_ab_in, b_fox_f, b_mlstm_i, b_mlstm_f, g_mlstm_norm, w_ab_out, w_mla_down, g_mla_q, w_mla_uq, g_mla_kv, w_mla_uk, w_mla_uv, w_mla_out, ln1_g, ln1_b, ln2_g, ln2_b, w_moe_group, b_moe_group, w_moe_router, b_moe_router, w_exp_gate, w_exp_up, w_exp_down):
    bp, sp, d = x_prompt.shape
    bs, ss_, _ = x_sample.shape
    xp = x_prompt.reshape(bp * sp, d)
    xs = x_sample.reshape(bs * ss_, d)
    ab_p, ab_s, c_p, c_s = [], [], [], []
    n_ab, n_c = (DEPTH + 1) // 2, DEPTH // 2
    assert min(n_ab, n_c) >= 2
    buf_p = buf_s = None
    for l in range(DEPTH):
        j = l // 2
        route_w = (jnp.concatenate([w_moe_group[l], w_moe_router[l]], 1),
                   jnp.concatenate([b_moe_group[l], b_moe_router[l]]))
        if l % 2 == 0:
            wts = _prep_ab_weights(w_ab_in[j], b_fox_f[j], b_mlstm_i[j], b_mlstm_f[j], g_mlstm_norm[j],
                                   w_ab_out[j], ln1_g[l], ln1_b[l])
            last = j == n_ab - 1
            prev = lambda sts, i: tuple(st[i] for st in sts) if last else ()
            xp1, cnt_p, st_p = ab_layer(xp, bp, sp, wts, route_w, None, prev(ab_p, 0), prev(ab_p, 1))
            xs1, cnt_s, st_s = ab_layer(xs, bs, ss_, wts, route_w,
                                        (cache_fox_k[j], cache_fox_v[j], cache_fox_logf[j],
                                         state_mlstm_c[j], state_mlstm_n[j], state_mlstm_m[j]),
                                        prev(ab_s, 0), prev(ab_s, 1))
            ab_p.append(st_p)
            ab_s.append(st_s)
        else:
            wts = _prep_mla_weights(w_mla_down[j], g_mla_q[j], w_mla_uq[j], g_mla_kv[j], w_mla_uk[j], w_mla_uv[j],
                                    w_mla_out[j], ln1_g[l], ln1_b[l])
            last = j == n_c - 1
            prev = lambda sts, i: tuple(st[i] for st in sts) if last else ()
            xp1, cnt_p, st_p = mla_layer(xp, bp, sp, wts, route_w, None, prev(c_p, 0), prev(c_p, 1))
            xs1, cnt_s, st_s = mla_layer(xs, bs, ss_, wts, route_w, (cache_mla_ckv[j], cache_mla_krope[j]),
                                         prev(c_s, 0), prev(c_s, 1))
            c_p.append(st_p)
            c_s.append(st_s)
        moe_w = (w_exp_gate[l].astype(BF16), w_exp_up[l].astype(BF16), w_exp_down[l].astype(BF16))
        xp, buf_p = hier_moe_ln(xp1, cnt_p, moe_w, ln2_g[l][None, :], ln2_b[l][None, :], buf_p)
        xs, buf_s = hier_moe_ln(xs1, cnt_s, moe_w, ln2_g[l][None, :], ln2_b[l][None, :], buf_s)

    def stack(groups, i):
        return jnp.stack([g[i] for g in groups])

    def states(ab, c, b, s):
        kv = [ab[-1][i].reshape(n_ab, b, s, H_FOX, DH_FOX) for i in (0, 1)]
        lat = [c[-1][0].reshape(n_c, b, s, KV_RANK), c[-1][1].reshape(n_c, b, s, ROPE)]
        return tuple(kv) + tuple(stack(ab, i) for i in (2, 3, 4, 5)) + tuple(lat)

    return (xp.reshape(bp, sp, d), xs.reshape(bs, ss_, d)) + states(ab_p, c_p, bp, sp) + states(ab_s, c_s, bs, ss_)
```

```python
import functools

import numpy as np
import jax
import jax.numpy as jnp
from jax import lax
from jax.experimental import pallas as pl
from jax.experimental.pallas import tpu as pltpu

F32 = jnp.float32
BF16 = jnp.bfloat16

LANE = 128
VMEM_LIMIT_BYTES = 48 * 1024 * 1024

H_FOX, DH_FOX = 8, 64
W_FOX = H_FOX * DH_FOX
H_ML, DH_ML = 4, 128
W_ML = H_ML * DH_ML
H_MLA, NOPE, ROPE, DV_MLA = 8, 128, 64, 128
Q_RANK, KV_RANK = 384, 256
ROPE_BASE = 10000.0
N_GROUPS, EXP_PER_GROUP, TOP_K = 4, 8, 2
N_EXPERTS = N_GROUPS * EXP_PER_GROUP
DEPTH = 4
ALPHA = (2 * DEPTH) ** 0.25
LN_EPS = 1e-5
RMS_EPS = 1e-6
CHUNK = 64
NEG_BIG = -1e30


def _params(*sem):
    return pltpu.CompilerParams(dimension_semantics=sem, vmem_limit_bytes=VMEM_LIMIT_BYTES)


def _dot(a, b):
    return jnp.dot(a, b, preferred_element_type=F32)


def _dot_nt(a, b):
    return lax.dot_general(a, b, (((1,), (1,)), ((), ())), preferred_element_type=F32)


def _split3(x):
    hi = x.astype(BF16)
    r = x - hi.astype(F32)
    mid = r.astype(BF16)
    lo = (r - mid.astype(F32)).astype(BF16)
    return hi, mid, lo


def _layer_norm(y, g, b):
    mu = jnp.mean(y, axis=1, keepdims=True)
    yc = y - mu
    var = jnp.mean(yc * yc, axis=1, keepdims=True)
    return yc * lax.rsqrt(var + LN_EPS) * g + b


def _sigmoid(x):
    return 1.0 / (1.0 + jnp.exp(-x))


def _row_tile(t):
    return min(512, t)


def _ab_proj_kernel(x_ref, w_ref, gwh_ref, gwl_ref, gb_ref, gmask_ref,
                    fq_ref, mq_ref, mk_ref, mv_ref, mo_ref, kbf_ref, vbf_ref, gates_ref, k_ref, v_ref):
    x = x_ref[...]
    xh = x.astype(BF16)
    c0 = 0
    for o_ref in (fq_ref, mq_ref, mk_ref, mv_ref, mo_ref):
        n = o_ref.shape[1]
        o_ref[...] = _dot(xh, w_ref[:, c0:c0 + n]).astype(BF16)
        c0 += n
    for bf_ref, state_ref in ((kbf_ref, k_ref), (vbf_ref, v_ref)):
        z = _dot(xh, w_ref[:, c0:c0 + W_FOX])
        c0 += W_FOX
        bf_ref[...] = z.astype(BF16)
        state_ref[...] = z
    xl = (x - xh.astype(F32)).astype(BF16)
    g = _dot(xh, gwh_ref[...]) + _dot(xh, gwl_ref[...]) + _dot(xl, gwh_ref[...]) + gb_ref[...]
    log_sig = jnp.minimum(g, 0.0) - jnp.log1p(jnp.exp(-jnp.abs(g)))
    gates_ref[...] = jnp.where(gmask_ref[...] > 0.0, log_sig, g)


def ab_proj(x, w_big, w_gate, b_gate, log_sigmoid_cols):
    t, k = x.shape
    tm = _row_tile(t)
    n_gate = w_gate.shape[1]
    wp = jnp.zeros((k, LANE), F32).at[:, :n_gate].set(w_gate)
    wh = wp.astype(BF16)
    wl = (wp - wh.astype(F32)).astype(BF16)
    bp = jnp.zeros((1, LANE), F32).at[0, :n_gate].set(b_gate)
    mask = jnp.zeros((1, LANE), F32).at[0, :n_gate].set(jnp.asarray(log_sigmoid_cols, F32))
    bf = lambda n: jax.ShapeDtypeStruct((t, n), BF16)
    return pl.pallas_call(
        _ab_proj_kernel,
        grid=(t // tm,),
        in_specs=[_rows(tm, k), _whole(w_big.shape), _whole((k, LANE)), _whole((k, LANE)), _whole((1, LANE)),
                  _whole((1, LANE))],
        out_specs=[_rows(tm, W_FOX)] + [_rows(tm, W_ML)] * 4 + [_rows(tm, W_FOX)] * 2 + [_rows(tm, LANE)]
        + [_rows(tm, W_FOX)] * 2,
        out_shape=[bf(W_FOX)] + [bf(W_ML)] * 4 + [bf(W_FOX)] * 2 + [jax.ShapeDtypeStruct((t, LANE), F32)]
        + [jax.ShapeDtypeStruct((t, W_FOX), F32)] * 2,
        compiler_params=_params("parallel"),
        name="ab_proj",
    )(x, w_big, wh, wl, bp, mask)


def _lanes(x, w):
    if w < LANE:
        return x[:, :w]
    return x if w == LANE else jnp.concatenate([x] * (w // LANE), axis=1)


def _softmax_step(s, at, m_s, l_s, acc_s, v):
    w = s.shape[1]
    m_prev = m_s[at]
    m_new = jnp.maximum(m_prev, jnp.max(s, axis=1, keepdims=True))
    p = jnp.exp(s - _lanes(m_new, w))
    a = jnp.exp(m_prev - m_new)
    l_s[at] = a * l_s[at] + jnp.sum(p, axis=1, keepdims=True)
    acc_s[at] = acc_s[at] * _lanes(a, acc_s.shape[-1]) + _dot(p.astype(BF16), v)
    m_s[at] = m_new


def _fox_kernel(q_ref, k_ref, v_ref, *rest, tq, tk, q_off, cblk):
    pk_ref, pv_ref = rest[:2] if len(rest) == 9 else (k_ref, v_ref)
    lf_ref, o_ref, ncum_ref, qm_s, m_s, l_s, acc_s = rest[-7:]
    qi = pl.program_id(1)
    sub = cblk // LANE
    n_pairs = H_FOX // 2

    @pl.when(qi == 0)
    def _():
        r = lax.broadcasted_iota(jnp.int32, (cblk, cblk), 0)
        c = lax.broadcasted_iota(jnp.int32, (cblk, cblk), 1)
        upper = jnp.where(r <= c, 1.0, 0.0).astype(BF16)
        carry = jnp.zeros((lf_ref.shape[1], 1), F32)
        for j in range(lf_ref.shape[2] // cblk):
            g1, g2, g3 = _split3(lf_ref[0, :, j * cblk:(j + 1) * cblk])
            cum = _dot(g1, upper) + _dot(g2, upper) + _dot(g3, upper) + carry
            carry = cum[:, cblk - 1:cblk]
            for h in range(H_FOX):
                for u in range(sub):
                    ncum_ref[h, j * sub + u] = -cum[h:h + 1, u * LANE:(u + 1) * LANE]

    lane = lax.broadcasted_iota(jnp.int32, (tq, LANE), 1)
    for hp in range(n_pairs):
        q = q_ref[0, :, hp * LANE:(hp + 1) * LANE] * (DH_FOX ** -0.5)
        zero = jnp.zeros_like(q)
        qm_s[hp, 0:tq, :] = jnp.where(lane < DH_FOX, q, zero)
        qm_s[hp, tq:2 * tq, :] = jnp.where(lane >= DH_FOX, q, zero)
    m_s[...] = jnp.full(m_s.shape, NEG_BIG, F32)
    l_s[...] = jnp.zeros(l_s.shape, F32)
    acc_s[...] = jnp.zeros(acc_s.shape, F32)
    q_start = q_off + qi * tq

    def update(k_src, v_src, off, w, bias_of_head, mask):
        for hp in range(n_pairs):
            kt = k_src[0, pl.ds(off, w), hp * LANE:(hp + 1) * LANE]
            vt = v_src[0, pl.ds(off, w), hp * LANE:(hp + 1) * LANE]
            s = _dot_nt(qm_s[hp], kt)
            s = jnp.concatenate([s[:tq] + bias_of_head(2 * hp), s[tq:] + bias_of_head(2 * hp + 1)], axis=0)
            if mask is not None:
                s = jnp.where(mask, s, -jnp.inf)
            _softmax_step(s, hp, m_s, l_s, acc_s, vt)

    def full_tile(j, carry):
        def bias(h):
            return jnp.concatenate([ncum_ref[h, j * (tk // LANE) + u] for u in range(tk // LANE)], axis=1)
        update(pk_ref, pv_ref, pl.multiple_of(j * tk, tk), tk, bias, None)
        return carry

    lax.fori_loop(0, q_start // tk, full_tile, 0)

    def diag_bias(h):
        if tq >= LANE:
            return jnp.concatenate([ncum_ref[h, q_start // LANE + u] for u in range(tq // LANE)], axis=1)
        lo = q_off % LANE
        return ncum_ref[h, q_start // LANE][:, lo:lo + tq]

    r = lax.broadcasted_iota(jnp.int32, (2 * tq, tq), 0)
    c = lax.broadcasted_iota(jnp.int32, (2 * tq, tq), 1)
    update(k_ref, v_ref, pl.multiple_of(qi * tq, tq), tq, diag_bias, c <= jnp.where(r >= tq, r - tq, r))
    for hp in range(n_pairs):
        o = acc_s[hp] / l_s[hp]
        o_ref[0, :, hp * LANE:(hp + 1) * LANE] = jnp.where(lane < DH_FOX, o[:tq], o[tq:]).astype(o_ref.dtype)


def fox_attention(q, k, v, logf_rows, *, tq, past=None, tk=256, cblk=256):
    b, lq, _ = q.shape
    lkp = logf_rows.shape[2]
    q_off = 0 if past is None else past[0].shape[1]
    nq = lq // tq
    assert k.shape[1] == lq and q_off % tk == 0 and (tq % tk == 0 or lq == tq) and lkp % cblk == 0
    assert tq >= LANE or (nq == 1 and q_off % LANE + tq <= LANE)
    kern = functools.partial(_fox_kernel, tq=tq, tk=tk, q_off=q_off, cblk=cblk)
    whole_seq = [pl.BlockSpec((1, a.shape[1], W_FOX), lambda bi, qi: (bi, 0, 0)) for a in (k, v) + tuple(past or ())]
    return pl.pallas_call(
        kern,
        grid=(b, nq),
        in_specs=[pl.BlockSpec((1, tq, W_FOX), lambda bi, qi: (bi, qi, 0))] + whole_seq + [
            pl.BlockSpec((1, logf_rows.shape[1], lkp), lambda bi, qi: (bi, 0, 0)),
        ],
        out_specs=pl.BlockSpec((1, tq, W_FOX), lambda bi, qi: (bi, qi, 0)),
        out_shape=jax.ShapeDtypeStruct((b, lq, W_FOX), BF16),
        scratch_shapes=[pltpu.VMEM((H_FOX, lkp // LANE, 1, LANE), F32), pltpu.VMEM((H_FOX // 2, 2 * tq, LANE), BF16),
                        pltpu.VMEM((H_FOX // 2, 2 * tq, LANE), F32), pltpu.VMEM((H_FOX // 2, 2 * tq, LANE), F32),
                        pltpu.VMEM((H_FOX // 2, 2 * tq, LANE), F32)],
        compiler_params=_params("parallel", "arbitrary"),
        name="fox_attention",
    )(q, k, v, *(past or ()), logf_rows)


def _mlstm_kernel(q_ref, k_ref, v_ref, g_ref, c0_ref, n0_ref, m0_ref,
                  h_ref, c_ref, n_ref, m_ref, c_s, n_s, m_s, *, chunk):
    ci = pl.program_id(1)
    L = chunk
    scale = DH_ML ** -0.5

    @pl.when(ci == 0)
    def _():
        c_s[...] = c0_ref[0]
        n_s[...] = n0_ref[0]
        m_s[...] = m0_ref[0]

    r = lax.broadcasted_iota(jnp.int32, (L, L), 0)
    c = lax.broadcasted_iota(jnp.int32, (L, L), 1)
    eye = r == c
    causal = c <= r
    upper = jnp.where(r <= c, 1.0, 0.0).astype(BF16)
    g = g_ref[0]
    g1, g2, g3 = _split3(g)
    cum = _dot(g1, upper) + _dot(g2, upper) + _dot(g3, upper)

    def to_col(row):
        return jnp.sum(jnp.where(eye, row, 0.0), axis=1, keepdims=True)

    for h in range(H_ML):
        sl = slice(h * DH_ML, (h + 1) * DH_ML)
        qh, kh, vh = q_ref[0, :, sl], k_ref[0, :, sl], v_ref[0, :, sl]
        ig_row = g[h:h + 1, :]
        bh_row = cum[H_ML + h:H_ML + h + 1, :]
        bh_col, ig_col = to_col(bh_row), to_col(ig_row)
        m0 = m_s[h]
        c0 = c_s[h]
        n0 = n_s[h]
        logd = jnp.where(causal, bh_col - bh_row + ig_row, -jnp.inf)
        inter = bh_col + m0
        m_col = jnp.maximum(inter, jnp.max(logd, axis=1, keepdims=True))
        d = jnp.exp(logd - m_col)
        a_col = jnp.exp(inter - m_col)
        s = _dot_nt(qh, kh) * scale * d
        num = _dot(s.astype(BF16), vh) + _dot_nt(qh, c0.astype(BF16)) * a_col
        qf = qh.astype(F32)
        den = jnp.sum(s, axis=1, keepdims=True) + a_col * jnp.sum(qf * n0, axis=1, keepdims=True)
        den = jnp.maximum(jnp.abs(den), jnp.exp(-m_col))
        h_ref[0, :, sl] = (num / den).astype(h_ref.dtype)

        m_l = m_col[L - 1:L, :]
        bh_l = bh_row[:, L - 1:L]
        a_l = jnp.exp(bh_l + m0 - m_l)
        w_col = jnp.exp(bh_l - bh_col + ig_col - m_l)
        vw_t = jnp.transpose(vh.astype(F32) * w_col).astype(BF16)
        c_s[h] = a_l * c0 + _dot(vw_t, kh) * scale
        n_s[h] = a_l * n0 + jnp.sum(kh.astype(F32) * w_col, axis=0, keepdims=True) * scale
        m_s[h] = m_l

    @pl.when(ci == pl.num_programs(1) - 1)
    def _():
        c_ref[0] = c_s[...]
        n_ref[0] = n_s[...]
        m_ref[0] = m_s[...]


def mlstm(q, k, v, gate_rows, c0, n0, m0, *, chunk):
    b, s, _ = q.shape
    nc = s // chunk
    seq = pl.BlockSpec((1, chunk, W_ML), lambda bi, ci: (bi, ci, 0))
    c_spec = pl.BlockSpec((1, H_ML, DH_ML, DH_ML), lambda bi, ci: (bi, 0, 0, 0))
    n_spec = pl.BlockSpec((1, H_ML, 1, DH_ML), lambda bi, ci: (bi, 0, 0, 0))
    m_spec = pl.BlockSpec((1, H_ML, 1, 1), lambda bi, ci: (bi, 0, 0, 0))
    return pl.pallas_call(
        functools.partial(_mlstm_kernel, chunk=chunk),
        grid=(b, nc),
        in_specs=[seq, seq, seq, pl.BlockSpec((1, gate_rows.shape[1], chunk), lambda bi, ci: (bi, 0, ci)),
                  c_spec, n_spec, m_spec],
        out_specs=[seq, c_spec, n_spec, m_spec],
        out_shape=[jax.ShapeDtypeStruct((b, s, W_ML), BF16),
                   jax.ShapeDtypeStruct(c0.shape, F32), jax.ShapeDtypeStruct(n0.shape, F32),
                   jax.ShapeDtypeStruct(m0.shape, F32)],
        scratch_shapes=[pltpu.VMEM((H_ML, DH_ML, DH_ML), F32), pltpu.VMEM((H_ML, 1, DH_ML), F32),
                        pltpu.VMEM((H_ML, 1, 1), F32)],
        compiler_params=_params("parallel", "arbitrary"),
        name="mlstm",
    )(q, k, v, gate_rows, c0, n0, m0)


N_PAIRS = EXP_PER_GROUP * (EXP_PER_GROUP - 1) // 2
N_BUCKETS = N_GROUPS * N_PAIRS
assert N_BUCKETS <= LANE
EXT = LANE
FEW_TOKENS = 1024


def _route_store(out, wrh_ref, wrl_ref, br_ref, ltri_ref, o_ref, cnt_ref, cnt_s):
    tm, d = out.shape

    @pl.when(pl.program_id(0) == 0)
    def _():
        cnt_s[...] = jnp.zeros(cnt_s.shape, F32)

    xh = out.astype(BF16)
    xl = (out - xh.astype(F32)).astype(BF16)
    z = _dot(xh, wrh_ref[...]) + _dot(xh, wrl_ref[...]) + _dot(xl, wrh_ref[...]) + br_ref[...]
    lane = lax.broadcasted_iota(jnp.int32, (tm, LANE), 1).astype(F32)
    far = float(LANE)
    neg = -jnp.inf
    gl = jnp.where(lane < N_GROUPS, z, neg)
    gmax = jnp.max(gl, axis=1, keepdims=True)
    g_idx = jnp.min(jnp.where(gl == gmax, lane, far), axis=1, keepdims=True)
    g_gate = 1.0 / jnp.sum(jnp.exp(gl - gmax), axis=1, keepdims=True)
    lo = N_GROUPS + EXP_PER_GROUP * g_idx
    el = jnp.where(jnp.logical_and(lane >= lo, lane < lo + EXP_PER_GROUP), z, neg)
    v1 = jnp.max(el, axis=1, keepdims=True)
    i1 = jnp.min(jnp.where(el == v1, lane, far), axis=1, keepdims=True)
    el2 = jnp.where(lane == i1, neg, el)
    v2 = jnp.max(el2, axis=1, keepdims=True)
    i2 = jnp.min(jnp.where(el2 == v2, lane, far), axis=1, keepdims=True)
    e21 = jnp.exp(v2 - v1)
    w1 = g_gate / (1.0 + e21)
    w2 = w1 * e21
    j1, j2 = i1 - lo, i2 - lo
    ja, jb = jnp.minimum(j1, j2), jnp.maximum(j1, j2)
    pair = ja * (2 * EXP_PER_GROUP - 1 - ja) * 0.5 + (jb - ja - 1.0)
    bucket = g_idx * N_PAIRS + pair
    first_low = j1 < j2
    wa = jnp.where(first_low, w1, w2)
    wb = jnp.where(first_low, w2, w1)
    onehot = jnp.where(lane == bucket, 1.0, 0.0)
    earlier = _dot(ltri_ref[...], onehot.astype(BF16))
    cnt = cnt_s[...]
    rank = jnp.sum(onehot * (earlier + cnt), axis=1, keepdims=True)
    cnt_new = cnt + jnp.sum(onehot, axis=0, keepdims=True)
    cnt_s[...] = cnt_new
    cnt_ref[...] = cnt_new
    o_ref[:, :d] = out
    e_lo = lo - N_GROUPS + ja
    e_hi = lo - N_GROUPS + jb
    record = (bucket, rank, wa, wb, e_lo, e_hi)
    ext = jnp.zeros((tm, LANE), F32)
    for k, col in enumerate(record):
        ext = jnp.where(lane == float(k), col, ext)
    o_ref[:, d:] = ext


def _route_operands(w_route, b_route, tm):
    n = w_route.shape[1]
    wp = jnp.zeros((w_route.shape[0], LANE), F32).at[:, :n].set(w_route)
    wh = wp.astype(BF16)
    wl = (wp - wh.astype(F32)).astype(BF16)
    bp = jnp.zeros((1, LANE), F32).at[0, :n].set(b_route)
    ltri = jnp.asarray(np.tril(np.ones((tm, tm), np.float32), -1), BF16)
    return wh, wl, bp, ltri


def _route_specs(k, tm):
    return [_whole((k, LANE)), _whole((k, LANE)), _whole((1, LANE)), _whole((tm, tm))]


def _ab_out_kernel(fo_ref, mh_ref, mo_ref, x_ref, gml_ref, w_ref, g_ref, b_ref, wrh_ref, wrl_ref, br_ref, ltri_ref,
                   o_ref, cnt_ref, cnt_s):
    parts = [fo_ref[...]]
    for h in range(H_ML):
        sl = slice(h * DH_ML, (h + 1) * DH_ML)
        mh = mh_ref[:, sl].astype(F32)
        ms = jnp.mean(mh * mh, axis=1, keepdims=True)
        nm = mh * lax.rsqrt(ms + RMS_EPS) * gml_ref[:, sl]
        parts.append((nm * _sigmoid(mo_ref[:, sl].astype(F32))).astype(BF16))
    cat = jnp.concatenate(parts, axis=1)
    y = ALPHA * x_ref[...] + _dot(cat, w_ref[...])
    out = _layer_norm(y, g_ref[...], b_ref[...])
    _route_store(out, wrh_ref, wrl_ref, br_ref, ltri_ref, o_ref, cnt_ref, cnt_s)


def _rows(tm, n):
    return pl.BlockSpec((tm, n), lambda i: (i, 0))


def _whole(shape):
    return pl.BlockSpec(shape, lambda i: (0,) * len(shape))


def _routed_out_call(kern, name, t, d, tm, in_specs, operands, route_w):
    w_route, b_route = route_w
    return pl.pallas_call(
        kern,
        grid=(t // tm,),
        in_specs=in_specs + _route_specs(d, tm),
        out_specs=[_rows(tm, d + EXT), _whole((1, LANE))],
        out_shape=[jax.ShapeDtypeStruct((t, d + EXT), F32), jax.ShapeDtypeStruct((1, LANE), F32)],
        scratch_shapes=[pltpu.VMEM((1, LANE), F32)],
        compiler_params=_params("arbitrary"),
        name=name,
    )(*operands, *_route_operands(w_route, b_route, tm))


def ab_out(fo, mh, mo, x, g_ml, w_out, ln_g, ln_b, route_w):
    t, d = x.shape
    tm = _row_tile(t)
    in_specs = [_rows(tm, W_FOX), _rows(tm, W_ML), _rows(tm, W_ML), _rows(tm, d), _whole((1, W_ML)),
                _whole(w_out.shape), _whole((1, d)), _whole((1, d))]
    return _routed_out_call(_ab_out_kernel, "ab_out", t, d, tm, in_specs,
                            (fo, mh, mo, x, g_ml, w_out, ln_g, ln_b), route_w)


def _rms(z, g):
    return z * lax.rsqrt(jnp.mean(z * z, axis=1, keepdims=True) + RMS_EPS) * g


def _mla_down_kernel(x_ref, w_ref, gq_ref, gkv_ref, cc_ref, ss_ref, *rest, n_prev):
    prev_c, prev_r = rest[:n_prev], rest[n_prev:2 * n_prev]
    cq_ref, ckv_ref, kr_ref, ckvb_ref, kr2_ref = rest[2 * n_prev:]

    def put(state_ref, prev, value):
        if n_prev:
            for j, p in enumerate(prev):
                state_ref[j] = p[...]
            state_ref[n_prev] = value
        else:
            state_ref[...] = value

    xb = x_ref[...].astype(BF16)
    cq_ref[...] = _rms(_dot(xb, w_ref[:, 0:Q_RANK]), gq_ref[...]).astype(BF16)
    ckv = _rms(_dot(xb, w_ref[:, Q_RANK:Q_RANK + KV_RANK]), gkv_ref[...])
    put(ckv_ref, prev_c, ckv)
    ckvb_ref[...] = ckv.astype(BF16)
    c0 = Q_RANK + KV_RANK
    rope = (_dot(xb, w_ref[:, c0:c0 + LANE]) * cc_ref[...] + _dot(xb, w_ref[:, c0 + LANE:c0 + 2 * LANE]) * ss_ref[...])
    put(kr_ref, prev_r, rope[:, :ROPE])
    kr2_ref[...] = rope.astype(BF16)


def _table_spec(table, tm, s):
    if table.shape[0] == tm:
        return pl.BlockSpec((tm, LANE), lambda i: (0, 0))
    per = s // tm
    return pl.BlockSpec((tm, LANE), lambda i: (i % per, 0))


def mla_down(x, w, g_q, g_kv, cc, ss, s, prev_c=(), prev_r=()):
    t, d = x.shape
    tm = _row_tile(t)
    tab = _table_spec(cc, tm, s)
    n_prev = len(prev_c)

    def state(width):
        if n_prev:
            return (pl.BlockSpec((n_prev + 1, tm, width), lambda i: (0, i, 0)),
                    jax.ShapeDtypeStruct((n_prev + 1, t, width), F32))
        return _rows(tm, width), jax.ShapeDtypeStruct((t, width), F32)

    (c_spec, c_shape), (r_spec, r_shape) = state(KV_RANK), state(ROPE)
    return pl.pallas_call(
        functools.partial(_mla_down_kernel, n_prev=n_prev),
        grid=(t // tm,),
        in_specs=[_rows(tm, d), _whole(w.shape), _whole((1, Q_RANK)), _whole((1, KV_RANK)), tab, tab]
        + [_rows(tm, KV_RANK)] * n_prev + [_rows(tm, ROPE)] * n_prev,
        out_specs=[_rows(tm, Q_RANK), c_spec, r_spec, _rows(tm, KV_RANK), _rows(tm, LANE)],
        out_shape=[jax.ShapeDtypeStruct((t, Q_RANK), BF16), c_shape, r_shape,
                   jax.ShapeDtypeStruct((t, KV_RANK), BF16), jax.ShapeDtypeStruct((t, LANE), BF16)],
        compiler_params=_params("parallel"),
        name="mla_down",
    )(x, w, g_q, g_kv, cc, ss, *prev_c, *prev_r)


def _mla_uq_kernel(cq_ref, w_ref, wuk_ref, cc_ref, ss_ref, qlat_ref, qrope_ref):
    cq = cq_ref[...]
    scale = (NOPE + ROPE) ** -0.5
    r0 = H_MLA * NOPE
    r1 = r0 + H_MLA * ROPE
    qn = _dot(cq, w_ref[:, :r0]).astype(BF16)
    for h in range(H_MLA):
        qlat_ref[:, h * KV_RANK:(h + 1) * KV_RANK] = (
            _dot(qn[:, h * NOPE:(h + 1) * NOPE], wuk_ref[h]) * scale).astype(BF16)
    qr = _dot(cq, w_ref[:, r0:])
    cc, ss = cc_ref[...] * scale, ss_ref[...] * scale
    for p in range(H_MLA // 2):
        a, b = p * LANE, (r1 - r0) + p * LANE
        qrope_ref[:, p * LANE:(p + 1) * LANE] = (qr[:, a:a + LANE] * cc + qr[:, b:b + LANE] * ss).astype(BF16)


def mla_uq(cq, w, wuk_t, cc, ss, s):
    t = cq.shape[0]
    tm = _row_tile(t)
    tab = _table_spec(cc, tm, s)
    return pl.pallas_call(
        _mla_uq_kernel,
        grid=(t // tm,),
        in_specs=[_rows(tm, Q_RANK), _whole(w.shape), _whole(wuk_t.shape), tab, tab],
        out_specs=[_rows(tm, H_MLA * KV_RANK), _rows(tm, H_MLA * ROPE)],
        out_shape=[jax.ShapeDtypeStruct((t, H_MLA * KV_RANK), BF16), jax.ShapeDtypeStruct((t, H_MLA * ROPE), BF16)],
        compiler_params=_params("parallel"),
        name="mla_uq",
    )(cq, w, wuk_t, cc, ss)


def _mla_attn_kernel(ql_ref, qr_ref, kc_ref, kr_ref, *rest, tq, tk, q_off):
    pc_ref, pr_ref = rest[:2] if len(rest) == 8 else (kc_ref, kr_ref)
    o_ref, ql_s, qr_s, m_s, l_s, acc_s = rest[-6:]
    qi = pl.program_id(1)
    lane = lax.broadcasted_iota(jnp.int32, (tq, LANE), 1)
    for h in range(H_MLA):
        rows = slice(h * tq, (h + 1) * tq)
        ql_s[rows, :] = ql_ref[0, :, h * KV_RANK:(h + 1) * KV_RANK]
        qr = qr_ref[0, :, (h // 2) * LANE:(h // 2 + 1) * LANE]
        keep = (lane < ROPE) if h % 2 == 0 else (lane >= ROPE)
        qr_s[rows, :] = jnp.where(keep, qr, jnp.zeros_like(qr))
    m_s[...] = jnp.full(m_s.shape, NEG_BIG, F32)
    l_s[...] = jnp.zeros(l_s.shape, F32)
    acc_s[...] = jnp.zeros(acc_s.shape, F32)
    q_start = q_off + qi * tq

    def update(c_src, r_src, off, w, mask):
        kc = c_src[0, pl.ds(off, w), :]
        kr = r_src[0, pl.ds(off, w), :]
        s = _dot_nt(ql_s[...], kc) + _dot_nt(qr_s[...], kr)
        if mask is not None:
            s = jnp.where(mask, s, -jnp.inf)
        _softmax_step(s, Ellipsis, m_s, l_s, acc_s, kc)

    def full_tile(j, carry):
        update(pc_ref, pr_ref, pl.multiple_of(j * tk, tk), tk, None)
        return carry

    lax.fori_loop(0, q_start // tk, full_tile, 0)
    r = lax.broadcasted_iota(jnp.int32, (H_MLA * tq, tq), 0)
    c = lax.broadcasted_iota(jnp.int32, (H_MLA * tq, tq), 1)
    update(kc_ref, kr_ref, pl.multiple_of(qi * tq, tq), tq, (c // CHUNK) <= ((r & (tq - 1)) // CHUNK))
    for h in range(H_MLA):
        rows = slice(h * tq, (h + 1) * tq)
        o = acc_s[rows, :] / _lanes(l_s[rows, :], KV_RANK)
        o_ref[0, :, h * KV_RANK:(h + 1) * KV_RANK] = o.astype(o_ref.dtype)


def _mla_attn_pipelined_kernel(ql_ref, qr_ref, kc_ref, kr_ref, o_ref, ql_s, qr_s, m_s, l_s, acc_s, s_buf, *, tq):
    qi = pl.program_id(1)
    lane = lax.broadcasted_iota(jnp.int32, (tq, LANE), 1)
    for h in range(H_MLA):
        rows = slice(h * tq, (h + 1) * tq)
        ql_s[rows, :] = ql_ref[0, :, h * KV_RANK:(h + 1) * KV_RANK]
        qr = qr_ref[0, :, (h // 2) * LANE:(h // 2 + 1) * LANE]
        keep = (lane < ROPE) if h % 2 == 0 else (lane >= ROPE)
        qr_s[rows, :] = jnp.where(keep, qr, jnp.zeros_like(qr))
    m_s[...] = jnp.full(m_s.shape, NEG_BIG, F32)
    l_s[...] = jnp.zeros(l_s.shape, F32)
    acc_s[...] = jnp.zeros(acc_s.shape, F32)

    def keys(ref, j):
        return ref[0, pl.ds(pl.multiple_of(j * tq, tq), tq), :]

    def score(j, slot):
        s_buf[slot] = _dot_nt(ql_s[...], keys(kc_ref, j)) + _dot_nt(qr_s[...], keys(kr_ref, j))

    def consume(j, slot, mask):
        s = s_buf[slot]
        if mask is not None:
            s = jnp.where(mask, s, -jnp.inf)
        _softmax_step(s, Ellipsis, m_s, l_s, acc_s, keys(kc_ref, j))

    score(0, 0)

    def two_tiles(ip, carry):
        j = 2 * ip
        score(j + 1, 1)
        consume(j, 0, None)
        score(j + 2, 0)
        consume(j + 1, 1, None)
        return carry

    lax.fori_loop(0, qi // 2, two_tiles, 0)
    r = lax.broadcasted_iota(jnp.int32, (H_MLA * tq, tq), 0)
    c = lax.broadcasted_iota(jnp.int32, (H_MLA * tq, tq), 1)
    mask = (c // CHUNK) <= ((r & (tq - 1)) // CHUNK)

    @pl.when(qi % 2 == 1)
    def _():
        score(qi, 1)
        consume(qi - 1, 0, None)
        consume(qi, 1, mask)

    @pl.when(qi % 2 == 0)
    def _():
        consume(qi, 0, mask)

    for h in range(H_MLA):
        rows = slice(h * tq, (h + 1) * tq)
        o = acc_s[rows, :] / _lanes(l_s[rows, :], KV_RANK)
        o_ref[0, :, h * KV_RANK:(h + 1) * KV_RANK] = o.astype(o_ref.dtype)


def mla_attention(q_lat, q_rope, kc, kr2, *, tq, tk, past=None):
    b, lq, _ = q_lat.shape
    q_off = 0 if past is None else past[0].shape[1]
    assert kc.shape[1] == lq and q_off % tk == 0 and (tq % tk == 0 or lq == tq)
    assert tq % CHUNK == 0 and (tq & (tq - 1)) == 0
    scratch = [pltpu.VMEM((H_MLA * tq, KV_RANK), BF16), pltpu.VMEM((H_MLA * tq, LANE), BF16),
               pltpu.VMEM((H_MLA * tq, LANE), F32), pltpu.VMEM((H_MLA * tq, LANE), F32),
               pltpu.VMEM((H_MLA * tq, KV_RANK), F32)]
    if past is None and tk == tq:
        kern = functools.partial(_mla_attn_pipelined_kernel, tq=tq)
        scratch.append(pltpu.VMEM((2, H_MLA * tq, tq), F32))
    else:
        kern = functools.partial(_mla_attn_kernel, tq=tq, tk=tk, q_off=q_off)
    whole_seq = [pl.BlockSpec((1,) + a.shape[1:], lambda bi, qi: (bi, 0, 0)) for a in (kc, kr2) + tuple(past or ())]
    return pl.pallas_call(
        kern,
        grid=(b, lq // tq),
        in_specs=[
            pl.BlockSpec((1, tq, H_MLA * KV_RANK), lambda bi, qi: (bi, qi, 0)),
            pl.BlockSpec((1, tq, H_MLA * ROPE), lambda bi, qi: (bi, qi, 0)),
        ] + whole_seq,
        out_specs=pl.BlockSpec((1, tq, H_MLA * KV_RANK), lambda bi, qi: (bi, qi, 0)),
        out_shape=jax.ShapeDtypeStruct((b, lq, H_MLA * KV_RANK), BF16),
        scratch_shapes=scratch,
        compiler_params=_params("parallel", "arbitrary"),
        name="mla_attention",
    )(q_lat, q_rope, kc, kr2, *(past or ()))


def _mla_out_kernel(ol_ref, wuv_ref, w_ref, x_ref, g_ref, b_ref, wrh_ref, wrl_ref, br_ref, ltri_ref,
                    o_ref, cnt_ref, cnt_s):
    parts = [_dot(ol_ref[:, h * KV_RANK:(h + 1) * KV_RANK], wuv_ref[h]).astype(BF16) for h in range(H_MLA)]
    y = ALPHA * x_ref[...] + _dot(jnp.concatenate(parts, axis=1), w_ref[...])
    out = _layer_norm(y, g_ref[...], b_ref[...])
    _route_store(out, wrh_ref, wrl_ref, br_ref, ltri_ref, o_ref, cnt_ref, cnt_s)


def mla_out(o_lat, wuv, w_out, x, ln_g, ln_b, route_w):
    t, d = x.shape
    tm = _row_tile(t)
    in_specs = [_rows(tm, H_MLA * KV_RANK), _whole(wuv.shape), _whole(w_out.shape), _rows(tm, d),
                _whole((1, d)), _whole((1, d))]
    return _routed_out_call(_mla_out_kernel, "mla_out", t, d, tm, in_specs,
                            (o_lat, wuv, w_out, x, ln_g, ln_b), route_w)


ROW_SLOTS = 3


def _row_scatter_kernel(idx_ref, src_ref, dst_init_ref, dst_ref, buf, in_sem, out_sem, *, rows, n_steps):
    del dst_init_ref

    def load(i, slot):
        return pltpu.make_async_copy(src_ref.at[pl.ds(pl.multiple_of(i * rows, rows), rows)], buf.at[slot],
                                     in_sem.at[slot])

    def rows_done(slot):
        return pltpu.make_async_copy(buf.at[slot], dst_ref.at[pl.ds(0, rows)], out_sem.at[slot])

    i = pl.program_id(0)
    slot = i % ROW_SLOTS
    nxt = (i + 1) % ROW_SLOTS

    @pl.when(i == 0)
    def _():
        load(0, 0).start()

    @pl.when(i + 1 < n_steps)
    def _():
        @pl.when(i + 1 >= ROW_SLOTS)
        def _():
            rows_done(nxt).wait()
        load(i + 1, nxt).start()

    load(i, slot).wait()
    base = i * rows

    for r in range(rows):
        pltpu.make_async_copy(buf.at[slot, pl.ds(r, 1)], dst_ref.at[pl.ds(idx_ref[base + r], 1)],
                              out_sem.at[slot]).start()

    @pl.when(i == n_steps - 1)
    def _():
        for k in range(min(ROW_SLOTS, n_steps)):
            rows_done((n_steps - 1 - k) % ROW_SLOTS).wait()


def _row_gather_kernel(idx_ref, src_ref, dst_ref, buf, row_sem, out_sem, *, rows, n_steps):
    def rows_done(slot):
        return pltpu.make_async_copy(src_ref.at[pl.ds(0, rows)], buf.at[slot], row_sem.at[slot])

    def store(i, slot):
        return pltpu.make_async_copy(buf.at[slot], dst_ref.at[pl.ds(pl.multiple_of(i * rows, rows), rows)],
                                     out_sem.at[slot])

    i = pl.program_id(0)
    slot = i % ROW_SLOTS

    @pl.when(i >= ROW_SLOTS)
    def _():
        store(i - ROW_SLOTS, slot).wait()

    base = i * rows

    for r in range(rows):
        pltpu.make_async_copy(src_ref.at[pl.ds(idx_ref[base + r], 1)], buf.at[slot, pl.ds(r, 1)],
                              row_sem.at[slot]).start()

    @pl.when(i >= 1)
    def _():
        prev = (i + ROW_SLOTS - 1) % ROW_SLOTS
        rows_done(prev).wait()
        store(i - 1, prev).start()

    @pl.when(i == n_steps - 1)
    def _():
        last = (n_steps - 1) % ROW_SLOTS
        rows_done(last).wait()
        store(n_steps - 1, last).start()
        for k in range(min(ROW_SLOTS, n_steps)):
            store(n_steps - 1 - k, (n_steps - 1 - k) % ROW_SLOTS).wait()


def row_move(idx, src, dst_rows, *, scatter, dst_init=None):
    t = idx.shape[0]
    width = src.shape[1]
    rows = min(512, t)
    any_spec = pl.BlockSpec(memory_space=pl.ANY)
    operands = [idx, src] + ([dst_init] if scatter else [])
    grid_spec = pltpu.PrefetchScalarGridSpec(
        num_scalar_prefetch=1, grid=(t // rows,), in_specs=[any_spec] * (len(operands) - 1), out_specs=any_spec,
        scratch_shapes=[pltpu.VMEM((ROW_SLOTS, rows, width), src.dtype),
                        pltpu.SemaphoreType.DMA((ROW_SLOTS,)), pltpu.SemaphoreType.DMA((ROW_SLOTS,))])
    kern = _row_scatter_kernel if scatter else _row_gather_kernel
    return pl.pallas_call(
        functools.partial(kern, rows=rows, n_steps=t // rows),
        grid_spec=grid_spec,
        out_shape=jax.ShapeDtypeStruct((dst_rows, width), src.dtype),
        input_output_aliases={2: 0} if scatter else {},
        compiler_params=_params("arbitrary"),
        name="row_scatter" if scatter else "row_gather",
    )(*operands)


def _moe_kernel(ta_ref, tb_ref, nu_ref, xs_ref, w1a_ref, w3a_ref, w2a_ref, w1b_ref, w3b_ref, w2b_ref,
                g_ref, b_ref, o_ref, pend):
    t = pl.program_id(0)
    d = o_ref.shape[1]
    n_used = nu_ref[0]

    @pl.when(t == 0)
    def _():
        pend[...] = jnp.zeros(pend.shape, F32)

    @pl.when(t < n_used)
    def _():
        o_ref[...] = _layer_norm(pend[...], g_ref[...], b_ref[...])
        x = xs_ref[:, :d]
        ext = xs_ref[:, d:]
        xb = x.astype(BF16)

        def ffn(w1_ref, w3_ref, w2_ref):
            h1 = _dot(xb, w1_ref[0])
            h3 = _dot(xb, w3_ref[0])
            return _dot((h1 * _sigmoid(h1) * h3).astype(BF16), w2_ref[0])

        y = ext[:, 2:3] * ffn(w1a_ref, w3a_ref, w2a_ref) + ext[:, 3:4] * ffn(w1b_ref, w3b_ref, w2b_ref)
        pend[...] = ALPHA * x + y

    @pl.when(t == n_used)
    def _():
        o_ref[...] = _layer_norm(pend[...], g_ref[...], b_ref[...])

    @pl.when(t > n_used)
    def _():
        o_ref[...] = jnp.zeros(o_ref.shape, o_ref.dtype)


def moe_ln(tile_ea, tile_eb, n_used, xs, w1, w3, w2, ln_g, ln_b, tm):
    p, wide = xs.shape
    d = wide - EXT
    de = w1.shape[2]
    n = p // tm

    def w_spec(shape, which):
        return pl.BlockSpec(shape, lambda t, ta, tb, nu: ((ta, tb)[which][jnp.minimum(t, n - 1)], 0, 0))

    grid_spec = pltpu.PrefetchScalarGridSpec(
        num_scalar_prefetch=3,
        grid=(n + 1,),
        in_specs=[pl.BlockSpec((tm, wide), lambda t, ta, tb, nu: (jnp.minimum(t, n - 1), 0)),
                  w_spec((1, d, de), 0), w_spec((1, d, de), 0), w_spec((1, de, d), 0),
                  w_spec((1, d, de), 1), w_spec((1, d, de), 1), w_spec((1, de, d), 1),
                  pl.BlockSpec((1, d), lambda t, ta, tb, nu: (0, 0)),
                  pl.BlockSpec((1, d), lambda t, ta, tb, nu: (0, 0))],
        out_specs=pl.BlockSpec((tm, d), lambda t, ta, tb, nu: (jnp.maximum(t - 1, 0), 0)),
        scratch_shapes=[pltpu.VMEM((tm, d), F32)],
    )
    return pl.pallas_call(
        _moe_kernel,
        grid_spec=grid_spec,
        out_shape=jax.ShapeDtypeStruct((p, d), F32),
        compiler_params=_params("arbitrary"),
        name="moe_ln",
    )(tile_ea, tile_eb, n_used, xs, w1, w3, w2, w1, w3, w2, ln_g, ln_b)


def _moe_few_kernel(x_ref, w1_ref, w3_ref, w2_ref, g_ref, b_ref, o_ref, acc_s):
    e = pl.program_id(0)
    d = o_ref.shape[1]

    @pl.when(e == 0)
    def _():
        acc_s[...] = jnp.zeros(acc_s.shape, F32)

    x = x_ref[:, :d]
    ext = x_ref[:, d:]
    xb = x.astype(BF16)
    h1 = _dot(xb, w1_ref[0])
    h3 = _dot(xb, w3_ref[0])
    y = _dot((h1 * _sigmoid(h1) * h3).astype(BF16), w2_ref[0])
    ef = e.astype(F32)
    gate = jnp.where(ext[:, 4:5] == ef, ext[:, 2:3], 0.0) + jnp.where(ext[:, 5:6] == ef, ext[:, 3:4], 0.0)
    acc_s[...] += gate * y

    @pl.when(e == pl.num_programs(0) - 1)
    def _():
        o_ref[...] = _layer_norm(ALPHA * x + acc_s[...], g_ref[...], b_ref[...])


def moe_ln_few(x1ext, w1, w3, w2, ln_g, ln_b):
    t, wide = x1ext.shape
    d = wide - EXT
    de = w1.shape[2]
    return pl.pallas_call(
        _moe_few_kernel,
        grid=(w1.shape[0],),
        in_specs=[_whole((t, wide)), pl.BlockSpec((1, d, de), lambda e: (e, 0, 0)),
                  pl.BlockSpec((1, d, de), lambda e: (e, 0, 0)), pl.BlockSpec((1, de, d), lambda e: (e, 0, 0)),
                  _whole((1, d)), _whole((1, d))],
        out_specs=_whole((t, d)),
        out_shape=jax.ShapeDtypeStruct((t, d), F32),
        scratch_shapes=[pltpu.VMEM((t, d), F32)],
        compiler_params=_params("arbitrary"),
        name="moe_ln_few",
    )(x1ext, w1, w3, w2, ln_g, ln_b)


def _bucket_experts():
    ea, eb = [], []
    for g in range(N_GROUPS):
        for ja in range(EXP_PER_GROUP):
            for jb in range(ja + 1, EXP_PER_GROUP):
                ea.append(g * EXP_PER_GROUP + ja)
                eb.append(g * EXP_PER_GROUP + jb)
    return np.asarray(ea, np.int32), np.asarray(eb, np.int32)


def _bucket_layout(x1ext, counts, tm):
    t, wide = x1ext.shape
    d = wide - EXT
    bucket = x1ext[:, d].astype(jnp.int32)
    rank = x1ext[:, d + 1].astype(jnp.int32)
    counts = counts[0, :N_BUCKETS].astype(jnp.int32)
    padded = ((counts + tm - 1) // tm) * tm
    row_end = jnp.cumsum(padded)
    row_start = row_end - padded
    ids = jnp.arange(N_BUCKETS, dtype=jnp.int32)
    pos = rank + jnp.sum(jnp.where(bucket[:, None] == ids[None, :], row_start[None, :], 0), axis=1)
    n_rows = _round_up(t, tm) + N_BUCKETS * tm
    n_tiles = n_rows // tm
    n_used = (row_end[-1] // tm).astype(jnp.int32)
    tile_row = jnp.minimum(jnp.arange(n_tiles, dtype=jnp.int32), n_used - 1) * tm
    tile_bucket = jnp.sum((row_end[None, :] <= tile_row[:, None]).astype(jnp.int32), axis=1)
    ea, eb = _bucket_experts()
    in_bucket = tile_bucket[:, None] == ids[None, :]
    tile_ea = jnp.sum(jnp.where(in_bucket, jnp.asarray(ea)[None, :], 0), axis=1)
    tile_eb = jnp.sum(jnp.where(in_bucket, jnp.asarray(eb)[None, :], 0), axis=1)
    return pos.astype(jnp.int32), tile_ea, tile_eb, n_used.reshape(1), n_rows


def hier_moe_ln(x1ext, counts, moe_w, ln_g, ln_b, sorted_buf=None):
    w1, w3, w2 = moe_w
    t, wide = x1ext.shape
    if t <= FEW_TOKENS:
        return moe_ln_few(x1ext, w1, w3, w2, ln_g, ln_b), None
    tm = 256
    pos, tile_ea, tile_eb, n_used, n_rows = _bucket_layout(x1ext, counts, tm)
    if sorted_buf is None:
        sorted_buf = jnp.zeros((n_rows, wide), F32)
    xs = row_move(pos, x1ext, n_rows, scatter=True, dst_init=sorted_buf)
    ys = moe_ln(tile_ea, tile_eb, n_used, xs, w1, w3, w2, ln_g, ln_b, tm)
    return row_move(pos, ys, t, scatter=False), xs


def _rope_tables(s, offset, tm):
    half = ROPE // 2
    inv_freq = ROPE_BASE ** (-jnp.arange(half, dtype=F32) / half)
    ang = (offset + jnp.arange(s)).astype(F32)[:, None] * inv_freq[None, :]
    cos, sin = jnp.cos(ang), jnp.sin(ang)
    cc = jnp.concatenate([cos, cos, cos, cos], -1)
    ss = jnp.concatenate([-sin, sin, -sin, sin], -1)
    if tm > s:
        cc, ss = jnp.tile(cc, (tm // s, 1)), jnp.tile(ss, (tm // s, 1))
    return cc, ss


def _round_up(n, m):
    return (n + m - 1) // m * m


def ab_layer(x, b, s, wts, route_w, cache):
    w_big, w_gate, b_gate, g_ml, w_out, ln_g, ln_b = wts
    t = b * s
    ls_cols = np.array([1.0] * H_FOX + [0.0] * H_ML + [1.0] * H_ML)
    fq, mq, mk, mv, mo, fk_bf, fv_bf, gates, fk, fv = ab_proj(x, w_big, w_gate, b_gate, ls_cols)
    flf = gates[:, :H_FOX].reshape(b, s, H_FOX)
    ml_rows = jnp.swapaxes(gates[:, H_FOX:H_FOX + 2 * H_ML].reshape(b, s, 2 * H_ML), 1, 2)
    ml_rows = jnp.pad(ml_rows, ((0, 0), (0, 16 - 2 * H_ML), (0, 0)))
    if cache is None:
        past_kv, lf_all = None, flf
        c0 = jnp.zeros((b, H_ML, DH_ML, DH_ML), F32)
        n0 = jnp.zeros((b, H_ML, 1, DH_ML), F32)
        m0 = jnp.zeros((b, H_ML, 1, 1), F32)
        tq, chunk = min(512, s), 256
    else:
        ck, cv, clf, c0, n0, m0 = cache
        past = ck.shape[1]
        past_kv = (ck.reshape(b, past, W_FOX).astype(BF16), cv.reshape(b, past, W_FOX).astype(BF16))
        lf_all = jnp.concatenate([clf, flf], 1)
        n0 = n0.reshape(b, H_ML, 1, DH_ML)
        m0 = m0.reshape(b, H_ML, 1, 1)
        tq, chunk = s, s
    lk = lf_all.shape[1]
    lf_rows = jnp.pad(jnp.swapaxes(lf_all, 1, 2), ((0, 0), (0, 16 - H_FOX), (0, _round_up(lk, 256) - lk)))
    fo = fox_attention(fq.reshape(b, s, W_FOX), fk_bf.reshape(b, s, W_FOX), fv_bf.reshape(b, s, W_FOX), lf_rows,
                       tq=tq, past=past_kv, tk=256 if cache is not None else tq)
    mh, c_new, n_new, m_new = mlstm(mq.reshape(b, s, W_ML), mk.reshape(b, s, W_ML), mv.reshape(b, s, W_ML),
                                    ml_rows, c0, n0, m0, chunk=chunk)
    x1ext, counts = ab_out(fo.reshape(t, W_FOX), mh.reshape(t, W_ML), mo, x, g_ml, w_out, ln_g, ln_b, route_w)
    state = (fk, fv, flf, c_new, n_new.reshape(b, H_ML, DH_ML), m_new.reshape(b, H_ML))
    return x1ext, counts, state


def mla_layer(x, b, s, wts, route_w, cache, prev_c=(), prev_r=()):
    w_down, g_q, g_kv, w_uq, wuk_t, wuv, w_out, ln_g, ln_b = wts
    t = b * s
    tm = _row_tile(t)
    past = 0 if cache is None else cache[0].shape[1]
    cc, ss = _rope_tables(s, past, tm)
    cq, ckv, kr, ckv_bf, kr2_bf = mla_down(x, w_down, g_q, g_kv, cc, ss, s, prev_c, prev_r)
    q_lat, q_rope = mla_uq(cq, w_uq, wuk_t, cc, ss, s)
    kc = ckv_bf.reshape(b, s, KV_RANK)
    kr2 = kr2_bf.reshape(b, s, LANE)
    if cache is None:
        tq, past_k = 256, None
    else:
        c_ckv, c_kr = cache
        c_kr_bf = c_kr.astype(BF16)
        tq, past_k = s, (c_ckv.astype(BF16), jnp.concatenate([c_kr_bf, c_kr_bf], -1))
    o_lat = mla_attention(q_lat.reshape(b, s, -1), q_rope.reshape(b, s, -1), kc, kr2, tq=tq,
                          tk=tq if cache is None else 256, past=past_k)
    x1ext, counts = mla_out(o_lat.reshape(t, -1), wuv, w_out, x, ln_g, ln_b, route_w)
    return x1ext, counts, (ckv, kr)


def _prep_ab_weights(w_in, b_fox_f, b_ml_i, b_ml_f, g_ml, w_out, ln_g, ln_b):
    sizes = (W_FOX, W_FOX, W_FOX, H_FOX, W_ML, W_ML, W_ML, H_ML, H_ML, W_ML)
    idx = np.cumsum(sizes[:-1]).tolist()
    fq, fk, fv, ff, mq, mk, mv, mi, mf, mo = jnp.split(w_in, idx, axis=1)
    w_big = jnp.concatenate([fq, mq, mk, mv, mo, fk, fv], 1).astype(BF16)
    w_gate = jnp.concatenate([ff, mi, mf], 1)
    b_gate = jnp.concatenate([b_fox_f, b_ml_i, b_ml_f])
    return (w_big, w_gate, b_gate, g_ml[None, :], w_out.astype(BF16), ln_g[None, :], ln_b[None, :])


def _prep_mla_weights(w_down, g_q, w_uq, g_kv, w_uk, w_uv, w_out, ln_g, ln_b):
    half = ROPE // 2
    cq_w, ckv_w, kr_w = jnp.split(w_down, [Q_RANK, Q_RANK + KV_RANK], axis=1)
    kr_sw = jnp.concatenate([kr_w[:, half:], kr_w[:, :half]], 1)
    w_down_p = jnp.concatenate([cq_w, ckv_w, kr_w, kr_w, kr_sw, kr_sw], 1).astype(BF16)
    uq = w_uq.reshape(Q_RANK, H_MLA, NOPE + ROPE)
    uq_nope = uq[:, :, :NOPE].reshape(Q_RANK, H_MLA * NOPE)
    uq_rope = uq[:, :, NOPE:]
    uq_rope_sw = jnp.concatenate([uq_rope[..., half:], uq_rope[..., :half]], -1)
    w_uq_p = jnp.concatenate([uq_nope, uq_rope.reshape(Q_RANK, -1), uq_rope_sw.reshape(Q_RANK, -1)], 1).astype(BF16)
    wuk_t = jnp.transpose(w_uk, (1, 2, 0)).astype(BF16)
    wuv = jnp.transpose(w_uv, (1, 0, 2)).astype(BF16)
    return (w_down_p, g_q[None, :], g_kv[None, :], w_uq_p, wuk_t, wuv, w_out.astype(BF16), ln_g[None, :], ln_b[None, :])


def kernel(x_prompt, x_sample, cache_fox_k, cache_fox_v, cache_fox_logf, state_mlstm_c, state_mlstm_n, state_mlstm_m, cache_mla_ckv, cache_mla_krope, w_ab_in, b_fox_f, b_mlstm_i, b_mlstm_f, g_mlstm_norm, w_ab_out, w_mla_down, g_mla_q, w_mla_uq, g_mla_kv, w_mla_uk, w_mla_uv, w_mla_out, ln1_g, ln1_b, ln2_g, ln2_b, w_moe_group, b_moe_group, w_moe_router, b_moe_router, w_exp_gate, w_exp_up, w_exp_down):
    bp, sp, d = x_prompt.shape
    bs, ss_, _ = x_sample.shape
    xp = x_prompt.reshape(bp * sp, d)
    xs = x_sample.reshape(bs * ss_, d)
    ab_p, ab_s, c_p, c_s = [], [], [], []
    n_c = DEPTH // 2
    assert n_c >= 2
    buf_p = buf_s = None
    for l in range(DEPTH):
        j = l // 2
        route_w = (jnp.concatenate([w_moe_group[l], w_moe_router[l]], 1),
                   jnp.concatenate([b_moe_group[l], b_moe_router[l]]))
        if l % 2 == 0:
            wts = _prep_ab_weights(w_ab_in[j], b_fox_f[j], b_mlstm_i[j], b_mlstm_f[j], g_mlstm_norm[j],
                                   w_ab_out[j], ln1_g[l], ln1_b[l])
            xp1, cnt_p, st_p = ab_layer(xp, bp, sp, wts, route_w, None)
            xs1, cnt_s, st_s = ab_layer(xs, bs, ss_, wts, route_w,
                                        (cache_fox_k[j], cache_fox_v[j], cache_fox_logf[j],
                                         state_mlstm_c[j], state_mlstm_n[j], state_mlstm_m[j]))
            ab_p.append(st_p)
            ab_s.append(st_s)
        else:
            wts = _prep_mla_weights(w_mla_down[j], g_mla_q[j], w_mla_uq[j], g_mla_kv[j], w_mla_uk[j], w_mla_uv[j],
                                    w_mla_out[j], ln1_g[l], ln1_b[l])
            last = j == n_c - 1
            prev = lambda sts, i: tuple(st[i] for st in sts) if last else ()
            xp1, cnt_p, st_p = mla_layer(xp, bp, sp, wts, route_w, None, prev(c_p, 0), prev(c_p, 1))
            xs1, cnt_s, st_s = mla_layer(xs, bs, ss_, wts, route_w, (cache_mla_ckv[j], cache_mla_krope[j]),
                                         prev(c_s, 0), prev(c_s, 1))
            c_p.append(st_p)
            c_s.append(st_s)
        moe_w = (w_exp_gate[l].astype(BF16), w_exp_up[l].astype(BF16), w_exp_down[l].astype(BF16))
        xp, buf_p = hier_moe_ln(xp1, cnt_p, moe_w, ln2_g[l][None, :], ln2_b[l][None, :], buf_p)
        xs, buf_s = hier_moe_ln(xs1, cnt_s, moe_w, ln2_g[l][None, :], ln2_b[l][None, :], buf_s)

    def stack(groups, i):
        return jnp.stack([g[i] for g in groups])

    def states(ab, c, b, s):
        kv = [jnp.stack([st[i].reshape(b, s, H_FOX, DH_FOX) for st in ab]) for i in (0, 1)]
        lat = [c[-1][0].reshape(n_c, b, s, KV_RANK), c[-1][1].reshape(n_c, b, s, ROPE)]
        return tuple(kv) + tuple(stack(ab, i) for i in (2, 3, 4, 5)) + tuple(lat)

    return (xp.reshape(bp, sp, d), xs.reshape(bs, ss_, d)) + states(ab_p, c_p, bp, sp) + states(ab_s, c_s, bs, ss_)
```

```python
import functools

import numpy as np
import jax
import jax.numpy as jnp
from jax import lax
from jax.experimental import pallas as pl
from jax.experimental.pallas import tpu as pltpu

F32 = jnp.float32
BF16 = jnp.bfloat16

LANE = 128
VMEM_LIMIT_BYTES = 48 * 1024 * 1024

H_FOX, DH_FOX = 8, 64
W_FOX = H_FOX * DH_FOX
H_ML, DH_ML = 4, 128
W_ML = H_ML * DH_ML
H_MLA, NOPE, ROPE, DV_MLA = 8, 128, 64, 128
Q_RANK, KV_RANK = 384, 256
ROPE_BASE = 10000.0
N_GROUPS, EXP_PER_GROUP, TOP_K = 4, 8, 2
N_EXPERTS = N_GROUPS * EXP_PER_GROUP
DEPTH = 4
ALPHA = (2 * DEPTH) ** 0.25
LN_EPS = 1e-5
RMS_EPS = 1e-6
CHUNK = 64
NEG_BIG = -1e30
LOG2E = 1.4426950408889634


def _params(*sem):
    return pltpu.CompilerParams(dimension_semantics=sem, vmem_limit_bytes=VMEM_LIMIT_BYTES)


def _dot(a, b):
    return jnp.dot(a, b, preferred_element_type=F32)


def _dot_nt(a, b):
    return lax.dot_general(a, b, (((1,), (1,)), ((), ())), preferred_element_type=F32)


def _split3(x):
    hi = x.astype(BF16)
    r = x - hi.astype(F32)
    mid = r.astype(BF16)
    lo = (r - mid.astype(F32)).astype(BF16)
    return hi, mid, lo


def _layer_norm(y, g, b):
    mu = jnp.mean(y, axis=1, keepdims=True)
    yc = y - mu
    var = jnp.mean(yc * yc, axis=1, keepdims=True)
    return yc * lax.rsqrt(var + LN_EPS) * g + b


def _sigmoid(x):
    return 1.0 / (1.0 + jnp.exp(-x))


def _row_tile(t):
    return min(512, t)


def _ab_proj_kernel(x_ref, w_ref, gw_ref, gb_ref, gmask_ref,
                    fq_ref, mq_ref, mk_ref, mv_ref, mo_ref, kbf_ref, vbf_ref, gates_ref, k_ref, v_ref):
    x = x_ref[...]
    xh = x.astype(BF16)
    c0 = 0
    for o_ref in (fq_ref, mq_ref, mk_ref, mv_ref, mo_ref):
        n = o_ref.shape[1]
        o_ref[...] = _dot(xh, w_ref[:, c0:c0 + n]).astype(BF16)
        c0 += n
    for bf_ref, state_ref in ((kbf_ref, k_ref), (vbf_ref, v_ref)):
        z = _dot(xh, w_ref[:, c0:c0 + W_FOX])
        c0 += W_FOX
        bf_ref[...] = z.astype(BF16)
        state_ref[...] = z
    xl = (x - xh.astype(F32)).astype(BF16)
    gg = _dot(xh, gw_ref[...])
    g = gg[:, :LANE] + gg[:, LANE:] + _dot(xl, gw_ref[:, :LANE]) + gb_ref[...]
    log_sig = jnp.minimum(g, 0.0) - jnp.log1p(jnp.exp(-jnp.abs(g)))
    gates_ref[...] = jnp.where(gmask_ref[...] > 0.0, log_sig, g)


def ab_proj(x, w_big, w_gate, b_gate, log_sigmoid_cols):
    t, k = x.shape
    tm = _row_tile(t)
    n_gate = w_gate.shape[1]
    wp = jnp.zeros((k, LANE), F32).at[:, :n_gate].set(w_gate)
    wh = wp.astype(BF16)
    wl = (wp - wh.astype(F32)).astype(BF16)
    bp = jnp.zeros((1, LANE), F32).at[0, :n_gate].set(b_gate)
    mask = jnp.zeros((1, LANE), F32).at[0, :n_gate].set(jnp.asarray(log_sigmoid_cols, F32))
    bf = lambda n: jax.ShapeDtypeStruct((t, n), BF16)
    return pl.pallas_call(
        _ab_proj_kernel,
        grid=(t // tm,),
        in_specs=[_rows(tm, k), _whole(w_big.shape), _whole((k, 2 * LANE)), _whole((1, LANE)), _whole((1, LANE))],
        out_specs=[_rows(tm, W_FOX)] + [_rows(tm, W_ML)] * 4 + [_rows(tm, W_FOX)] * 2 + [_rows(tm, LANE)]
        + [_rows(tm, W_FOX)] * 2,
        out_shape=[bf(W_FOX)] + [bf(W_ML)] * 4 + [bf(W_FOX)] * 2 + [jax.ShapeDtypeStruct((t, LANE), F32)]
        + [jax.ShapeDtypeStruct((t, W_FOX), F32)] * 2,
        compiler_params=_params("parallel"),
        name="ab_proj",
    )(x, w_big, jnp.concatenate([wh, wl], 1), bp, mask)


def _lanes(x, w):
    if w < LANE:
        return x[:, :w]
    return x if w == LANE else jnp.concatenate([x] * (w // LANE), axis=1)


def _softmax_step(s, at, m_s, l_s, acc_s, v):
    w = s.shape[1]
    m_prev = m_s[at]
    m_new = jnp.maximum(m_prev, jnp.max(s, axis=1, keepdims=True))
    p = jnp.exp2(s - _lanes(m_new, w))
    a = jnp.exp2(m_prev - m_new)
    l_s[at] = a * l_s[at] + jnp.sum(p, axis=1, keepdims=True)
    acc_s[at] = acc_s[at] * _lanes(a, acc_s.shape[-1]) + _dot(p.astype(BF16), v)
    m_s[at] = m_new


def _fox_kernel(q_ref, k_ref, v_ref, *rest, tq, tk, q_off, cblk):
    pk_ref, pv_ref = rest[:2] if len(rest) == 9 else (k_ref, v_ref)
    lf_ref, o_ref, ncum_ref, qm_s, m_s, l_s, acc_s = rest[-7:]
    qi = pl.program_id(1)
    sub = cblk // LANE
    n_pairs = H_FOX // 2

    @pl.when(qi == 0)
    def _():
        r = lax.broadcasted_iota(jnp.int32, (cblk, cblk), 0)
        c = lax.broadcasted_iota(jnp.int32, (cblk, cblk), 1)
        upper = jnp.where(r <= c, 1.0, 0.0).astype(BF16)
        carry = jnp.zeros((lf_ref.shape[1], 1), F32)
        for j in range(lf_ref.shape[2] // cblk):
            g1, g2, g3 = _split3(lf_ref[0, :, j * cblk:(j + 1) * cblk])
            cum = _dot(g1, upper) + _dot(g2, upper) + _dot(g3, upper) + carry
            carry = cum[:, cblk - 1:cblk]
            for h in range(H_FOX):
                for u in range(sub):
                    ncum_ref[h, j * sub + u] = cum[h:h + 1, u * LANE:(u + 1) * LANE] * (-LOG2E)

    lane = lax.broadcasted_iota(jnp.int32, (tq, LANE), 1)
    for hp in range(n_pairs):
        q = q_ref[0, :, hp * LANE:(hp + 1) * LANE]
        zero = jnp.zeros_like(q)
        qm_s[hp, 0:tq, :] = jnp.where(lane < DH_FOX, q, zero)
        qm_s[hp, tq:2 * tq, :] = jnp.where(lane >= DH_FOX, q, zero)
    m_s[...] = jnp.full(m_s.shape, NEG_BIG, F32)
    l_s[...] = jnp.zeros(l_s.shape, F32)
    acc_s[...] = jnp.zeros(acc_s.shape, F32)
    q_start = q_off + qi * tq

    def update(k_src, v_src, off, w, bias_of_head, mask):
        for hp in range(n_pairs):
            kt = k_src[0, pl.ds(off, w), hp * LANE:(hp + 1) * LANE]
            vt = v_src[0, pl.ds(off, w), hp * LANE:(hp + 1) * LANE]
            s = _dot_nt(qm_s[hp], kt)
            s = jnp.concatenate([s[:tq] + bias_of_head(2 * hp), s[tq:] + bias_of_head(2 * hp + 1)], axis=0)
            if mask is not None:
                s = jnp.where(mask, s, -jnp.inf)
            _softmax_step(s, hp, m_s, l_s, acc_s, vt)

    def full_tile(j, carry):
        def bias(h):
            return jnp.concatenate([ncum_ref[h, j * (tk // LANE) + u] for u in range(tk // LANE)], axis=1)
        update(pk_ref, pv_ref, pl.multiple_of(j * tk, tk), tk, bias, None)
        return carry

    lax.fori_loop(0, q_start // tk, full_tile, 0)

    def diag_bias(h):
        if tq >= LANE:
            return jnp.concatenate([ncum_ref[h, q_start // LANE + u] for u in range(tq // LANE)], axis=1)
        lo = q_off % LANE
        return ncum_ref[h, q_start // LANE][:, lo:lo + tq]

    r = lax.broadcasted_iota(jnp.int32, (2 * tq, tq), 0)
    c = lax.broadcasted_iota(jnp.int32, (2 * tq, tq), 1)
    update(k_ref, v_ref, pl.multiple_of(qi * tq, tq), tq, diag_bias, c <= jnp.where(r >= tq, r - tq, r))
    for hp in range(n_pairs):
        o = acc_s[hp] / l_s[hp]
        o_ref[0, :, hp * LANE:(hp + 1) * LANE] = jnp.where(lane < DH_FOX, o[:tq], o[tq:]).astype(o_ref.dtype)


def fox_attention(q, k, v, logf_rows, *, tq, past=None, tk=256, cblk=256):
    b, lq, _ = q.shape
    lkp = logf_rows.shape[2]
    q_off = 0 if past is None else past[0].shape[1]
    nq = lq // tq
    assert k.shape[1] == lq and q_off % tk == 0 and (tq % tk == 0 or lq == tq) and lkp % cblk == 0
    assert tq >= LANE or (nq == 1 and q_off % LANE + tq <= LANE)
    kern = functools.partial(_fox_kernel, tq=tq, tk=tk, q_off=q_off, cblk=cblk)
    whole_seq = [pl.BlockSpec((1, a.shape[1], W_FOX), lambda bi, qi: (bi, 0, 0)) for a in (k, v) + tuple(past or ())]
    return pl.pallas_call(
        kern,
        grid=(b, nq),
        in_specs=[pl.BlockSpec((1, tq, W_FOX), lambda bi, qi: (bi, qi, 0))] + whole_seq + [
            pl.BlockSpec((1, logf_rows.shape[1], lkp), lambda bi, qi: (bi, 0, 0)),
        ],
        out_specs=pl.BlockSpec((1, tq, W_FOX), lambda bi, qi: (bi, qi, 0)),
        out_shape=jax.ShapeDtypeStruct((b, lq, W_FOX), BF16),
        scratch_shapes=[pltpu.VMEM((H_FOX, lkp // LANE, 1, LANE), F32), pltpu.VMEM((H_FOX // 2, 2 * tq, LANE), BF16),
                        pltpu.VMEM((H_FOX // 2, 2 * tq, LANE), F32), pltpu.VMEM((H_FOX // 2, 2 * tq, LANE), F32),
                        pltpu.VMEM((H_FOX // 2, 2 * tq, LANE), F32)],
        compiler_params=_params("parallel", "arbitrary"),
        name="fox_attention",
    )(q, k, v, *(past or ()), logf_rows)


def _mlstm_kernel(q_ref, k_ref, v_ref, g_ref, c0_ref, n0_ref, m0_ref,
                  h_ref, c_ref, n_ref, m_ref, c_s, n_s, m_s, *, chunk):
    ci = pl.program_id(1)
    L = chunk
    scale = DH_ML ** -0.5

    @pl.when(ci == 0)
    def _():
        c_s[...] = c0_ref[0]
        n_s[...] = n0_ref[0]
        m_s[...] = m0_ref[0]

    r = lax.broadcasted_iota(jnp.int32, (L, L), 0)
    c = lax.broadcasted_iota(jnp.int32, (L, L), 1)
    eye = r == c
    causal = c <= r
    upper = jnp.where(r <= c, 1.0, 0.0).astype(BF16)
    g = g_ref[0]
    g1, g2, g3 = _split3(g)
    cum = _dot(g1, upper) + _dot(g2, upper) + _dot(g3, upper)

    def to_col(row):
        return jnp.sum(jnp.where(eye, row, 0.0), axis=1, keepdims=True)

    for h in range(H_ML):
        sl = slice(h * DH_ML, (h + 1) * DH_ML)
        qh, kh, vh = q_ref[0, :, sl], k_ref[0, :, sl], v_ref[0, :, sl]
        ig_row = g[h:h + 1, :]
        bh_row = cum[H_ML + h:H_ML + h + 1, :]
        bh_col, ig_col = to_col(bh_row), to_col(ig_row)
        m0 = m_s[h]
        c0 = c_s[h]
        n0 = n_s[h]
        logd = jnp.where(causal, bh_col - bh_row + ig_row, -jnp.inf)
        inter = bh_col + m0
        m_col = jnp.maximum(inter, jnp.max(logd, axis=1, keepdims=True))
        d = jnp.exp(logd - m_col)
        a_col = jnp.exp(inter - m_col)
        s = _dot_nt(qh, kh) * scale * d
        num = _dot(s.astype(BF16), vh) + _dot_nt(qh, c0.astype(BF16)) * a_col
        qf = qh.astype(F32)
        den = jnp.sum(s, axis=1, keepdims=True) + a_col * jnp.sum(qf * n0, axis=1, keepdims=True)
        den = jnp.maximum(jnp.abs(den), jnp.exp(-m_col))
        h_ref[0, :, sl] = (num / den).astype(h_ref.dtype)

        m_l = m_col[L - 1:L, :]
        bh_l = bh_row[:, L - 1:L]
        a_l = jnp.exp(bh_l + m0 - m_l)
        w_col = jnp.exp(bh_l - bh_col + ig_col - m_l)
        vw_t = jnp.transpose(vh.astype(F32) * w_col).astype(BF16)
        c_s[h] = a_l * c0 + _dot(vw_t, kh) * scale
        n_s[h] = a_l * n0 + jnp.sum(kh.astype(F32) * w_col, axis=0, keepdims=True) * scale
        m_s[h] = m_l

    @pl.when(ci == pl.num_programs(1) - 1)
    def _():
        c_ref[0] = c_s[...]
        n_ref[0] = n_s[...]
        m_ref[0] = m_s[...]


def mlstm(q, k, v, gate_rows, c0, n0, m0, *, chunk):
    b, s, _ = q.shape
    nc = s // chunk
    seq = pl.BlockSpec((1, chunk, W_ML), lambda bi, ci: (bi, ci, 0))
    c_spec = pl.BlockSpec((1, H_ML, DH_ML, DH_ML), lambda bi, ci: (bi, 0, 0, 0))
    n_spec = pl.BlockSpec((1, H_ML, 1, DH_ML), lambda bi, ci: (bi, 0, 0, 0))
    m_spec = pl.BlockSpec((1, H_ML, 1, 1), lambda bi, ci: (bi, 0, 0, 0))
    return pl.pallas_call(
        functools.partial(_mlstm_kernel, chunk=chunk),
        grid=(b, nc),
        in_specs=[seq, seq, seq, pl.BlockSpec((1, gate_rows.shape[1], chunk), lambda bi, ci: (bi, 0, ci)),
                  c_spec, n_spec, m_spec],
        out_specs=[seq, c_spec, n_spec, m_spec],
        out_shape=[jax.ShapeDtypeStruct((b, s, W_ML), BF16),
                   jax.ShapeDtypeStruct(c0.shape, F32), jax.ShapeDtypeStruct(n0.shape, F32),
                   jax.ShapeDtypeStruct(m0.shape, F32)],
        scratch_shapes=[pltpu.VMEM((H_ML, DH_ML, DH_ML), F32), pltpu.VMEM((H_ML, 1, DH_ML), F32),
                        pltpu.VMEM((H_ML, 1, 1), F32)],
        compiler_params=_params("parallel", "arbitrary"),
        name="mlstm",
    )(q, k, v, gate_rows, c0, n0, m0)


N_PAIRS = EXP_PER_GROUP * (EXP_PER_GROUP - 1) // 2
N_BUCKETS = N_GROUPS * N_PAIRS
assert N_BUCKETS <= LANE
EXT = LANE
FEW_TOKENS = 1024


def _route_store(out, wr_ref, br_ref, ltri_ref, o_ref, cnt_ref, cnt_s):
    tm, d = out.shape

    @pl.when(pl.program_id(0) == 0)
    def _():
        cnt_s[...] = jnp.zeros(cnt_s.shape, F32)

    xh = out.astype(BF16)
    xl = (out - xh.astype(F32)).astype(BF16)
    zz = _dot(xh, wr_ref[...])
    z = zz[:, :LANE] + zz[:, LANE:] + _dot(xl, wr_ref[:, :LANE]) + br_ref[...]
    lane = lax.broadcasted_iota(jnp.int32, (tm, LANE), 1).astype(F32)
    far = float(LANE)
    neg = -jnp.inf
    gl = jnp.where(lane < N_GROUPS, z, neg)
    gmax = jnp.max(gl, axis=1, keepdims=True)
    g_idx = jnp.min(jnp.where(gl == gmax, lane, far), axis=1, keepdims=True)
    g_gate = 1.0 / jnp.sum(jnp.exp(gl - gmax), axis=1, keepdims=True)
    lo = N_GROUPS + EXP_PER_GROUP * g_idx
    el = jnp.where(jnp.logical_and(lane >= lo, lane < lo + EXP_PER_GROUP), z, neg)
    v1 = jnp.max(el, axis=1, keepdims=True)
    i1 = jnp.min(jnp.where(el == v1, lane, far), axis=1, keepdims=True)
    el2 = jnp.where(lane == i1, neg, el)
    v2 = jnp.max(el2, axis=1, keepdims=True)
    i2 = jnp.min(jnp.where(el2 == v2, lane, far), axis=1, keepdims=True)
    e21 = jnp.exp(v2 - v1)
    w1 = g_gate / (1.0 + e21)
    w2 = w1 * e21
    j1, j2 = i1 - lo, i2 - lo
    ja, jb = jnp.minimum(j1, j2), jnp.maximum(j1, j2)
    pair = ja * (2 * EXP_PER_GROUP - 1 - ja) * 0.5 + (jb - ja - 1.0)
    bucket = g_idx * N_PAIRS + pair
    first_low = j1 < j2
    wa = jnp.where(first_low, w1, w2)
    wb = jnp.where(first_low, w2, w1)
    onehot = jnp.where(lane == bucket, 1.0, 0.0)
    earlier = _dot(ltri_ref[...], onehot.astype(BF16))
    cnt = cnt_s[...]
    rank = jnp.sum(onehot * (earlier + cnt), axis=1, keepdims=True)
    cnt_new = cnt + jnp.sum(onehot, axis=0, keepdims=True)
    cnt_s[...] = cnt_new
    cnt_ref[...] = cnt_new
    o_ref[:, :d] = out
    e_lo = lo - N_GROUPS + ja
    e_hi = lo - N_GROUPS + jb
    record = (bucket, rank, wa, wb, e_lo, e_hi)
    ext = jnp.zeros((tm, LANE), F32)
    for k, col in enumerate(record):
        ext = jnp.where(lane == float(k), col, ext)
    o_ref[:, d:] = ext


def _route_operands(w_route, b_route, tm):
    n = w_route.shape[1]
    wp = jnp.zeros((w_route.shape[0], LANE), F32).at[:, :n].set(w_route)
    wh = wp.astype(BF16)
    wl = (wp - wh.astype(F32)).astype(BF16)
    bp = jnp.zeros((1, LANE), F32).at[0, :n].set(b_route)
    ltri = jnp.asarray(np.tril(np.ones((tm, tm), np.float32), -1), BF16)
    return jnp.concatenate([wh, wl], 1), bp, ltri


def _route_specs(k, tm):
    return [_whole((k, 2 * LANE)), _whole((1, LANE)), _whole((tm, tm))]


def _ab_out_kernel(fo_ref, mh_ref, mo_ref, x_ref, gml_ref, w_ref, g_ref, b_ref, wr_ref, br_ref, ltri_ref,
                   o_ref, cnt_ref, cnt_s):
    parts = [fo_ref[...]]
    for h in range(H_ML):
        sl = slice(h * DH_ML, (h + 1) * DH_ML)
        mh = mh_ref[:, sl].astype(F32)
        ms = jnp.mean(mh * mh, axis=1, keepdims=True)
        nm = mh * lax.rsqrt(ms + RMS_EPS) * gml_ref[:, sl]
        parts.append((nm * _sigmoid(mo_ref[:, sl].astype(F32))).astype(BF16))
    cat = jnp.concatenate(parts, axis=1)
    y = ALPHA * x_ref[...] + _dot(cat, w_ref[...])
    out = _layer_norm(y, g_ref[...], b_ref[...])
    _route_store(out, wr_ref, br_ref, ltri_ref, o_ref, cnt_ref, cnt_s)


def _rows(tm, n):
    return pl.BlockSpec((tm, n), lambda i: (i, 0))


def _whole(shape):
    return pl.BlockSpec(shape, lambda i: (0,) * len(shape))


def _routed_out_call(kern, name, t, d, tm, in_specs, operands, route_w):
    w_route, b_route = route_w
    return pl.pallas_call(
        kern,
        grid=(t // tm,),
        in_specs=in_specs + _route_specs(d, tm),
        out_specs=[_rows(tm, d + EXT), _whole((1, LANE))],
        out_shape=[jax.ShapeDtypeStruct((t, d + EXT), F32), jax.ShapeDtypeStruct((1, LANE), F32)],
        scratch_shapes=[pltpu.VMEM((1, LANE), F32)],
        compiler_params=_params("arbitrary"),
        name=name,
    )(*operands, *_route_operands(w_route, b_route, tm))


def ab_out(fo, mh, mo, x, g_ml, w_out, ln_g, ln_b, route_w):
    t, d = x.shape
    tm = _row_tile(t)
    in_specs = [_rows(tm, W_FOX), _rows(tm, W_ML), _rows(tm, W_ML), _rows(tm, d), _whole((1, W_ML)),
                _whole(w_out.shape), _whole((1, d)), _whole((1, d))]
    return _routed_out_call(_ab_out_kernel, "ab_out", t, d, tm, in_specs,
                            (fo, mh, mo, x, g_ml, w_out, ln_g, ln_b), route_w)


def _rms(z, g):
    return z * lax.rsqrt(jnp.mean(z * z, axis=1, keepdims=True) + RMS_EPS) * g


def _mla_down_kernel(x_ref, w_ref, gq_ref, gkv_ref, cc_ref, ss_ref, *rest, n_prev):
    prev_c, prev_r = rest[:n_prev], rest[n_prev:2 * n_prev]
    cq_ref, ckv_ref, kr_ref, ckvb_ref, kr2_ref = rest[2 * n_prev:]

    def put(state_ref, prev, value):
        if n_prev:
            for j, p in enumerate(prev):
                state_ref[j] = p[...]
            state_ref[n_prev] = value
        else:
            state_ref[...] = value

    xb = x_ref[...].astype(BF16)
    cq_ref[...] = _rms(_dot(xb, w_ref[:, 0:Q_RANK]), gq_ref[...]).astype(BF16)
    ckv = _rms(_dot(xb, w_ref[:, Q_RANK:Q_RANK + KV_RANK]), gkv_ref[...])
    put(ckv_ref, prev_c, ckv)
    ckvb_ref[...] = ckv.astype(BF16)
    c0 = Q_RANK + KV_RANK
    rope = (_dot(xb, w_ref[:, c0:c0 + LANE]) * cc_ref[...] + _dot(xb, w_ref[:, c0 + LANE:c0 + 2 * LANE]) * ss_ref[...])
    put(kr_ref, prev_r, rope[:, :ROPE])
    kr2_ref[...] = rope.astype(BF16)


def _table_spec(table, tm, s):
    if table.shape[0] == tm:
        return pl.BlockSpec((tm, LANE), lambda i: (0, 0))
    per = s // tm
    return pl.BlockSpec((tm, LANE), lambda i: (i % per, 0))


def mla_down(x, w, g_q, g_kv, cc, ss, s, prev_c=(), prev_r=()):
    t, d = x.shape
    tm = _row_tile(t)
    tab = _table_spec(cc, tm, s)
    n_prev = len(prev_c)

    def state(width):
        if n_prev:
            return (pl.BlockSpec((n_prev + 1, tm, width), lambda i: (0, i, 0)),
                    jax.ShapeDtypeStruct((n_prev + 1, t, width), F32))
        return _rows(tm, width), jax.ShapeDtypeStruct((t, width), F32)

    (c_spec, c_shape), (r_spec, r_shape) = state(KV_RANK), state(ROPE)
    return pl.pallas_call(
        functools.partial(_mla_down_kernel, n_prev=n_prev),
        grid=(t // tm,),
        in_specs=[_rows(tm, d), _whole(w.shape), _whole((1, Q_RANK)), _whole((1, KV_RANK)), tab, tab]
        + [_rows(tm, KV_RANK)] * n_prev + [_rows(tm, ROPE)] * n_prev,
        out_specs=[_rows(tm, Q_RANK), c_spec, r_spec, _rows(tm, KV_RANK), _rows(tm, LANE)],
        out_shape=[jax.ShapeDtypeStruct((t, Q_RANK), BF16), c_shape, r_shape,
                   jax.ShapeDtypeStruct((t, KV_RANK), BF16), jax.ShapeDtypeStruct((t, LANE), BF16)],
        compiler_params=_params("parallel"),
        name="mla_down",
    )(x, w, g_q, g_kv, cc, ss, *prev_c, *prev_r)


def _mla_uq_kernel(cq_ref, w_ref, wuk_ref, cc_ref, ss_ref, qlat_ref, qrope_ref):
    cq = cq_ref[...]
    scale = (NOPE + ROPE) ** -0.5 * LOG2E
    r0 = H_MLA * NOPE
    r1 = r0 + H_MLA * ROPE
    qn = _dot(cq, w_ref[:, :r0]).astype(BF16)
    for h in range(H_MLA):
        qlat_ref[:, h * KV_RANK:(h + 1) * KV_RANK] = (
            _dot(qn[:, h * NOPE:(h + 1) * NOPE], wuk_ref[h]) * scale).astype(BF16)
    qr = _dot(cq, w_ref[:, r0:])
    cc, ss = cc_ref[...] * scale, ss_ref[...] * scale
    for p in range(H_MLA // 2):
        a, b = p * LANE, (r1 - r0) + p * LANE
        qrope_ref[:, p * LANE:(p + 1) * LANE] = (qr[:, a:a + LANE] * cc + qr[:, b:b + LANE] * ss).astype(BF16)


def mla_uq(cq, w, wuk_t, cc, ss, s):
    t = cq.shape[0]
    tm = _row_tile(t)
    tab = _table_spec(cc, tm, s)
    return pl.pallas_call(
        _mla_uq_kernel,
        grid=(t // tm,),
        in_specs=[_rows(tm, Q_RANK), _whole(w.shape), _whole(wuk_t.shape), tab, tab],
        out_specs=[_rows(tm, H_MLA * KV_RANK), _rows(tm, H_MLA * ROPE)],
        out_shape=[jax.ShapeDtypeStruct((t, H_MLA * KV_RANK), BF16), jax.ShapeDtypeStruct((t, H_MLA * ROPE), BF16)],
        compiler_params=_params("parallel"),
        name="mla_uq",
    )(cq, w, wuk_t, cc, ss)


def _mla_attn_kernel(ql_ref, qr_ref, kc_ref, kr_ref, *rest, tq, tk, q_off):
    pc_ref, pr_ref = rest[:2] if len(rest) == 8 else (kc_ref, kr_ref)
    o_ref, ql_s, qr_s, m_s, l_s, acc_s = rest[-6:]
    qi = pl.program_id(1)
    lane = lax.broadcasted_iota(jnp.int32, (tq, LANE), 1)
    for h in range(H_MLA):
        rows = slice(h * tq, (h + 1) * tq)
        ql_s[rows, :] = ql_ref[0, :, h * KV_RANK:(h + 1) * KV_RANK]
        qr = qr_ref[0, :, (h // 2) * LANE:(h // 2 + 1) * LANE]
        keep = (lane < ROPE) if h % 2 == 0 else (lane >= ROPE)
        qr_s[rows, :] = jnp.where(keep, qr, jnp.zeros_like(qr))
    m_s[...] = jnp.full(m_s.shape, NEG_BIG, F32)
    l_s[...] = jnp.zeros(l_s.shape, F32)
    acc_s[...] = jnp.zeros(acc_s.shape, F32)
    q_start = q_off + qi * tq

    def update(c_src, r_src, off, w, mask):
        kc = c_src[0, pl.ds(off, w), :]
        kr = r_src[0, pl.ds(off, w), :]
        s = _dot_nt(ql_s[...], kc) + _dot_nt(qr_s[...], kr)
        if mask is not None:
            s = jnp.where(mask, s, -jnp.inf)
        _softmax_step(s, Ellipsis, m_s, l_s, acc_s, kc)

    def full_tile(j, carry):
        update(pc_ref, pr_ref, pl.multiple_of(j * tk, tk), tk, None)
        return carry

    lax.fori_loop(0, q_start // tk, full_tile, 0)
    r = lax.broadcasted_iota(jnp.int32, (H_MLA * tq, tq), 0)
    c = lax.broadcasted_iota(jnp.int32, (H_MLA * tq, tq), 1)
    update(kc_ref, kr_ref, pl.multiple_of(qi * tq, tq), tq, (c // CHUNK) <= ((r & (tq - 1)) // CHUNK))
    for h in range(H_MLA):
        rows = slice(h * tq, (h + 1) * tq)
        o = acc_s[rows, :] / _lanes(l_s[rows, :], KV_RANK)
        o_ref[0, :, h * KV_RANK:(h + 1) * KV_RANK] = o.astype(o_ref.dtype)


def _mla_attn_pipelined_kernel(ql_ref, qr_ref, kc_ref, kr_ref, o_ref, ql_s, qr_s, m_s, l_s, acc_s, s_buf, *, tq):
    qi = pl.program_id(1)
    lane = lax.broadcasted_iota(jnp.int32, (tq, LANE), 1)
    for h in range(H_MLA):
        rows = slice(h * tq, (h + 1) * tq)
        ql_s[rows, :] = ql_ref[0, :, h * KV_RANK:(h + 1) * KV_RANK]
        qr = qr_ref[0, :, (h // 2) * LANE:(h // 2 + 1) * LANE]
        keep = (lane < ROPE) if h % 2 == 0 else (lane >= ROPE)
        qr_s[rows, :] = jnp.where(keep, qr, jnp.zeros_like(qr))
    m_s[...] = jnp.full(m_s.shape, NEG_BIG, F32)
    l_s[...] = jnp.zeros(l_s.shape, F32)
    acc_s[...] = jnp.zeros(acc_s.shape, F32)

    def keys(ref, j):
        return ref[0, pl.ds(pl.multiple_of(j * tq, tq), tq), :]

    def score(j, slot):
        s_buf[slot] = _dot_nt(ql_s[...], keys(kc_ref, j)) + _dot_nt(qr_s[...], keys(kr_ref, j))

    def consume(j, slot, mask):
        s = s_buf[slot]
        if mask is not None:
            s = jnp.where(mask, s, -jnp.inf)
        _softmax_step(s, Ellipsis, m_s, l_s, acc_s, keys(kc_ref, j))

    score(0, 0)

    def two_tiles(ip, carry):
        j = 2 * ip
        score(j + 1, 1)
        consume(j, 0, None)
        score(j + 2, 0)
        consume(j + 1, 1, None)
        return carry

    lax.fori_loop(0, qi // 2, two_tiles, 0)
    r = lax.broadcasted_iota(jnp.int32, (H_MLA * tq, tq), 0)
    c = lax.broadcasted_iota(jnp.int32, (H_MLA * tq, tq), 1)
    mask = (c // CHUNK) <= ((r & (tq - 1)) // CHUNK)

    @pl.when(qi % 2 == 1)
    def _():
        score(qi, 1)
        consume(qi - 1, 0, None)
        consume(qi, 1, mask)

    @pl.when(qi % 2 == 0)
    def _():
        consume(qi, 0, mask)

    for h in range(H_MLA):
        rows = slice(h * tq, (h + 1) * tq)
        o = acc_s[rows, :] / _lanes(l_s[rows, :], KV_RANK)
        o_ref[0, :, h * KV_RANK:(h + 1) * KV_RANK] = o.astype(o_ref.dtype)


def mla_attention(q_lat, q_rope, kc, kr2, *, tq, tk, past=None):
    b, lq, _ = q_lat.shape
    q_off = 0 if past is None else past[0].shape[1]
    assert kc.shape[1] == lq and q_off % tk == 0 and (tq % tk == 0 or lq == tq)
    assert tq % CHUNK == 0 and (tq & (tq - 1)) == 0
    scratch = [pltpu.VMEM((H_MLA * tq, KV_RANK), BF16), pltpu.VMEM((H_MLA * tq, LANE), BF16),
               pltpu.VMEM((H_MLA * tq, LANE), F32), pltpu.VMEM((H_MLA * tq, LANE), F32),
               pltpu.VMEM((H_MLA * tq, KV_RANK), F32)]
    if past is None and tk == tq:
        kern = functools.partial(_mla_attn_pipelined_kernel, tq=tq)
        scratch.append(pltpu.VMEM((2, H_MLA * tq, tq), F32))
    else:
        kern = functools.partial(_mla_attn_kernel, tq=tq, tk=tk, q_off=q_off)
    whole_seq = [pl.BlockSpec((1,) + a.shape[1:], lambda bi, qi: (bi, 0, 0)) for a in (kc, kr2) + tuple(past or ())]
    return pl.pallas_call(
        kern,
        grid=(b, lq // tq),
        in_specs=[
            pl.BlockSpec((1, tq, H_MLA * KV_RANK), lambda bi, qi: (bi, qi, 0)),
            pl.BlockSpec((1, tq, H_MLA * ROPE), lambda bi, qi: (bi, qi, 0)),
        ] + whole_seq,
        out_specs=pl.BlockSpec((1, tq, H_MLA * KV_RANK), lambda bi, qi: (bi, qi, 0)),
        out_shape=jax.ShapeDtypeStruct((b, lq, H_MLA * KV_RANK), BF16),
        scratch_shapes=scratch,
        compiler_params=_params("parallel", "arbitrary"),
        name="mla_attention",
    )(q_lat, q_rope, kc, kr2, *(past or ()))


def _mla_out_kernel(ol_ref, wuv_ref, w_ref, x_ref, g_ref, b_ref, wr_ref, br_ref, ltri_ref,
                    o_ref, cnt_ref, cnt_s):
    parts = [_dot(ol_ref[:, h * KV_RANK:(h + 1) * KV_RANK], wuv_ref[h]).astype(BF16) for h in range(H_MLA)]
    y = ALPHA * x_ref[...] + _dot(jnp.concatenate(parts, axis=1), w_ref[...])
    out = _layer_norm(y, g_ref[...], b_ref[...])
    _route_store(out, wr_ref, br_ref, ltri_ref, o_ref, cnt_ref, cnt_s)


def mla_out(o_lat, wuv, w_out, x, ln_g, ln_b, route_w):
    t, d = x.shape
    tm = _row_tile(t)
    in_specs = [_rows(tm, H_MLA * KV_RANK), _whole(wuv.shape), _whole(w_out.shape), _rows(tm, d),
                _whole((1, d)), _whole((1, d))]
    return _routed_out_call(_mla_out_kernel, "mla_out", t, d, tm, in_specs,
                            (o_lat, wuv, w_out, x, ln_g, ln_b), route_w)


ROW_SLOTS = 3


def _row_scatter_kernel(idx_ref, src_ref, dst_init_ref, dst_ref, buf, in_sem, out_sem, *, rows, n_steps):
    del dst_init_ref

    def load(i, slot):
        return pltpu.make_async_copy(src_ref.at[pl.ds(pl.multiple_of(i * rows, rows), rows)], buf.at[slot],
                                     in_sem.at[slot])

    def rows_done(slot):
        return pltpu.make_async_copy(buf.at[slot], dst_ref.at[pl.ds(0, rows)], out_sem.at[slot])

    i = pl.program_id(0)
    slot = i % ROW_SLOTS
    nxt = (i + 1) % ROW_SLOTS

    @pl.when(i == 0)
    def _():
        load(0, 0).start()

    @pl.when(i + 1 < n_steps)
    def _():
        @pl.when(i + 1 >= ROW_SLOTS)
        def _():
            rows_done(nxt).wait()
        load(i + 1, nxt).start()

    load(i, slot).wait()
    base = i * rows

    for r in range(rows):
        pltpu.make_async_copy(buf.at[slot, pl.ds(r, 1)], dst_ref.at[pl.ds(idx_ref[base + r], 1)],
                              out_sem.at[slot]).start()

    @pl.when(i == n_steps - 1)
    def _():
        for k in range(min(ROW_SLOTS, n_steps)):
            rows_done((n_steps - 1 - k) % ROW_SLOTS).wait()


def _row_gather_kernel(idx_ref, src_ref, dst_ref, buf, row_sem, out_sem, *, rows, n_steps):
    def rows_done(slot):
        return pltpu.make_async_copy(src_ref.at[pl.ds(0, rows)], buf.at[slot], row_sem.at[slot])

    def store(i, slot):
        return pltpu.make_async_copy(buf.at[slot], dst_ref.at[pl.ds(pl.multiple_of(i * rows, rows), rows)],
                                     out_sem.at[slot])

    i = pl.program_id(0)
    slot = i % ROW_SLOTS

    @pl.when(i >= ROW_SLOTS)
    def _():
        store(i - ROW_SLOTS, slot).wait()

    base = i * rows

    for r in range(rows):
        pltpu.make_async_copy(src_ref.at[pl.ds(idx_ref[base + r], 1)], buf.at[slot, pl.ds(r, 1)],
                              row_sem.at[slot]).start()

    @pl.when(i >= 1)
    def _():
        prev = (i + ROW_SLOTS - 1) % ROW_SLOTS
        rows_done(prev).wait()
        store(i - 1, prev).start()

    @pl.when(i == n_steps - 1)
    def _():
        last = (n_steps - 1) % ROW_SLOTS
        rows_done(last).wait()
        store(n_steps - 1, last).start()
        for k in range(min(ROW_SLOTS, n_steps)):
            store(n_steps - 1 - k, (n_steps - 1 - k) % ROW_SLOTS).wait()


def row_move(idx, src, dst_rows, *, scatter, dst_init=None):
    t = idx.shape[0]
    width = src.shape[1]
    rows = min(512, t)
    any_spec = pl.BlockSpec(memory_space=pl.ANY)
    operands = [idx, src] + ([dst_init] if scatter else [])
    grid_spec = pltpu.PrefetchScalarGridSpec(
        num_scalar_prefetch=1, grid=(t // rows,), in_specs=[any_spec] * (len(operands) - 1), out_specs=any_spec,
        scratch_shapes=[pltpu.VMEM((ROW_SLOTS, rows, width), src.dtype),
                        pltpu.SemaphoreType.DMA((ROW_SLOTS,)), pltpu.SemaphoreType.DMA((ROW_SLOTS,))])
    kern = _row_scatter_kernel if scatter else _row_gather_kernel
    return pl.pallas_call(
        functools.partial(kern, rows=rows, n_steps=t // rows),
        grid_spec=grid_spec,
        out_shape=jax.ShapeDtypeStruct((dst_rows, width), src.dtype),
        input_output_aliases={2: 0} if scatter else {},
        compiler_params=_params("arbitrary"),
        name="row_scatter" if scatter else "row_gather",
    )(*operands)


def _moe_kernel(ta_ref, tb_ref, nu_ref, xs_ref, w1a_ref, w3a_ref, w2a_ref, w1b_ref, w3b_ref, w2b_ref,
                g_ref, b_ref, o_ref, pend):
    t = pl.program_id(0)
    d = o_ref.shape[1]
    n_used = nu_ref[0]

    @pl.when(t == 0)
    def _():
        pend[...] = jnp.zeros(pend.shape, F32)

    @pl.when(t < n_used)
    def _():
        o_ref[...] = _layer_norm(pend[...], g_ref[...], b_ref[...])
        x = xs_ref[:, :d]
        ext = xs_ref[:, d:]
        xb = x.astype(BF16)

        def ffn(w1_ref, w3_ref, w2_ref):
            h1 = _dot(xb, w1_ref[0].astype(BF16))
            h3 = _dot(xb, w3_ref[0].astype(BF16))
            return _dot((h1 * _sigmoid(h1) * h3).astype(BF16), w2_ref[0].astype(BF16))

        y = ext[:, 2:3] * ffn(w1a_ref, w3a_ref, w2a_ref) + ext[:, 3:4] * ffn(w1b_ref, w3b_ref, w2b_ref)
        pend[...] = ALPHA * x + y

    @pl.when(t == n_used)
    def _():
        o_ref[...] = _layer_norm(pend[...], g_ref[...], b_ref[...])

    @pl.when(t > n_used)
    def _():
        o_ref[...] = jnp.zeros(o_ref.shape, o_ref.dtype)


def moe_ln(tile_ea, tile_eb, n_used, xs, w1, w3, w2, ln_g, ln_b, tm):
    p, wide = xs.shape
    d = wide - EXT
    de = w1.shape[2]
    n = p // tm

    def w_spec(shape, which):
        return pl.BlockSpec(shape, lambda t, ta, tb, nu: ((ta, tb)[which][jnp.minimum(t, n - 1)], 0, 0))

    grid_spec = pltpu.PrefetchScalarGridSpec(
        num_scalar_prefetch=3,
        grid=(n + 1,),
        in_specs=[pl.BlockSpec((tm, wide), lambda t, ta, tb, nu: (jnp.minimum(t, n - 1), 0)),
                  w_spec((1, d, de), 0), w_spec((1, d, de), 0), w_spec((1, de, d), 0),
                  w_spec((1, d, de), 1), w_spec((1, d, de), 1), w_spec((1, de, d), 1),
                  pl.BlockSpec((1, d), lambda t, ta, tb, nu: (0, 0)),
                  pl.BlockSpec((1, d), lambda t, ta, tb, nu: (0, 0))],
        out_specs=pl.BlockSpec((tm, d), lambda t, ta, tb, nu: (jnp.maximum(t - 1, 0), 0)),
        scratch_shapes=[pltpu.VMEM((tm, d), F32)],
    )
    return pl.pallas_call(
        _moe_kernel,
        grid_spec=grid_spec,
        out_shape=jax.ShapeDtypeStruct((p, d), F32),
        compiler_params=_params("arbitrary"),
        name="moe_ln",
    )(tile_ea, tile_eb, n_used, xs, w1, w3, w2, w1, w3, w2, ln_g, ln_b)


def _moe_few_kernel(x_ref, w1_ref, w3_ref, w2_ref, g_ref, b_ref, o_ref, acc_s):
    e = pl.program_id(0)
    d = o_ref.shape[1]

    @pl.when(e == 0)
    def _():
        acc_s[...] = jnp.zeros(acc_s.shape, F32)

    x = x_ref[:, :d]
    ext = x_ref[:, d:]
    xb = x.astype(BF16)
    h1 = _dot(xb, w1_ref[0].astype(BF16))
    h3 = _dot(xb, w3_ref[0].astype(BF16))
    y = _dot((h1 * _sigmoid(h1) * h3).astype(BF16), w2_ref[0].astype(BF16))
    ef = e.astype(F32)
    gate = jnp.where(ext[:, 4:5] == ef, ext[:, 2:3], 0.0) + jnp.where(ext[:, 5:6] == ef, ext[:, 3:4], 0.0)
    acc_s[...] += gate * y

    @pl.when(e == pl.num_programs(0) - 1)
    def _():
        o_ref[...] = _layer_norm(ALPHA * x + acc_s[...], g_ref[...], b_ref[...])


def moe_ln_few(x1ext, w1, w3, w2, ln_g, ln_b):
    t, wide = x1ext.shape
    d = wide - EXT
    de = w1.shape[2]
    return pl.pallas_call(
        _moe_few_kernel,
        grid=(w1.shape[0],),
        in_specs=[_whole((t, wide)), pl.BlockSpec((1, d, de), lambda e: (e, 0, 0)),
                  pl.BlockSpec((1, d, de), lambda e: (e, 0, 0)), pl.BlockSpec((1, de, d), lambda e: (e, 0, 0)),
                  _whole((1, d)), _whole((1, d))],
        out_specs=_whole((t, d)),
        out_shape=jax.ShapeDtypeStruct((t, d), F32),
        scratch_shapes=[pltpu.VMEM((t, d), F32)],
        compiler_params=_params("arbitrary"),
        name="moe_ln_few",
    )(x1ext, w1, w3, w2, ln_g, ln_b)


def _bucket_experts():
    ea, eb = [], []
    for g in range(N_GROUPS):
        for ja in range(EXP_PER_GROUP):
            for jb in range(ja + 1, EXP_PER_GROUP):
                ea.append(g * EXP_PER_GROUP + ja)
                eb.append(g * EXP_PER_GROUP + jb)
    return np.asarray(ea, np.int32), np.asarray(eb, np.int32)


def _bucket_layout(x1ext, counts, tm):
    t, wide = x1ext.shape
    d = wide - EXT
    bucket = x1ext[:, d].astype(jnp.int32)
    rank = x1ext[:, d + 1].astype(jnp.int32)
    counts = counts[0, :N_BUCKETS].astype(jnp.int32)
    padded = ((counts + tm - 1) // tm) * tm
    row_end = jnp.cumsum(padded)
    row_start = row_end - padded
    ids = jnp.arange(N_BUCKETS, dtype=jnp.int32)
    pos = rank + jnp.sum(jnp.where(bucket[:, None] == ids[None, :], row_start[None, :], 0), axis=1)
    n_rows = _round_up(t, tm) + N_BUCKETS * tm
    n_tiles = n_rows // tm
    n_used = (row_end[-1] // tm).astype(jnp.int32)
    tile_row = jnp.minimum(jnp.arange(n_tiles, dtype=jnp.int32), n_used - 1) * tm
    tile_bucket = jnp.sum((row_end[None, :] <= tile_row[:, None]).astype(jnp.int32), axis=1)
    ea, eb = _bucket_experts()
    in_bucket = tile_bucket[:, None] == ids[None, :]
    tile_ea = jnp.sum(jnp.where(in_bucket, jnp.asarray(ea)[None, :], 0), axis=1)
    tile_eb = jnp.sum(jnp.where(in_bucket, jnp.asarray(eb)[None, :], 0), axis=1)
    return pos.astype(jnp.int32), tile_ea, tile_eb, n_used.reshape(1), n_rows


def hier_moe_ln(x1ext, counts, moe_w, ln_g, ln_b, sorted_buf=None):
    w1, w3, w2 = moe_w
    t, wide = x1ext.shape
    if t <= FEW_TOKENS:
        return moe_ln_few(x1ext, w1, w3, w2, ln_g, ln_b), None
    tm = 256
    pos, tile_ea, tile_eb, n_used, n_rows = _bucket_layout(x1ext, counts, tm)
    if sorted_buf is None:
        sorted_buf = jnp.zeros((n_rows, wide), F32)
    xs = row_move(pos, x1ext, n_rows, scatter=True, dst_init=sorted_buf)
    ys = moe_ln(tile_ea, tile_eb, n_used, xs, w1, w3, w2, ln_g, ln_b, tm)
    return row_move(pos, ys, t, scatter=False), xs


def _rope_tables(s, offset, tm):
    half = ROPE // 2
    inv_freq = ROPE_BASE ** (-jnp.arange(half, dtype=F32) / half)
    ang = (offset + jnp.arange(s)).astype(F32)[:, None] * inv_freq[None, :]
    cos, sin = jnp.cos(ang), jnp.sin(ang)
    cc = jnp.concatenate([cos, cos, cos, cos], -1)
    ss = jnp.concatenate([-sin, sin, -sin, sin], -1)
    if tm > s:
        cc, ss = jnp.tile(cc, (tm // s, 1)), jnp.tile(ss, (tm // s, 1))
    return cc, ss


def _round_up(n, m):
    return (n + m - 1) // m * m


def ab_layer(x, b, s, wts, route_w, cache):
    w_big, w_gate, b_gate, g_ml, w_out, ln_g, ln_b = wts
    t = b * s
    ls_cols = np.array([1.0] * H_FOX + [0.0] * H_ML + [1.0] * H_ML)
    fq, mq, mk, mv, mo, fk_bf, fv_bf, gates, fk, fv = ab_proj(x, w_big, w_gate, b_gate, ls_cols)
    flf = gates[:, :H_FOX].reshape(b, s, H_FOX)
    ml_rows = jnp.swapaxes(gates[:, H_FOX:H_FOX + 2 * H_ML].reshape(b, s, 2 * H_ML), 1, 2)
    ml_rows = jnp.pad(ml_rows, ((0, 0), (0, 16 - 2 * H_ML), (0, 0)))
    if cache is None:
        past_kv, lf_all = None, flf
        c0 = jnp.zeros((b, H_ML, DH_ML, DH_ML), F32)
        n0 = jnp.zeros((b, H_ML, 1, DH_ML), F32)
        m0 = jnp.zeros((b, H_ML, 1, 1), F32)
        tq, chunk = min(512, s), 256
    else:
        ck, cv, clf, c0, n0, m0 = cache
        past = ck.shape[1]
        past_kv = (ck.reshape(b, past, W_FOX).astype(BF16), cv.reshape(b, past, W_FOX).astype(BF16))
        lf_all = jnp.concatenate([clf, flf], 1)
        n0 = n0.reshape(b, H_ML, 1, DH_ML)
        m0 = m0.reshape(b, H_ML, 1, 1)
        tq, chunk = s, s
    lk = lf_all.shape[1]
    lf_rows = jnp.pad(jnp.swapaxes(lf_all, 1, 2), ((0, 0), (0, 16 - H_FOX), (0, _round_up(lk, 256) - lk)))
    fo = fox_attention(fq.reshape(b, s, W_FOX), fk_bf.reshape(b, s, W_FOX), fv_bf.reshape(b, s, W_FOX), lf_rows,
                       tq=tq, past=past_kv, tk=256 if cache is not None else tq)
    mh, c_new, n_new, m_new = mlstm(mq.reshape(b, s, W_ML), mk.reshape(b, s, W_ML), mv.reshape(b, s, W_ML),
                                    ml_rows, c0, n0, m0, chunk=chunk)
    x1ext, counts = ab_out(fo.reshape(t, W_FOX), mh.reshape(t, W_ML), mo, x, g_ml, w_out, ln_g, ln_b, route_w)
    state = (fk, fv, flf, c_new, n_new.reshape(b, H_ML, DH_ML), m_new.reshape(b, H_ML))
    return x1ext, counts, state


def mla_layer(x, b, s, wts, route_w, cache, prev_c=(), prev_r=()):
    w_down, g_q, g_kv, w_uq, wuk_t, wuv, w_out, ln_g, ln_b = wts
    t = b * s
    tm = _row_tile(t)
    past = 0 if cache is None else cache[0].shape[1]
    cc, ss = _rope_tables(s, past, tm)
    cq, ckv, kr, ckv_bf, kr2_bf = mla_down(x, w_down, g_q, g_kv, cc, ss, s, prev_c, prev_r)
    q_lat, q_rope = mla_uq(cq, w_uq, wuk_t, cc, ss, s)
    kc = ckv_bf.reshape(b, s, KV_RANK)
    kr2 = kr2_bf.reshape(b, s, LANE)
    if cache is None:
        tq, past_k = 256, None
    else:
        c_ckv, c_kr = cache
        c_kr_bf = c_kr.astype(BF16)
        tq, past_k = s, (c_ckv.astype(BF16), jnp.concatenate([c_kr_bf, c_kr_bf], -1))
    o_lat = mla_attention(q_lat.reshape(b, s, -1), q_rope.reshape(b, s, -1), kc, kr2, tq=tq,
                          tk=tq if cache is None else 256, past=past_k)
    x1ext, counts = mla_out(o_lat.reshape(t, -1), wuv, w_out, x, ln_g, ln_b, route_w)
    return x1ext, counts, (ckv, kr)


def _prep_ab_weights(w_in, b_fox_f, b_ml_i, b_ml_f, g_ml, w_out, ln_g, ln_b):
    sizes = (W_FOX, W_FOX, W_FOX, H_FOX, W_ML, W_ML, W_ML, H_ML, H_ML, W_ML)
    idx = np.cumsum(sizes[:-1]).tolist()
    fq, fk, fv, ff, mq, mk, mv, mi, mf, mo = jnp.split(w_in, idx, axis=1)
    w_big = jnp.concatenate([fq * (DH_FOX ** -0.5 * LOG2E), mq, mk, mv, mo, fk, fv], 1).astype(BF16)
    w_gate = jnp.concatenate([ff, mi, mf], 1)
    b_gate = jnp.concatenate([b_fox_f, b_ml_i, b_ml_f])
    return (w_big, w_gate, b_gate, g_ml[None, :], w_out.astype(BF16), ln_g[None, :], ln_b[None, :])


def _prep_mla_weights(w_down, g_q, w_uq, g_kv, w_uk, w_uv, w_out, ln_g, ln_b):
    half = ROPE // 2
    cq_w, ckv_w, kr_w = jnp.split(w_down, [Q_RANK, Q_RANK + KV_RANK], axis=1)
    kr_sw = jnp.concatenate([kr_w[:, half:], kr_w[:, :half]], 1)
    w_down_p = jnp.concatenate([cq_w, ckv_w, kr_w, kr_w, kr_sw, kr_sw], 1).astype(BF16)
    uq = w_uq.reshape(Q_RANK, H_MLA, NOPE + ROPE)
    uq_nope = uq[:, :, :NOPE].reshape(Q_RANK, H_MLA * NOPE)
    uq_rope = uq[:, :, NOPE:]
    uq_rope_sw = jnp.concatenate([uq_rope[..., half:], uq_rope[..., :half]], -1)
    w_uq_p = jnp.concatenate([uq_nope, uq_rope.reshape(Q_RANK, -1), uq_rope_sw.reshape(Q_RANK, -1)], 1).astype(BF16)
    wuk_t = jnp.transpose(w_uk, (1, 2, 0)).astype(BF16)
    wuv = jnp.transpose(w_uv, (1, 0, 2)).astype(BF16)
    return (w_down_p, g_q[None, :], g_kv[None, :], w_uq_p, wuk_t, wuv, w_out.astype(BF16), ln_g[None, :], ln_b[None, :])


def kernel(x_prompt, x_sample, cache_fox_k, cache_fox_v, cache_fox_logf, state_mlstm_c, state_mlstm_n, state_mlstm_m, cache_mla_ckv, cache_mla_krope, w_ab_in, b_fox_f, b_mlstm_i, b_mlstm_f, g_mlstm_norm, w_ab_out, w_mla_down, g_mla_q, w_mla_uq, g_mla_kv, w_mla_uk, w_mla_uv, w_mla_out, ln1_g, ln1_b, ln2_g, ln2_b, w_moe_group, b_moe_group, w_moe_router, b_moe_router, w_exp_gate, w_exp_up, w_exp_down):
    bp, sp, d = x_prompt.shape
    bs, ss_, _ = x_sample.shape
    xp = x_prompt.reshape(bp * sp, d)
    xs = x_sample.reshape(bs * ss_, d)
    ab_p, ab_s, c_p, c_s = [], [], [], []
    n_c = DEPTH // 2
    assert n_c >= 2
    buf_p = buf_s = None
    for l in range(DEPTH):
        j = l // 2
        route_w = (jnp.concatenate([w_moe_group[l], w_moe_router[l]], 1),
                   jnp.concatenate([b_moe_group[l], b_moe_router[l]]))
        if l % 2 == 0:
            wts = _prep_ab_weights(w_ab_in[j], b_fox_f[j], b_mlstm_i[j], b_mlstm_f[j], g_mlstm_norm[j],
                                   w_ab_out[j], ln1_g[l], ln1_b[l])
            xp1, cnt_p, st_p = ab_layer(xp, bp, sp, wts, route_w, None)
            xs1, cnt_s, st_s = ab_layer(xs, bs, ss_, wts, route_w,
                                        (cache_fox_k[j], cache_fox_v[j], cache_fox_logf[j],
                                         state_mlstm_c[j], state_mlstm_n[j], state_mlstm_m[j]))
            ab_p.append(st_p)
            ab_s.append(st_s)
        else:
            wts = _prep_mla_weights(w_mla_down[j], g_mla_q[j], w_mla_uq[j], g_mla_kv[j], w_mla_uk[j], w_mla_uv[j],
                                    w_mla_out[j], ln1_g[l], ln1_b[l])
            last = j == n_c - 1
            prev = lambda sts, i: tuple(st[i] for st in sts) if last else ()
            xp1, cnt_p, st_p = mla_layer(xp, bp, sp, wts, route_w, None, prev(c_p, 0), prev(c_p, 1))
            xs1, cnt_s, st_s = mla_layer(xs, bs, ss_, wts, route_w, (cache_mla_ckv[j], cache_mla_krope[j]),
                                         prev(c_s, 0), prev(c_s, 1))
            c_p.append(st_p)
            c_s.append(st_s)
        moe_w = (w_exp_gate[l], w_exp_up[l], w_exp_down[l])
        xp, buf_p = hier_moe_ln(xp1, cnt_p, moe_w, ln2_g[l][None, :], ln2_b[l][None, :], buf_p)
        xs, buf_s = hier_moe_ln(xs1, cnt_s, moe_w, ln2_g[l][None, :], ln2_b[l][None, :], buf_s)

    def stack(groups, i):
        return jnp.stack([g[i] for g in groups])

    def states(ab, c, b, s):
        kv = [jnp.stack([st[i].reshape(b, s, H_FOX, DH_FOX) for st in ab]) for i in (0, 1)]
        lat = [c[-1][0].reshape(n_c, b, s, KV_RANK), c[-1][1].reshape(n_c, b, s, ROPE)]
        return tuple(kv) + tuple(stack(ab, i) for i in (2, 3, 4, 5)) + tuple(lat)

    return (xp.reshape(bp, sp, d), xs.reshape(bs, ss_, d)) + states(ab_p, c_p, bp, sp) + states(ab_s, c_s, bs, ss_)
```

```python
import functools

import numpy as np
import jax
import jax.numpy as jnp
from jax import lax
from jax.experimental import pallas as pl
from jax.experimental.pallas import tpu as pltpu

F32 = jnp.float32
BF16 = jnp.bfloat16

LANE = 128
VMEM_LIMIT_BYTES = 48 * 1024 * 1024

H_FOX, DH_FOX = 8, 64
W_FOX = H_FOX * DH_FOX
H_ML, DH_ML = 4, 128
W_ML = H_ML * DH_ML
H_MLA, NOPE, ROPE, DV_MLA = 8, 128, 64, 128
Q_RANK, KV_RANK = 384, 256
ROPE_BASE = 10000.0
N_GROUPS, EXP_PER_GROUP, TOP_K = 4, 8, 2
N_EXPERTS = N_GROUPS * EXP_PER_GROUP
DEPTH = 4
ALPHA = (2 * DEPTH) ** 0.25
LN_EPS = 1e-5
RMS_EPS = 1e-6
CHUNK = 64
NEG_BIG = -1e30
LOG2E = 1.4426950408889634


def _params(*sem):
    return pltpu.CompilerParams(dimension_semantics=sem, vmem_limit_bytes=VMEM_LIMIT_BYTES)


def _dot(a, b):
    return jnp.dot(a, b, preferred_element_type=F32)


def _dot_nt(a, b):
    return lax.dot_general(a, b, (((1,), (1,)), ((), ())), preferred_element_type=F32)


def _split3(x):
    hi = x.astype(BF16)
    r = x - hi.astype(F32)
    mid = r.astype(BF16)
    lo = (r - mid.astype(F32)).astype(BF16)
    return hi, mid, lo


def _layer_norm(y, g, b):
    mu = jnp.mean(y, axis=1, keepdims=True)
    yc = y - mu
    var = jnp.mean(yc * yc, axis=1, keepdims=True)
    return yc * lax.rsqrt(var + LN_EPS) * g + b


def _sigmoid(x):
    return 1.0 / (1.0 + jnp.exp(-x))


def _row_tile(t):
    return min(512, t)


def _ab_proj_kernel(x_ref, w_ref, gw_ref, gb_ref, gmask_ref,
                    fq_ref, mq_ref, mk_ref, mv_ref, mo_ref, kbf_ref, vbf_ref, gates_ref, k_ref, v_ref):
    x = x_ref[...]
    xh = x.astype(BF16)
    c0 = 0
    for o_ref in (fq_ref, mq_ref, mk_ref, mv_ref, mo_ref):
        n = o_ref.shape[1]
        o_ref[...] = _dot(xh, w_ref[:, c0:c0 + n]).astype(BF16)
        c0 += n
    for bf_ref, state_ref in ((kbf_ref, k_ref), (vbf_ref, v_ref)):
        z = _dot(xh, w_ref[:, c0:c0 + W_FOX])
        c0 += W_FOX
        bf_ref[...] = z.astype(BF16)
        state_ref[...] = z
    xl = (x - xh.astype(F32)).astype(BF16)
    gg = _dot(xh, gw_ref[...])
    g = gg[:, :LANE] + gg[:, LANE:] + _dot(xl, gw_ref[:, :LANE]) + gb_ref[...]
    log_sig = jnp.minimum(g, 0.0) - jnp.log1p(jnp.exp(-jnp.abs(g)))
    gates_ref[...] = jnp.where(gmask_ref[...] > 0.0, log_sig, g)


def ab_proj(x, w_big, w_gate, b_gate, log_sigmoid_cols):
    t, k = x.shape
    tm = _row_tile(t)
    n_gate = w_gate.shape[1]
    wp = jnp.zeros((k, LANE), F32).at[:, :n_gate].set(w_gate)
    wh = wp.astype(BF16)
    wl = (wp - wh.astype(F32)).astype(BF16)
    bp = jnp.zeros((1, LANE), F32).at[0, :n_gate].set(b_gate)
    mask = jnp.zeros((1, LANE), F32).at[0, :n_gate].set(jnp.asarray(log_sigmoid_cols, F32))
    bf = lambda n: jax.ShapeDtypeStruct((t, n), BF16)
    return pl.pallas_call(
        _ab_proj_kernel,
        grid=(t // tm,),
        in_specs=[_rows(tm, k), _whole(w_big.shape), _whole((k, 2 * LANE)), _whole((1, LANE)), _whole((1, LANE))],
        out_specs=[_rows(tm, W_FOX)] + [_rows(tm, W_ML)] * 4 + [_rows(tm, W_FOX)] * 2 + [_rows(tm, LANE)]
        + [_rows(tm, W_FOX)] * 2,
        out_shape=[bf(W_FOX)] + [bf(W_ML)] * 4 + [bf(W_FOX)] * 2 + [jax.ShapeDtypeStruct((t, LANE), F32)]
        + [jax.ShapeDtypeStruct((t, W_FOX), F32)] * 2,
        compiler_params=_params("parallel"),
        name="ab_proj",
    )(x, w_big, jnp.concatenate([wh, wl], 1), bp, mask)


def _lanes(x, w):
    if w < LANE:
        return x[:, :w]
    return x if w == LANE else jnp.concatenate([x] * (w // LANE), axis=1)


def _softmax_step(s, at, m_s, l_s, acc_s, v):
    w = s.shape[1]
    m_prev = m_s[at]
    m_new = jnp.maximum(m_prev, jnp.max(s, axis=1, keepdims=True))
    p = jnp.exp2(s - _lanes(m_new, w))
    a = jnp.exp2(m_prev - m_new)
    l_s[at] = a * l_s[at] + jnp.sum(p, axis=1, keepdims=True)
    acc_s[at] = acc_s[at] * _lanes(a, acc_s.shape[-1]) + _dot(p.astype(BF16), v)
    m_s[at] = m_new


def _fox_kernel(q_ref, k_ref, v_ref, *rest, tq, tk, q_off, cblk):
    pk_ref, pv_ref = rest[:2] if len(rest) == 9 else (k_ref, v_ref)
    lf_ref, o_ref, ncum_ref, qm_s, m_s, l_s, acc_s = rest[-7:]
    qi = pl.program_id(1)
    sub = cblk // LANE
    n_pairs = H_FOX // 2

    @pl.when(qi == 0)
    def _():
        r = lax.broadcasted_iota(jnp.int32, (cblk, cblk), 0)
        c = lax.broadcasted_iota(jnp.int32, (cblk, cblk), 1)
        upper = jnp.where(r <= c, 1.0, 0.0).astype(BF16)
        carry = jnp.zeros((lf_ref.shape[1], 1), F32)
        for j in range(lf_ref.shape[2] // cblk):
            g1, g2, g3 = _split3(lf_ref[0, :, j * cblk:(j + 1) * cblk])
            cum = _dot(g1, upper) + _dot(g2, upper) + _dot(g3, upper) + carry
            carry = cum[:, cblk - 1:cblk]
            for h in range(H_FOX):
                for u in range(sub):
                    ncum_ref[h, j * sub + u] = cum[h:h + 1, u * LANE:(u + 1) * LANE] * (-LOG2E)

    lane = lax.broadcasted_iota(jnp.int32, (tq, LANE), 1)
    for hp in range(n_pairs):
        q = q_ref[0, :, hp * LANE:(hp + 1) * LANE]
        zero = jnp.zeros_like(q)
        qm_s[hp, 0:tq, :] = jnp.where(lane < DH_FOX, q, zero)
        qm_s[hp, tq:2 * tq, :] = jnp.where(lane >= DH_FOX, q, zero)
    m_s[...] = jnp.full(m_s.shape, NEG_BIG, F32)
    l_s[...] = jnp.zeros(l_s.shape, F32)
    acc_s[...] = jnp.zeros(acc_s.shape, F32)
    q_start = q_off + qi * tq

    def update(k_src, v_src, off, w, bias_of_head, mask):
        for hp in range(n_pairs):
            kt = k_src[0, pl.ds(off, w), hp * LANE:(hp + 1) * LANE]
            vt = v_src[0, pl.ds(off, w), hp * LANE:(hp + 1) * LANE]
            s = _dot_nt(qm_s[hp], kt)
            s = jnp.concatenate([s[:tq] + bias_of_head(2 * hp), s[tq:] + bias_of_head(2 * hp + 1)], axis=0)
            if mask is not None:
                s = jnp.where(mask, s, -jnp.inf)
            _softmax_step(s, hp, m_s, l_s, acc_s, vt)

    def full_tile(j, carry):
        def bias(h):
            return jnp.concatenate([ncum_ref[h, j * (tk // LANE) + u] for u in range(tk // LANE)], axis=1)
        update(pk_ref, pv_ref, pl.multiple_of(j * tk, tk), tk, bias, None)
        return carry

    lax.fori_loop(0, q_start // tk, full_tile, 0)

    def diag_bias(h):
        if tq >= LANE:
            return jnp.concatenate([ncum_ref[h, q_start // LANE + u] for u in range(tq // LANE)], axis=1)
        lo = q_off % LANE
        return ncum_ref[h, q_start // LANE][:, lo:lo + tq]

    r = lax.broadcasted_iota(jnp.int32, (2 * tq, tq), 0)
    c = lax.broadcasted_iota(jnp.int32, (2 * tq, tq), 1)
    update(k_ref, v_ref, pl.multiple_of(qi * tq, tq), tq, diag_bias, c <= jnp.where(r >= tq, r - tq, r))
    for hp in range(n_pairs):
        o = acc_s[hp] / l_s[hp]
        o_ref[0, :, hp * LANE:(hp + 1) * LANE] = jnp.where(lane < DH_FOX, o[:tq], o[tq:]).astype(o_ref.dtype)


def fox_attention(q, k, v, logf_rows, *, tq, past=None, tk=256, cblk=256):
    b, lq, _ = q.shape
    lkp = logf_rows.shape[2]
    q_off = 0 if past is None else past[0].shape[1]
    nq = lq // tq
    assert k.shape[1] == lq and q_off % tk == 0 and (tq % tk == 0 or lq == tq) and lkp % cblk == 0
    assert tq >= LANE or (nq == 1 and q_off % LANE + tq <= LANE)
    kern = functools.partial(_fox_kernel, tq=tq, tk=tk, q_off=q_off, cblk=cblk)
    whole_seq = [pl.BlockSpec((1, a.shape[1], W_FOX), lambda bi, qi: (bi, 0, 0)) for a in (k, v) + tuple(past or ())]
    return pl.pallas_call(
        kern,
        grid=(b, nq),
        in_specs=[pl.BlockSpec((1, tq, W_FOX), lambda bi, qi: (bi, qi, 0))] + whole_seq + [
            pl.BlockSpec((1, logf_rows.shape[1], lkp), lambda bi, qi: (bi, 0, 0)),
        ],
        out_specs=pl.BlockSpec((1, tq, W_FOX), lambda bi, qi: (bi, qi, 0)),
        out_shape=jax.ShapeDtypeStruct((b, lq, W_FOX), BF16),
        scratch_shapes=[pltpu.VMEM((H_FOX, lkp // LANE, 1, LANE), F32), pltpu.VMEM((H_FOX // 2, 2 * tq, LANE), BF16),
                        pltpu.VMEM((H_FOX // 2, 2 * tq, LANE), F32), pltpu.VMEM((H_FOX // 2, 2 * tq, LANE), F32),
                        pltpu.VMEM((H_FOX // 2, 2 * tq, LANE), F32)],
        compiler_params=_params("parallel", "arbitrary"),
        name="fox_attention",
    )(q, k, v, *(past or ()), logf_rows)


def _mlstm_kernel(q_ref, k_ref, v_ref, g_ref, c0_ref, n0_ref, m0_ref,
                  h_ref, c_ref, n_ref, m_ref, c_s, n_s, m_s, *, chunk):
    ci = pl.program_id(1)
    L = chunk
    scale = DH_ML ** -0.5

    @pl.when(ci == 0)
    def _():
        c_s[...] = c0_ref[0]
        n_s[...] = n0_ref[0]
        m_s[...] = m0_ref[0]

    r = lax.broadcasted_iota(jnp.int32, (L, L), 0)
    c = lax.broadcasted_iota(jnp.int32, (L, L), 1)
    eye = r == c
    causal = c <= r
    upper = jnp.where(r <= c, 1.0, 0.0).astype(BF16)
    g = g_ref[0]
    g1, g2, g3 = _split3(g)
    cum = _dot(g1, upper) + _dot(g2, upper) + _dot(g3, upper)

    def to_col(row):
        return jnp.sum(jnp.where(eye, row, 0.0), axis=1, keepdims=True)

    for h in range(H_ML):
        sl = slice(h * DH_ML, (h + 1) * DH_ML)
        qh, kh, vh = q_ref[0, :, sl], k_ref[0, :, sl], v_ref[0, :, sl]
        ig_row = g[h:h + 1, :]
        bh_row = cum[H_ML + h:H_ML + h + 1, :]
        bh_col, ig_col = to_col(bh_row), to_col(ig_row)
        m0 = m_s[h]
        c0 = c_s[h]
        n0 = n_s[h]
        logd = jnp.where(causal, bh_col - bh_row + ig_row, -jnp.inf)
        inter = bh_col + m0
        m_col = jnp.maximum(inter, jnp.max(logd, axis=1, keepdims=True))
        d = jnp.exp(logd - m_col)
        a_col = jnp.exp(inter - m_col)
        s = _dot_nt(qh, kh) * scale * d
        num = _dot(s.astype(BF16), vh) + _dot_nt(qh, c0.astype(BF16)) * a_col
        qf = qh.astype(F32)
        den = jnp.sum(s, axis=1, keepdims=True) + a_col * jnp.sum(qf * n0, axis=1, keepdims=True)
        den = jnp.maximum(jnp.abs(den), jnp.exp(-m_col))
        h_ref[0, :, sl] = (num / den).astype(h_ref.dtype)

        m_l = m_col[L - 1:L, :]
        bh_l = bh_row[:, L - 1:L]
        a_l = jnp.exp(bh_l + m0 - m_l)
        w_col = jnp.exp(bh_l - bh_col + ig_col - m_l)
        vw_t = jnp.transpose(vh.astype(F32) * w_col).astype(BF16)
        c_s[h] = a_l * c0 + _dot(vw_t, kh) * scale
        n_s[h] = a_l * n0 + jnp.sum(kh.astype(F32) * w_col, axis=0, keepdims=True) * scale
        m_s[h] = m_l

    @pl.when(ci == pl.num_programs(1) - 1)
    def _():
        c_ref[0] = c_s[...]
        n_ref[0] = n_s[...]
        m_ref[0] = m_s[...]


def mlstm(q, k, v, gate_rows, c0, n0, m0, *, chunk):
    b, s, _ = q.shape
    nc = s // chunk
    seq = pl.BlockSpec((1, chunk, W_ML), lambda bi, ci: (bi, ci, 0))
    c_spec = pl.BlockSpec((1, H_ML, DH_ML, DH_ML), lambda bi, ci: (bi, 0, 0, 0))
    n_spec = pl.BlockSpec((1, H_ML, 1, DH_ML), lambda bi, ci: (bi, 0, 0, 0))
    m_spec = pl.BlockSpec((1, H_ML, 1, 1), lambda bi, ci: (bi, 0, 0, 0))
    return pl.pallas_call(
        functools.partial(_mlstm_kernel, chunk=chunk),
        grid=(b, nc),
        in_specs=[seq, seq, seq, pl.BlockSpec((1, gate_rows.shape[1], chunk), lambda bi, ci: (bi, 0, ci)),
                  c_spec, n_spec, m_spec],
        out_specs=[seq, c_spec, n_spec, m_spec],
        out_shape=[jax.ShapeDtypeStruct((b, s, W_ML), BF16),
                   jax.ShapeDtypeStruct(c0.shape, F32), jax.ShapeDtypeStruct(n0.shape, F32),
                   jax.ShapeDtypeStruct(m0.shape, F32)],
        scratch_shapes=[pltpu.VMEM((H_ML, DH_ML, DH_ML), F32), pltpu.VMEM((H_ML, 1, DH_ML), F32),
                        pltpu.VMEM((H_ML, 1, 1), F32)],
        compiler_params=_params("parallel", "arbitrary"),
        name="mlstm",
    )(q, k, v, gate_rows, c0, n0, m0)


N_PAIRS = EXP_PER_GROUP * (EXP_PER_GROUP - 1) // 2
N_BUCKETS = N_GROUPS * N_PAIRS
assert N_BUCKETS <= LANE
EXT = LANE
FEW_TOKENS = 1024


def _route_store(out, wr_ref, br_ref, ltri_ref, o_ref, cnt_ref, cnt_s):
    tm, d = out.shape

    @pl.when(pl.program_id(0) == 0)
    def _():
        cnt_s[...] = jnp.zeros(cnt_s.shape, F32)

    xh = out.astype(BF16)
    xl = (out - xh.astype(F32)).astype(BF16)
    zz = _dot(xh, wr_ref[...])
    z = zz[:, :LANE] + zz[:, LANE:] + _dot(xl, wr_ref[:, :LANE]) + br_ref[...]
    lane = lax.broadcasted_iota(jnp.int32, (tm, LANE), 1).astype(F32)
    far = float(LANE)
    neg = -jnp.inf
    gl = jnp.where(lane < N_GROUPS, z, neg)
    gmax = jnp.max(gl, axis=1, keepdims=True)
    g_idx = jnp.min(jnp.where(gl == gmax, lane, far), axis=1, keepdims=True)
    g_gate = 1.0 / jnp.sum(jnp.exp(gl - gmax), axis=1, keepdims=True)
    lo = N_GROUPS + EXP_PER_GROUP * g_idx
    el = jnp.where(jnp.logical_and(lane >= lo, lane < lo + EXP_PER_GROUP), z, neg)
    v1 = jnp.max(el, axis=1, keepdims=True)
    i1 = jnp.min(jnp.where(el == v1, lane, far), axis=1, keepdims=True)
    el2 = jnp.where(lane == i1, neg, el)
    v2 = jnp.max(el2, axis=1, keepdims=True)
    i2 = jnp.min(jnp.where(el2 == v2, lane, far), axis=1, keepdims=True)
    e21 = jnp.exp(v2 - v1)
    w1 = g_gate / (1.0 + e21)
    w2 = w1 * e21
    j1, j2 = i1 - lo, i2 - lo
    ja, jb = jnp.minimum(j1, j2), jnp.maximum(j1, j2)
    pair = ja * (2 * EXP_PER_GROUP - 1 - ja) * 0.5 + (jb - ja - 1.0)
    bucket = g_idx * N_PAIRS + pair
    first_low = j1 < j2
    wa = jnp.where(first_low, w1, w2)
    wb = jnp.where(first_low, w2, w1)
    onehot = jnp.where(lane == bucket, 1.0, 0.0)
    earlier = _dot(ltri_ref[...], onehot.astype(BF16))
    cnt = cnt_s[...]
    rank = jnp.sum(onehot * (earlier + cnt), axis=1, keepdims=True)
    cnt_new = cnt + jnp.sum(onehot, axis=0, keepdims=True)
    cnt_s[...] = cnt_new
    cnt_ref[...] = cnt_new
    o_ref[:, :d] = out
    e_lo = lo - N_GROUPS + ja
    e_hi = lo - N_GROUPS + jb
    record = (bucket, rank, wa, wb, e_lo, e_hi)
    ext = jnp.zeros((tm, LANE), F32)
    for k, col in enumerate(record):
        ext = jnp.where(lane == float(k), col, ext)
    o_ref[:, d:] = ext


def _route_operands(w_route, b_route, tm):
    n = w_route.shape[1]
    wp = jnp.zeros((w_route.shape[0], LANE), F32).at[:, :n].set(w_route)
    wh = wp.astype(BF16)
    wl = (wp - wh.astype(F32)).astype(BF16)
    bp = jnp.zeros((1, LANE), F32).at[0, :n].set(b_route)
    ltri = jnp.asarray(np.tril(np.ones((tm, tm), np.float32), -1), BF16)
    return jnp.concatenate([wh, wl], 1), bp, ltri


def _route_specs(k, tm):
    return [_whole((k, 2 * LANE)), _whole((1, LANE)), _whole((tm, tm))]


def _ab_out_kernel(fo_ref, mh_ref, mo_ref, x_ref, gml_ref, w_ref, g_ref, b_ref, wr_ref, br_ref, ltri_ref,
                   o_ref, cnt_ref, cnt_s):
    parts = [fo_ref[...]]
    for h in range(H_ML):
        sl = slice(h * DH_ML, (h + 1) * DH_ML)
        mh = mh_ref[:, sl].astype(F32)
        ms = jnp.mean(mh * mh, axis=1, keepdims=True)
        nm = mh * lax.rsqrt(ms + RMS_EPS) * gml_ref[:, sl]
        parts.append((nm * _sigmoid(mo_ref[:, sl].astype(F32))).astype(BF16))
    cat = jnp.concatenate(parts, axis=1)
    y = ALPHA * x_ref[...] + _dot(cat, w_ref[...])
    out = _layer_norm(y, g_ref[...], b_ref[...])
    _route_store(out, wr_ref, br_ref, ltri_ref, o_ref, cnt_ref, cnt_s)


def _rows(tm, n):
    return pl.BlockSpec((tm, n), lambda i: (i, 0))


def _whole(shape):
    return pl.BlockSpec(shape, lambda i: (0,) * len(shape))


def _routed_out_call(kern, name, t, d, tm, in_specs, operands, route_w):
    w_route, b_route = route_w
    return pl.pallas_call(
        kern,
        grid=(t // tm,),
        in_specs=in_specs + _route_specs(d, tm),
        out_specs=[_rows(tm, d + EXT), _whole((1, LANE))],
        out_shape=[jax.ShapeDtypeStruct((t, d + EXT), F32), jax.ShapeDtypeStruct((1, LANE), F32)],
        scratch_shapes=[pltpu.VMEM((1, LANE), F32)],
        compiler_params=_params("arbitrary"),
        name=name,
    )(*operands, *_route_operands(w_route, b_route, tm))


def ab_out(fo, mh, mo, x, g_ml, w_out, ln_g, ln_b, route_w):
    t, d = x.shape
    tm = _row_tile(t)
    in_specs = [_rows(tm, W_FOX), _rows(tm, W_ML), _rows(tm, W_ML), _rows(tm, d), _whole((1, W_ML)),
                _whole(w_out.shape), _whole((1, d)), _whole((1, d))]
    return _routed_out_call(_ab_out_kernel, "ab_out", t, d, tm, in_specs,
                            (fo, mh, mo, x, g_ml, w_out, ln_g, ln_b), route_w)


def _rms(z, g):
    return z * lax.rsqrt(jnp.mean(z * z, axis=1, keepdims=True) + RMS_EPS) * g


def _mla_down_kernel(x_ref, w_ref, gq_ref, gkv_ref, cc_ref, ss_ref, *rest, n_prev):
    prev_c, prev_r = rest[:n_prev], rest[n_prev:2 * n_prev]
    cq_ref, ckv_ref, kr_ref, ckvb_ref, kr2_ref = rest[2 * n_prev:]

    def put(state_ref, prev, value):
        if n_prev:
            for j, p in enumerate(prev):
                state_ref[j] = p[...]
            state_ref[n_prev] = value
        else:
            state_ref[...] = value

    xb = x_ref[...].astype(BF16)
    cq_ref[...] = _rms(_dot(xb, w_ref[:, 0:Q_RANK]), gq_ref[...]).astype(BF16)
    ckv = _rms(_dot(xb, w_ref[:, Q_RANK:Q_RANK + KV_RANK]), gkv_ref[...])
    put(ckv_ref, prev_c, ckv)
    ckvb_ref[...] = ckv.astype(BF16)
    c0 = Q_RANK + KV_RANK
    rope = (_dot(xb, w_ref[:, c0:c0 + LANE]) * cc_ref[...] + _dot(xb, w_ref[:, c0 + LANE:c0 + 2 * LANE]) * ss_ref[...])
    put(kr_ref, prev_r, rope[:, :ROPE])
    kr2_ref[...] = rope.astype(BF16)


def _table_spec(table, tm, s):
    if table.shape[0] == tm:
        return pl.BlockSpec((tm, LANE), lambda i: (0, 0))
    per = s // tm
    return pl.BlockSpec((tm, LANE), lambda i: (i % per, 0))


def mla_down(x, w, g_q, g_kv, cc, ss, s, prev_c=(), prev_r=()):
    t, d = x.shape
    tm = _row_tile(t)
    tab = _table_spec(cc, tm, s)
    n_prev = len(prev_c)

    def state(width):
        if n_prev:
            return (pl.BlockSpec((n_prev + 1, tm, width), lambda i: (0, i, 0)),
                    jax.ShapeDtypeStruct((n_prev + 1, t, width), F32))
        return _rows(tm, width), jax.ShapeDtypeStruct((t, width), F32)

    (c_spec, c_shape), (r_spec, r_shape) = state(KV_RANK), state(ROPE)
    return pl.pallas_call(
        functools.partial(_mla_down_kernel, n_prev=n_prev),
        grid=(t // tm,),
        in_specs=[_rows(tm, d), _whole(w.shape), _whole((1, Q_RANK)), _whole((1, KV_RANK)), tab, tab]
        + [_rows(tm, KV_RANK)] * n_prev + [_rows(tm, ROPE)] * n_prev,
        out_specs=[_rows(tm, Q_RANK), c_spec, r_spec, _rows(tm, KV_RANK), _rows(tm, LANE)],
        out_shape=[jax.ShapeDtypeStruct((t, Q_RANK), BF16), c_shape, r_shape,
                   jax.ShapeDtypeStruct((t, KV_RANK), BF16), jax.ShapeDtypeStruct((t, LANE), BF16)],
        compiler_params=_params("parallel"),
        name="mla_down",
    )(x, w, g_q, g_kv, cc, ss, *prev_c, *prev_r)


def _mla_uq_kernel(cq_ref, w_ref, wuk_ref, cc_ref, ss_ref, qlat_ref, qrope_ref):
    cq = cq_ref[...]
    scale = (NOPE + ROPE) ** -0.5 * LOG2E
    r0 = H_MLA * NOPE
    r1 = r0 + H_MLA * ROPE
    qn = _dot(cq, w_ref[:, :r0]).astype(BF16)
    for h in range(H_MLA):
        qlat_ref[:, h * KV_RANK:(h + 1) * KV_RANK] = (
            _dot(qn[:, h * NOPE:(h + 1) * NOPE], wuk_ref[h]) * scale).astype(BF16)
    qr = _dot(cq, w_ref[:, r0:])
    cc, ss = cc_ref[...] * scale, ss_ref[...] * scale
    for p in range(H_MLA // 2):
        a, b = p * LANE, (r1 - r0) + p * LANE
        qrope_ref[:, p * LANE:(p + 1) * LANE] = (qr[:, a:a + LANE] * cc + qr[:, b:b + LANE] * ss).astype(BF16)


def mla_uq(cq, w, wuk_t, cc, ss, s):
    t = cq.shape[0]
    tm = _row_tile(t)
    tab = _table_spec(cc, tm, s)
    return pl.pallas_call(
        _mla_uq_kernel,
        grid=(t // tm,),
        in_specs=[_rows(tm, Q_RANK), _whole(w.shape), _whole(wuk_t.shape), tab, tab],
        out_specs=[_rows(tm, H_MLA * KV_RANK), _rows(tm, H_MLA * ROPE)],
        out_shape=[jax.ShapeDtypeStruct((t, H_MLA * KV_RANK), BF16), jax.ShapeDtypeStruct((t, H_MLA * ROPE), BF16)],
        compiler_params=_params("parallel"),
        name="mla_uq",
    )(cq, w, wuk_t, cc, ss)


def _mla_attn_kernel(ql_ref, qr_ref, kc_ref, kr_ref, *rest, tq, tk, q_off):
    pc_ref, pr_ref = rest[:2] if len(rest) == 8 else (kc_ref, kr_ref)
    o_ref, ql_s, qr_s, m_s, l_s, acc_s = rest[-6:]
    qi = pl.program_id(1)
    lane = lax.broadcasted_iota(jnp.int32, (tq, LANE), 1)
    for h in range(H_MLA):
        rows = slice(h * tq, (h + 1) * tq)
        ql_s[rows, :] = ql_ref[0, :, h * KV_RANK:(h + 1) * KV_RANK]
        qr = qr_ref[0, :, (h // 2) * LANE:(h // 2 + 1) * LANE]
        keep = (lane < ROPE) if h % 2 == 0 else (lane >= ROPE)
        qr_s[rows, :] = jnp.where(keep, qr, jnp.zeros_like(qr))
    m_s[...] = jnp.full(m_s.shape, NEG_BIG, F32)
    l_s[...] = jnp.zeros(l_s.shape, F32)
    acc_s[...] = jnp.zeros(acc_s.shape, F32)
    q_start = q_off + qi * tq

    def update(c_src, r_src, off, w, mask):
        kc = c_src[0, pl.ds(off, w), :]
        kr = r_src[0, pl.ds(off, w), :]
        s = _dot_nt(ql_s[...], kc) + _dot_nt(qr_s[...], kr)
        if mask is not None:
            s = jnp.where(mask, s, -jnp.inf)
        _softmax_step(s, Ellipsis, m_s, l_s, acc_s, kc)

    def full_tile(j, carry):
        update(pc_ref, pr_ref, pl.multiple_of(j * tk, tk), tk, None)
        return carry

    lax.fori_loop(0, q_start // tk, full_tile, 0)
    r = lax.broadcasted_iota(jnp.int32, (H_MLA * tq, tq), 0)
    c = lax.broadcasted_iota(jnp.int32, (H_MLA * tq, tq), 1)
    update(kc_ref, kr_ref, pl.multiple_of(qi * tq, tq), tq, (c // CHUNK) <= ((r & (tq - 1)) // CHUNK))
    for h in range(H_MLA):
        rows = slice(h * tq, (h + 1) * tq)
        o = acc_s[rows, :] / _lanes(l_s[rows, :], KV_RANK)
        o_ref[0, :, h * KV_RANK:(h + 1) * KV_RANK] = o.astype(o_ref.dtype)


def _mla_attn_pipelined_kernel(ql_ref, qr_ref, kc_ref, kr_ref, o_ref, ql_s, qr_s, m_s, l_s, acc_s, s_buf, *, tq):
    qi = pl.program_id(1)
    lane = lax.broadcasted_iota(jnp.int32, (tq, LANE), 1)
    for h in range(H_MLA):
        rows = slice(h * tq, (h + 1) * tq)
        ql_s[rows, :] = ql_ref[0, :, h * KV_RANK:(h + 1) * KV_RANK]
        qr = qr_ref[0, :, (h // 2) * LANE:(h // 2 + 1) * LANE]
        keep = (lane < ROPE) if h % 2 == 0 else (lane >= ROPE)
        qr_s[rows, :] = jnp.where(keep, qr, jnp.zeros_like(qr))
    m_s[...] = jnp.full(m_s.shape, NEG_BIG, F32)
    l_s[...] = jnp.zeros(l_s.shape, F32)
    acc_s[...] = jnp.zeros(acc_s.shape, F32)

    def keys(ref, j):
        return ref[0, pl.ds(pl.multiple_of(j * tq, tq), tq), :]

    def score(j, slot):
        s_buf[slot] = _dot_nt(ql_s[...], keys(kc_ref, j)) + _dot_nt(qr_s[...], keys(kr_ref, j))

    def consume(j, slot, mask):
        s = s_buf[slot]
        if mask is not None:
            s = jnp.where(mask, s, -jnp.inf)
        _softmax_step(s, Ellipsis, m_s, l_s, acc_s, keys(kc_ref, j))

    score(0, 0)

    def two_tiles(ip, carry):
        j = 2 * ip
        score(j + 1, 1)
        consume(j, 0, None)
        score(j + 2, 0)
        consume(j + 1, 1, None)
        return carry

    lax.fori_loop(0, qi // 2, two_tiles, 0)
    r = lax.broadcasted_iota(jnp.int32, (H_MLA * tq, tq), 0)
    c = lax.broadcasted_iota(jnp.int32, (H_MLA * tq, tq), 1)
    mask = (c // CHUNK) <= ((r & (tq - 1)) // CHUNK)

    @pl.when(qi % 2 == 1)
    def _():
        score(qi, 1)
        consume(qi - 1, 0, None)
        consume(qi, 1, mask)

    @pl.when(qi % 2 == 0)
    def _():
        consume(qi, 0, mask)

    for h in range(H_MLA):
        rows = slice(h * tq, (h + 1) * tq)
        o = acc_s[rows, :] / _lanes(l_s[rows, :], KV_RANK)
        o_ref[0, :, h * KV_RANK:(h + 1) * KV_RANK] = o.astype(o_ref.dtype)


def mla_attention(q_lat, q_rope, kc, kr2, *, tq, tk, past=None):
    b, lq, _ = q_lat.shape
    q_off = 0 if past is None else past[0].shape[1]
    assert kc.shape[1] == lq and q_off % tk == 0 and (tq % tk == 0 or lq == tq)
    assert tq % CHUNK == 0 and (tq & (tq - 1)) == 0
    scratch = [pltpu.VMEM((H_MLA * tq, KV_RANK), BF16), pltpu.VMEM((H_MLA * tq, LANE), BF16),
               pltpu.VMEM((H_MLA * tq, LANE), F32), pltpu.VMEM((H_MLA * tq, LANE), F32),
               pltpu.VMEM((H_MLA * tq, KV_RANK), F32)]
    if past is None and tk == tq:
        kern = functools.partial(_mla_attn_pipelined_kernel, tq=tq)
        scratch.append(pltpu.VMEM((2, H_MLA * tq, tq), F32))
    else:
        kern = functools.partial(_mla_attn_kernel, tq=tq, tk=tk, q_off=q_off)
    whole_seq = [pl.BlockSpec((1,) + a.shape[1:], lambda bi, qi: (bi, 0, 0)) for a in (kc, kr2) + tuple(past or ())]
    return pl.pallas_call(
        kern,
        grid=(b, lq // tq),
        in_specs=[
            pl.BlockSpec((1, tq, H_MLA * KV_RANK), lambda bi, qi: (bi, qi, 0)),
            pl.BlockSpec((1, tq, H_MLA * ROPE), lambda bi, qi: (bi, qi, 0)),
        ] + whole_seq,
        out_specs=pl.BlockSpec((1, tq, H_MLA * KV_RANK), lambda bi, qi: (bi, qi, 0)),
        out_shape=jax.ShapeDtypeStruct((b, lq, H_MLA * KV_RANK), BF16),
        scratch_shapes=scratch,
        compiler_params=_params("parallel", "arbitrary"),
        name="mla_attention",
    )(q_lat, q_rope, kc, kr2, *(past or ()))


def _mla_out_kernel(ol_ref, wuv_ref, w_ref, x_ref, g_ref, b_ref, wr_ref, br_ref, ltri_ref,
                    o_ref, cnt_ref, cnt_s):
    parts = [_dot(ol_ref[:, h * KV_RANK:(h + 1) * KV_RANK], wuv_ref[h]).astype(BF16) for h in range(H_MLA)]
    y = ALPHA * x_ref[...] + _dot(jnp.concatenate(parts, axis=1), w_ref[...])
    out = _layer_norm(y, g_ref[...], b_ref[...])
    _route_store(out, wr_ref, br_ref, ltri_ref, o_ref, cnt_ref, cnt_s)


def mla_out(o_lat, wuv, w_out, x, ln_g, ln_b, route_w):
    t, d = x.shape
    tm = _row_tile(t)
    in_specs = [_rows(tm, H_MLA * KV_RANK), _whole(wuv.shape), _whole(w_out.shape), _rows(tm, d),
                _whole((1, d)), _whole((1, d))]
    return _routed_out_call(_mla_out_kernel, "mla_out", t, d, tm, in_specs,
                            (o_lat, wuv, w_out, x, ln_g, ln_b), route_w)


ROW_SLOTS = 3


def _row_scatter_kernel(idx_ref, src_ref, dst_init_ref, dst_ref, buf, in_sem, out_sem, *, rows, n_steps):
    del dst_init_ref

    def load(i, slot):
        return pltpu.make_async_copy(src_ref.at[pl.ds(pl.multiple_of(i * rows, rows), rows)], buf.at[slot],
                                     in_sem.at[slot])

    def rows_done(slot):
        return pltpu.make_async_copy(buf.at[slot], dst_ref.at[pl.ds(0, rows)], out_sem.at[slot])

    i = pl.program_id(0)
    slot = i % ROW_SLOTS
    nxt = (i + 1) % ROW_SLOTS

    @pl.when(i == 0)
    def _():
        load(0, 0).start()

    @pl.when(i + 1 < n_steps)
    def _():
        @pl.when(i + 1 >= ROW_SLOTS)
        def _():
            rows_done(nxt).wait()
        load(i + 1, nxt).start()

    load(i, slot).wait()
    base = i * rows

    for r in range(rows):
        pltpu.make_async_copy(buf.at[slot, pl.ds(r, 1)], dst_ref.at[pl.ds(idx_ref[base + r], 1)],
                              out_sem.at[slot]).start()

    @pl.when(i == n_steps - 1)
    def _():
        for k in range(min(ROW_SLOTS, n_steps)):
            rows_done((n_steps - 1 - k) % ROW_SLOTS).wait()


def _row_gather_kernel(idx_ref, src_ref, dst_ref, buf, row_sem, out_sem, *, rows, n_steps):
    def rows_done(slot):
        return pltpu.make_async_copy(src_ref.at[pl.ds(0, rows)], buf.at[slot], row_sem.at[slot])

    def store(i, slot):
        return pltpu.make_async_copy(buf.at[slot], dst_ref.at[pl.ds(pl.multiple_of(i * rows, rows), rows)],
                                     out_sem.at[slot])

    i = pl.program_id(0)
    slot = i % ROW_SLOTS

    @pl.when(i >= ROW_SLOTS)
    def _():
        store(i - ROW_SLOTS, slot).wait()

    base = i * rows

    for r in range(rows):
        pltpu.make_async_copy(src_ref.at[pl.ds(idx_ref[base + r], 1)], buf.at[slot, pl.ds(r, 1)],
                              row_sem.at[slot]).start()

    @pl.when(i >= 1)
    def _():
        prev = (i + ROW_SLOTS - 1) % ROW_SLOTS
        rows_done(prev).wait()
        store(i - 1, prev).start()

    @pl.when(i == n_steps - 1)
    def _():
        last = (n_steps - 1) % ROW_SLOTS
        rows_done(last).wait()
        store(n_steps - 1, last).start()
        for k in range(min(ROW_SLOTS, n_steps)):
            store(n_steps - 1 - k, (n_steps - 1 - k) % ROW_SLOTS).wait()


def row_move(idx, src, dst_rows, *, scatter, dst_init=None):
    t = idx.shape[0]
    width = src.shape[1]
    rows = min(512, t)
    any_spec = pl.BlockSpec(memory_space=pl.ANY)
    operands = [idx, src] + ([dst_init] if scatter else [])
    grid_spec = pltpu.PrefetchScalarGridSpec(
        num_scalar_prefetch=1, grid=(t // rows,), in_specs=[any_spec] * (len(operands) - 1), out_specs=any_spec,
        scratch_shapes=[pltpu.VMEM((ROW_SLOTS, rows, width), src.dtype),
                        pltpu.SemaphoreType.DMA((ROW_SLOTS,)), pltpu.SemaphoreType.DMA((ROW_SLOTS,))])
    kern = _row_scatter_kernel if scatter else _row_gather_kernel
    return pl.pallas_call(
        functools.partial(kern, rows=rows, n_steps=t // rows),
        grid_spec=grid_spec,
        out_shape=jax.ShapeDtypeStruct((dst_rows, width), src.dtype),
        input_output_aliases={2: 0} if scatter else {},
        compiler_params=_params("arbitrary"),
        name="row_scatter" if scatter else "row_gather",
    )(*operands)


def _moe_kernel(ta_ref, tb_ref, nu_ref, xs_ref, w1a_ref, w3a_ref, w2a_ref, w1b_ref, w3b_ref, w2b_ref,
                g_ref, b_ref, o_ref, pend):
    t = pl.program_id(0)
    d = o_ref.shape[1]
    n_used = nu_ref[0]

    @pl.when(t == 0)
    def _():
        pend[...] = jnp.zeros(pend.shape, F32)

    @pl.when(t < n_used)
    def _():
        o_ref[...] = _layer_norm(pend[...], g_ref[...], b_ref[...])
        x = xs_ref[:, :d]
        ext = xs_ref[:, d:]
        xb = x.astype(BF16)

        def ffn(w1_ref, w3_ref, w2_ref):
            h1 = _dot(xb, w1_ref[0])
            h3 = _dot(xb, w3_ref[0])
            return _dot((h1 * _sigmoid(h1) * h3).astype(BF16), w2_ref[0])

        y = ext[:, 2:3] * ffn(w1a_ref, w3a_ref, w2a_ref) + ext[:, 3:4] * ffn(w1b_ref, w3b_ref, w2b_ref)
        pend[...] = ALPHA * x + y

    @pl.when(t == n_used)
    def _():
        o_ref[...] = _layer_norm(pend[...], g_ref[...], b_ref[...])

    @pl.when(t > n_used)
    def _():
        o_ref[...] = jnp.zeros(o_ref.shape, o_ref.dtype)


def moe_ln(tile_ea, tile_eb, n_used, xs, w1, w3, w2, ln_g, ln_b, tm):
    p, wide = xs.shape
    d = wide - EXT
    de = w1.shape[2]
    n = p // tm

    def w_spec(shape, which):
        return pl.BlockSpec(shape, lambda t, ta, tb, nu: ((ta, tb)[which][jnp.minimum(t, n - 1)], 0, 0))

    grid_spec = pltpu.PrefetchScalarGridSpec(
        num_scalar_prefetch=3,
        grid=(n + 1,),
        in_specs=[pl.BlockSpec((tm, wide), lambda t, ta, tb, nu: (jnp.minimum(t, n - 1), 0)),
                  w_spec((1, d, de), 0), w_spec((1, d, de), 0), w_spec((1, de, d), 0),
                  w_spec((1, d, de), 1), w_spec((1, d, de), 1), w_spec((1, de, d), 1),
                  pl.BlockSpec((1, d), lambda t, ta, tb, nu: (0, 0)),
                  pl.BlockSpec((1, d), lambda t, ta, tb, nu: (0, 0))],
        out_specs=pl.BlockSpec((tm, d), lambda t, ta, tb, nu: (jnp.maximum(t - 1, 0), 0)),
        scratch_shapes=[pltpu.VMEM((tm, d), F32)],
    )
    return pl.pallas_call(
        _moe_kernel,
        grid_spec=grid_spec,
        out_shape=jax.ShapeDtypeStruct((p, d), F32),
        compiler_params=_params("arbitrary"),
        name="moe_ln",
    )(tile_ea, tile_eb, n_used, xs, w1, w3, w2, w1, w3, w2, ln_g, ln_b)


def _moe_few_kernel(x_ref, w1_ref, w3_ref, w2_ref, g_ref, b_ref, o_ref, acc_s):
    e = pl.program_id(0)
    d = o_ref.shape[1]

    @pl.when(e == 0)
    def _():
        acc_s[...] = jnp.zeros(acc_s.shape, F32)

    x = x_ref[:, :d]
    ext = x_ref[:, d:]
    xb = x.astype(BF16)
    h1 = _dot(xb, w1_ref[0])
    h3 = _dot(xb, w3_ref[0])
    y = _dot((h1 * _sigmoid(h1) * h3).astype(BF16), w2_ref[0])
    ef = e.astype(F32)
    gate = jnp.where(ext[:, 4:5] == ef, ext[:, 2:3], 0.0) + jnp.where(ext[:, 5:6] == ef, ext[:, 3:4], 0.0)
    acc_s[...] += gate * y

    @pl.when(e == pl.num_programs(0) - 1)
    def _():
        o_ref[...] = _layer_norm(ALPHA * x + acc_s[...], g_ref[...], b_ref[...])


def moe_ln_few(x1ext, w1, w3, w2, ln_g, ln_b):
    t, wide = x1ext.shape
    d = wide - EXT
    de = w1.shape[2]
    return pl.pallas_call(
        _moe_few_kernel,
        grid=(w1.shape[0],),
        in_specs=[_whole((t, wide)), pl.BlockSpec((1, d, de), lambda e: (e, 0, 0)),
                  pl.BlockSpec((1, d, de), lambda e: (e, 0, 0)), pl.BlockSpec((1, de, d), lambda e: (e, 0, 0)),
                  _whole((1, d)), _whole((1, d))],
        out_specs=_whole((t, d)),
        out_shape=jax.ShapeDtypeStruct((t, d), F32),
        scratch_shapes=[pltpu.VMEM((t, d), F32)],
        compiler_params=_params("arbitrary"),
        name="moe_ln_few",
    )(x1ext, w1, w3, w2, ln_g, ln_b)


def _bucket_experts():
    ea, eb = [], []
    for g in range(N_GROUPS):
        for ja in range(EXP_PER_GROUP):
            for jb in range(ja + 1, EXP_PER_GROUP):
                ea.append(g * EXP_PER_GROUP + ja)
                eb.append(g * EXP_PER_GROUP + jb)
    return np.asarray(ea, np.int32), np.asarray(eb, np.int32)


def _bucket_layout(x1ext, counts, tm):
    t, wide = x1ext.shape
    d = wide - EXT
    bucket = x1ext[:, d].astype(jnp.int32)
    rank = x1ext[:, d + 1].astype(jnp.int32)
    counts = counts[0, :N_BUCKETS].astype(jnp.int32)
    padded = ((counts + tm - 1) // tm) * tm
    row_end = jnp.cumsum(padded)
    row_start = row_end - padded
    ids = jnp.arange(N_BUCKETS, dtype=jnp.int32)
    pos = rank + jnp.sum(jnp.where(bucket[:, None] == ids[None, :], row_start[None, :], 0), axis=1)
    n_rows = _round_up(t, tm) + N_BUCKETS * tm
    n_tiles = n_rows // tm
    n_used = (row_end[-1] // tm).astype(jnp.int32)
    tile_row = jnp.minimum(jnp.arange(n_tiles, dtype=jnp.int32), n_used - 1) * tm
    tile_bucket = jnp.sum((row_end[None, :] <= tile_row[:, None]).astype(jnp.int32), axis=1)
    ea, eb = _bucket_experts()
    in_bucket = tile_bucket[:, None] == ids[None, :]
    tile_ea = jnp.sum(jnp.where(in_bucket, jnp.asarray(ea)[None, :], 0), axis=1)
    tile_eb = jnp.sum(jnp.where(in_bucket, jnp.asarray(eb)[None, :], 0), axis=1)
    return pos.astype(jnp.int32), tile_ea, tile_eb, n_used.reshape(1), n_rows


def hier_moe_ln(x1ext, counts, moe_w, ln_g, ln_b, sorted_buf=None):
    w1, w3, w2 = moe_w
    t, wide = x1ext.shape
    if t <= FEW_TOKENS:
        return moe_ln_few(x1ext, w1, w3, w2, ln_g, ln_b), None
    tm = 256
    pos, tile_ea, tile_eb, n_used, n_rows = _bucket_layout(x1ext, counts, tm)
    if sorted_buf is None:
        sorted_buf = jnp.zeros((n_rows, wide), F32)
    xs = row_move(pos, x1ext, n_rows, scatter=True, dst_init=sorted_buf)
    ys = moe_ln(tile_ea, tile_eb, n_used, xs, w1, w3, w2, ln_g, ln_b, tm)
    return row_move(pos, ys, t, scatter=False), xs


def _rope_tables(s, offset, tm):
    half = ROPE // 2
    inv_freq = ROPE_BASE ** (-jnp.arange(half, dtype=F32) / half)
    ang = (offset + jnp.arange(s)).astype(F32)[:, None] * inv_freq[None, :]
    cos, sin = jnp.cos(ang), jnp.sin(ang)
    cc = jnp.concatenate([cos, cos, cos, cos], -1)
    ss = jnp.concatenate([-sin, sin, -sin, sin], -1)
    if tm > s:
        cc, ss = jnp.tile(cc, (tm // s, 1)), jnp.tile(ss, (tm // s, 1))
    return cc, ss


def _round_up(n, m):
    return (n + m - 1) // m * m


def ab_layer(x, b, s, wts, route_w, cache):
    w_big, w_gate, b_gate, g_ml, w_out, ln_g, ln_b = wts
    t = b * s
    ls_cols = np.array([1.0] * H_FOX + [0.0] * H_ML + [1.0] * H_ML)
    fq, mq, mk, mv, mo, fk_bf, fv_bf, gates, fk, fv = ab_proj(x, w_big, w_gate, b_gate, ls_cols)
    flf = gates[:, :H_FOX].reshape(b, s, H_FOX)
    ml_rows = jnp.swapaxes(gates[:, H_FOX:H_FOX + 2 * H_ML].reshape(b, s, 2 * H_ML), 1, 2)
    ml_rows = jnp.pad(ml_rows, ((0, 0), (0, 16 - 2 * H_ML), (0, 0)))
    if cache is None:
        past_kv, lf_all = None, flf
        c0 = jnp.zeros((b, H_ML, DH_ML, DH_ML), F32)
        n0 = jnp.zeros((b, H_ML, 1, DH_ML), F32)
        m0 = jnp.zeros((b, H_ML, 1, 1), F32)
        tq, chunk = min(512, s), 256
    else:
        ck, cv, clf, c0, n0, m0 = cache
        past = ck.shape[1]
        past_kv = (ck.reshape(b, past, W_FOX).astype(BF16), cv.reshape(b, past, W_FOX).astype(BF16))
        lf_all = jnp.concatenate([clf, flf], 1)
        n0 = n0.reshape(b, H_ML, 1, DH_ML)
        m0 = m0.reshape(b, H_ML, 1, 1)
        tq, chunk = s, s
    lk = lf_all.shape[1]
    lf_rows = jnp.pad(jnp.swapaxes(lf_all, 1, 2), ((0, 0), (0, 16 - H_FOX), (0, _round_up(lk, 256) - lk)))
    fo = fox_attention(fq.reshape(b, s, W_FOX), fk_bf.reshape(b, s, W_FOX), fv_bf.reshape(b, s, W_FOX), lf_rows,
                       tq=tq, past=past_kv, tk=256 if cache is not None else tq)
    mh, c_new, n_new, m_new = mlstm(mq.reshape(b, s, W_ML), mk.reshape(b, s, W_ML), mv.reshape(b, s, W_ML),
                                    ml_rows, c0, n0, m0, chunk=chunk)
    x1ext, counts = ab_out(fo.reshape(t, W_FOX), mh.reshape(t, W_ML), mo, x, g_ml, w_out, ln_g, ln_b, route_w)
    state = (fk, fv, flf, c_new, n_new.reshape(b, H_ML, DH_ML), m_new.reshape(b, H_ML))
    return x1ext, counts, state


def mla_layer(x, b, s, wts, route_w, cache, prev_c=(), prev_r=()):
    w_down, g_q, g_kv, w_uq, wuk_t, wuv, w_out, ln_g, ln_b = wts
    t = b * s
    tm = _row_tile(t)
    past = 0 if cache is None else cache[0].shape[1]
    cc, ss = _rope_tables(s, past, tm)
    cq, ckv, kr, ckv_bf, kr2_bf = mla_down(x, w_down, g_q, g_kv, cc, ss, s, prev_c, prev_r)
    q_lat, q_rope = mla_uq(cq, w_uq, wuk_t, cc, ss, s)
    kc = ckv_bf.reshape(b, s, KV_RANK)
    kr2 = kr2_bf.reshape(b, s, LANE)
    if cache is None:
        tq, past_k = 256, None
    else:
        c_ckv, c_kr = cache
        c_kr_bf = c_kr.astype(BF16)
        tq, past_k = s, (c_ckv.astype(BF16), jnp.concatenate([c_kr_bf, c_kr_bf], -1))
    o_lat = mla_attention(q_lat.reshape(b, s, -1), q_rope.reshape(b, s, -1), kc, kr2, tq=tq,
                          tk=tq if cache is None else 256, past=past_k)
    x1ext, counts = mla_out(o_lat.reshape(t, -1), wuv, w_out, x, ln_g, ln_b, route_w)
    return x1ext, counts, (ckv, kr)


def _prep_ab_weights(w_in, b_fox_f, b_ml_i, b_ml_f, g_ml, w_out, ln_g, ln_b):
    sizes = (W_FOX, W_FOX, W_FOX, H_FOX, W_ML, W_ML, W_ML, H_ML, H_ML, W_ML)
    idx = np.cumsum(sizes[:-1]).tolist()
    fq, fk, fv, ff, mq, mk, mv, mi, mf, mo = jnp.split(w_in, idx, axis=1)
    w_big = jnp.concatenate([fq * (DH_FOX ** -0.5 * LOG2E), mq, mk, mv, mo, fk, fv], 1).astype(BF16)
    w_gate = jnp.concatenate([ff, mi, mf], 1)
    b_gate = jnp.concatenate([b_fox_f, b_ml_i, b_ml_f])
    return (w_big, w_gate, b_gate, g_ml[None, :], w_out.astype(BF16), ln_g[None, :], ln_b[None, :])


def _prep_mla_weights(w_down, g_q, w_uq, g_kv, w_uk, w_uv, w_out, ln_g, ln_b):
    half = ROPE // 2
    cq_w, ckv_w, kr_w = jnp.split(w_down, [Q_RANK, Q_RANK + KV_RANK], axis=1)
    kr_sw = jnp.concatenate([kr_w[:, half:], kr_w[:, :half]], 1)
    w_down_p = jnp.concatenate([cq_w, ckv_w, kr_w, kr_w, kr_sw, kr_sw], 1).astype(BF16)
    uq = w_uq.reshape(Q_RANK, H_MLA, NOPE + ROPE)
    uq_nope = uq[:, :, :NOPE].reshape(Q_RANK, H_MLA * NOPE)
    uq_rope = uq[:, :, NOPE:]
    uq_rope_sw = jnp.concatenate([uq_rope[..., half:], uq_rope[..., :half]], -1)
    w_uq_p = jnp.concatenate([uq_nope, uq_rope.reshape(Q_RANK, -1), uq_rope_sw.reshape(Q_RANK, -1)], 1).astype(BF16)
    wuk_t = jnp.transpose(w_uk, (1, 2, 0)).astype(BF16)
    wuv = jnp.transpose(w_uv, (1, 0, 2)).astype(BF16)
    return (w_down_p, g_q[None, :], g_kv[None, :], w_uq_p, wuk_t, wuv, w_out.astype(BF16), ln_g[None, :], ln_b[None, :])


def kernel(x_prompt, x_sample, cache_fox_k, cache_fox_v, cache_fox_logf, state_mlstm_c, state_mlstm_n, state_mlstm_m, cache_mla_ckv, cache_mla_krope, w_ab_in, b_fox_f, b_mlstm_i, b_mlstm_f, g_mlstm_norm, w_ab_out, w_mla_down, g_mla_q, w_mla_uq, g_mla_kv, w_mla_uk, w_mla_uv, w_mla_out, ln1_g, ln1_b, ln2_g, ln2_b, w_moe_group, b_moe_group, w_moe_router, b_moe_router, w_exp_gate, w_exp_up, w_exp_down):
    bp, sp, d = x_prompt.shape
    bs, ss_, _ = x_sample.shape
    xp = x_prompt.reshape(bp * sp, d)
    xs = x_sample.reshape(bs * ss_, d)
    ab_p, ab_s, c_p, c_s = [], [], [], []
    n_c = DEPTH // 2
    assert n_c >= 2
    buf_p = buf_s = None
    for l in range(DEPTH):
        j = l // 2
        route_w = (jnp.concatenate([w_moe_group[l], w_moe_router[l]], 1),
                   jnp.concatenate([b_moe_group[l], b_moe_router[l]]))
        if l % 2 == 0:
            wts = _prep_ab_weights(w_ab_in[j], b_fox_f[j], b_mlstm_i[j], b_mlstm_f[j], g_mlstm_norm[j],
                                   w_ab_out[j], ln1_g[l], ln1_b[l])
            xp1, cnt_p, st_p = ab_layer(xp, bp, sp, wts, route_w, None)
            xs1, cnt_s, st_s = ab_layer(xs, bs, ss_, wts, route_w,
                                        (cache_fox_k[j], cache_fox_v[j], cache_fox_logf[j],
                                         state_mlstm_c[j], state_mlstm_n[j], state_mlstm_m[j]))
            ab_p.append(st_p)
            ab_s.append(st_s)
        else:
            wts = _prep_mla_weights(w_mla_down[j], g_mla_q[j], w_mla_uq[j], g_mla_kv[j], w_mla_uk[j], w_mla_uv[j],
                                    w_mla_out[j], ln1_g[l], ln1_b[l])
            last = j == n_c - 1
            prev = lambda sts, i: tuple(st[i] for st in sts) if last else ()
            xp1, cnt_p, st_p = mla_layer(xp, bp, sp, wts, route_w, None, prev(c_p, 0), prev(c_p, 1))
            xs1, cnt_s, st_s = mla_layer(xs, bs, ss_, wts, route_w, (cache_mla_ckv[j], cache_mla_krope[j]),
                                         prev(c_s, 0), prev(c_s, 1))
            c_p.append(st_p)
            c_s.append(st_s)
        moe_w = (w_exp_gate[l].astype(BF16), w_exp_up[l].astype(BF16), w_exp_down[l].astype(BF16))
        xp, buf_p = hier_moe_ln(xp1, cnt_p, moe_w, ln2_g[l][None, :], ln2_b[l][None, :], buf_p)
        xs, buf_s = hier_moe_ln(xs1, cnt_s, moe_w, ln2_g[l][None, :], ln2_b[l][None, :], buf_s)

    def stack(groups, i):
        return jnp.stack([g[i] for g in groups])

    def states(ab, c, b, s):
        kv = [jnp.stack([st[i].reshape(b, s, H_FOX, DH_FOX) for st in ab]) for i in (0, 1)]
        lat = [c[-1][0].reshape(n_c, b, s, KV_RANK), c[-1][1].reshape(n_c, b, s, ROPE)]
        return tuple(kv) + tuple(stack(ab, i) for i in (2, 3, 4, 5)) + tuple(lat)

    return (xp.reshape(bp, sp, d), xs.reshape(bs, ss_, d)) + states(ab_p, c_p, bp, sp) + states(ab_s, c_s, bs, ss_)
```

```python
import functools

import numpy as np
import jax
import jax.numpy as jnp
from jax import lax
from jax.experimental import pallas as pl
from jax.experimental.pallas import tpu as pltpu

F32 = jnp.float32
BF16 = jnp.bfloat16

LANE = 128
BF16_SUBLANES = 16
VMEM_LIMIT_BYTES = 48 * 1024 * 1024

ROW_TILE = 512
FOX_TILE = 512
MLA_TILE = 256
PAST_KV_TILE = 256
MLSTM_CHUNK = 256
MOE_TILE = 256

H_FOX, DH_FOX = 8, 64
W_FOX = H_FOX * DH_FOX
H_ML, DH_ML = 4, 128
W_ML = H_ML * DH_ML
H_MLA, NOPE, ROPE = 8, 128, 64
Q_RANK, KV_RANK = 384, 256
ROPE_BASE = 10000.0
N_GROUPS, EXP_PER_GROUP = 4, 8
DEPTH = 4
ALPHA = (2 * DEPTH) ** 0.25
LN_EPS = 1e-5
RMS_EPS = 1e-6
CHUNK = 64
NEG_BIG = -1e30
LOG2E = 1.4426950408889634


def _params(*sem):
    return pltpu.CompilerParams(dimension_semantics=sem, vmem_limit_bytes=VMEM_LIMIT_BYTES)


def _dot(a, b):
    return jnp.dot(a, b, preferred_element_type=F32)


def _dot_nt(a, b):
    return lax.dot_general(a, b, (((1,), (1,)), ((), ())), preferred_element_type=F32)


def _split3(x):
    hi = x.astype(BF16)
    r = x - hi.astype(F32)
    mid = r.astype(BF16)
    lo = (r - mid.astype(F32)).astype(BF16)
    return hi, mid, lo


def _layer_norm(y, g, b):
    mu = jnp.mean(y, axis=1, keepdims=True)
    yc = y - mu
    var = jnp.mean(yc * yc, axis=1, keepdims=True)
    return yc * lax.rsqrt(var + LN_EPS) * g + b


def _sigmoid(x):
    return 1.0 / (1.0 + jnp.exp(-x))


def _row_tile(t):
    return min(ROW_TILE, t)


def _ab_proj_kernel(x_ref, w_ref, gw_ref, gb_ref, gmask_ref,
                    fq_ref, mq_ref, mk_ref, mv_ref, mo_ref, kbf_ref, vbf_ref, gates_ref, k_ref, v_ref):
    x = x_ref[...]
    xh = x.astype(BF16)
    c0 = 0
    for o_ref in (fq_ref, mq_ref, mk_ref, mv_ref, mo_ref):
        n = o_ref.shape[1]
        o_ref[...] = _dot(xh, w_ref[:, c0:c0 + n]).astype(BF16)
        c0 += n
    for bf_ref, state_ref in ((kbf_ref, k_ref), (vbf_ref, v_ref)):
        z = _dot(xh, w_ref[:, c0:c0 + W_FOX])
        c0 += W_FOX
        bf_ref[...] = z.astype(BF16)
        state_ref[...] = z
    xl = (x - xh.astype(F32)).astype(BF16)
    gg = _dot(xh, gw_ref[...])
    g = gg[:, :LANE] + gg[:, LANE:] + _dot(xl, gw_ref[:, :LANE]) + gb_ref[...]
    log_sig = jnp.minimum(g, 0.0) - jnp.log1p(jnp.exp(-jnp.abs(g)))
    gates_ref[...] = jnp.where(gmask_ref[...] > 0.0, log_sig, g)


def ab_proj(x, w_big, w_gate, b_gate, log_sigmoid_cols):
    t, k = x.shape
    tm = _row_tile(t)
    n_gate = w_gate.shape[1]
    wp = jnp.zeros((k, LANE), F32).at[:, :n_gate].set(w_gate)
    wh = wp.astype(BF16)
    wl = (wp - wh.astype(F32)).astype(BF16)
    bp = jnp.zeros((1, LANE), F32).at[0, :n_gate].set(b_gate)
    mask = jnp.zeros((1, LANE), F32).at[0, :n_gate].set(jnp.asarray(log_sigmoid_cols, F32))
    bf = lambda n: jax.ShapeDtypeStruct((t, n), BF16)
    return pl.pallas_call(
        _ab_proj_kernel,
        grid=(t // tm,),
        in_specs=[_rows(tm, k), _whole(w_big.shape), _whole((k, 2 * LANE)), _whole((1, LANE)), _whole((1, LANE))],
        out_specs=[_rows(tm, W_FOX)] + [_rows(tm, W_ML)] * 4 + [_rows(tm, W_FOX)] * 2 + [_rows(tm, LANE)]
        + [_rows(tm, W_FOX)] * 2,
        out_shape=[bf(W_FOX)] + [bf(W_ML)] * 4 + [bf(W_FOX)] * 2 + [jax.ShapeDtypeStruct((t, LANE), F32)]
        + [jax.ShapeDtypeStruct((t, W_FOX), F32)] * 2,
        compiler_params=_params("parallel"),
        name="ab_proj",
    )(x, w_big, jnp.concatenate([wh, wl], 1), bp, mask)


def _lanes(x, w):
    if w < LANE:
        return x[:, :w]
    return x if w == LANE else jnp.concatenate([x] * (w // LANE), axis=1)


def _softmax_step(s, at, m_s, l_s, acc_s, v):
    w = s.shape[1]
    m_prev = m_s[at]
    m_new = jnp.maximum(m_prev, jnp.max(s, axis=1, keepdims=True))
    p = jnp.exp2(s - _lanes(m_new, w))
    a = jnp.exp2(m_prev - m_new)
    l_s[at] = a * l_s[at] + jnp.sum(p, axis=1, keepdims=True)
    acc_s[at] = acc_s[at] * _lanes(a, acc_s.shape[-1]) + _dot(p.astype(BF16), v)
    m_s[at] = m_new


def _fox_kernel(q_ref, k_ref, v_ref, *rest, tq, tk, q_off, cblk):
    pk_ref, pv_ref = rest[:2] if len(rest) == 9 else (k_ref, v_ref)
    lf_ref, o_ref, ncum_ref, qm_s, m_s, l_s, acc_s = rest[-7:]
    qi = pl.program_id(1)
    sub = cblk // LANE
    n_pairs = H_FOX // 2

    @pl.when(qi == 0)
    def _():
        r = lax.broadcasted_iota(jnp.int32, (cblk, cblk), 0)
        c = lax.broadcasted_iota(jnp.int32, (cblk, cblk), 1)
        upper = jnp.where(r <= c, 1.0, 0.0).astype(BF16)
        carry = jnp.zeros((lf_ref.shape[1], 1), F32)
        for j in range(lf_ref.shape[2] // cblk):
            g1, g2, g3 = _split3(lf_ref[0, :, j * cblk:(j + 1) * cblk])
            cum = _dot(g1, upper) + _dot(g2, upper) + _dot(g3, upper) + carry
            carry = cum[:, cblk - 1:cblk]
            for h in range(H_FOX):
                for u in range(sub):
                    ncum_ref[h, j * sub + u] = cum[h:h + 1, u * LANE:(u + 1) * LANE] * (-LOG2E)

    lane = lax.broadcasted_iota(jnp.int32, (tq, LANE), 1)
    for hp in range(n_pairs):
        q = q_ref[0, :, hp * LANE:(hp + 1) * LANE]
        zero = jnp.zeros_like(q)
        qm_s[hp, 0:tq, :] = jnp.where(lane < DH_FOX, q, zero)
        qm_s[hp, tq:2 * tq, :] = jnp.where(lane >= DH_FOX, q, zero)
    m_s[...] = jnp.full(m_s.shape, NEG_BIG, F32)
    l_s[...] = jnp.zeros(l_s.shape, F32)
    acc_s[...] = jnp.zeros(acc_s.shape, F32)
    q_start = q_off + qi * tq

    def update(k_src, v_src, off, w, bias_of_head, mask):
        for hp in range(n_pairs):
            kt = k_src[0, pl.ds(off, w), hp * LANE:(hp + 1) * LANE]
            vt = v_src[0, pl.ds(off, w), hp * LANE:(hp + 1) * LANE]
            s = _dot_nt(qm_s[hp], kt)
            s = jnp.concatenate([s[:tq] + bias_of_head(2 * hp), s[tq:] + bias_of_head(2 * hp + 1)], axis=0)
            if mask is not None:
                s = jnp.where(mask, s, -jnp.inf)
            _softmax_step(s, hp, m_s, l_s, acc_s, vt)

    def full_tile(j, carry):
        def bias(h):
            return jnp.concatenate([ncum_ref[h, j * (tk // LANE) + u] for u in range(tk // LANE)], axis=1)
        update(pk_ref, pv_ref, pl.multiple_of(j * tk, tk), tk, bias, None)
        return carry

    lax.fori_loop(0, q_start // tk, full_tile, 0)

    def diag_bias(h):
        if tq >= LANE:
            return jnp.concatenate([ncum_ref[h, q_start // LANE + u] for u in range(tq // LANE)], axis=1)
        lo = q_off % LANE
        return ncum_ref[h, q_start // LANE][:, lo:lo + tq]

    r = lax.broadcasted_iota(jnp.int32, (2 * tq, tq), 0)
    c = lax.broadcasted_iota(jnp.int32, (2 * tq, tq), 1)
    update(k_ref, v_ref, pl.multiple_of(qi * tq, tq), tq, diag_bias, c <= jnp.where(r >= tq, r - tq, r))
    for hp in range(n_pairs):
        o = acc_s[hp] / l_s[hp]
        o_ref[0, :, hp * LANE:(hp + 1) * LANE] = jnp.where(lane < DH_FOX, o[:tq], o[tq:]).astype(o_ref.dtype)


def fox_attention(q, k, v, logf_rows, *, tq, tk, past=None, cblk=PAST_KV_TILE):
    b, lq, _ = q.shape
    lkp = logf_rows.shape[2]
    q_off = 0 if past is None else past[0].shape[1]
    nq = lq // tq
    assert k.shape[1] == lq and q_off % tk == 0 and (tq % tk == 0 or lq == tq) and lkp % cblk == 0
    assert tq >= LANE or (nq == 1 and q_off % LANE + tq <= LANE)
    kern = functools.partial(_fox_kernel, tq=tq, tk=tk, q_off=q_off, cblk=cblk)
    whole_seq = [pl.BlockSpec((1, a.shape[1], W_FOX), lambda bi, qi: (bi, 0, 0)) for a in (k, v) + tuple(past or ())]
    return pl.pallas_call(
        kern,
        grid=(b, nq),
        in_specs=[pl.BlockSpec((1, tq, W_FOX), lambda bi, qi: (bi, qi, 0))] + whole_seq + [
            pl.BlockSpec((1, logf_rows.shape[1], lkp), lambda bi, qi: (bi, 0, 0)),
        ],
        out_specs=pl.BlockSpec((1, tq, W_FOX), lambda bi, qi: (bi, qi, 0)),
        out_shape=jax.ShapeDtypeStruct((b, lq, W_FOX), BF16),
        scratch_shapes=[pltpu.VMEM((H_FOX, lkp // LANE, 1, LANE), F32), pltpu.VMEM((H_FOX // 2, 2 * tq, LANE), BF16),
                        pltpu.VMEM((H_FOX // 2, 2 * tq, LANE), F32), pltpu.VMEM((H_FOX // 2, 2 * tq, LANE), F32),
                        pltpu.VMEM((H_FOX // 2, 2 * tq, LANE), F32)],
        compiler_params=_params("parallel", "arbitrary"),
        name="fox_attention",
    )(q, k, v, *(past or ()), logf_rows)


def _mlstm_kernel(q_ref, k_ref, v_ref, g_ref, c0_ref, n0_ref, m0_ref,
                  h_ref, c_ref, n_ref, m_ref, c_s, n_s, m_s, *, chunk):
    ci = pl.program_id(1)
    L = chunk
    scale = DH_ML ** -0.5

    @pl.when(ci == 0)
    def _():
        c_s[...] = c0_ref[0]
        n_s[...] = n0_ref[0]
        m_s[...] = m0_ref[0]

    r = lax.broadcasted_iota(jnp.int32, (L, L), 0)
    c = lax.broadcasted_iota(jnp.int32, (L, L), 1)
    eye = r == c
    causal = c <= r
    upper = jnp.where(r <= c, 1.0, 0.0).astype(BF16)
    g = g_ref[0]
    g1, g2, g3 = _split3(g)
    cum = _dot(g1, upper) + _dot(g2, upper) + _dot(g3, upper)

    def to_col(row):
        return jnp.sum(jnp.where(eye, row, 0.0), axis=1, keepdims=True)

    for h in range(H_ML):
        sl = slice(h * DH_ML, (h + 1) * DH_ML)
        qh, kh, vh = q_ref[0, :, sl], k_ref[0, :, sl], v_ref[0, :, sl]
        ig_row = g[h:h + 1, :]
        bh_row = cum[H_ML + h:H_ML + h + 1, :]
        bh_col, ig_col = to_col(bh_row), to_col(ig_row)
        m0 = m_s[h]
        c0 = c_s[h]
        n0 = n_s[h]
        logd = jnp.where(causal, bh_col - bh_row + ig_row, -jnp.inf)
        inter = bh_col + m0
        m_col = jnp.maximum(inter, jnp.max(logd, axis=1, keepdims=True))
        d = jnp.exp(logd - m_col)
        a_col = jnp.exp(inter - m_col)
        s = _dot_nt(qh, kh) * scale * d
        num = _dot(s.astype(BF16), vh) + _dot_nt(qh, c0.astype(BF16)) * a_col
        qf = qh.astype(F32)
        den = jnp.sum(s, axis=1, keepdims=True) + a_col * jnp.sum(qf * n0, axis=1, keepdims=True)
        den = jnp.maximum(jnp.abs(den), jnp.exp(-m_col))
        h_ref[0, :, sl] = (num / den).astype(h_ref.dtype)

        m_l = m_col[L - 1:L, :]
        bh_l = bh_row[:, L - 1:L]
        a_l = jnp.exp(bh_l + m0 - m_l)
        w_col = jnp.exp(bh_l - bh_col + ig_col - m_l)
        vw_t = jnp.transpose(vh.astype(F32) * w_col).astype(BF16)
        c_s[h] = a_l * c0 + _dot(vw_t, kh) * scale
        n_s[h] = a_l * n0 + jnp.sum(kh.astype(F32) * w_col, axis=0, keepdims=True) * scale
        m_s[h] = m_l

    @pl.when(ci == pl.num_programs(1) - 1)
    def _():
        c_ref[0] = c_s[...]
        n_ref[0] = n_s[...]
        m_ref[0] = m_s[...]


def mlstm(q, k, v, gate_rows, c0, n0, m0, *, chunk):
    b, s, _ = q.shape
    nc = s // chunk
    seq = pl.BlockSpec((1, chunk, W_ML), lambda bi, ci: (bi, ci, 0))
    c_spec = pl.BlockSpec((1, H_ML, DH_ML, DH_ML), lambda bi, ci: (bi, 0, 0, 0))
    n_spec = pl.BlockSpec((1, H_ML, 1, DH_ML), lambda bi, ci: (bi, 0, 0, 0))
    m_spec = pl.BlockSpec((1, H_ML, 1, 1), lambda bi, ci: (bi, 0, 0, 0))
    return pl.pallas_call(
        functools.partial(_mlstm_kernel, chunk=chunk),
        grid=(b, nc),
        in_specs=[seq, seq, seq, pl.BlockSpec((1, gate_rows.shape[1], chunk), lambda bi, ci: (bi, 0, ci)),
                  c_spec, n_spec, m_spec],
        out_specs=[seq, c_spec, n_spec, m_spec],
        out_shape=[jax.ShapeDtypeStruct((b, s, W_ML), BF16),
                   jax.ShapeDtypeStruct(c0.shape, F32), jax.ShapeDtypeStruct(n0.shape, F32),
                   jax.ShapeDtypeStruct(m0.shape, F32)],
        scratch_shapes=[pltpu.VMEM((H_ML, DH_ML, DH_ML), F32), pltpu.VMEM((H_ML, 1, DH_ML), F32),
                        pltpu.VMEM((H_ML, 1, 1), F32)],
        compiler_params=_params("parallel", "arbitrary"),
        name="mlstm",
    )(q, k, v, gate_rows, c0, n0, m0)


N_PAIRS = EXP_PER_GROUP * (EXP_PER_GROUP - 1) // 2
N_BUCKETS = N_GROUPS * N_PAIRS
assert N_BUCKETS <= LANE
EXT = LANE
FEW_TOKENS = 1024


def _route_store(out, wr_ref, br_ref, ltri_ref, o_ref, cnt_ref, cnt_s):
    tm, d = out.shape

    @pl.when(pl.program_id(0) == 0)
    def _():
        cnt_s[...] = jnp.zeros(cnt_s.shape, F32)

    xh = out.astype(BF16)
    xl = (out - xh.astype(F32)).astype(BF16)
    zz = _dot(xh, wr_ref[...])
    z = zz[:, :LANE] + zz[:, LANE:] + _dot(xl, wr_ref[:, :LANE]) + br_ref[...]
    lane = lax.broadcasted_iota(jnp.int32, (tm, LANE), 1).astype(F32)
    far = float(LANE)
    neg = -jnp.inf
    gl = jnp.where(lane < N_GROUPS, z, neg)
    gmax = jnp.max(gl, axis=1, keepdims=True)
    g_idx = jnp.min(jnp.where(gl == gmax, lane, far), axis=1, keepdims=True)
    g_gate = 1.0 / jnp.sum(jnp.exp(gl - gmax), axis=1, keepdims=True)
    lo = N_GROUPS + EXP_PER_GROUP * g_idx
    el = jnp.where(jnp.logical_and(lane >= lo, lane < lo + EXP_PER_GROUP), z, neg)
    v1 = jnp.max(el, axis=1, keepdims=True)
    i1 = jnp.min(jnp.where(el == v1, lane, far), axis=1, keepdims=True)
    el2 = jnp.where(lane == i1, neg, el)
    v2 = jnp.max(el2, axis=1, keepdims=True)
    i2 = jnp.min(jnp.where(el2 == v2, lane, far), axis=1, keepdims=True)
    e21 = jnp.exp(v2 - v1)
    w1 = g_gate / (1.0 + e21)
    w2 = w1 * e21
    j1, j2 = i1 - lo, i2 - lo
    ja, jb = jnp.minimum(j1, j2), jnp.maximum(j1, j2)
    pair = ja * (2 * EXP_PER_GROUP - 1 - ja) * 0.5 + (jb - ja - 1.0)
    bucket = g_idx * N_PAIRS + pair
    first_low = j1 < j2
    wa = jnp.where(first_low, w1, w2)
    wb = jnp.where(first_low, w2, w1)
    onehot = jnp.where(lane == bucket, 1.0, 0.0)
    earlier = _dot(ltri_ref[...], onehot.astype(BF16))
    cnt = cnt_s[...]
    rank = jnp.sum(onehot * (earlier + cnt), axis=1, keepdims=True)
    cnt_new = cnt + jnp.sum(onehot, axis=0, keepdims=True)
    cnt_s[...] = cnt_new
    cnt_ref[...] = cnt_new
    o_ref[:, :d] = out
    e_lo = lo - N_GROUPS + ja
    e_hi = lo - N_GROUPS + jb
    record = (bucket, rank, wa, wb, e_lo, e_hi)
    ext = jnp.zeros((tm, LANE), F32)
    for k, col in enumerate(record):
        ext = jnp.where(lane == float(k), col, ext)
    o_ref[:, d:] = ext


def _route_operands(w_route, b_route, tm):
    n = w_route.shape[1]
    wp = jnp.zeros((w_route.shape[0], LANE), F32).at[:, :n].set(w_route)
    wh = wp.astype(BF16)
    wl = (wp - wh.astype(F32)).astype(BF16)
    bp = jnp.zeros((1, LANE), F32).at[0, :n].set(b_route)
    ltri = jnp.asarray(np.tril(np.ones((tm, tm), np.float32), -1), BF16)
    return jnp.concatenate([wh, wl], 1), bp, ltri


def _route_specs(k, tm):
    return [_whole((k, 2 * LANE)), _whole((1, LANE)), _whole((tm, tm))]


def _ab_out_kernel(fo_ref, mh_ref, mo_ref, x_ref, gml_ref, w_ref, g_ref, b_ref, wr_ref, br_ref, ltri_ref,
                   o_ref, cnt_ref, cnt_s):
    parts = [fo_ref[...]]
    for h in range(H_ML):
        sl = slice(h * DH_ML, (h + 1) * DH_ML)
        mh = mh_ref[:, sl].astype(F32)
        ms = jnp.mean(mh * mh, axis=1, keepdims=True)
        nm = mh * lax.rsqrt(ms + RMS_EPS) * gml_ref[:, sl]
        parts.append((nm * _sigmoid(mo_ref[:, sl].astype(F32))).astype(BF16))
    cat = jnp.concatenate(parts, axis=1)
    y = ALPHA * x_ref[...] + _dot(cat, w_ref[...])
    out = _layer_norm(y, g_ref[...], b_ref[...])
    _route_store(out, wr_ref, br_ref, ltri_ref, o_ref, cnt_ref, cnt_s)


def _rows(tm, n):
    return pl.BlockSpec((tm, n), lambda i: (i, 0))


def _whole(shape):
    return pl.BlockSpec(shape, lambda i: (0,) * len(shape))


def _routed_out_call(kern, name, t, d, tm, in_specs, operands, route_w):
    w_route, b_route = route_w
    return pl.pallas_call(
        kern,
        grid=(t // tm,),
        in_specs=in_specs + _route_specs(d, tm),
        out_specs=[_rows(tm, d + EXT), _whole((1, LANE))],
        out_shape=[jax.ShapeDtypeStruct((t, d + EXT), F32), jax.ShapeDtypeStruct((1, LANE), F32)],
        scratch_shapes=[pltpu.VMEM((1, LANE), F32)],
        compiler_params=_params("arbitrary"),
        name=name,
    )(*operands, *_route_operands(w_route, b_route, tm))


def ab_out(fo, mh, mo, x, g_ml, w_out, ln_g, ln_b, route_w):
    t, d = x.shape
    tm = _row_tile(t)
    in_specs = [_rows(tm, W_FOX), _rows(tm, W_ML), _rows(tm, W_ML), _rows(tm, d), _whole((1, W_ML)),
                _whole(w_out.shape), _whole((1, d)), _whole((1, d))]
    return _routed_out_call(_ab_out_kernel, "ab_out", t, d, tm, in_specs,
                            (fo, mh, mo, x, g_ml, w_out, ln_g, ln_b), route_w)


def _rms(z, g):
    return z * lax.rsqrt(jnp.mean(z * z, axis=1, keepdims=True) + RMS_EPS) * g


def _mla_down_kernel(x_ref, w_ref, gq_ref, gkv_ref, cc_ref, ss_ref, *rest, n_prev):
    prev_c, prev_r = rest[:n_prev], rest[n_prev:2 * n_prev]
    cq_ref, ckv_ref, kr_ref, ckvb_ref, kr2_ref = rest[2 * n_prev:]

    def put(state_ref, prev, value):
        if n_prev:
            for j, p in enumerate(prev):
                state_ref[j] = p[...]
            state_ref[n_prev] = value
        else:
            state_ref[...] = value

    xb = x_ref[...].astype(BF16)
    cq_ref[...] = _rms(_dot(xb, w_ref[:, 0:Q_RANK]), gq_ref[...]).astype(BF16)
    ckv = _rms(_dot(xb, w_ref[:, Q_RANK:Q_RANK + KV_RANK]), gkv_ref[...])
    put(ckv_ref, prev_c, ckv)
    ckvb_ref[...] = ckv.astype(BF16)
    c0 = Q_RANK + KV_RANK
    rope = (_dot(xb, w_ref[:, c0:c0 + LANE]) * cc_ref[...] + _dot(xb, w_ref[:, c0 + LANE:c0 + 2 * LANE]) * ss_ref[...])
    put(kr_ref, prev_r, rope[:, :ROPE])
    kr2_ref[...] = rope.astype(BF16)


def _table_spec(table, tm, s):
    if table.shape[0] == tm:
        return pl.BlockSpec((tm, LANE), lambda i: (0, 0))
    per = s // tm
    return pl.BlockSpec((tm, LANE), lambda i: (i % per, 0))


def mla_down(x, w, g_q, g_kv, cc, ss, s, prev_c=(), prev_r=()):
    t, d = x.shape
    tm = _row_tile(t)
    tab = _table_spec(cc, tm, s)
    n_prev = len(prev_c)

    def state(width):
        if n_prev:
            return (pl.BlockSpec((n_prev + 1, tm, width), lambda i: (0, i, 0)),
                    jax.ShapeDtypeStruct((n_prev + 1, t, width), F32))
        return _rows(tm, width), jax.ShapeDtypeStruct((t, width), F32)

    (c_spec, c_shape), (r_spec, r_shape) = state(KV_RANK), state(ROPE)
    return pl.pallas_call(
        functools.partial(_mla_down_kernel, n_prev=n_prev),
        grid=(t // tm,),
        in_specs=[_rows(tm, d), _whole(w.shape), _whole((1, Q_RANK)), _whole((1, KV_RANK)), tab, tab]
        + [_rows(tm, KV_RANK)] * n_prev + [_rows(tm, ROPE)] * n_prev,
        out_specs=[_rows(tm, Q_RANK), c_spec, r_spec, _rows(tm, KV_RANK), _rows(tm, LANE)],
        out_shape=[jax.ShapeDtypeStruct((t, Q_RANK), BF16), c_shape, r_shape,
                   jax.ShapeDtypeStruct((t, KV_RANK), BF16), jax.ShapeDtypeStruct((t, LANE), BF16)],
        compiler_params=_params("parallel"),
        name="mla_down",
    )(x, w, g_q, g_kv, cc, ss, *prev_c, *prev_r)


def _mla_uq_kernel(cq_ref, w_ref, wuk_ref, cc_ref, ss_ref, qlat_ref, qrope_ref):
    cq = cq_ref[...]
    scale = (NOPE + ROPE) ** -0.5 * LOG2E
    r0 = H_MLA * NOPE
    r1 = r0 + H_MLA * ROPE
    qn = _dot(cq, w_ref[:, :r0]).astype(BF16)
    for h in range(H_MLA):
        qlat_ref[:, h * KV_RANK:(h + 1) * KV_RANK] = (
            _dot(qn[:, h * NOPE:(h + 1) * NOPE], wuk_ref[h]) * scale).astype(BF16)
    qr = _dot(cq, w_ref[:, r0:])
    cc, ss = cc_ref[...] * scale, ss_ref[...] * scale
    for p in range(H_MLA // 2):
        a, b = p * LANE, (r1 - r0) + p * LANE
        qrope_ref[:, p * LANE:(p + 1) * LANE] = (qr[:, a:a + LANE] * cc + qr[:, b:b + LANE] * ss).astype(BF16)


def mla_uq(cq, w, wuk_t, cc, ss, s):
    t = cq.shape[0]
    tm = _row_tile(t)
    tab = _table_spec(cc, tm, s)
    return pl.pallas_call(
        _mla_uq_kernel,
        grid=(t // tm,),
        in_specs=[_rows(tm, Q_RANK), _whole(w.shape), _whole(wuk_t.shape), tab, tab],
        out_specs=[_rows(tm, H_MLA * KV_RANK), _rows(tm, H_MLA * ROPE)],
        out_shape=[jax.ShapeDtypeStruct((t, H_MLA * KV_RANK), BF16), jax.ShapeDtypeStruct((t, H_MLA * ROPE), BF16)],
        compiler_params=_params("parallel"),
        name="mla_uq",
    )(cq, w, wuk_t, cc, ss)


def _mla_attn_kernel(ql_ref, qr_ref, kc_ref, kr_ref, *rest, tq, tk, q_off):
    pc_ref, pr_ref = rest[:2] if len(rest) == 8 else (kc_ref, kr_ref)
    o_ref, ql_s, qr_s, m_s, l_s, acc_s = rest[-6:]
    qi = pl.program_id(1)
    lane = lax.broadcasted_iota(jnp.int32, (tq, LANE), 1)
    for h in range(H_MLA):
        rows = slice(h * tq, (h + 1) * tq)
        ql_s[rows, :] = ql_ref[0, :, h * KV_RANK:(h + 1) * KV_RANK]
        qr = qr_ref[0, :, (h // 2) * LANE:(h // 2 + 1) * LANE]
        keep = (lane < ROPE) if h % 2 == 0 else (lane >= ROPE)
        qr_s[rows, :] = jnp.where(keep, qr, jnp.zeros_like(qr))
    m_s[...] = jnp.full(m_s.shape, NEG_BIG, F32)
    l_s[...] = jnp.zeros(l_s.shape, F32)
    acc_s[...] = jnp.zeros(acc_s.shape, F32)
    q_start = q_off + qi * tq

    def update(c_src, r_src, off, w, mask):
        kc = c_src[0, pl.ds(off, w), :]
        kr = r_src[0, pl.ds(off, w), :]
        s = _dot_nt(ql_s[...], kc) + _dot_nt(qr_s[...], kr)
        if mask is not None:
            s = jnp.where(mask, s, -jnp.inf)
        _softmax_step(s, Ellipsis, m_s, l_s, acc_s, kc)

    def full_tile(j, carry):
        update(pc_ref, pr_ref, pl.multiple_of(j * tk, tk), tk, None)
        return carry

    lax.fori_loop(0, q_start // tk, full_tile, 0)
    r = lax.broadcasted_iota(jnp.int32, (H_MLA * tq, tq), 0)
    c = lax.broadcasted_iota(jnp.int32, (H_MLA * tq, tq), 1)
    update(kc_ref, kr_ref, pl.multiple_of(qi * tq, tq), tq, (c // CHUNK) <= ((r & (tq - 1)) // CHUNK))
    for h in range(H_MLA):
        rows = slice(h * tq, (h + 1) * tq)
        o = acc_s[rows, :] / _lanes(l_s[rows, :], KV_RANK)
        o_ref[0, :, h * KV_RANK:(h + 1) * KV_RANK] = o.astype(o_ref.dtype)


def _mla_attn_pipelined_kernel(ql_ref, qr_ref, kc_ref, kr_ref, o_ref, ql_s, qr_s, m_s, l_s, acc_s, s_buf, *, tq):
    qi = pl.program_id(1)
    lane = lax.broadcasted_iota(jnp.int32, (tq, LANE), 1)
    for h in range(H_MLA):
        rows = slice(h * tq, (h + 1) * tq)
        ql_s[rows, :] = ql_ref[0, :, h * KV_RANK:(h + 1) * KV_RANK]
        qr = qr_ref[0, :, (h // 2) * LANE:(h // 2 + 1) * LANE]
        keep = (lane < ROPE) if h % 2 == 0 else (lane >= ROPE)
        qr_s[rows, :] = jnp.where(keep, qr, jnp.zeros_like(qr))
    m_s[...] = jnp.full(m_s.shape, NEG_BIG, F32)
    l_s[...] = jnp.zeros(l_s.shape, F32)
    acc_s[...] = jnp.zeros(acc_s.shape, F32)

    def keys(ref, j):
        return ref[0, pl.ds(pl.multiple_of(j * tq, tq), tq), :]

    def score(j, slot):
        s_buf[slot] = _dot_nt(ql_s[...], keys(kc_ref, j)) + _dot_nt(qr_s[...], keys(kr_ref, j))

    def consume(j, slot, mask):
        s = s_buf[slot]
        if mask is not None:
            s = jnp.where(mask, s, -jnp.inf)
        _softmax_step(s, Ellipsis, m_s, l_s, acc_s, keys(kc_ref, j))

    score(0, 0)

    def two_tiles(ip, carry):
        j = 2 * ip
        score(j + 1, 1)
        consume(j, 0, None)
        score(j + 2, 0)
        consume(j + 1, 1, None)
        return carry

    lax.fori_loop(0, qi // 2, two_tiles, 0)
    r = lax.broadcasted_iota(jnp.int32, (H_MLA * tq, tq), 0)
    c = lax.broadcasted_iota(jnp.int32, (H_MLA * tq, tq), 1)
    mask = (c // CHUNK) <= ((r & (tq - 1)) // CHUNK)

    @pl.when(qi % 2 == 1)
    def _():
        score(qi, 1)
        consume(qi - 1, 0, None)
        consume(qi, 1, mask)

    @pl.when(qi % 2 == 0)
    def _():
        consume(qi, 0, mask)

    for h in range(H_MLA):
        rows = slice(h * tq, (h + 1) * tq)
        o = acc_s[rows, :] / _lanes(l_s[rows, :], KV_RANK)
        o_ref[0, :, h * KV_RANK:(h + 1) * KV_RANK] = o.astype(o_ref.dtype)


def mla_attention(q_lat, q_rope, kc, kr2, *, tq, tk, past=None):
    b, lq, _ = q_lat.shape
    q_off = 0 if past is None else past[0].shape[1]
    assert kc.shape[1] == lq and q_off % tk == 0 and (tq % tk == 0 or lq == tq)
    assert tq % CHUNK == 0 and (tq & (tq - 1)) == 0
    scratch = [pltpu.VMEM((H_MLA * tq, KV_RANK), BF16), pltpu.VMEM((H_MLA * tq, LANE), BF16),
               pltpu.VMEM((H_MLA * tq, LANE), F32), pltpu.VMEM((H_MLA * tq, LANE), F32),
               pltpu.VMEM((H_MLA * tq, KV_RANK), F32)]
    if past is None and tk == tq:
        kern = functools.partial(_mla_attn_pipelined_kernel, tq=tq)
        scratch.append(pltpu.VMEM((2, H_MLA * tq, tq), F32))
    else:
        kern = functools.partial(_mla_attn_kernel, tq=tq, tk=tk, q_off=q_off)
    whole_seq = [pl.BlockSpec((1,) + a.shape[1:], lambda bi, qi: (bi, 0, 0)) for a in (kc, kr2) + tuple(past or ())]
    return pl.pallas_call(
        kern,
        grid=(b, lq // tq),
        in_specs=[
            pl.BlockSpec((1, tq, H_MLA * KV_RANK), lambda bi, qi: (bi, qi, 0)),
            pl.BlockSpec((1, tq, H_MLA * ROPE), lambda bi, qi: (bi, qi, 0)),
        ] + whole_seq,
        out_specs=pl.BlockSpec((1, tq, H_MLA * KV_RANK), lambda bi, qi: (bi, qi, 0)),
        out_shape=jax.ShapeDtypeStruct((b, lq, H_MLA * KV_RANK), BF16),
        scratch_shapes=scratch,
        compiler_params=_params("parallel", "arbitrary"),
        name="mla_attention",
    )(q_lat, q_rope, kc, kr2, *(past or ()))


def _mla_out_kernel(ol_ref, wuv_ref, w_ref, x_ref, g_ref, b_ref, wr_ref, br_ref, ltri_ref,
                    o_ref, cnt_ref, cnt_s):
    parts = [_dot(ol_ref[:, h * KV_RANK:(h + 1) * KV_RANK], wuv_ref[h]).astype(BF16) for h in range(H_MLA)]
    y = ALPHA * x_ref[...] + _dot(jnp.concatenate(parts, axis=1), w_ref[...])
    out = _layer_norm(y, g_ref[...], b_ref[...])
    _route_store(out, wr_ref, br_ref, ltri_ref, o_ref, cnt_ref, cnt_s)


def mla_out(o_lat, wuv, w_out, x, ln_g, ln_b, route_w):
    t, d = x.shape
    tm = _row_tile(t)
    in_specs = [_rows(tm, H_MLA * KV_RANK), _whole(wuv.shape), _whole(w_out.shape), _rows(tm, d),
                _whole((1, d)), _whole((1, d))]
    return _routed_out_call(_mla_out_kernel, "mla_out", t, d, tm, in_specs,
                            (o_lat, wuv, w_out, x, ln_g, ln_b), route_w)


ROW_SLOTS = 3


def _row_scatter_kernel(idx_ref, src_ref, dst_init_ref, dst_ref, buf, in_sem, out_sem, *, rows, n_steps):
    del dst_init_ref

    def load(i, slot):
        return pltpu.make_async_copy(src_ref.at[pl.ds(pl.multiple_of(i * rows, rows), rows)], buf.at[slot],
                                     in_sem.at[slot])

    def rows_done(slot):
        return pltpu.make_async_copy(buf.at[slot], dst_ref.at[pl.ds(0, rows)], out_sem.at[slot])

    i = pl.program_id(0)
    slot = i % ROW_SLOTS
    nxt = (i + 1) % ROW_SLOTS

    @pl.when(i == 0)
    def _():
        load(0, 0).start()

    @pl.when(i + 1 < n_steps)
    def _():
        @pl.when(i + 1 >= ROW_SLOTS)
        def _():
            rows_done(nxt).wait()
        load(i + 1, nxt).start()

    load(i, slot).wait()
    base = i * rows

    for r in range(rows):
        pltpu.make_async_copy(buf.at[slot, pl.ds(r, 1)], dst_ref.at[pl.ds(idx_ref[base + r], 1)],
                              out_sem.at[slot]).start()

    @pl.when(i == n_steps - 1)
    def _():
        for k in range(min(ROW_SLOTS, n_steps)):
            rows_done((n_steps - 1 - k) % ROW_SLOTS).wait()


def _row_gather_kernel(idx_ref, src_ref, dst_ref, buf, row_sem, out_sem, *, rows, n_steps):
    def rows_done(slot):
        return pltpu.make_async_copy(src_ref.at[pl.ds(0, rows)], buf.at[slot], row_sem.at[slot])

    def store(i, slot):
        return pltpu.make_async_copy(buf.at[slot], dst_ref.at[pl.ds(pl.multiple_of(i * rows, rows), rows)],
                                     out_sem.at[slot])

    i = pl.program_id(0)
    slot = i % ROW_SLOTS

    @pl.when(i >= ROW_SLOTS)
    def _():
        store(i - ROW_SLOTS, slot).wait()

    base = i * rows

    for r in range(rows):
        pltpu.make_async_copy(src_ref.at[pl.ds(idx_ref[base + r], 1)], buf.at[slot, pl.ds(r, 1)],
                              row_sem.at[slot]).start()

    @pl.when(i >= 1)
    def _():
        prev = (i + ROW_SLOTS - 1) % ROW_SLOTS
        rows_done(prev).wait()
        store(i - 1, prev).start()

    @pl.when(i == n_steps - 1)
    def _():
        last = (n_steps - 1) % ROW_SLOTS
        rows_done(last).wait()
        store(n_steps - 1, last).start()
        for k in range(min(ROW_SLOTS, n_steps)):
            store(n_steps - 1 - k, (n_steps - 1 - k) % ROW_SLOTS).wait()


def row_move(idx, src, dst_rows, *, scatter, dst_init=None):
    t = idx.shape[0]
    width = src.shape[1]
    rows = _row_tile(t)
    any_spec = pl.BlockSpec(memory_space=pl.ANY)
    operands = [idx, src] + ([dst_init] if scatter else [])
    grid_spec = pltpu.PrefetchScalarGridSpec(
        num_scalar_prefetch=1, grid=(t // rows,), in_specs=[any_spec] * (len(operands) - 1), out_specs=any_spec,
        scratch_shapes=[pltpu.VMEM((ROW_SLOTS, rows, width), src.dtype),
                        pltpu.SemaphoreType.DMA((ROW_SLOTS,)), pltpu.SemaphoreType.DMA((ROW_SLOTS,))])
    kern = _row_scatter_kernel if scatter else _row_gather_kernel
    return pl.pallas_call(
        functools.partial(kern, rows=rows, n_steps=t // rows),
        grid_spec=grid_spec,
        out_shape=jax.ShapeDtypeStruct((dst_rows, width), src.dtype),
        input_output_aliases={2: 0} if scatter else {},
        compiler_params=_params("arbitrary"),
        name="row_scatter" if scatter else "row_gather",
    )(*operands)


def _moe_kernel(ta_ref, tb_ref, nu_ref, xs_ref, w1a_ref, w3a_ref, w2a_ref, w1b_ref, w3b_ref, w2b_ref,
                g_ref, b_ref, o_ref, pend):
    t = pl.program_id(0)
    d = o_ref.shape[1]
    n_used = nu_ref[0]

    @pl.when(t == 0)
    def _():
        pend[...] = jnp.zeros(pend.shape, F32)

    @pl.when(t < n_used)
    def _():
        o_ref[...] = _layer_norm(pend[...], g_ref[...], b_ref[...])
        x = xs_ref[:, :d]
        ext = xs_ref[:, d:]
        xb = x.astype(BF16)

        def ffn(w1_ref, w3_ref, w2_ref):
            h1 = _dot(xb, w1_ref[0])
            h3 = _dot(xb, w3_ref[0])
            return _dot((h1 * _sigmoid(h1) * h3).astype(BF16), w2_ref[0])

        y = ext[:, 2:3] * ffn(w1a_ref, w3a_ref, w2a_ref) + ext[:, 3:4] * ffn(w1b_ref, w3b_ref, w2b_ref)
        pend[...] = ALPHA * x + y

    @pl.when(t == n_used)
    def _():
        o_ref[...] = _layer_norm(pend[...], g_ref[...], b_ref[...])

    @pl.when(t > n_used)
    def _():
        o_ref[...] = jnp.zeros(o_ref.shape, o_ref.dtype)


def moe_ln(tile_ea, tile_eb, n_used, xs, w1, w3, w2, ln_g, ln_b, tm):
    p, wide = xs.shape
    d = wide - EXT
    de = w1.shape[2]
    n = p // tm

    def w_spec(shape, which):
        return pl.BlockSpec(shape, lambda t, ta, tb, nu: ((ta, tb)[which][jnp.minimum(t, n - 1)], 0, 0))

    grid_spec = pltpu.PrefetchScalarGridSpec(
        num_scalar_prefetch=3,
        grid=(n + 1,),
        in_specs=[pl.BlockSpec((tm, wide), lambda t, ta, tb, nu: (jnp.minimum(t, n - 1), 0)),
                  w_spec((1, d, de), 0), w_spec((1, d, de), 0), w_spec((1, de, d), 0),
                  w_spec((1, d, de), 1), w_spec((1, d, de), 1), w_spec((1, de, d), 1),
                  pl.BlockSpec((1, d), lambda t, ta, tb, nu: (0, 0)),
                  pl.BlockSpec((1, d), lambda t, ta, tb, nu: (0, 0))],
        out_specs=pl.BlockSpec((tm, d), lambda t, ta, tb, nu: (jnp.maximum(t - 1, 0), 0)),
        scratch_shapes=[pltpu.VMEM((tm, d), F32)],
    )
    return pl.pallas_call(
        _moe_kernel,
        grid_spec=grid_spec,
        out_shape=jax.ShapeDtypeStruct((p, d), F32),
        compiler_params=_params("arbitrary"),
        name="moe_ln",
    )(tile_ea, tile_eb, n_used, xs, w1, w3, w2, w1, w3, w2, ln_g, ln_b)


def _moe_few_kernel(x_ref, w1_ref, w3_ref, w2_ref, g_ref, b_ref, o_ref, acc_s):
    e = pl.program_id(0)
    d = o_ref.shape[1]

    @pl.when(e == 0)
    def _():
        acc_s[...] = jnp.zeros(acc_s.shape, F32)

    x = x_ref[:, :d]
    ext = x_ref[:, d:]
    xb = x.astype(BF16)
    h1 = _dot(xb, w1_ref[0])
    h3 = _dot(xb, w3_ref[0])
    y = _dot((h1 * _sigmoid(h1) * h3).astype(BF16), w2_ref[0])
    ef = e.astype(F32)
    gate = jnp.where(ext[:, 4:5] == ef, ext[:, 2:3], 0.0) + jnp.where(ext[:, 5:6] == ef, ext[:, 3:4], 0.0)
    acc_s[...] += gate * y

    @pl.when(e == pl.num_programs(0) - 1)
    def _():
        o_ref[...] = _layer_norm(ALPHA * x + acc_s[...], g_ref[...], b_ref[...])


def moe_ln_few(x1ext, w1, w3, w2, ln_g, ln_b):
    t, wide = x1ext.shape
    d = wide - EXT
    de = w1.shape[2]
    return pl.pallas_call(
        _moe_few_kernel,
        grid=(w1.shape[0],),
        in_specs=[_whole((t, wide)), pl.BlockSpec((1, d, de), lambda e: (e, 0, 0)),
                  pl.BlockSpec((1, d, de), lambda e: (e, 0, 0)), pl.BlockSpec((1, de, d), lambda e: (e, 0, 0)),
                  _whole((1, d)), _whole((1, d))],
        out_specs=_whole((t, d)),
        out_shape=jax.ShapeDtypeStruct((t, d), F32),
        scratch_shapes=[pltpu.VMEM((t, d), F32)],
        compiler_params=_params("arbitrary"),
        name="moe_ln_few",
    )(x1ext, w1, w3, w2, ln_g, ln_b)


def _bucket_experts():
    ea, eb = [], []
    for g in range(N_GROUPS):
        for ja in range(EXP_PER_GROUP):
            for jb in range(ja + 1, EXP_PER_GROUP):
                ea.append(g * EXP_PER_GROUP + ja)
                eb.append(g * EXP_PER_GROUP + jb)
    return np.asarray(ea, np.int32), np.asarray(eb, np.int32)


def _bucket_layout(x1ext, counts, tm):
    t, wide = x1ext.shape
    d = wide - EXT
    bucket = x1ext[:, d].astype(jnp.int32)
    rank = x1ext[:, d + 1].astype(jnp.int32)
    counts = counts[0, :N_BUCKETS].astype(jnp.int32)
    padded = ((counts + tm - 1) // tm) * tm
    row_end = jnp.cumsum(padded)
    row_start = row_end - padded
    ids = jnp.arange(N_BUCKETS, dtype=jnp.int32)
    pos = rank + jnp.sum(jnp.where(bucket[:, None] == ids[None, :], row_start[None, :], 0), axis=1)
    n_rows = _round_up(t, tm) + N_BUCKETS * tm
    n_tiles = n_rows // tm
    n_used = (row_end[-1] // tm).astype(jnp.int32)
    tile_row = jnp.minimum(jnp.arange(n_tiles, dtype=jnp.int32), n_used - 1) * tm
    tile_bucket = jnp.sum((row_end[None, :] <= tile_row[:, None]).astype(jnp.int32), axis=1)
    ea, eb = _bucket_experts()
    in_bucket = tile_bucket[:, None] == ids[None, :]
    tile_ea = jnp.sum(jnp.where(in_bucket, jnp.asarray(ea)[None, :], 0), axis=1)
    tile_eb = jnp.sum(jnp.where(in_bucket, jnp.asarray(eb)[None, :], 0), axis=1)
    return pos.astype(jnp.int32), tile_ea, tile_eb, n_used.reshape(1), n_rows


def hier_moe_ln(x1ext, counts, moe_w, ln_g, ln_b, sorted_buf=None):
    w1, w3, w2 = moe_w
    t, wide = x1ext.shape
    if t <= FEW_TOKENS:
        return moe_ln_few(x1ext, w1, w3, w2, ln_g, ln_b), None
    tm = MOE_TILE
    pos, tile_ea, tile_eb, n_used, n_rows = _bucket_layout(x1ext, counts, tm)
    if sorted_buf is None:
        sorted_buf = jnp.zeros((n_rows, wide), F32)
    xs = row_move(pos, x1ext, n_rows, scatter=True, dst_init=sorted_buf)
    ys = moe_ln(tile_ea, tile_eb, n_used, xs, w1, w3, w2, ln_g, ln_b, tm)
    return row_move(pos, ys, t, scatter=False), xs


def _rope_tables(s, offset, tm):
    half = ROPE // 2
    inv_freq = ROPE_BASE ** (-jnp.arange(half, dtype=F32) / half)
    ang = (offset + jnp.arange(s)).astype(F32)[:, None] * inv_freq[None, :]
    cos, sin = jnp.cos(ang), jnp.sin(ang)
    cc = jnp.concatenate([cos, cos, cos, cos], -1)
    ss = jnp.concatenate([-sin, sin, -sin, sin], -1)
    if tm > s:
        cc, ss = jnp.tile(cc, (tm // s, 1)), jnp.tile(ss, (tm // s, 1))
    return cc, ss


def _round_up(n, m):
    return (n + m - 1) // m * m


def ab_layer(x, b, s, wts, route_w, cache):
    w_big, w_gate, b_gate, g_ml, w_out, ln_g, ln_b = wts
    t = b * s
    ls_cols = np.array([1.0] * H_FOX + [0.0] * H_ML + [1.0] * H_ML)
    fq, mq, mk, mv, mo, fk_bf, fv_bf, gates, fk, fv = ab_proj(x, w_big, w_gate, b_gate, ls_cols)
    flf = gates[:, :H_FOX].reshape(b, s, H_FOX)
    ml_rows = jnp.swapaxes(gates[:, H_FOX:H_FOX + 2 * H_ML].reshape(b, s, 2 * H_ML), 1, 2)
    ml_rows = jnp.pad(ml_rows, ((0, 0), (0, BF16_SUBLANES - 2 * H_ML), (0, 0)))
    if cache is None:
        past_kv, lf_all = None, flf
        c0 = jnp.zeros((b, H_ML, DH_ML, DH_ML), F32)
        n0 = jnp.zeros((b, H_ML, 1, DH_ML), F32)
        m0 = jnp.zeros((b, H_ML, 1, 1), F32)
        tq, chunk = min(FOX_TILE, s), min(MLSTM_CHUNK, s)
    else:
        ck, cv, clf, c0, n0, m0 = cache
        past = ck.shape[1]
        past_kv = (ck.reshape(b, past, W_FOX).astype(BF16), cv.reshape(b, past, W_FOX).astype(BF16))
        lf_all = jnp.concatenate([clf, flf], 1)
        n0 = n0.reshape(b, H_ML, 1, DH_ML)
        m0 = m0.reshape(b, H_ML, 1, 1)
        tq, chunk = s, s
    lk = lf_all.shape[1]
    lf_rows = jnp.pad(jnp.swapaxes(lf_all, 1, 2),
                      ((0, 0), (0, BF16_SUBLANES - H_FOX), (0, _round_up(lk, PAST_KV_TILE) - lk)))
    fo = fox_attention(fq.reshape(b, s, W_FOX), fk_bf.reshape(b, s, W_FOX), fv_bf.reshape(b, s, W_FOX), lf_rows,
                       tq=tq, past=past_kv, tk=PAST_KV_TILE if cache is not None else tq)
    mh, c_new, n_new, m_new = mlstm(mq.reshape(b, s, W_ML), mk.reshape(b, s, W_ML), mv.reshape(b, s, W_ML),
                                    ml_rows, c0, n0, m0, chunk=chunk)
    x1ext, counts = ab_out(fo.reshape(t, W_FOX), mh.reshape(t, W_ML), mo, x, g_ml, w_out, ln_g, ln_b, route_w)
    state = (fk, fv, flf, c_new, n_new.reshape(b, H_ML, DH_ML), m_new.reshape(b, H_ML))
    return x1ext, counts, state


def mla_layer(x, b, s, wts, route_w, cache, prev_c=(), prev_r=()):
    w_down, g_q, g_kv, w_uq, wuk_t, wuv, w_out, ln_g, ln_b = wts
    t = b * s
    tm = _row_tile(t)
    past = 0 if cache is None else cache[0].shape[1]
    cc, ss = _rope_tables(s, past, tm)
    cq, ckv, kr, ckv_bf, kr2_bf = mla_down(x, w_down, g_q, g_kv, cc, ss, s, prev_c, prev_r)
    q_lat, q_rope = mla_uq(cq, w_uq, wuk_t, cc, ss, s)
    kc = ckv_bf.reshape(b, s, KV_RANK)
    kr2 = kr2_bf.reshape(b, s, LANE)
    if cache is None:
        tq, past_k = min(MLA_TILE, s), None
    else:
        c_ckv, c_kr = cache
        c_kr_bf = c_kr.astype(BF16)
        tq, past_k = s, (c_ckv.astype(BF16), jnp.concatenate([c_kr_bf, c_kr_bf], -1))
    o_lat = mla_attention(q_lat.reshape(b, s, -1), q_rope.reshape(b, s, -1), kc, kr2, tq=tq,
                          tk=tq if cache is None else PAST_KV_TILE, past=past_k)
    x1ext, counts = mla_out(o_lat.reshape(t, -1), wuv, w_out, x, ln_g, ln_b, route_w)
    return x1ext, counts, (ckv, kr)


def _prep_ab_weights(w_in, b_fox_f, b_ml_i, b_ml_f, g_ml, w_out, ln_g, ln_b):
    sizes = (W_FOX, W_FOX, W_FOX, H_FOX, W_ML, W_ML, W_ML, H_ML, H_ML, W_ML)
    idx = np.cumsum(sizes[:-1]).tolist()
    fq, fk, fv, ff, mq, mk, mv, mi, mf, mo = jnp.split(w_in, idx, axis=1)
    w_big = jnp.concatenate([fq * (DH_FOX ** -0.5 * LOG2E), mq, mk, mv, mo, fk, fv], 1).astype(BF16)
    w_gate = jnp.concatenate([ff, mi, mf], 1)
    b_gate = jnp.concatenate([b_fox_f, b_ml_i, b_ml_f])
    return (w_big, w_gate, b_gate, g_ml[None, :], w_out.astype(BF16), ln_g[None, :], ln_b[None, :])


def _prep_mla_weights(w_down, g_q, w_uq, g_kv, w_uk, w_uv, w_out, ln_g, ln_b):
    half = ROPE // 2
    cq_w, ckv_w, kr_w = jnp.split(w_down, [Q_RANK, Q_RANK + KV_RANK], axis=1)
    kr_sw = jnp.concatenate([kr_w[:, half:], kr_w[:, :half]], 1)
    w_down_p = jnp.concatenate([cq_w, ckv_w, kr_w, kr_w, kr_sw, kr_sw], 1).astype(BF16)
    uq = w_uq.reshape(Q_RANK, H_MLA, NOPE + ROPE)
    uq_nope = uq[:, :, :NOPE].reshape(Q_RANK, H_MLA * NOPE)
    uq_rope = uq[:, :, NOPE:]
    uq_rope_sw = jnp.concatenate([uq_rope[..., half:], uq_rope[..., :half]], -1)
    w_uq_p = jnp.concatenate([uq_nope, uq_rope.reshape(Q_RANK, -1), uq_rope_sw.reshape(Q_RANK, -1)], 1).astype(BF16)
    wuk_t = jnp.transpose(w_uk, (1, 2, 0)).astype(BF16)
    wuv = jnp.transpose(w_uv, (1, 0, 2)).astype(BF16)
    return (w_down_p, g_q[None, :], g_kv[None, :], w_uq_p, wuk_t, wuv, w_out.astype(BF16), ln_g[None, :], ln_b[None, :])


def kernel(x_prompt, x_sample, cache_fox_k, cache_fox_v, cache_fox_logf, state_mlstm_c, state_mlstm_n, state_mlstm_m, cache_mla_ckv, cache_mla_krope, w_ab_in, b_fox_f, b_mlstm_i, b_mlstm_f, g_mlstm_norm, w_ab_out, w_mla_down, g_mla_q, w_mla_uq, g_mla_kv, w_mla_uk, w_mla_uv, w_mla_out, ln1_g, ln1_b, ln2_g, ln2_b, w_moe_group, b_moe_group, w_moe_router, b_moe_router, w_exp_gate, w_exp_up, w_exp_down):
    bp, sp, d = x_prompt.shape
    bs, ss_, _ = x_sample.shape
    xp = x_prompt.reshape(bp * sp, d)
    xs = x_sample.reshape(bs * ss_, d)
    ab_p, ab_s, c_p, c_s = [], [], [], []
    n_c = DEPTH // 2
    assert n_c >= 2
    buf_p = buf_s = None
    for l in range(DEPTH):
        j = l // 2
        route_w = (jnp.concatenate([w_moe_group[l], w_moe_router[l]], 1),
                   jnp.concatenate([b_moe_group[l], b_moe_router[l]]))
        if l % 2 == 0:
            wts = _prep_ab_weights(w_ab_in[j], b_fox_f[j], b_mlstm_i[j], b_mlstm_f[j], g_mlstm_norm[j],
                                   w_ab_out[j], ln1_g[l], ln1_b[l])
            xp1, cnt_p, st_p = ab_layer(xp, bp, sp, wts, route_w, None)
            xs1, cnt_s, st_s = ab_layer(xs, bs, ss_, wts, route_w,
                                        (cache_fox_k[j], cache_fox_v[j], cache_fox_logf[j],
                                         state_mlstm_c[j], state_mlstm_n[j], state_mlstm_m[j]))
            ab_p.append(st_p)
            ab_s.append(st_s)
        else:
            wts = _prep_mla_weights(w_mla_down[j], g_mla_q[j], w_mla_uq[j], g_mla_kv[j], w_mla_uk[j], w_mla_uv[j],
                                    w_mla_out[j], ln1_g[l], ln1_b[l])
            last = j == n_c - 1
            prev = lambda sts, i: tuple(st[i] for st in sts) if last else ()
            xp1, cnt_p, st_p = mla_layer(xp, bp, sp, wts, route_w, None, prev(c_p, 0), prev(c_p, 1))
            xs1, cnt_s, st_s = mla_layer(xs, bs, ss_, wts, route_w, (cache_mla_ckv[j], cache_mla_krope[j]),
                                         prev(c_s, 0), prev(c_s, 1))
            c_p.append(st_p)
            c_s.append(st_s)
        moe_w = (w_exp_gate[l].astype(BF16), w_exp_up[l].astype(BF16), w_exp_down[l].astype(BF16))
        xp, buf_p = hier_moe_ln(xp1, cnt_p, moe_w, ln2_g[l][None, :], ln2_b[l][None, :], buf_p)
        xs, buf_s = hier_moe_ln(xs1, cnt_s, moe_w, ln2_g[l][None, :], ln2_b[l][None, :], buf_s)

    def stack(groups, i):
        return jnp.stack([g[i] for g in groups])

    def states(ab, c, b, s):
        kv = [jnp.stack([st[i].reshape(b, s, H_FOX, DH_FOX) for st in ab]) for i in (0, 1)]
        lat = [c[-1][0].reshape(n_c, b, s, KV_RANK), c[-1][1].reshape(n_c, b, s, ROPE)]
        return tuple(kv) + tuple(stack(ab, i) for i in (2, 3, 4, 5)) + tuple(lat)

    return (xp.reshape(bp, sp, d), xs.reshape(bs, ss_, d)) + states(ab_p, c_p, bp, sp) + states(ab_s, c_s, bs, ss_)
```

```python
import functools

import numpy as np
import jax
import jax.numpy as jnp
from jax import lax
from jax.experimental import pallas as pl
from jax.experimental.pallas import tpu as pltpu

F32 = jnp.float32
BF16 = jnp.bfloat16

LANE = 128
BF16_SUBLANES = 16
VMEM_LIMIT_BYTES = 48 * 1024 * 1024

ROW_TILE = 512
FOX_TILE = 512
MLA_TILE = 256
PAST_KV_TILE = 256
MLSTM_CHUNK = 256
MOE_TILE = 256

H_FOX, DH_FOX = 8, 64
W_FOX = H_FOX * DH_FOX
H_ML, DH_ML = 4, 128
W_ML = H_ML * DH_ML
H_MLA, NOPE, ROPE = 8, 128, 64
Q_RANK, KV_RANK = 384, 256
ROPE_BASE = 10000.0
N_GROUPS, EXP_PER_GROUP = 4, 8
DEPTH = 4
ALPHA = (2 * DEPTH) ** 0.25
LN_EPS = 1e-5
RMS_EPS = 1e-6
CHUNK = 64
NEG_BIG = -1e30
LOG2E = 1.4426950408889634


def _params(*sem):
    return pltpu.CompilerParams(dimension_semantics=sem, vmem_limit_bytes=VMEM_LIMIT_BYTES)


def _dot(a, b):
    return jnp.dot(a, b, preferred_element_type=F32)


def _dot_nt(a, b):
    return lax.dot_general(a, b, (((1,), (1,)), ((), ())), preferred_element_type=F32)


def _split3(x):
    hi = x.astype(BF16)
    r = x - hi.astype(F32)
    mid = r.astype(BF16)
    lo = (r - mid.astype(F32)).astype(BF16)
    return hi, mid, lo


def _layer_norm(y, g, b):
    mu = jnp.mean(y, axis=1, keepdims=True)
    yc = y - mu
    var = jnp.mean(yc * yc, axis=1, keepdims=True)
    return yc * lax.rsqrt(var + LN_EPS) * g + b


def _sigmoid(x):
    return 1.0 / (1.0 + jnp.exp(-x))


def _row_tile(t):
    return min(ROW_TILE, t)


def _ab_proj_kernel(x_ref, w_ref, gw_ref, gb_ref, gmask_ref,
                    fq_ref, mq_ref, mk_ref, mv_ref, mo_ref, kbf_ref, vbf_ref, gates_ref, k_ref, v_ref):
    x = x_ref[...]
    xh = x.astype(BF16)
    c0 = 0
    for o_ref in (fq_ref, mq_ref, mk_ref, mv_ref, mo_ref):
        n = o_ref.shape[1]
        o_ref[...] = _dot(xh, w_ref[:, c0:c0 + n]).astype(BF16)
        c0 += n
    for bf_ref, state_ref in ((kbf_ref, k_ref), (vbf_ref, v_ref)):
        z = _dot(xh, w_ref[:, c0:c0 + W_FOX])
        c0 += W_FOX
        bf_ref[...] = z.astype(BF16)
        state_ref[...] = z
    xl = (x - xh.astype(F32)).astype(BF16)
    gg = _dot(xh, gw_ref[...])
    g = gg[:, :LANE] + gg[:, LANE:] + _dot(xl, gw_ref[:, :LANE]) + gb_ref[...]
    log_sig = jnp.minimum(g, 0.0) - jnp.log1p(jnp.exp(-jnp.abs(g)))
    gates_ref[...] = jnp.where(gmask_ref[...] > 0.0, log_sig, g)


def ab_proj(x, w_big, w_gate, b_gate, log_sigmoid_cols):
    t, k = x.shape
    tm = _row_tile(t)
    n_gate = w_gate.shape[1]
    wp = jnp.zeros((k, LANE), F32).at[:, :n_gate].set(w_gate)
    wh = wp.astype(BF16)
    wl = (wp - wh.astype(F32)).astype(BF16)
    bp = jnp.zeros((1, LANE), F32).at[0, :n_gate].set(b_gate)
    mask = jnp.zeros((1, LANE), F32).at[0, :n_gate].set(jnp.asarray(log_sigmoid_cols, F32))
    bf = lambda n: jax.ShapeDtypeStruct((t, n), BF16)
    return pl.pallas_call(
        _ab_proj_kernel,
        grid=(t // tm,),
        in_specs=[_rows(tm, k), _whole(w_big.shape), _whole((k, 2 * LANE)), _whole((1, LANE)), _whole((1, LANE))],
        out_specs=[_rows(tm, W_FOX)] + [_rows(tm, W_ML)] * 4 + [_rows(tm, W_FOX)] * 2 + [_rows(tm, LANE)]
        + [_rows(tm, W_FOX)] * 2,
        out_shape=[bf(W_FOX)] + [bf(W_ML)] * 4 + [bf(W_FOX)] * 2 + [jax.ShapeDtypeStruct((t, LANE), F32)]
        + [jax.ShapeDtypeStruct((t, W_FOX), F32)] * 2,
        compiler_params=_params("parallel"),
        name="ab_proj",
    )(x, w_big, jnp.concatenate([wh, wl], 1), bp, mask)


def _lanes(x, w):
    if w < LANE:
        return x[:, :w]
    return x if w == LANE else jnp.concatenate([x] * (w // LANE), axis=1)


def _softmax_step(s, at, m_s, l_s, acc_s, v):
    w = s.shape[1]
    m_prev = m_s[at]
    m_new = jnp.maximum(m_prev, jnp.max(s, axis=1, keepdims=True))
    p = jnp.exp2(s - _lanes(m_new, w))
    a = jnp.exp2(m_prev - m_new)
    l_s[at] = a * l_s[at] + jnp.sum(p, axis=1, keepdims=True)
    acc_s[at] = acc_s[at] * _lanes(a, acc_s.shape[-1]) + _dot(p.astype(BF16), v)
    m_s[at] = m_new


def _fox_kernel(q_ref, k_ref, v_ref, *rest, tq, tk, q_off, cblk):
    pk_ref, pv_ref = rest[:2] if len(rest) == 9 else (k_ref, v_ref)
    lf_ref, o_ref, ncum_ref, qm_s, m_s, l_s, acc_s = rest[-7:]
    qi = pl.program_id(1)
    sub = cblk // LANE
    n_pairs = H_FOX // 2

    @pl.when(qi == 0)
    def _():
        r = lax.broadcasted_iota(jnp.int32, (cblk, cblk), 0)
        c = lax.broadcasted_iota(jnp.int32, (cblk, cblk), 1)
        upper = jnp.where(r <= c, 1.0, 0.0).astype(BF16)
        carry = jnp.zeros((lf_ref.shape[1], 1), F32)
        for j in range(lf_ref.shape[2] // cblk):
            g1, g2, g3 = _split3(lf_ref[0, :, j * cblk:(j + 1) * cblk])
            cum = _dot(g1, upper) + _dot(g2, upper) + _dot(g3, upper) + carry
            carry = cum[:, cblk - 1:cblk]
            for h in range(H_FOX):
                for u in range(sub):
                    ncum_ref[h, j * sub + u] = cum[h:h + 1, u * LANE:(u + 1) * LANE] * (-LOG2E)

    lane = lax.broadcasted_iota(jnp.int32, (tq, LANE), 1)
    for hp in range(n_pairs):
        q = q_ref[0, :, hp * LANE:(hp + 1) * LANE]
        zero = jnp.zeros_like(q)
        qm_s[hp, 0:tq, :] = jnp.where(lane < DH_FOX, q, zero)
        qm_s[hp, tq:2 * tq, :] = jnp.where(lane >= DH_FOX, q, zero)
    m_s[...] = jnp.full(m_s.shape, NEG_BIG, F32)
    l_s[...] = jnp.zeros(l_s.shape, F32)
    acc_s[...] = jnp.zeros(acc_s.shape, F32)
    q_start = q_off + qi * tq

    def update(k_src, v_src, off, w, bias_of_head, mask):
        for hp in range(n_pairs):
            kt = k_src[0, pl.ds(off, w), hp * LANE:(hp + 1) * LANE]
            vt = v_src[0, pl.ds(off, w), hp * LANE:(hp + 1) * LANE]
            s = _dot_nt(qm_s[hp], kt)
            s = jnp.concatenate([s[:tq] + bias_of_head(2 * hp), s[tq:] + bias_of_head(2 * hp + 1)], axis=0)
            if mask is not None:
                s = jnp.where(mask, s, -jnp.inf)
            _softmax_step(s, hp, m_s, l_s, acc_s, vt)

    def full_tile(j, carry):
        def bias(h):
            return jnp.concatenate([ncum_ref[h, j * (tk // LANE) + u] for u in range(tk // LANE)], axis=1)
        update(pk_ref, pv_ref, pl.multiple_of(j * tk, tk), tk, bias, None)
        return carry

    lax.fori_loop(0, q_start // tk, full_tile, 0)

    def diag_bias(h):
        if tq >= LANE:
            return jnp.concatenate([ncum_ref[h, q_start // LANE + u] for u in range(tq // LANE)], axis=1)
        lo = q_off % LANE
        return ncum_ref[h, q_start // LANE][:, lo:lo + tq]

    r = lax.broadcasted_iota(jnp.int32, (2 * tq, tq), 0)
    c = lax.broadcasted_iota(jnp.int32, (2 * tq, tq), 1)
    update(k_ref, v_ref, pl.multiple_of(qi * tq, tq), tq, diag_bias, c <= jnp.where(r >= tq, r - tq, r))
    for hp in range(n_pairs):
        o = acc_s[hp] / l_s[hp]
        o_ref[0, :, hp * LANE:(hp + 1) * LANE] = jnp.where(lane < DH_FOX, o[:tq], o[tq:]).astype(o_ref.dtype)


def fox_attention(q, k, v, logf_rows, *, tq, tk, past=None, cblk=PAST_KV_TILE):
    b, lq, _ = q.shape
    lkp = logf_rows.shape[2]
    q_off = 0 if past is None else past[0].shape[1]
    nq = lq // tq
    assert k.shape[1] == lq and q_off % tk == 0 and (tq % tk == 0 or lq == tq) and lkp % cblk == 0
    assert tq >= LANE or (nq == 1 and q_off % LANE + tq <= LANE)
    kern = functools.partial(_fox_kernel, tq=tq, tk=tk, q_off=q_off, cblk=cblk)
    whole_seq = [pl.BlockSpec((1, a.shape[1], W_FOX), lambda bi, qi: (bi, 0, 0)) for a in (k, v) + tuple(past or ())]
    return pl.pallas_call(
        kern,
        grid=(b, nq),
        in_specs=[pl.BlockSpec((1, tq, W_FOX), lambda bi, qi: (bi, qi, 0))] + whole_seq + [
            pl.BlockSpec((1, logf_rows.shape[1], lkp), lambda bi, qi: (bi, 0, 0)),
        ],
        out_specs=pl.BlockSpec((1, tq, W_FOX), lambda bi, qi: (bi, qi, 0)),
        out_shape=jax.ShapeDtypeStruct((b, lq, W_FOX), BF16),
        scratch_shapes=[pltpu.VMEM((H_FOX, lkp // LANE, 1, LANE), F32), pltpu.VMEM((H_FOX // 2, 2 * tq, LANE), BF16),
                        pltpu.VMEM((H_FOX // 2, 2 * tq, LANE), F32), pltpu.VMEM((H_FOX // 2, 2 * tq, LANE), F32),
                        pltpu.VMEM((H_FOX // 2, 2 * tq, LANE), F32)],
        compiler_params=_params("parallel", "arbitrary"),
        name="fox_attention",
    )(q, k, v, *(past or ()), logf_rows)


def _mlstm_kernel(q_ref, k_ref, v_ref, g_ref, c0_ref, n0_ref, m0_ref,
                  h_ref, c_ref, n_ref, m_ref, c_s, n_s, m_s, *, chunk):
    ci = pl.program_id(1)
    L = chunk
    scale = DH_ML ** -0.5

    @pl.when(ci == 0)
    def _():
        c_s[...] = c0_ref[0]
        n_s[...] = n0_ref[0]
        m_s[...] = m0_ref[0]

    r = lax.broadcasted_iota(jnp.int32, (L, L), 0)
    c = lax.broadcasted_iota(jnp.int32, (L, L), 1)
    eye = r == c
    causal = c <= r
    upper = jnp.where(r <= c, 1.0, 0.0).astype(BF16)
    g = g_ref[0]
    g1, g2, g3 = _split3(g)
    cum = _dot(g1, upper) + _dot(g2, upper) + _dot(g3, upper)

    def to_col(row):
        return jnp.sum(jnp.where(eye, row, 0.0), axis=1, keepdims=True)

    for h in range(H_ML):
        sl = slice(h * DH_ML, (h + 1) * DH_ML)
        qh, kh, vh = q_ref[0, :, sl], k_ref[0, :, sl], v_ref[0, :, sl]
        ig_row = g[h:h + 1, :]
        bh_row = cum[H_ML + h:H_ML + h + 1, :]
        bh_col, ig_col = to_col(bh_row), to_col(ig_row)
        m0 = m_s[h]
        c0 = c_s[h]
        n0 = n_s[h]
        logd = jnp.where(causal, bh_col - bh_row + ig_row, -jnp.inf)
        inter = bh_col + m0
        m_col = jnp.maximum(inter, jnp.max(logd, axis=1, keepdims=True))
        d = jnp.exp(logd - m_col)
        a_col = jnp.exp(inter - m_col)
        s = _dot_nt(qh, kh) * scale * d
        num = _dot(s.astype(BF16), vh) + _dot_nt(qh, c0.astype(BF16)) * a_col
        qf = qh.astype(F32)
        den = jnp.sum(s, axis=1, keepdims=True) + a_col * jnp.sum(qf * n0, axis=1, keepdims=True)
        den = jnp.maximum(jnp.abs(den), jnp.exp(-m_col))
        h_ref[0, :, sl] = (num / den).astype(h_ref.dtype)

        m_l = m_col[L - 1:L, :]
        bh_l = bh_row[:, L - 1:L]
        a_l = jnp.exp(bh_l + m0 - m_l)
        w_col = jnp.exp(bh_l - bh_col + ig_col - m_l)
        vw_t = jnp.transpose(vh.astype(F32) * w_col).astype(BF16)
        c_s[h] = a_l * c0 + _dot(vw_t, kh) * scale
        n_s[h] = a_l * n0 + jnp.sum(kh.astype(F32) * w_col, axis=0, keepdims=True) * scale
        m_s[h] = m_l

    @pl.when(ci == pl.num_programs(1) - 1)
    def _():
        c_ref[0] = c_s[...]
        n_ref[0] = n_s[...]
        m_ref[0] = m_s[...]


def mlstm(q, k, v, gate_rows, c0, n0, m0, *, chunk):
    b, s, _ = q.shape
    nc = s // chunk
    seq = pl.BlockSpec((1, chunk, W_ML), lambda bi, ci: (bi, ci, 0))
    c_spec = pl.BlockSpec((1, H_ML, DH_ML, DH_ML), lambda bi, ci: (bi, 0, 0, 0))
    n_spec = pl.BlockSpec((1, H_ML, 1, DH_ML), lambda bi, ci: (bi, 0, 0, 0))
    m_spec = pl.BlockSpec((1, H_ML, 1, 1), lambda bi, ci: (bi, 0, 0, 0))
    return pl.pallas_call(
        functools.partial(_mlstm_kernel, chunk=chunk),
        grid=(b, nc),
        in_specs=[seq, seq, seq, pl.BlockSpec((1, gate_rows.shape[1], chunk), lambda bi, ci: (bi, 0, ci)),
                  c_spec, n_spec, m_spec],
        out_specs=[seq, c_spec, n_spec, m_spec],
        out_shape=[jax.ShapeDtypeStruct((b, s, W_ML), BF16),
                   jax.ShapeDtypeStruct(c0.shape, F32), jax.ShapeDtypeStruct(n0.shape, F32),
                   jax.ShapeDtypeStruct(m0.shape, F32)],
        scratch_shapes=[pltpu.VMEM((H_ML, DH_ML, DH_ML), F32), pltpu.VMEM((H_ML, 1, DH_ML), F32),
                        pltpu.VMEM((H_ML, 1, 1), F32)],
        compiler_params=_params("parallel", "arbitrary"),
        name="mlstm",
    )(q, k, v, gate_rows, c0, n0, m0)


N_PAIRS = EXP_PER_GROUP * (EXP_PER_GROUP - 1) // 2
N_BUCKETS = N_GROUPS * N_PAIRS
assert N_BUCKETS <= LANE
EXT = LANE
FEW_TOKENS = 1024


def _route_store(out, wr_ref, br_ref, ltri_ref, o_ref, cnt_ref, cnt_s):
    tm, d = out.shape

    @pl.when(pl.program_id(0) == 0)
    def _():
        cnt_s[...] = jnp.zeros(cnt_s.shape, F32)

    xh = out.astype(BF16)
    xl = (out - xh.astype(F32)).astype(BF16)
    zz = _dot(xh, wr_ref[...])
    z = zz[:, :LANE] + zz[:, LANE:] + _dot(xl, wr_ref[:, :LANE]) + br_ref[...]
    lane = lax.broadcasted_iota(jnp.int32, (tm, LANE), 1).astype(F32)
    far = float(LANE)
    neg = -jnp.inf
    gl = jnp.where(lane < N_GROUPS, z, neg)
    gmax = jnp.max(gl, axis=1, keepdims=True)
    g_idx = jnp.min(jnp.where(gl == gmax, lane, far), axis=1, keepdims=True)
    g_gate = 1.0 / jnp.sum(jnp.exp(gl - gmax), axis=1, keepdims=True)
    lo = N_GROUPS + EXP_PER_GROUP * g_idx
    el = jnp.where(jnp.logical_and(lane >= lo, lane < lo + EXP_PER_GROUP), z, neg)
    v1 = jnp.max(el, axis=1, keepdims=True)
    i1 = jnp.min(jnp.where(el == v1, lane, far), axis=1, keepdims=True)
    el2 = jnp.where(lane == i1, neg, el)
    v2 = jnp.max(el2, axis=1, keepdims=True)
    i2 = jnp.min(jnp.where(el2 == v2, lane, far), axis=1, keepdims=True)
    e21 = jnp.exp(v2 - v1)
    w1 = g_gate / (1.0 + e21)
    w2 = w1 * e21
    j1, j2 = i1 - lo, i2 - lo
    ja, jb = jnp.minimum(j1, j2), jnp.maximum(j1, j2)
    pair = ja * (2 * EXP_PER_GROUP - 1 - ja) * 0.5 + (jb - ja - 1.0)
    bucket = g_idx * N_PAIRS + pair
    first_low = j1 < j2
    wa = jnp.where(first_low, w1, w2)
    wb = jnp.where(first_low, w2, w1)
    onehot = jnp.where(lane == bucket, 1.0, 0.0)
    earlier = _dot(ltri_ref[...], onehot.astype(BF16))
    cnt = cnt_s[...]
    rank = jnp.sum(onehot * (earlier + cnt), axis=1, keepdims=True)
    cnt_new = cnt + jnp.sum(onehot, axis=0, keepdims=True)
    cnt_s[...] = cnt_new
    cnt_ref[...] = cnt_new
    o_ref[:, :d] = out
    e_lo = lo - N_GROUPS + ja
    e_hi = lo - N_GROUPS + jb
    record = (bucket, rank, wa, wb, e_lo, e_hi)
    ext = jnp.zeros((tm, LANE), F32)
    for k, col in enumerate(record):
        ext = jnp.where(lane == float(k), col, ext)
    o_ref[:, d:] = ext


def _route_operands(w_route, b_route, tm):
    n = w_route.shape[1]
    wp = jnp.zeros((w_route.shape[0], LANE), F32).at[:, :n].set(w_route)
    wh = wp.astype(BF16)
    wl = (wp - wh.astype(F32)).astype(BF16)
    bp = jnp.zeros((1, LANE), F32).at[0, :n].set(b_route)
    ltri = jnp.asarray(np.tril(np.ones((tm, tm), np.float32), -1), BF16)
    return jnp.concatenate([wh, wl], 1), bp, ltri


def _route_specs(k, tm):
    return [_whole((k, 2 * LANE)), _whole((1, LANE)), _whole((tm, tm))]


def _ab_out_kernel(fo_ref, mh_ref, mo_ref, x_ref, gml_ref, w_ref, g_ref, b_ref, wr_ref, br_ref, ltri_ref,
                   o_ref, cnt_ref, cnt_s):
    parts = [fo_ref[...]]
    for h in range(H_ML):
        sl = slice(h * DH_ML, (h + 1) * DH_ML)
        mh = mh_ref[:, sl].astype(F32)
        ms = jnp.mean(mh * mh, axis=1, keepdims=True)
        nm = mh * lax.rsqrt(ms + RMS_EPS) * gml_ref[:, sl]
        parts.append((nm * _sigmoid(mo_ref[:, sl].astype(F32))).astype(BF16))
    cat = jnp.concatenate(parts, axis=1)
    y = ALPHA * x_ref[...] + _dot(cat, w_ref[...])
    out = _layer_norm(y, g_ref[...], b_ref[...])
    _route_store(out, wr_ref, br_ref, ltri_ref, o_ref, cnt_ref, cnt_s)


def _rows(tm, n):
    return pl.BlockSpec((tm, n), lambda i: (i, 0))


def _whole(shape):
    return pl.BlockSpec(shape, lambda i: (0,) * len(shape))


def _routed_out_call(kern, name, t, d, tm, in_specs, operands, route_w):
    w_route, b_route = route_w
    return pl.pallas_call(
        kern,
        grid=(t // tm,),
        in_specs=in_specs + _route_specs(d, tm),
        out_specs=[_rows(tm, d + EXT), _whole((1, LANE))],
        out_shape=[jax.ShapeDtypeStruct((t, d + EXT), F32), jax.ShapeDtypeStruct((1, LANE), F32)],
        scratch_shapes=[pltpu.VMEM((1, LANE), F32)],
        compiler_params=_params("arbitrary"),
        name=name,
    )(*operands, *_route_operands(w_route, b_route, tm))


def ab_out(fo, mh, mo, x, g_ml, w_out, ln_g, ln_b, route_w):
    t, d = x.shape
    tm = _row_tile(t)
    in_specs = [_rows(tm, W_FOX), _rows(tm, W_ML), _rows(tm, W_ML), _rows(tm, d), _whole((1, W_ML)),
                _whole(w_out.shape), _whole((1, d)), _whole((1, d))]
    return _routed_out_call(_ab_out_kernel, "ab_out", t, d, tm, in_specs,
                            (fo, mh, mo, x, g_ml, w_out, ln_g, ln_b), route_w)


def _rms(z, g):
    return z * lax.rsqrt(jnp.mean(z * z, axis=1, keepdims=True) + RMS_EPS) * g


def _mla_down_kernel(x_ref, w_ref, gq_ref, gkv_ref, cc_ref, ss_ref, *rest, n_prev):
    prev_c, prev_r = rest[:n_prev], rest[n_prev:2 * n_prev]
    cq_ref, ckv_ref, kr_ref, ckvb_ref, kr2_ref = rest[2 * n_prev:]

    def put(state_ref, prev, value):
        if n_prev:
            for j, p in enumerate(prev):
                state_ref[j] = p[...]
            state_ref[n_prev] = value
        else:
            state_ref[...] = value

    xb = x_ref[...].astype(BF16)
    cq_ref[...] = _rms(_dot(xb, w_ref[:, 0:Q_RANK]), gq_ref[...]).astype(BF16)
    ckv = _rms(_dot(xb, w_ref[:, Q_RANK:Q_RANK + KV_RANK]), gkv_ref[...])
    put(ckv_ref, prev_c, ckv)
    ckvb_ref[...] = ckv.astype(BF16)
    c0 = Q_RANK + KV_RANK
    rope = (_dot(xb, w_ref[:, c0:c0 + LANE]) * cc_ref[...] + _dot(xb, w_ref[:, c0 + LANE:c0 + 2 * LANE]) * ss_ref[...])
    put(kr_ref, prev_r, rope[:, :ROPE])
    kr2_ref[...] = rope.astype(BF16)


def _table_spec(table, tm, s):
    if table.shape[0] == tm:
        return pl.BlockSpec((tm, LANE), lambda i: (0, 0))
    per = s // tm
    return pl.BlockSpec((tm, LANE), lambda i: (i % per, 0))


def mla_down(x, w, g_q, g_kv, cc, ss, s, prev_c=(), prev_r=()):
    t, d = x.shape
    tm = _row_tile(t)
    tab = _table_spec(cc, tm, s)
    n_prev = len(prev_c)

    def state(width):
        if n_prev:
            return (pl.BlockSpec((n_prev + 1, tm, width), lambda i: (0, i, 0)),
                    jax.ShapeDtypeStruct((n_prev + 1, t, width), F32))
        return _rows(tm, width), jax.ShapeDtypeStruct((t, width), F32)

    (c_spec, c_shape), (r_spec, r_shape) = state(KV_RANK), state(ROPE)
    return pl.pallas_call(
        functools.partial(_mla_down_kernel, n_prev=n_prev),
        grid=(t // tm,),
        in_specs=[_rows(tm, d), _whole(w.shape), _whole((1, Q_RANK)), _whole((1, KV_RANK)), tab, tab]
        + [_rows(tm, KV_RANK)] * n_prev + [_rows(tm, ROPE)] * n_prev,
        out_specs=[_rows(tm, Q_RANK), c_spec, r_spec, _rows(tm, KV_RANK), _rows(tm, LANE)],
        out_shape=[jax.ShapeDtypeStruct((t, Q_RANK), BF16), c_shape, r_shape,
                   jax.ShapeDtypeStruct((t, KV_RANK), BF16), jax.ShapeDtypeStruct((t, LANE), BF16)],
        compiler_params=_params("parallel"),
        name="mla_down",
    )(x, w, g_q, g_kv, cc, ss, *prev_c, *prev_r)


def _mla_uq_kernel(cq_ref, w_ref, wuk_ref, cc_ref, ss_ref, qlat_ref, qrope_ref):
    cq = cq_ref[...]
    scale = (NOPE + ROPE) ** -0.5 * LOG2E
    r0 = H_MLA * NOPE
    r1 = r0 + H_MLA * ROPE
    qn = _dot(cq, w_ref[:, :r0]).astype(BF16)
    for h in range(H_MLA):
        qlat_ref[:, h * KV_RANK:(h + 1) * KV_RANK] = (
            _dot(qn[:, h * NOPE:(h + 1) * NOPE], wuk_ref[h]) * scale).astype(BF16)
    qr = _dot(cq, w_ref[:, r0:])
    cc, ss = cc_ref[...] * scale, ss_ref[...] * scale
    for p in range(H_MLA // 2):
        a, b = p * LANE, (r1 - r0) + p * LANE
        qrope_ref[:, p * LANE:(p + 1) * LANE] = (qr[:, a:a + LANE] * cc + qr[:, b:b + LANE] * ss).astype(BF16)


def mla_uq(cq, w, wuk_t, cc, ss, s):
    t = cq.shape[0]
    tm = _row_tile(t)
    tab = _table_spec(cc, tm, s)
    return pl.pallas_call(
        _mla_uq_kernel,
        grid=(t // tm,),
        in_specs=[_rows(tm, Q_RANK), _whole(w.shape), _whole(wuk_t.shape), tab, tab],
        out_specs=[_rows(tm, H_MLA * KV_RANK), _rows(tm, H_MLA * ROPE)],
        out_shape=[jax.ShapeDtypeStruct((t, H_MLA * KV_RANK), BF16), jax.ShapeDtypeStruct((t, H_MLA * ROPE), BF16)],
        compiler_params=_params("parallel"),
        name="mla_uq",
    )(cq, w, wuk_t, cc, ss)


def _mla_attn_kernel(ql_ref, qr_ref, kc_ref, kr_ref, *rest, tq, tk, q_off):
    pc_ref, pr_ref = rest[:2] if len(rest) == 8 else (kc_ref, kr_ref)
    o_ref, ql_s, qr_s, m_s, l_s, acc_s = rest[-6:]
    qi = pl.program_id(1)
    lane = lax.broadcasted_iota(jnp.int32, (tq, LANE), 1)
    for h in range(H_MLA):
        rows = slice(h * tq, (h + 1) * tq)
        ql_s[rows, :] = ql_ref[0, :, h * KV_RANK:(h + 1) * KV_RANK]
        qr = qr_ref[0, :, (h // 2) * LANE:(h // 2 + 1) * LANE]
        keep = (lane < ROPE) if h % 2 == 0 else (lane >= ROPE)
        qr_s[rows, :] = jnp.where(keep, qr, jnp.zeros_like(qr))
    m_s[...] = jnp.full(m_s.shape, NEG_BIG, F32)
    l_s[...] = jnp.zeros(l_s.shape, F32)
    acc_s[...] = jnp.zeros(acc_s.shape, F32)
    q_start = q_off + qi * tq

    def update(c_src, r_src, off, w, mask):
        kc = c_src[0, pl.ds(off, w), :]
        kr = r_src[0, pl.ds(off, w), :]
        s = _dot_nt(ql_s[...], kc) + _dot_nt(qr_s[...], kr)
        if mask is not None:
            s = jnp.where(mask, s, -jnp.inf)
        _softmax_step(s, Ellipsis, m_s, l_s, acc_s, kc)

    def full_tile(j, carry):
        update(pc_ref, pr_ref, pl.multiple_of(j * tk, tk), tk, None)
        return carry

    lax.fori_loop(0, q_start // tk, full_tile, 0)
    r = lax.broadcasted_iota(jnp.int32, (H_MLA * tq, tq), 0)
    c = lax.broadcasted_iota(jnp.int32, (H_MLA * tq, tq), 1)
    update(kc_ref, kr_ref, pl.multiple_of(qi * tq, tq), tq, (c // CHUNK) <= ((r & (tq - 1)) // CHUNK))
    for h in range(H_MLA):
        rows = slice(h * tq, (h + 1) * tq)
        o = acc_s[rows, :] / _lanes(l_s[rows, :], KV_RANK)
        o_ref[0, :, h * KV_RANK:(h + 1) * KV_RANK] = o.astype(o_ref.dtype)


def _mla_attn_pipelined_kernel(ql_ref, qr_ref, kc_ref, kr_ref, o_ref, ql_s, qr_s, m_s, l_s, acc_s, s_buf, *, tq):
    qi = pl.program_id(1)
    lane = lax.broadcasted_iota(jnp.int32, (tq, LANE), 1)
    for h in range(H_MLA):
        rows = slice(h * tq, (h + 1) * tq)
        ql_s[rows, :] = ql_ref[0, :, h * KV_RANK:(h + 1) * KV_RANK]
        qr = qr_ref[0, :, (h // 2) * LANE:(h // 2 + 1) * LANE]
        keep = (lane < ROPE) if h % 2 == 0 else (lane >= ROPE)
        qr_s[rows, :] = jnp.where(keep, qr, jnp.zeros_like(qr))
    m_s[...] = jnp.full(m_s.shape, NEG_BIG, F32)
    l_s[...] = jnp.zeros(l_s.shape, F32)
    acc_s[...] = jnp.zeros(acc_s.shape, F32)

    def keys(ref, j):
        return ref[0, pl.ds(pl.multiple_of(j * tq, tq), tq), :]

    def score(j, slot):
        s_buf[slot] = _dot_nt(ql_s[...], keys(kc_ref, j)) + _dot_nt(qr_s[...], keys(kr_ref, j))

    def consume(j, slot, mask):
        s = s_buf[slot]
        if mask is not None:
            s = jnp.where(mask, s, -jnp.inf)
        _softmax_step(s, Ellipsis, m_s, l_s, acc_s, keys(kc_ref, j))

    score(0, 0)

    def two_tiles(ip, carry):
        j = 2 * ip
        score(j + 1, 1)
        consume(j, 0, None)
        score(j + 2, 0)
        consume(j + 1, 1, None)
        return carry

    lax.fori_loop(0, qi // 2, two_tiles, 0)
    r = lax.broadcasted_iota(jnp.int32, (H_MLA * tq, tq), 0)
    c = lax.broadcasted_iota(jnp.int32, (H_MLA * tq, tq), 1)
    mask = (c // CHUNK) <= ((r & (tq - 1)) // CHUNK)

    @pl.when(qi % 2 == 1)
    def _():
        score(qi, 1)
        consume(qi - 1, 0, None)
        consume(qi, 1, mask)

    @pl.when(qi % 2 == 0)
    def _():
        consume(qi, 0, mask)

    for h in range(H_MLA):
        rows = slice(h * tq, (h + 1) * tq)
        o = acc_s[rows, :] / _lanes(l_s[rows, :], KV_RANK)
        o_ref[0, :, h * KV_RANK:(h + 1) * KV_RANK] = o.astype(o_ref.dtype)


def mla_attention(q_lat, q_rope, kc, kr2, *, tq, tk, past=None):
    b, lq, _ = q_lat.shape
    q_off = 0 if past is None else past[0].shape[1]
    assert kc.shape[1] == lq and q_off % tk == 0 and (tq % tk == 0 or lq == tq)
    assert tq % CHUNK == 0 and (tq & (tq - 1)) == 0
    scratch = [pltpu.VMEM((H_MLA * tq, KV_RANK), BF16), pltpu.VMEM((H_MLA * tq, LANE), BF16),
               pltpu.VMEM((H_MLA * tq, LANE), F32), pltpu.VMEM((H_MLA * tq, LANE), F32),
               pltpu.VMEM((H_MLA * tq, KV_RANK), F32)]
    if past is None and tk == tq:
        kern = functools.partial(_mla_attn_pipelined_kernel, tq=tq)
        scratch.append(pltpu.VMEM((2, H_MLA * tq, tq), F32))
    else:
        kern = functools.partial(_mla_attn_kernel, tq=tq, tk=tk, q_off=q_off)
    whole_seq = [pl.BlockSpec((1,) + a.shape[1:], lambda bi, qi: (bi, 0, 0)) for a in (kc, kr2) + tuple(past or ())]
    return pl.pallas_call(
        kern,
        grid=(b, lq // tq),
        in_specs=[
            pl.BlockSpec((1, tq, H_MLA * KV_RANK), lambda bi, qi: (bi, qi, 0)),
            pl.BlockSpec((1, tq, H_MLA * ROPE), lambda bi, qi: (bi, qi, 0)),
        ] + whole_seq,
        out_specs=pl.BlockSpec((1, tq, H_MLA * KV_RANK), lambda bi, qi: (bi, qi, 0)),
        out_shape=jax.ShapeDtypeStruct((b, lq, H_MLA * KV_RANK), BF16),
        scratch_shapes=scratch,
        compiler_params=_params("parallel", "arbitrary"),
        name="mla_attention",
    )(q_lat, q_rope, kc, kr2, *(past or ()))


def _mla_out_kernel(ol_ref, wuv_ref, w_ref, x_ref, g_ref, b_ref, wr_ref, br_ref, ltri_ref,
                    o_ref, cnt_ref, cnt_s):
    parts = [_dot(ol_ref[:, h * KV_RANK:(h + 1) * KV_RANK], wuv_ref[h]).astype(BF16) for h in range(H_MLA)]
    y = ALPHA * x_ref[...] + _dot(jnp.concatenate(parts, axis=1), w_ref[...])
    out = _layer_norm(y, g_ref[...], b_ref[...])
    _route_store(out, wr_ref, br_ref, ltri_ref, o_ref, cnt_ref, cnt_s)


def mla_out(o_lat, wuv, w_out, x, ln_g, ln_b, route_w):
    t, d = x.shape
    tm = _row_tile(t)
    in_specs = [_rows(tm, H_MLA * KV_RANK), _whole(wuv.shape), _whole(w_out.shape), _rows(tm, d),
                _whole((1, d)), _whole((1, d))]
    return _routed_out_call(_mla_out_kernel, "mla_out", t, d, tm, in_specs,
                            (o_lat, wuv, w_out, x, ln_g, ln_b), route_w)


ROW_SLOTS = 3


def _row_scatter_kernel(idx_ref, src_ref, dst_init_ref, dst_ref, buf, in_sem, out_sem, *, rows, n_steps):
    del dst_init_ref

    def load(i, slot):
        return pltpu.make_async_copy(src_ref.at[pl.ds(pl.multiple_of(i * rows, rows), rows)], buf.at[slot],
                                     in_sem.at[slot])

    def rows_done(slot):
        return pltpu.make_async_copy(buf.at[slot], dst_ref.at[pl.ds(0, rows)], out_sem.at[slot])

    i = pl.program_id(0)
    slot = i % ROW_SLOTS
    nxt = (i + 1) % ROW_SLOTS

    @pl.when(i == 0)
    def _():
        load(0, 0).start()

    @pl.when(i + 1 < n_steps)
    def _():
        @pl.when(i + 1 >= ROW_SLOTS)
        def _():
            rows_done(nxt).wait()
        load(i + 1, nxt).start()

    load(i, slot).wait()
    base = i * rows

    for r in range(rows):
        pltpu.make_async_copy(buf.at[slot, pl.ds(r, 1)], dst_ref.at[pl.ds(idx_ref[base + r], 1)],
                              out_sem.at[slot]).start()

    @pl.when(i == n_steps - 1)
    def _():
        for k in range(min(ROW_SLOTS, n_steps)):
            rows_done((n_steps - 1 - k) % ROW_SLOTS).wait()


def _row_gather_kernel(idx_ref, src_ref, dst_ref, buf, row_sem, out_sem, *, rows, n_steps):
    def rows_done(slot):
        return pltpu.make_async_copy(src_ref.at[pl.ds(0, rows)], buf.at[slot], row_sem.at[slot])

    def store(i, slot):
        return pltpu.make_async_copy(buf.at[slot], dst_ref.at[pl.ds(pl.multiple_of(i * rows, rows), rows)],
                                     out_sem.at[slot])

    i = pl.program_id(0)
    slot = i % ROW_SLOTS

    @pl.when(i >= ROW_SLOTS)
    def _():
        store(i - ROW_SLOTS, slot).wait()

    base = i * rows

    for r in range(rows):
        pltpu.make_async_copy(src_ref.at[pl.ds(idx_ref[base + r], 1)], buf.at[slot, pl.ds(r, 1)],
                              row_sem.at[slot]).start()

    @pl.when(i >= 1)
    def _():
        prev = (i + ROW_SLOTS - 1) % ROW_SLOTS
        rows_done(prev).wait()
        store(i - 1, prev).start()

    @pl.when(i == n_steps - 1)
    def _():
        last = (n_steps - 1) % ROW_SLOTS
        rows_done(last).wait()
        store(n_steps - 1, last).start()
        for k in range(min(ROW_SLOTS, n_steps)):
            store(n_steps - 1 - k, (n_steps - 1 - k) % ROW_SLOTS).wait()


def row_move(idx, src, dst_rows, *, scatter, dst_init=None):
    t = idx.shape[0]
    width = src.shape[1]
    rows = _row_tile(t)
    any_spec = pl.BlockSpec(memory_space=pl.ANY)
    operands = [idx, src] + ([dst_init] if scatter else [])
    grid_spec = pltpu.PrefetchScalarGridSpec(
        num_scalar_prefetch=1, grid=(t // rows,), in_specs=[any_spec] * (len(operands) - 1), out_specs=any_spec,
        scratch_shapes=[pltpu.VMEM((ROW_SLOTS, rows, width), src.dtype),
                        pltpu.SemaphoreType.DMA((ROW_SLOTS,)), pltpu.SemaphoreType.DMA((ROW_SLOTS,))])
    kern = _row_scatter_kernel if scatter else _row_gather_kernel
    return pl.pallas_call(
        functools.partial(kern, rows=rows, n_steps=t // rows),
        grid_spec=grid_spec,
        out_shape=jax.ShapeDtypeStruct((dst_rows, width), src.dtype),
        input_output_aliases={2: 0} if scatter else {},
        compiler_params=_params("arbitrary"),
        name="row_scatter" if scatter else "row_gather",
    )(*operands)


def _moe_kernel(ta_ref, tb_ref, nu_ref, xs_ref, w1a_ref, w3a_ref, w2a_ref, w1b_ref, w3b_ref, w2b_ref,
                g_ref, b_ref, o_ref, pend):
    t = pl.program_id(0)
    d = o_ref.shape[1]
    n_used = nu_ref[0]

    @pl.when(t == 0)
    def _():
        pend[...] = jnp.zeros(pend.shape, F32)

    @pl.when(t < n_used)
    def _():
        o_ref[...] = _layer_norm(pend[...], g_ref[...], b_ref[...])
        x = xs_ref[:, :d]
        ext = xs_ref[:, d:]
        xb = x.astype(BF16)

        def ffn(w1_ref, w3_ref, w2_ref):
            h1 = _dot(xb, w1_ref[0])
            h3 = _dot(xb, w3_ref[0])
            return _dot((h1 * _sigmoid(h1) * h3).astype(BF16), w2_ref[0])

        y = ext[:, 2:3] * ffn(w1a_ref, w3a_ref, w2a_ref) + ext[:, 3:4] * ffn(w1b_ref, w3b_ref, w2b_ref)
        pend[...] = ALPHA * x + y

    @pl.when(t == n_used)
    def _():
        o_ref[...] = _layer_norm(pend[...], g_ref[...], b_ref[...])

    @pl.when(t > n_used)
    def _():
        o_ref[...] = jnp.zeros(o_ref.shape, o_ref.dtype)


def moe_ln(tile_ea, tile_eb, n_used, xs, w1, w3, w2, ln_g, ln_b, tm):
    p, wide = xs.shape
    d = wide - EXT
    de = w1.shape[2]
    n = p // tm

    def w_spec(shape, which):
        return pl.BlockSpec(shape, lambda t, ta, tb, nu: ((ta, tb)[which][jnp.minimum(t, n - 1)], 0, 0))

    grid_spec = pltpu.PrefetchScalarGridSpec(
        num_scalar_prefetch=3,
        grid=(n + 1,),
        in_specs=[pl.BlockSpec((tm, wide), lambda t, ta, tb, nu: (jnp.minimum(t, n - 1), 0)),
                  w_spec((1, d, de), 0), w_spec((1, d, de), 0), w_spec((1, de, d), 0),
                  w_spec((1, d, de), 1), w_spec((1, d, de), 1), w_spec((1, de, d), 1),
                  pl.BlockSpec((1, d), lambda t, ta, tb, nu: (0, 0)),
                  pl.BlockSpec((1, d), lambda t, ta, tb, nu: (0, 0))],
        out_specs=pl.BlockSpec((tm, d), lambda t, ta, tb, nu: (jnp.maximum(t - 1, 0), 0)),
        scratch_shapes=[pltpu.VMEM((tm, d), F32)],
    )
    return pl.pallas_call(
        _moe_kernel,
        grid_spec=grid_spec,
        out_shape=jax.ShapeDtypeStruct((p, d), F32),
        compiler_params=_params("arbitrary"),
        name="moe_ln",
    )(tile_ea, tile_eb, n_used, xs, w1, w3, w2, w1, w3, w2, ln_g, ln_b)


def _moe_few_kernel(x_ref, w1_ref, w3_ref, w2_ref, g_ref, b_ref, o_ref, acc_s):
    e = pl.program_id(0)
    d = o_ref.shape[1]

    @pl.when(e == 0)
    def _():
        acc_s[...] = jnp.zeros(acc_s.shape, F32)

    x = x_ref[:, :d]
    ext = x_ref[:, d:]
    xb = x.astype(BF16)
    h1 = _dot(xb, w1_ref[0])
    h3 = _dot(xb, w3_ref[0])
    y = _dot((h1 * _sigmoid(h1) * h3).astype(BF16), w2_ref[0])
    ef = e.astype(F32)
    gate = jnp.where(ext[:, 4:5] == ef, ext[:, 2:3], 0.0) + jnp.where(ext[:, 5:6] == ef, ext[:, 3:4], 0.0)
    acc_s[...] += gate * y

    @pl.when(e == pl.num_programs(0) - 1)
    def _():
        o_ref[...] = _layer_norm(ALPHA * x + acc_s[...], g_ref[...], b_ref[...])


def moe_ln_few(x1ext, w1, w3, w2, ln_g, ln_b):
    t, wide = x1ext.shape
    d = wide - EXT
    de = w1.shape[2]
    return pl.pallas_call(
        _moe_few_kernel,
        grid=(w1.shape[0],),
        in_specs=[_whole((t, wide)), pl.BlockSpec((1, d, de), lambda e: (e, 0, 0)),
                  pl.BlockSpec((1, d, de), lambda e: (e, 0, 0)), pl.BlockSpec((1, de, d), lambda e: (e, 0, 0)),
                  _whole((1, d)), _whole((1, d))],
        out_specs=_whole((t, d)),
        out_shape=jax.ShapeDtypeStruct((t, d), F32),
        scratch_shapes=[pltpu.VMEM((t, d), F32)],
        compiler_params=_params("arbitrary"),
        name="moe_ln_few",
    )(x1ext, w1, w3, w2, ln_g, ln_b)


def _bucket_experts():
    ea, eb = [], []
    for g in range(N_GROUPS):
        for ja in range(EXP_PER_GROUP):
            for jb in range(ja + 1, EXP_PER_GROUP):
                ea.append(g * EXP_PER_GROUP + ja)
                eb.append(g * EXP_PER_GROUP + jb)
    return np.asarray(ea, np.int32), np.asarray(eb, np.int32)


def _bucket_layout(x1ext, counts, tm):
    t, wide = x1ext.shape
    d = wide - EXT
    bucket = x1ext[:, d].astype(jnp.int32)
    rank = x1ext[:, d + 1].astype(jnp.int32)
    counts = counts[0, :N_BUCKETS].astype(jnp.int32)
    padded = ((counts + tm - 1) // tm) * tm
    row_end = jnp.cumsum(padded)
    row_start = row_end - padded
    ids = jnp.arange(N_BUCKETS, dtype=jnp.int32)
    pos = rank + jnp.sum(jnp.where(bucket[:, None] == ids[None, :], row_start[None, :], 0), axis=1)
    n_rows = _round_up(t, tm) + N_BUCKETS * tm
    n_tiles = n_rows // tm
    n_used = (row_end[-1] // tm).astype(jnp.int32)
    tile_row = jnp.minimum(jnp.arange(n_tiles, dtype=jnp.int32), n_used - 1) * tm
    tile_bucket = jnp.sum((row_end[None, :] <= tile_row[:, None]).astype(jnp.int32), axis=1)
    ea, eb = _bucket_experts()
    in_bucket = tile_bucket[:, None] == ids[None, :]
    tile_ea = jnp.sum(jnp.where(in_bucket, jnp.asarray(ea)[None, :], 0), axis=1)
    tile_eb = jnp.sum(jnp.where(in_bucket, jnp.asarray(eb)[None, :], 0), axis=1)
    return pos.astype(jnp.int32), tile_ea, tile_eb, n_used.reshape(1), n_rows


def hier_moe_ln(x1ext, counts, moe_w, ln_g, ln_b, sorted_buf=None):
    w1, w3, w2 = moe_w
    t, wide = x1ext.shape
    if t <= FEW_TOKENS:
        return moe_ln_few(x1ext, w1, w3, w2, ln_g, ln_b), None
    tm = MOE_TILE
    pos, tile_ea, tile_eb, n_used, n_rows = _bucket_layout(x1ext, counts, tm)
    if sorted_buf is None:
        sorted_buf = jnp.zeros((n_rows, wide), F32)
    xs = row_move(pos, x1ext, n_rows, scatter=True, dst_init=sorted_buf)
    ys = moe_ln(tile_ea, tile_eb, n_used, xs, w1, w3, w2, ln_g, ln_b, tm)
    return row_move(pos, ys, t, scatter=False), xs


def _rope_tables(s, offset, tm):
    half = ROPE // 2
    inv_freq = ROPE_BASE ** (-jnp.arange(half, dtype=F32) / half)
    ang = (offset + jnp.arange(s)).astype(F32)[:, None] * inv_freq[None, :]
    cos, sin = jnp.cos(ang), jnp.sin(ang)
    cc = jnp.concatenate([cos, cos, cos, cos], -1)
    ss = jnp.concatenate([-sin, sin, -sin, sin], -1)
    if tm > s:
        cc, ss = jnp.tile(cc, (tm // s, 1)), jnp.tile(ss, (tm // s, 1))
    return cc, ss


def _cast_kernel(x_ref, o_ref):
    o_ref[...] = x_ref[0].astype(o_ref.dtype)


def cast_experts(w, layer):
    _, e, r, c = w.shape
    return pl.pallas_call(
        _cast_kernel, grid=(e,),
        in_specs=[pl.BlockSpec((1, 1, r, c), lambda i: (layer, i, 0, 0))],
        out_specs=pl.BlockSpec((1, r, c), lambda i: (i, 0, 0)),
        out_shape=jax.ShapeDtypeStruct((e, r, c), BF16), compiler_params=_params("parallel"), name="cast_experts",
    )(w)


def _round_up(n, m):
    return (n + m - 1) // m * m


def ab_layer(x, b, s, wts, route_w, cache):
    w_big, w_gate, b_gate, g_ml, w_out, ln_g, ln_b = wts
    t = b * s
    ls_cols = np.array([1.0] * H_FOX + [0.0] * H_ML + [1.0] * H_ML)
    fq, mq, mk, mv, mo, fk_bf, fv_bf, gates, fk, fv = ab_proj(x, w_big, w_gate, b_gate, ls_cols)
    flf = gates[:, :H_FOX].reshape(b, s, H_FOX)
    ml_rows = jnp.swapaxes(gates[:, H_FOX:H_FOX + 2 * H_ML].reshape(b, s, 2 * H_ML), 1, 2)
    ml_rows = jnp.pad(ml_rows, ((0, 0), (0, BF16_SUBLANES - 2 * H_ML), (0, 0)))
    if cache is None:
        past_kv, lf_all = None, flf
        c0 = jnp.zeros((b, H_ML, DH_ML, DH_ML), F32)
        n0 = jnp.zeros((b, H_ML, 1, DH_ML), F32)
        m0 = jnp.zeros((b, H_ML, 1, 1), F32)
        tq, chunk = min(FOX_TILE, s), min(MLSTM_CHUNK, s)
    else:
        ck, cv, clf, c0, n0, m0 = cache
        past = ck.shape[1]
        past_kv = (ck.reshape(b, past, W_FOX).astype(BF16), cv.reshape(b, past, W_FOX).astype(BF16))
        lf_all = jnp.concatenate([clf, flf], 1)
        n0 = n0.reshape(b, H_ML, 1, DH_ML)
        m0 = m0.reshape(b, H_ML, 1, 1)
        tq, chunk = s, s
    lk = lf_all.shape[1]
    lf_rows = jnp.pad(jnp.swapaxes(lf_all, 1, 2),
                      ((0, 0), (0, BF16_SUBLANES - H_FOX), (0, _round_up(lk, PAST_KV_TILE) - lk)))
    fo = fox_attention(fq.reshape(b, s, W_FOX), fk_bf.reshape(b, s, W_FOX), fv_bf.reshape(b, s, W_FOX), lf_rows,
                       tq=tq, past=past_kv, tk=PAST_KV_TILE if cache is not None else tq)
    mh, c_new, n_new, m_new = mlstm(mq.reshape(b, s, W_ML), mk.reshape(b, s, W_ML), mv.reshape(b, s, W_ML),
                                    ml_rows, c0, n0, m0, chunk=chunk)
    x1ext, counts = ab_out(fo.reshape(t, W_FOX), mh.reshape(t, W_ML), mo, x, g_ml, w_out, ln_g, ln_b, route_w)
    state = (fk, fv, flf, c_new, n_new.reshape(b, H_ML, DH_ML), m_new.reshape(b, H_ML))
    return x1ext, counts, state


def mla_layer(x, b, s, wts, route_w, cache, prev_c=(), prev_r=()):
    w_down, g_q, g_kv, w_uq, wuk_t, wuv, w_out, ln_g, ln_b = wts
    t = b * s
    tm = _row_tile(t)
    past = 0 if cache is None else cache[0].shape[1]
    cc, ss = _rope_tables(s, past, tm)
    cq, ckv, kr, ckv_bf, kr2_bf = mla_down(x, w_down, g_q, g_kv, cc, ss, s, prev_c, prev_r)
    q_lat, q_rope = mla_uq(cq, w_uq, wuk_t, cc, ss, s)
    kc = ckv_bf.reshape(b, s, KV_RANK)
    kr2 = kr2_bf.reshape(b, s, LANE)
    if cache is None:
        tq, past_k = min(MLA_TILE, s), None
    else:
        c_ckv, c_kr = cache
        c_kr_bf = c_kr.astype(BF16)
        tq, past_k = s, (c_ckv.astype(BF16), jnp.concatenate([c_kr_bf, c_kr_bf], -1))
    o_lat = mla_attention(q_lat.reshape(b, s, -1), q_rope.reshape(b, s, -1), kc, kr2, tq=tq,
                          tk=tq if cache is None else PAST_KV_TILE, past=past_k)
    x1ext, counts = mla_out(o_lat.reshape(t, -1), wuv, w_out, x, ln_g, ln_b, route_w)
    return x1ext, counts, (ckv, kr)


def _prep_ab_weights(w_in, b_fox_f, b_ml_i, b_ml_f, g_ml, w_out, ln_g, ln_b):
    sizes = (W_FOX, W_FOX, W_FOX, H_FOX, W_ML, W_ML, W_ML, H_ML, H_ML, W_ML)
    idx = np.cumsum(sizes[:-1]).tolist()
    fq, fk, fv, ff, mq, mk, mv, mi, mf, mo = jnp.split(w_in, idx, axis=1)
    w_big = jnp.concatenate([fq * (DH_FOX ** -0.5 * LOG2E), mq, mk, mv, mo, fk, fv], 1).astype(BF16)
    w_gate = jnp.concatenate([ff, mi, mf], 1)
    b_gate = jnp.concatenate([b_fox_f, b_ml_i, b_ml_f])
    return (w_big, w_gate, b_gate, g_ml[None, :], w_out.astype(BF16), ln_g[None, :], ln_b[None, :])


def _prep_mla_weights(w_down, g_q, w_uq, g_kv, w_uk, w_uv, w_out, ln_g, ln_b):
    half = ROPE // 2
    cq_w, ckv_w, kr_w = jnp.split(w_down, [Q_RANK, Q_RANK + KV_RANK], axis=1)
    kr_sw = jnp.concatenate([kr_w[:, half:], kr_w[:, :half]], 1)
    w_down_p = jnp.concatenate([cq_w, ckv_w, kr_w, kr_w, kr_sw, kr_sw], 1).astype(BF16)
    uq = w_uq.reshape(Q_RANK, H_MLA, NOPE + ROPE)
    uq_nope = uq[:, :, :NOPE].reshape(Q_RANK, H_MLA * NOPE)
    uq_rope = uq[:, :, NOPE:]
    uq_rope_sw = jnp.concatenate([uq_rope[..., half:], uq_rope[..., :half]], -1)
    w_uq_p = jnp.concatenate([uq_nope, uq_rope.reshape(Q_RANK, -1), uq_rope_sw.reshape(Q_RANK, -1)], 1).astype(BF16)
    wuk_t = jnp.transpose(w_uk, (1, 2, 0)).astype(BF16)
    wuv = jnp.transpose(w_uv, (1, 0, 2)).astype(BF16)
    return (w_down_p, g_q[None, :], g_kv[None, :], w_uq_p, wuk_t, wuv, w_out.astype(BF16), ln_g[None, :], ln_b[None, :])


def kernel(x_prompt, x_sample, cache_fox_k, cache_fox_v, cache_fox_logf, state_mlstm_c, state_mlstm_n, state_mlstm_m, cache_mla_ckv, cache_mla_krope, w_ab_in, b_fox_f, b_mlstm_i, b_mlstm_f, g_mlstm_norm, w_ab_out, w_mla_down, g_mla_q, w_mla_uq, g_mla_kv, w_mla_uk, w_mla_uv, w_mla_out, ln1_g, ln1_b, ln2_g, ln2_b, w_moe_group, b_moe_group, w_moe_router, b_moe_router, w_exp_gate, w_exp_up, w_exp_down):
    bp, sp, d = x_prompt.shape
    bs, ss_, _ = x_sample.shape
    xp = x_prompt.reshape(bp * sp, d)
    xs = x_sample.reshape(bs * ss_, d)
    ab_p, ab_s, c_p, c_s = [], [], [], []
    n_c = DEPTH // 2
    assert n_c >= 2
    buf_p = buf_s = None
    for l in range(DEPTH):
        j = l // 2
        route_w = (jnp.concatenate([w_moe_group[l], w_moe_router[l]], 1),
                   jnp.concatenate([b_moe_group[l], b_moe_router[l]]))
        if l % 2 == 0:
            wts = _prep_ab_weights(w_ab_in[j], b_fox_f[j], b_mlstm_i[j], b_mlstm_f[j], g_mlstm_norm[j],
                                   w_ab_out[j], ln1_g[l], ln1_b[l])
            xp1, cnt_p, st_p = ab_layer(xp, bp, sp, wts, route_w, None)
            xs1, cnt_s, st_s = ab_layer(xs, bs, ss_, wts, route_w,
                                        (cache_fox_k[j], cache_fox_v[j], cache_fox_logf[j],
                                         state_mlstm_c[j], state_mlstm_n[j], state_mlstm_m[j]))
            ab_p.append(st_p)
            ab_s.append(st_s)
        else:
            wts = _prep_mla_weights(w_mla_down[j], g_mla_q[j], w_mla_uq[j], g_mla_kv[j], w_mla_uk[j], w_mla_uv[j],
                                    w_mla_out[j], ln1_g[l], ln1_b[l])
            last = j == n_c - 1
            prev = lambda sts, i: tuple(st[i] for st in sts) if last else ()
            xp1, cnt_p, st_p = mla_layer(xp, bp, sp, wts, route_w, None, prev(c_p, 0), prev(c_p, 1))
            xs1, cnt_s, st_s = mla_layer(xs, bs, ss_, wts, route_w, (cache_mla_ckv[j], cache_mla_krope[j]),
                                         prev(c_s, 0), prev(c_s, 1))
            c_p.append(st_p)
            c_s.append(st_s)
        moe_w = (cast_experts(w_exp_gate, l), cast_experts(w_exp_up, l), cast_experts(w_exp_down, l))
        xp, buf_p = hier_moe_ln(xp1, cnt_p, moe_w, ln2_g[l][None, :], ln2_b[l][None, :], buf_p)
        xs, buf_s = hier_moe_ln(xs1, cnt_s, moe_w, ln2_g[l][None, :], ln2_b[l][None, :], buf_s)

    def stack(groups, i):
        return jnp.stack([g[i] for g in groups])

    def states(ab, c, b, s):
        kv = [jnp.stack([st[i].reshape(b, s, H_FOX, DH_FOX) for st in ab]) for i in (0, 1)]
        lat = [c[-1][0].reshape(n_c, b, s, KV_RANK), c[-1][1].reshape(n_c, b, s, ROPE)]
        return tuple(kv) + tuple(stack(ab, i) for i in (2, 3, 4, 5)) + tuple(lat)

    return (xp.reshape(bp, sp, d), xs.reshape(bs, ss_, d)) + states(ab_p, c_p, bp, sp) + states(ab_s, c_s, bs, ss_)
```

```python
import functools

import numpy as np
import jax
import jax.numpy as jnp
from jax import lax
from jax.experimental import pallas as pl
from jax.experimental.pallas import tpu as pltpu

F32 = jnp.float32
BF16 = jnp.bfloat16

LANE = 128
BF16_SUBLANES = 16
VMEM_LIMIT_BYTES = 48 * 1024 * 1024

ROW_TILE = 512
FOX_TILE = 512
MLA_TILE = 256
PAST_KV_TILE = 256
MLSTM_CHUNK = 256
MOE_TILE = 256

H_FOX, DH_FOX = 8, 64
W_FOX = H_FOX * DH_FOX
H_ML, DH_ML = 4, 128
W_ML = H_ML * DH_ML
H_MLA, NOPE, ROPE = 8, 128, 64
Q_RANK, KV_RANK = 384, 256
ROPE_BASE = 10000.0
N_GROUPS, EXP_PER_GROUP = 4, 8
DEPTH = 4
ALPHA = (2 * DEPTH) ** 0.25
LN_EPS = 1e-5
RMS_EPS = 1e-6
CHUNK = 64
NEG_BIG = -1e30
LOG2E = 1.4426950408889634


def _params(*sem):
    return pltpu.CompilerParams(dimension_semantics=sem, vmem_limit_bytes=VMEM_LIMIT_BYTES)


def _dot(a, b):
    return jnp.dot(a, b, preferred_element_type=F32)


def _dot_nt(a, b):
    return lax.dot_general(a, b, (((1,), (1,)), ((), ())), preferred_element_type=F32)


def _split3(x):
    hi = x.astype(BF16)
    r = x - hi.astype(F32)
    mid = r.astype(BF16)
    lo = (r - mid.astype(F32)).astype(BF16)
    return hi, mid, lo


def _layer_norm(y, g, b):
    mu = jnp.mean(y, axis=1, keepdims=True)
    yc = y - mu
    var = jnp.mean(yc * yc, axis=1, keepdims=True)
    return yc * lax.rsqrt(var + LN_EPS) * g + b


def _sigmoid(x):
    return 1.0 / (1.0 + jnp.exp(-x))


def _row_tile(t):
    return min(ROW_TILE, t)


def _ab_proj_kernel(x_ref, w_ref, gw_ref, gb_ref, gmask_ref,
                    fq_ref, mq_ref, mk_ref, mv_ref, mo_ref, kbf_ref, vbf_ref, gates_ref, k_ref, v_ref):
    x = x_ref[...]
    xh = x.astype(BF16)
    c0 = 0
    for o_ref in (fq_ref, mq_ref, mk_ref, mv_ref, mo_ref):
        n = o_ref.shape[1]
        o_ref[...] = _dot(xh, w_ref[:, c0:c0 + n]).astype(BF16)
        c0 += n
    for bf_ref, state_ref in ((kbf_ref, k_ref), (vbf_ref, v_ref)):
        z = _dot(xh, w_ref[:, c0:c0 + W_FOX])
        c0 += W_FOX
        bf_ref[...] = z.astype(BF16)
        state_ref[...] = z
    xl = (x - xh.astype(F32)).astype(BF16)
    gg = _dot(xh, gw_ref[...])
    g = gg[:, :LANE] + gg[:, LANE:] + _dot(xl, gw_ref[:, :LANE]) + gb_ref[...]
    log_sig = jnp.minimum(g, 0.0) - jnp.log1p(jnp.exp(-jnp.abs(g)))
    gates_ref[...] = jnp.where(gmask_ref[...] > 0.0, log_sig, g)


def ab_proj(x, w_big, w_gate, b_gate, log_sigmoid_cols):
    t, k = x.shape
    tm = _row_tile(t)
    n_gate = w_gate.shape[1]
    wp = jnp.zeros((k, LANE), F32).at[:, :n_gate].set(w_gate)
    wh = wp.astype(BF16)
    wl = (wp - wh.astype(F32)).astype(BF16)
    bp = jnp.zeros((1, LANE), F32).at[0, :n_gate].set(b_gate)
    mask = jnp.zeros((1, LANE), F32).at[0, :n_gate].set(jnp.asarray(log_sigmoid_cols, F32))
    bf = lambda n: jax.ShapeDtypeStruct((t, n), BF16)
    return pl.pallas_call(
        _ab_proj_kernel,
        grid=(t // tm,),
        in_specs=[_rows(tm, k), _whole(w_big.shape), _whole((k, 2 * LANE)), _whole((1, LANE)), _whole((1, LANE))],
        out_specs=[_rows(tm, W_FOX)] + [_rows(tm, W_ML)] * 4 + [_rows(tm, W_FOX)] * 2 + [_rows(tm, LANE)]
        + [_rows(tm, W_FOX)] * 2,
        out_shape=[bf(W_FOX)] + [bf(W_ML)] * 4 + [bf(W_FOX)] * 2 + [jax.ShapeDtypeStruct((t, LANE), F32)]
        + [jax.ShapeDtypeStruct((t, W_FOX), F32)] * 2,
        compiler_params=_params("parallel"),
        name="ab_proj",
    )(x, w_big, jnp.concatenate([wh, wl], 1), bp, mask)


def _lanes(x, w):
    if w < LANE:
        return x[:, :w]
    return x if w == LANE else jnp.concatenate([x] * (w // LANE), axis=1)


def _softmax_step(s, at, m_s, l_s, acc_s, v):
    w = s.shape[1]
    m_prev = m_s[at]
    m_new = jnp.maximum(m_prev, jnp.max(s, axis=1, keepdims=True))
    p = jnp.exp2(s - _lanes(m_new, w))
    a = jnp.exp2(m_prev - m_new)
    l_s[at] = a * l_s[at] + jnp.sum(p, axis=1, keepdims=True)
    acc_s[at] = acc_s[at] * _lanes(a, acc_s.shape[-1]) + _dot(p.astype(BF16), v)
    m_s[at] = m_new


def _fox_kernel(q_ref, k_ref, v_ref, *rest, tq, tk, q_off, cblk):
    pk_ref, pv_ref = rest[:2] if len(rest) == 9 else (k_ref, v_ref)
    lf_ref, o_ref, ncum_ref, qm_s, m_s, l_s, acc_s = rest[-7:]
    qi = pl.program_id(1)
    sub = cblk // LANE
    n_pairs = H_FOX // 2

    @pl.when(qi == 0)
    def _():
        r = lax.broadcasted_iota(jnp.int32, (cblk, cblk), 0)
        c = lax.broadcasted_iota(jnp.int32, (cblk, cblk), 1)
        upper = jnp.where(r <= c, 1.0, 0.0).astype(BF16)
        carry = jnp.zeros((lf_ref.shape[1], 1), F32)
        for j in range(lf_ref.shape[2] // cblk):
            g1, g2, g3 = _split3(lf_ref[0, :, j * cblk:(j + 1) * cblk])
            cum = _dot(g1, upper) + _dot(g2, upper) + _dot(g3, upper) + carry
            carry = cum[:, cblk - 1:cblk]
            for h in range(H_FOX):
                for u in range(sub):
                    ncum_ref[h, j * sub + u] = cum[h:h + 1, u * LANE:(u + 1) * LANE] * (-LOG2E)

    lane = lax.broadcasted_iota(jnp.int32, (tq, LANE), 1)
    for hp in range(n_pairs):
        q = q_ref[0, :, hp * LANE:(hp + 1) * LANE]
        zero = jnp.zeros_like(q)
        qm_s[hp, 0:tq, :] = jnp.where(lane < DH_FOX, q, zero)
        qm_s[hp, tq:2 * tq, :] = jnp.where(lane >= DH_FOX, q, zero)
    m_s[...] = jnp.full(m_s.shape, NEG_BIG, F32)
    l_s[...] = jnp.zeros(l_s.shape, F32)
    acc_s[...] = jnp.zeros(acc_s.shape, F32)
    q_start = q_off + qi * tq

    def update(k_src, v_src, off, w, bias_of_head, mask):
        for hp in range(n_pairs):
            kt = k_src[0, pl.ds(off, w), hp * LANE:(hp + 1) * LANE]
            vt = v_src[0, pl.ds(off, w), hp * LANE:(hp + 1) * LANE]
            s = _dot_nt(qm_s[hp], kt)
            s = jnp.concatenate([s[:tq] + bias_of_head(2 * hp), s[tq:] + bias_of_head(2 * hp + 1)], axis=0)
            if mask is not None:
                s = jnp.where(mask, s, -jnp.inf)
            _softmax_step(s, hp, m_s, l_s, acc_s, vt)

    def full_tile(j, carry):
        def bias(h):
            return jnp.concatenate([ncum_ref[h, j * (tk // LANE) + u] for u in range(tk // LANE)], axis=1)
        update(pk_ref, pv_ref, pl.multiple_of(j * tk, tk), tk, bias, None)
        return carry

    lax.fori_loop(0, q_start // tk, full_tile, 0)

    def diag_bias(h):
        if tq >= LANE:
            return jnp.concatenate([ncum_ref[h, q_start // LANE + u] for u in range(tq // LANE)], axis=1)
        lo = q_off % LANE
        return ncum_ref[h, q_start // LANE][:, lo:lo + tq]

    r = lax.broadcasted_iota(jnp.int32, (2 * tq, tq), 0)
    c = lax.broadcasted_iota(jnp.int32, (2 * tq, tq), 1)
    update(k_ref, v_ref, pl.multiple_of(qi * tq, tq), tq, diag_bias, c <= jnp.where(r >= tq, r - tq, r))
    for hp in range(n_pairs):
        o = acc_s[hp] / l_s[hp]
        o_ref[0, :, hp * LANE:(hp + 1) * LANE] = jnp.where(lane < DH_FOX, o[:tq], o[tq:]).astype(o_ref.dtype)


def fox_attention(q, k, v, logf_rows, *, tq, tk, past=None, cblk=PAST_KV_TILE):
    b, lq, _ = q.shape
    lkp = logf_rows.shape[2]
    q_off = 0 if past is None else past[0].shape[1]
    nq = lq // tq
    assert k.shape[1] == lq and q_off % tk == 0 and (tq % tk == 0 or lq == tq) and lkp % cblk == 0
    assert tq >= LANE or (nq == 1 and q_off % LANE + tq <= LANE)
    kern = functools.partial(_fox_kernel, tq=tq, tk=tk, q_off=q_off, cblk=cblk)
    whole_seq = [pl.BlockSpec((1, a.shape[1], W_FOX), lambda bi, qi: (bi, 0, 0)) for a in (k, v) + tuple(past or ())]
    return pl.pallas_call(
        kern,
        grid=(b, nq),
        in_specs=[pl.BlockSpec((1, tq, W_FOX), lambda bi, qi: (bi, qi, 0))] + whole_seq + [
            pl.BlockSpec((1, logf_rows.shape[1], lkp), lambda bi, qi: (bi, 0, 0)),
        ],
        out_specs=pl.BlockSpec((1, tq, W_FOX), lambda bi, qi: (bi, qi, 0)),
        out_shape=jax.ShapeDtypeStruct((b, lq, W_FOX), BF16),
        scratch_shapes=[pltpu.VMEM((H_FOX, lkp // LANE, 1, LANE), F32), pltpu.VMEM((H_FOX // 2, 2 * tq, LANE), BF16),
                        pltpu.VMEM((H_FOX // 2, 2 * tq, LANE), F32), pltpu.VMEM((H_FOX // 2, 2 * tq, LANE), F32),
                        pltpu.VMEM((H_FOX // 2, 2 * tq, LANE), F32)],
        compiler_params=_params("parallel", "arbitrary"),
        name="fox_attention",
    )(q, k, v, *(past or ()), logf_rows)


def _mlstm_kernel(q_ref, k_ref, v_ref, g_ref, c0_ref, n0_ref, m0_ref,
                  h_ref, c_ref, n_ref, m_ref, c_s, n_s, m_s, *, chunk):
    ci = pl.program_id(1)
    L = chunk
    scale = DH_ML ** -0.5

    @pl.when(ci == 0)
    def _():
        c_s[...] = c0_ref[0]
        n_s[...] = n0_ref[0]
        m_s[...] = m0_ref[0]

    r = lax.broadcasted_iota(jnp.int32, (L, L), 0)
    c = lax.broadcasted_iota(jnp.int32, (L, L), 1)
    eye = r == c
    causal = c <= r
    upper = jnp.where(r <= c, 1.0, 0.0).astype(BF16)
    g = g_ref[0]
    g1, g2, g3 = _split3(g)
    cum = _dot(g1, upper) + _dot(g2, upper) + _dot(g3, upper)

    def to_col(row):
        return jnp.sum(jnp.where(eye, row, 0.0), axis=1, keepdims=True)

    for h in range(H_ML):
        sl = slice(h * DH_ML, (h + 1) * DH_ML)
        qh, kh, vh = q_ref[0, :, sl], k_ref[0, :, sl], v_ref[0, :, sl]
        ig_row = g[h:h + 1, :]
        bh_row = cum[H_ML + h:H_ML + h + 1, :]
        bh_col, ig_col = to_col(bh_row), to_col(ig_row)
        m0 = m_s[h]
        c0 = c_s[h]
        n0 = n_s[h]
        logd = jnp.where(causal, bh_col - bh_row + ig_row, -jnp.inf)
        inter = bh_col + m0
        m_col = jnp.maximum(inter, jnp.max(logd, axis=1, keepdims=True))
        d = jnp.exp(logd - m_col)
        a_col = jnp.exp(inter - m_col)
        s = _dot_nt(qh, kh) * scale * d
        num = _dot(s.astype(BF16), vh) + _dot_nt(qh, c0.astype(BF16)) * a_col
        qf = qh.astype(F32)
        den = jnp.sum(s, axis=1, keepdims=True) + a_col * jnp.sum(qf * n0, axis=1, keepdims=True)
        den = jnp.maximum(jnp.abs(den), jnp.exp(-m_col))
        h_ref[0, :, sl] = (num / den).astype(h_ref.dtype)

        m_l = m_col[L - 1:L, :]
        bh_l = bh_row[:, L - 1:L]
        a_l = jnp.exp(bh_l + m0 - m_l)
        w_col = jnp.exp(bh_l - bh_col + ig_col - m_l)
        vw_t = jnp.transpose(vh.astype(F32) * w_col).astype(BF16)
        c_s[h] = a_l * c0 + _dot(vw_t, kh) * scale
        n_s[h] = a_l * n0 + jnp.sum(kh.astype(F32) * w_col, axis=0, keepdims=True) * scale
        m_s[h] = m_l

    @pl.when(ci == pl.num_programs(1) - 1)
    def _():
        c_ref[0] = c_s[...]
        n_ref[0] = n_s[...]
        m_ref[0] = m_s[...]


def mlstm(q, k, v, gate_rows, c0, n0, m0, *, chunk):
    b, s, _ = q.shape
    nc = s // chunk
    seq = pl.BlockSpec((1, chunk, W_ML), lambda bi, ci: (bi, ci, 0))
    c_spec = pl.BlockSpec((1, H_ML, DH_ML, DH_ML), lambda bi, ci: (bi, 0, 0, 0))
    n_spec = pl.BlockSpec((1, H_ML, 1, DH_ML), lambda bi, ci: (bi, 0, 0, 0))
    m_spec = pl.BlockSpec((1, H_ML, 1, 1), lambda bi, ci: (bi, 0, 0, 0))
    return pl.pallas_call(
        functools.partial(_mlstm_kernel, chunk=chunk),
        grid=(b, nc),
        in_specs=[seq, seq, seq, pl.BlockSpec((1, gate_rows.shape[1], chunk), lambda bi, ci: (bi, 0, ci)),
                  c_spec, n_spec, m_spec],
        out_specs=[seq, c_spec, n_spec, m_spec],
        out_shape=[jax.ShapeDtypeStruct((b, s, W_ML), BF16),
                   jax.ShapeDtypeStruct(c0.shape, F32), jax.ShapeDtypeStruct(n0.shape, F32),
                   jax.ShapeDtypeStruct(m0.shape, F32)],
        scratch_shapes=[pltpu.VMEM((H_ML, DH_ML, DH_ML), F32), pltpu.VMEM((H_ML, 1, DH_ML), F32),
                        pltpu.VMEM((H_ML, 1, 1), F32)],
        compiler_params=_params("parallel", "arbitrary"),
        name="mlstm",
    )(q, k, v, gate_rows, c0, n0, m0)


N_PAIRS = EXP_PER_GROUP * (EXP_PER_GROUP - 1) // 2
N_BUCKETS = N_GROUPS * N_PAIRS
assert N_BUCKETS <= LANE
EXT = LANE
FEW_TOKENS = 1024


def _route_store(out, wr_ref, br_ref, ltri_ref, o_ref, cnt_ref, cnt_s):
    tm, d = out.shape

    @pl.when(pl.program_id(0) == 0)
    def _():
        cnt_s[...] = jnp.zeros(cnt_s.shape, F32)

    xh = out.astype(BF16)
    xl = (out - xh.astype(F32)).astype(BF16)
    zz = _dot(xh, wr_ref[...])
    z = zz[:, :LANE] + zz[:, LANE:] + _dot(xl, wr_ref[:, :LANE]) + br_ref[...]
    lane = lax.broadcasted_iota(jnp.int32, (tm, LANE), 1).astype(F32)
    far = float(LANE)
    neg = -jnp.inf
    gl = jnp.where(lane < N_GROUPS, z, neg)
    gmax = jnp.max(gl, axis=1, keepdims=True)
    g_idx = jnp.min(jnp.where(gl == gmax, lane, far), axis=1, keepdims=True)
    g_gate = 1.0 / jnp.sum(jnp.exp(gl - gmax), axis=1, keepdims=True)
    lo = N_GROUPS + EXP_PER_GROUP * g_idx
    el = jnp.where(jnp.logical_and(lane >= lo, lane < lo + EXP_PER_GROUP), z, neg)
    v1 = jnp.max(el, axis=1, keepdims=True)
    i1 = jnp.min(jnp.where(el == v1, lane, far), axis=1, keepdims=True)
    el2 = jnp.where(lane == i1, neg, el)
    v2 = jnp.max(el2, axis=1, keepdims=True)
    i2 = jnp.min(jnp.where(el2 == v2, lane, far), axis=1, keepdims=True)
    e21 = jnp.exp(v2 - v1)
    w1 = g_gate / (1.0 + e21)
    w2 = w1 * e21
    j1, j2 = i1 - lo, i2 - lo
    ja, jb = jnp.minimum(j1, j2), jnp.maximum(j1, j2)
    pair = ja * (2 * EXP_PER_GROUP - 1 - ja) * 0.5 + (jb - ja - 1.0)
    bucket = g_idx * N_PAIRS + pair
    first_low = j1 < j2
    wa = jnp.where(first_low, w1, w2)
    wb = jnp.where(first_low, w2, w1)
    onehot = jnp.where(lane == bucket, 1.0, 0.0)
    earlier = _dot(ltri_ref[...], onehot.astype(BF16))
    cnt = cnt_s[...]
    rank = jnp.sum(onehot * (earlier + cnt), axis=1, keepdims=True)
    cnt_new = cnt + jnp.sum(onehot, axis=0, keepdims=True)
    cnt_s[...] = cnt_new
    cnt_ref[...] = cnt_new
    o_ref[:, :d] = out
    e_lo = lo - N_GROUPS + ja
    e_hi = lo - N_GROUPS + jb
    record = (bucket, rank, wa, wb, e_lo, e_hi)
    ext = jnp.zeros((tm, LANE), F32)
    for k, col in enumerate(record):
        ext = jnp.where(lane == float(k), col, ext)
    o_ref[:, d:] = ext


def _route_operands(w_route, b_route, tm):
    n = w_route.shape[1]
    wp = jnp.zeros((w_route.shape[0], LANE), F32).at[:, :n].set(w_route)
    wh = wp.astype(BF16)
    wl = (wp - wh.astype(F32)).astype(BF16)
    bp = jnp.zeros((1, LANE), F32).at[0, :n].set(b_route)
    ltri = jnp.asarray(np.tril(np.ones((tm, tm), np.float32), -1), BF16)
    return jnp.concatenate([wh, wl], 1), bp, ltri


def _route_specs(k, tm):
    return [_whole((k, 2 * LANE)), _whole((1, LANE)), _whole((tm, tm))]


def _ab_out_kernel(fo_ref, mh_ref, mo_ref, x_ref, gml_ref, w_ref, g_ref, b_ref, wr_ref, br_ref, ltri_ref,
                   o_ref, cnt_ref, cnt_s):
    parts = [fo_ref[...]]
    for h in range(H_ML):
        sl = slice(h * DH_ML, (h + 1) * DH_ML)
        mh = mh_ref[:, sl].astype(F32)
        ms = jnp.mean(mh * mh, axis=1, keepdims=True)
        nm = mh * lax.rsqrt(ms + RMS_EPS) * gml_ref[:, sl]
        parts.append((nm * _sigmoid(mo_ref[:, sl].astype(F32))).astype(BF16))
    cat = jnp.concatenate(parts, axis=1)
    y = ALPHA * x_ref[...] + _dot(cat, w_ref[...])
    out = _layer_norm(y, g_ref[...], b_ref[...])
    _route_store(out, wr_ref, br_ref, ltri_ref, o_ref, cnt_ref, cnt_s)


def _rows(tm, n):
    return pl.BlockSpec((tm, n), lambda i: (i, 0))


def _whole(shape):
    return pl.BlockSpec(shape, lambda i: (0,) * len(shape))


def _routed_out_call(kern, name, t, d, tm, in_specs, operands, route_w):
    w_route, b_route = route_w
    return pl.pallas_call(
        kern,
        grid=(t // tm,),
        in_specs=in_specs + _route_specs(d, tm),
        out_specs=[_rows(tm, d + EXT), _whole((1, LANE))],
        out_shape=[jax.ShapeDtypeStruct((t, d + EXT), F32), jax.ShapeDtypeStruct((1, LANE), F32)],
        scratch_shapes=[pltpu.VMEM((1, LANE), F32)],
        compiler_params=_params("arbitrary"),
        name=name,
    )(*operands, *_route_operands(w_route, b_route, tm))


def ab_out(fo, mh, mo, x, g_ml, w_out, ln_g, ln_b, route_w):
    t, d = x.shape
    tm = _row_tile(t)
    in_specs = [_rows(tm, W_FOX), _rows(tm, W_ML), _rows(tm, W_ML), _rows(tm, d), _whole((1, W_ML)),
                _whole(w_out.shape), _whole((1, d)), _whole((1, d))]
    return _routed_out_call(_ab_out_kernel, "ab_out", t, d, tm, in_specs,
                            (fo, mh, mo, x, g_ml, w_out, ln_g, ln_b), route_w)


def _rms(z, g):
    return z * lax.rsqrt(jnp.mean(z * z, axis=1, keepdims=True) + RMS_EPS) * g


def _mla_down_kernel(x_ref, w_ref, gq_ref, gkv_ref, cc_ref, ss_ref, *rest, n_prev):
    prev_c, prev_r = rest[:n_prev], rest[n_prev:2 * n_prev]
    cq_ref, ckv_ref, kr_ref, ckvb_ref, kr2_ref = rest[2 * n_prev:]

    def put(state_ref, prev, value):
        if n_prev:
            for j, p in enumerate(prev):
                state_ref[j] = p[...]
            state_ref[n_prev] = value
        else:
            state_ref[...] = value

    xb = x_ref[...].astype(BF16)
    cq_ref[...] = _rms(_dot(xb, w_ref[:, 0:Q_RANK]), gq_ref[...]).astype(BF16)
    ckv = _rms(_dot(xb, w_ref[:, Q_RANK:Q_RANK + KV_RANK]), gkv_ref[...])
    put(ckv_ref, prev_c, ckv)
    ckvb_ref[...] = ckv.astype(BF16)
    c0 = Q_RANK + KV_RANK
    rope = (_dot(xb, w_ref[:, c0:c0 + LANE]) * cc_ref[...] + _dot(xb, w_ref[:, c0 + LANE:c0 + 2 * LANE]) * ss_ref[...])
    put(kr_ref, prev_r, rope[:, :ROPE])
    kr2_ref[...] = rope.astype(BF16)


def _table_spec(table, tm, s):
    if table.shape[0] == tm:
        return pl.BlockSpec((tm, LANE), lambda i: (0, 0))
    per = s // tm
    return pl.BlockSpec((tm, LANE), lambda i: (i % per, 0))


def mla_down(x, w, g_q, g_kv, cc, ss, s, prev_c=(), prev_r=()):
    t, d = x.shape
    tm = _row_tile(t)
    tab = _table_spec(cc, tm, s)
    n_prev = len(prev_c)

    def state(width):
        if n_prev:
            return (pl.BlockSpec((n_prev + 1, tm, width), lambda i: (0, i, 0)),
                    jax.ShapeDtypeStruct((n_prev + 1, t, width), F32))
        return _rows(tm, width), jax.ShapeDtypeStruct((t, width), F32)

    (c_spec, c_shape), (r_spec, r_shape) = state(KV_RANK), state(ROPE)
    return pl.pallas_call(
        functools.partial(_mla_down_kernel, n_prev=n_prev),
        grid=(t // tm,),
        in_specs=[_rows(tm, d), _whole(w.shape), _whole((1, Q_RANK)), _whole((1, KV_RANK)), tab, tab]
        + [_rows(tm, KV_RANK)] * n_prev + [_rows(tm, ROPE)] * n_prev,
        out_specs=[_rows(tm, Q_RANK), c_spec, r_spec, _rows(tm, KV_RANK), _rows(tm, LANE)],
        out_shape=[jax.ShapeDtypeStruct((t, Q_RANK), BF16), c_shape, r_shape,
                   jax.ShapeDtypeStruct((t, KV_RANK), BF16), jax.ShapeDtypeStruct((t, LANE), BF16)],
        compiler_params=_params("parallel"),
        name="mla_down",
    )(x, w, g_q, g_kv, cc, ss, *prev_c, *prev_r)


def _mla_uq_kernel(cq_ref, w_ref, wuk_ref, cc_ref, ss_ref, qlat_ref, qrope_ref):
    cq = cq_ref[...]
    scale = (NOPE + ROPE) ** -0.5 * LOG2E
    r0 = H_MLA * NOPE
    r1 = r0 + H_MLA * ROPE
    qn = _dot(cq, w_ref[:, :r0]).astype(BF16)
    for h in range(H_MLA):
        qlat_ref[:, h * KV_RANK:(h + 1) * KV_RANK] = (
            _dot(qn[:, h * NOPE:(h + 1) * NOPE], wuk_ref[h]) * scale).astype(BF16)
    qr = _dot(cq, w_ref[:, r0:])
    cc, ss = cc_ref[...] * scale, ss_ref[...] * scale
    for p in range(H_MLA // 2):
        a, b = p * LANE, (r1 - r0) + p * LANE
        qrope_ref[:, p * LANE:(p + 1) * LANE] = (qr[:, a:a + LANE] * cc + qr[:, b:b + LANE] * ss).astype(BF16)


def mla_uq(cq, w, wuk_t, cc, ss, s):
    t = cq.shape[0]
    tm = _row_tile(t)
    tab = _table_spec(cc, tm, s)
    return pl.pallas_call(
        _mla_uq_kernel,
        grid=(t // tm,),
        in_specs=[_rows(tm, Q_RANK), _whole(w.shape), _whole(wuk_t.shape), tab, tab],
        out_specs=[_rows(tm, H_MLA * KV_RANK), _rows(tm, H_MLA * ROPE)],
        out_shape=[jax.ShapeDtypeStruct((t, H_MLA * KV_RANK), BF16), jax.ShapeDtypeStruct((t, H_MLA * ROPE), BF16)],
        compiler_params=_params("parallel"),
        name="mla_uq",
    )(cq, w, wuk_t, cc, ss)


def _mla_attn_kernel(ql_ref, qr_ref, kc_ref, kr_ref, *rest, tq, tk, q_off):
    pc_ref, pr_ref = rest[:2] if len(rest) == 8 else (kc_ref, kr_ref)
    o_ref, ql_s, qr_s, m_s, l_s, acc_s = rest[-6:]
    qi = pl.program_id(1)
    lane = lax.broadcasted_iota(jnp.int32, (tq, LANE), 1)
    for h in range(H_MLA):
        rows = slice(h * tq, (h + 1) * tq)
        ql_s[rows, :] = ql_ref[0, :, h * KV_RANK:(h + 1) * KV_RANK]
        qr = qr_ref[0, :, (h // 2) * LANE:(h // 2 + 1) * LANE]
        keep = (lane < ROPE) if h % 2 == 0 else (lane >= ROPE)
        qr_s[rows, :] = jnp.where(keep, qr, jnp.zeros_like(qr))
    m_s[...] = jnp.full(m_s.shape, NEG_BIG, F32)
    l_s[...] = jnp.zeros(l_s.shape, F32)
    acc_s[...] = jnp.zeros(acc_s.shape, F32)
    q_start = q_off + qi * tq

    def update(c_src, r_src, off, w, mask):
        kc = c_src[0, pl.ds(off, w), :]
        kr = r_src[0, pl.ds(off, w), :]
        s = _dot_nt(ql_s[...], kc) + _dot_nt(qr_s[...], kr)
        if mask is not None:
            s = jnp.where(mask, s, -jnp.inf)
        _softmax_step(s, Ellipsis, m_s, l_s, acc_s, kc)

    def full_tile(j, carry):
        update(pc_ref, pr_ref, pl.multiple_of(j * tk, tk), tk, None)
        return carry

    lax.fori_loop(0, q_start // tk, full_tile, 0)
    r = lax.broadcasted_iota(jnp.int32, (H_MLA * tq, tq), 0)
    c = lax.broadcasted_iota(jnp.int32, (H_MLA * tq, tq), 1)
    update(kc_ref, kr_ref, pl.multiple_of(qi * tq, tq), tq, (c // CHUNK) <= ((r & (tq - 1)) // CHUNK))
    for h in range(H_MLA):
        rows = slice(h * tq, (h + 1) * tq)
        o = acc_s[rows, :] / _lanes(l_s[rows, :], KV_RANK)
        o_ref[0, :, h * KV_RANK:(h + 1) * KV_RANK] = o.astype(o_ref.dtype)


def _mla_attn_pipelined_kernel(ql_ref, qr_ref, kc_ref, kr_ref, o_ref, ql_s, qr_s, m_s, l_s, acc_s, s_buf, *, tq):
    qi = pl.program_id(1)
    lane = lax.broadcasted_iota(jnp.int32, (tq, LANE), 1)
    for h in range(H_MLA):
        rows = slice(h * tq, (h + 1) * tq)
        ql_s[rows, :] = ql_ref[0, :, h * KV_RANK:(h + 1) * KV_RANK]
        qr = qr_ref[0, :, (h // 2) * LANE:(h // 2 + 1) * LANE]
        keep = (lane < ROPE) if h % 2 == 0 else (lane >= ROPE)
        qr_s[rows, :] = jnp.where(keep, qr, jnp.zeros_like(qr))
    m_s[...] = jnp.full(m_s.shape, NEG_BIG, F32)
    l_s[...] = jnp.zeros(l_s.shape, F32)
    acc_s[...] = jnp.zeros(acc_s.shape, F32)

    def keys(ref, j):
        return ref[0, pl.ds(pl.multiple_of(j * tq, tq), tq), :]

    def score(j, slot):
        s_buf[slot] = _dot_nt(ql_s[...], keys(kc_ref, j)) + _dot_nt(qr_s[...], keys(kr_ref, j))

    def consume(j, slot, mask):
        s = s_buf[slot]
        if mask is not None:
            s = jnp.where(mask, s, -jnp.inf)
        _softmax_step(s, Ellipsis, m_s, l_s, acc_s, keys(kc_ref, j))

    score(0, 0)

    def two_tiles(ip, carry):
        j = 2 * ip
        score(j + 1, 1)
        consume(j, 0, None)
        score(j + 2, 0)
        consume(j + 1, 1, None)
        return carry

    lax.fori_loop(0, qi // 2, two_tiles, 0)
    r = lax.broadcasted_iota(jnp.int32, (H_MLA * tq, tq), 0)
    c = lax.broadcasted_iota(jnp.int32, (H_MLA * tq, tq), 1)
    mask = (c // CHUNK) <= ((r & (tq - 1)) // CHUNK)

    @pl.when(qi % 2 == 1)
    def _():
        score(qi, 1)
        consume(qi - 1, 0, None)
        consume(qi, 1, mask)

    @pl.when(qi % 2 == 0)
    def _():
        consume(qi, 0, mask)

    for h in range(H_MLA):
        rows = slice(h * tq, (h + 1) * tq)
        o = acc_s[rows, :] / _lanes(l_s[rows, :], KV_RANK)
        o_ref[0, :, h * KV_RANK:(h + 1) * KV_RANK] = o.astype(o_ref.dtype)


def mla_attention(q_lat, q_rope, kc, kr2, *, tq, tk, past=None):
    b, lq, _ = q_lat.shape
    q_off = 0 if past is None else past[0].shape[1]
    assert kc.shape[1] == lq and q_off % tk == 0 and (tq % tk == 0 or lq == tq)
    assert tq % CHUNK == 0 and (tq & (tq - 1)) == 0
    scratch = [pltpu.VMEM((H_MLA * tq, KV_RANK), BF16), pltpu.VMEM((H_MLA * tq, LANE), BF16),
               pltpu.VMEM((H_MLA * tq, LANE), F32), pltpu.VMEM((H_MLA * tq, LANE), F32),
               pltpu.VMEM((H_MLA * tq, KV_RANK), F32)]
    if past is None and tk == tq:
        kern = functools.partial(_mla_attn_pipelined_kernel, tq=tq)
        scratch.append(pltpu.VMEM((2, H_MLA * tq, tq), F32))
    else:
        kern = functools.partial(_mla_attn_kernel, tq=tq, tk=tk, q_off=q_off)
    whole_seq = [pl.BlockSpec((1,) + a.shape[1:], lambda bi, qi: (bi, 0, 0)) for a in (kc, kr2) + tuple(past or ())]
    return pl.pallas_call(
        kern,
        grid=(b, lq // tq),
        in_specs=[
            pl.BlockSpec((1, tq, H_MLA * KV_RANK), lambda bi, qi: (bi, qi, 0)),
            pl.BlockSpec((1, tq, H_MLA * ROPE), lambda bi, qi: (bi, qi, 0)),
        ] + whole_seq,
        out_specs=pl.BlockSpec((1, tq, H_MLA * KV_RANK), lambda bi, qi: (bi, qi, 0)),
        out_shape=jax.ShapeDtypeStruct((b, lq, H_MLA * KV_RANK), BF16),
        scratch_shapes=scratch,
        compiler_params=_params("parallel", "arbitrary"),
        name="mla_attention",
    )(q_lat, q_rope, kc, kr2, *(past or ()))


def _mla_out_kernel(ol_ref, wuv_ref, w_ref, x_ref, g_ref, b_ref, wr_ref, br_ref, ltri_ref,
                    o_ref, cnt_ref, cnt_s):
    parts = [_dot(ol_ref[:, h * KV_RANK:(h + 1) * KV_RANK], wuv_ref[h]).astype(BF16) for h in range(H_MLA)]
    y = ALPHA * x_ref[...] + _dot(jnp.concatenate(parts, axis=1), w_ref[...])
    out = _layer_norm(y, g_ref[...], b_ref[...])
    _route_store(out, wr_ref, br_ref, ltri_ref, o_ref, cnt_ref, cnt_s)


def mla_out(o_lat, wuv, w_out, x, ln_g, ln_b, route_w):
    t, d = x.shape
    tm = _row_tile(t)
    in_specs = [_rows(tm, H_MLA * KV_RANK), _whole(wuv.shape), _whole(w_out.shape), _rows(tm, d),
                _whole((1, d)), _whole((1, d))]
    return _routed_out_call(_mla_out_kernel, "mla_out", t, d, tm, in_specs,
                            (o_lat, wuv, w_out, x, ln_g, ln_b), route_w)


ROW_SLOTS = 3


def _row_scatter_kernel(idx_ref, src_ref, dst_init_ref, dst_ref, buf, in_sem, out_sem, *, rows, n_steps):
    del dst_init_ref

    def load(i, slot):
        return pltpu.make_async_copy(src_ref.at[pl.ds(pl.multiple_of(i * rows, rows), rows)], buf.at[slot],
                                     in_sem.at[slot])

    def rows_done(slot):
        return pltpu.make_async_copy(buf.at[slot], dst_ref.at[pl.ds(0, rows)], out_sem.at[slot])

    i = pl.program_id(0)
    slot = i % ROW_SLOTS
    nxt = (i + 1) % ROW_SLOTS

    @pl.when(i == 0)
    def _():
        load(0, 0).start()

    @pl.when(i + 1 < n_steps)
    def _():
        @pl.when(i + 1 >= ROW_SLOTS)
        def _():
            rows_done(nxt).wait()
        load(i + 1, nxt).start()

    load(i, slot).wait()
    base = i * rows

    for r in range(rows):
        pltpu.make_async_copy(buf.at[slot, pl.ds(r, 1)], dst_ref.at[pl.ds(idx_ref[base + r], 1)],
                              out_sem.at[slot]).start(priority=r % 2)

    @pl.when(i == n_steps - 1)
    def _():
        for k in range(min(ROW_SLOTS, n_steps)):
            rows_done((n_steps - 1 - k) % ROW_SLOTS).wait()


def _row_gather_kernel(idx_ref, src_ref, dst_ref, buf, row_sem, out_sem, *, rows, n_steps):
    def rows_done(slot):
        return pltpu.make_async_copy(src_ref.at[pl.ds(0, rows)], buf.at[slot], row_sem.at[slot])

    def store(i, slot):
        return pltpu.make_async_copy(buf.at[slot], dst_ref.at[pl.ds(pl.multiple_of(i * rows, rows), rows)],
                                     out_sem.at[slot])

    i = pl.program_id(0)
    slot = i % ROW_SLOTS

    @pl.when(i >= ROW_SLOTS)
    def _():
        store(i - ROW_SLOTS, slot).wait()

    base = i * rows

    for r in range(rows):
        pltpu.make_async_copy(src_ref.at[pl.ds(idx_ref[base + r], 1)], buf.at[slot, pl.ds(r, 1)],
                              row_sem.at[slot]).start(priority=r % 2)

    @pl.when(i >= 1)
    def _():
        prev = (i + ROW_SLOTS - 1) % ROW_SLOTS
        rows_done(prev).wait()
        store(i - 1, prev).start()

    @pl.when(i == n_steps - 1)
    def _():
        last = (n_steps - 1) % ROW_SLOTS
        rows_done(last).wait()
        store(n_steps - 1, last).start()
        for k in range(min(ROW_SLOTS, n_steps)):
            store(n_steps - 1 - k, (n_steps - 1 - k) % ROW_SLOTS).wait()


def row_move(idx, src, dst_rows, *, scatter, dst_init=None):
    t = idx.shape[0]
    width = src.shape[1]
    rows = _row_tile(t)
    any_spec = pl.BlockSpec(memory_space=pl.ANY)
    operands = [idx, src] + ([dst_init] if scatter else [])
    grid_spec = pltpu.PrefetchScalarGridSpec(
        num_scalar_prefetch=1, grid=(t // rows,), in_specs=[any_spec] * (len(operands) - 1), out_specs=any_spec,
        scratch_shapes=[pltpu.VMEM((ROW_SLOTS, rows, width), src.dtype),
                        pltpu.SemaphoreType.DMA((ROW_SLOTS,)), pltpu.SemaphoreType.DMA((ROW_SLOTS,))])
    kern = _row_scatter_kernel if scatter else _row_gather_kernel
    return pl.pallas_call(
        functools.partial(kern, rows=rows, n_steps=t // rows),
        grid_spec=grid_spec,
        out_shape=jax.ShapeDtypeStruct((dst_rows, width), src.dtype),
        input_output_aliases={2: 0} if scatter else {},
        compiler_params=_params("arbitrary"),
        name="row_scatter" if scatter else "row_gather",
    )(*operands)


def _moe_kernel(ta_ref, tb_ref, nu_ref, xs_ref, w1a_ref, w3a_ref, w2a_ref, w1b_ref, w3b_ref, w2b_ref,
                g_ref, b_ref, o_ref, pend):
    t = pl.program_id(0)
    d = o_ref.shape[1]
    n_used = nu_ref[0]

    @pl.when(t == 0)
    def _():
        pend[...] = jnp.zeros(pend.shape, F32)

    @pl.when(t < n_used)
    def _():
        o_ref[...] = _layer_norm(pend[...], g_ref[...], b_ref[...])
        x = xs_ref[:, :d]
        ext = xs_ref[:, d:]
        xb = x.astype(BF16)

        def ffn(w1_ref, w3_ref, w2_ref):
            h1 = _dot(xb, w1_ref[0])
            h3 = _dot(xb, w3_ref[0])
            return _dot((h1 * _sigmoid(h1) * h3).astype(BF16), w2_ref[0])

        y = ext[:, 2:3] * ffn(w1a_ref, w3a_ref, w2a_ref) + ext[:, 3:4] * ffn(w1b_ref, w3b_ref, w2b_ref)
        pend[...] = ALPHA * x + y

    @pl.when(t == n_used)
    def _():
        o_ref[...] = _layer_norm(pend[...], g_ref[...], b_ref[...])

    @pl.when(t > n_used)
    def _():
        o_ref[...] = jnp.zeros(o_ref.shape, o_ref.dtype)


def moe_ln(tile_ea, tile_eb, n_used, xs, w1, w3, w2, ln_g, ln_b, tm):
    p, wide = xs.shape
    d = wide - EXT
    de = w1.shape[2]
    n = p // tm

    def w_spec(shape, which):
        return pl.BlockSpec(shape, lambda t, ta, tb, nu: ((ta, tb)[which][jnp.minimum(t, n - 1)], 0, 0))

    grid_spec = pltpu.PrefetchScalarGridSpec(
        num_scalar_prefetch=3,
        grid=(n + 1,),
        in_specs=[pl.BlockSpec((tm, wide), lambda t, ta, tb, nu: (jnp.minimum(t, n - 1), 0)),
                  w_spec((1, d, de), 0), w_spec((1, d, de), 0), w_spec((1, de, d), 0),
                  w_spec((1, d, de), 1), w_spec((1, d, de), 1), w_spec((1, de, d), 1),
                  pl.BlockSpec((1, d), lambda t, ta, tb, nu: (0, 0)),
                  pl.BlockSpec((1, d), lambda t, ta, tb, nu: (0, 0))],
        out_specs=pl.BlockSpec((tm, d), lambda t, ta, tb, nu: (jnp.maximum(t - 1, 0), 0)),
        scratch_shapes=[pltpu.VMEM((tm, d), F32)],
    )
    return pl.pallas_call(
        _moe_kernel,
        grid_spec=grid_spec,
        out_shape=jax.ShapeDtypeStruct((p, d), F32),
        compiler_params=_params("arbitrary"),
        name="moe_ln",
    )(tile_ea, tile_eb, n_used, xs, w1, w3, w2, w1, w3, w2, ln_g, ln_b)


def _moe_few_kernel(x_ref, w1_ref, w3_ref, w2_ref, g_ref, b_ref, o_ref, acc_s):
    e = pl.program_id(0)
    d = o_ref.shape[1]

    @pl.when(e == 0)
    def _():
        acc_s[...] = jnp.zeros(acc_s.shape, F32)

    x = x_ref[:, :d]
    ext = x_ref[:, d:]
    xb = x.astype(BF16)
    h1 = _dot(xb, w1_ref[0])
    h3 = _dot(xb, w3_ref[0])
    y = _dot((h1 * _sigmoid(h1) * h3).astype(BF16), w2_ref[0])
    ef = e.astype(F32)
    gate = jnp.where(ext[:, 4:5] == ef, ext[:, 2:3], 0.0) + jnp.where(ext[:, 5:6] == ef, ext[:, 3:4], 0.0)
    acc_s[...] += gate * y

    @pl.when(e == pl.num_programs(0) - 1)
    def _():
        o_ref[...] = _layer_norm(ALPHA * x + acc_s[...], g_ref[...], b_ref[...])


def moe_ln_few(x1ext, w1, w3, w2, ln_g, ln_b):
    t, wide = x1ext.shape
    d = wide - EXT
    de = w1.shape[2]
    return pl.pallas_call(
        _moe_few_kernel,
        grid=(w1.shape[0],),
        in_specs=[_whole((t, wide)), pl.BlockSpec((1, d, de), lambda e: (e, 0, 0)),
                  pl.BlockSpec((1, d, de), lambda e: (e, 0, 0)), pl.BlockSpec((1, de, d), lambda e: (e, 0, 0)),
                  _whole((1, d)), _whole((1, d))],
        out_specs=_whole((t, d)),
        out_shape=jax.ShapeDtypeStruct((t, d), F32),
        scratch_shapes=[pltpu.VMEM((t, d), F32)],
        compiler_params=_params("arbitrary"),
        name="moe_ln_few",
    )(x1ext, w1, w3, w2, ln_g, ln_b)


def _bucket_experts():
    ea, eb = [], []
    for g in range(N_GROUPS):
        for ja in range(EXP_PER_GROUP):
            for jb in range(ja + 1, EXP_PER_GROUP):
                ea.append(g * EXP_PER_GROUP + ja)
                eb.append(g * EXP_PER_GROUP + jb)
    return np.asarray(ea, np.int32), np.asarray(eb, np.int32)


def _bucket_layout(x1ext, counts, tm):
    t, wide = x1ext.shape
    d = wide - EXT
    bucket = x1ext[:, d].astype(jnp.int32)
    rank = x1ext[:, d + 1].astype(jnp.int32)
    counts = counts[0, :N_BUCKETS].astype(jnp.int32)
    padded = ((counts + tm - 1) // tm) * tm
    row_end = jnp.cumsum(padded)
    row_start = row_end - padded
    ids = jnp.arange(N_BUCKETS, dtype=jnp.int32)
    pos = rank + jnp.sum(jnp.where(bucket[:, None] == ids[None, :], row_start[None, :], 0), axis=1)
    n_rows = _round_up(t, tm) + N_BUCKETS * tm
    n_tiles = n_rows // tm
    n_used = (row_end[-1] // tm).astype(jnp.int32)
    tile_row = jnp.minimum(jnp.arange(n_tiles, dtype=jnp.int32), n_used - 1) * tm
    tile_bucket = jnp.sum((row_end[None, :] <= tile_row[:, None]).astype(jnp.int32), axis=1)
    ea, eb = _bucket_experts()
    in_bucket = tile_bucket[:, None] == ids[None, :]
    tile_ea = jnp.sum(jnp.where(in_bucket, jnp.asarray(ea)[None, :], 0), axis=1)
    tile_eb = jnp.sum(jnp.where(in_bucket, jnp.asarray(eb)[None, :], 0), axis=1)
    return pos.astype(jnp.int32), tile_ea, tile_eb, n_used.reshape(1), n_rows


def hier_moe_ln(x1ext, counts, moe_w, ln_g, ln_b, sorted_buf=None):
    w1, w3, w2 = moe_w
    t, wide = x1ext.shape
    if t <= FEW_TOKENS:
        return moe_ln_few(x1ext, w1, w3, w2, ln_g, ln_b), None
    tm = MOE_TILE
    pos, tile_ea, tile_eb, n_used, n_rows = _bucket_layout(x1ext, counts, tm)
    if sorted_buf is None:
        sorted_buf = jnp.zeros((n_rows, wide), F32)
    xs = row_move(pos, x1ext, n_rows, scatter=True, dst_init=sorted_buf)
    ys = moe_ln(tile_ea, tile_eb, n_used, xs, w1, w3, w2, ln_g, ln_b, tm)
    return row_move(pos, ys, t, scatter=False), xs


def _rope_tables(s, offset, tm):
    half = ROPE // 2
    inv_freq = ROPE_BASE ** (-jnp.arange(half, dtype=F32) / half)
    ang = (offset + jnp.arange(s)).astype(F32)[:, None] * inv_freq[None, :]
    cos, sin = jnp.cos(ang), jnp.sin(ang)
    cc = jnp.concatenate([cos, cos, cos, cos], -1)
    ss = jnp.concatenate([-sin, sin, -sin, sin], -1)
    if tm > s:
        cc, ss = jnp.tile(cc, (tm // s, 1)), jnp.tile(ss, (tm // s, 1))
    return cc, ss


def _round_up(n, m):
    return (n + m - 1) // m * m


def ab_layer(x, b, s, wts, route_w, cache):
    w_big, w_gate, b_gate, g_ml, w_out, ln_g, ln_b = wts
    t = b * s
    ls_cols = np.array([1.0] * H_FOX + [0.0] * H_ML + [1.0] * H_ML)
    fq, mq, mk, mv, mo, fk_bf, fv_bf, gates, fk, fv = ab_proj(x, w_big, w_gate, b_gate, ls_cols)
    flf = gates[:, :H_FOX].reshape(b, s, H_FOX)
    ml_rows = jnp.swapaxes(gates[:, H_FOX:H_FOX + 2 * H_ML].reshape(b, s, 2 * H_ML), 1, 2)
    ml_rows = jnp.pad(ml_rows, ((0, 0), (0, BF16_SUBLANES - 2 * H_ML), (0, 0)))
    if cache is None:
        past_kv, lf_all = None, flf
        c0 = jnp.zeros((b, H_ML, DH_ML, DH_ML), F32)
        n0 = jnp.zeros((b, H_ML, 1, DH_ML), F32)
        m0 = jnp.zeros((b, H_ML, 1, 1), F32)
        tq, chunk = min(FOX_TILE, s), min(MLSTM_CHUNK, s)
    else:
        ck, cv, clf, c0, n0, m0 = cache
        past = ck.shape[1]
        past_kv = (ck.reshape(b, past, W_FOX).astype(BF16), cv.reshape(b, past, W_FOX).astype(BF16))
        lf_all = jnp.concatenate([clf, flf], 1)
        n0 = n0.reshape(b, H_ML, 1, DH_ML)
        m0 = m0.reshape(b, H_ML, 1, 1)
        tq, chunk = s, s
    lk = lf_all.shape[1]
    lf_rows = jnp.pad(jnp.swapaxes(lf_all, 1, 2),
                      ((0, 0), (0, BF16_SUBLANES - H_FOX), (0, _round_up(lk, PAST_KV_TILE) - lk)))
    fo = fox_attention(fq.reshape(b, s, W_FOX), fk_bf.reshape(b, s, W_FOX), fv_bf.reshape(b, s, W_FOX), lf_rows,
                       tq=tq, past=past_kv, tk=PAST_KV_TILE if cache is not None else tq)
    mh, c_new, n_new, m_new = mlstm(mq.reshape(b, s, W_ML), mk.reshape(b, s, W_ML), mv.reshape(b, s, W_ML),
                                    ml_rows, c0, n0, m0, chunk=chunk)
    x1ext, counts = ab_out(fo.reshape(t, W_FOX), mh.reshape(t, W_ML), mo, x, g_ml, w_out, ln_g, ln_b, route_w)
    state = (fk, fv, flf, c_new, n_new.reshape(b, H_ML, DH_ML), m_new.reshape(b, H_ML))
    return x1ext, counts, state


def mla_layer(x, b, s, wts, route_w, cache, prev_c=(), prev_r=()):
    w_down, g_q, g_kv, w_uq, wuk_t, wuv, w_out, ln_g, ln_b = wts
    t = b * s
    tm = _row_tile(t)
    past = 0 if cache is None else cache[0].shape[1]
    cc, ss = _rope_tables(s, past, tm)
    cq, ckv, kr, ckv_bf, kr2_bf = mla_down(x, w_down, g_q, g_kv, cc, ss, s, prev_c, prev_r)
    q_lat, q_rope = mla_uq(cq, w_uq, wuk_t, cc, ss, s)
    kc = ckv_bf.reshape(b, s, KV_RANK)
    kr2 = kr2_bf.reshape(b, s, LANE)
    if cache is None:
        tq, past_k = min(MLA_TILE, s), None
    else:
        c_ckv, c_kr = cache
        c_kr_bf = c_kr.astype(BF16)
        tq, past_k = s, (c_ckv.astype(BF16), jnp.concatenate([c_kr_bf, c_kr_bf], -1))
    o_lat = mla_attention(q_lat.reshape(b, s, -1), q_rope.reshape(b, s, -1), kc, kr2, tq=tq,
                          tk=tq if cache is None else PAST_KV_TILE, past=past_k)
    x1ext, counts = mla_out(o_lat.reshape(t, -1), wuv, w_out, x, ln_g, ln_b, route_w)
    return x1ext, counts, (ckv, kr)


def _prep_ab_weights(w_in, b_fox_f, b_ml_i, b_ml_f, g_ml, w_out, ln_g, ln_b):
    sizes = (W_FOX, W_FOX, W_FOX, H_FOX, W_ML, W_ML, W_ML, H_ML, H_ML, W_ML)
    idx = np.cumsum(sizes[:-1]).tolist()
    fq, fk, fv, ff, mq, mk, mv, mi, mf, mo = jnp.split(w_in, idx, axis=1)
    w_big = jnp.concatenate([fq * (DH_FOX ** -0.5 * LOG2E), mq, mk, mv, mo, fk, fv], 1).astype(BF16)
    w_gate = jnp.concatenate([ff, mi, mf], 1)
    b_gate = jnp.concatenate([b_fox_f, b_ml_i, b_ml_f])
    return (w_big, w_gate, b_gate, g_ml[None, :], w_out.astype(BF16), ln_g[None, :], ln_b[None, :])


def _prep_mla_weights(w_down, g_q, w_uq, g_kv, w_uk, w_uv, w_out, ln_g, ln_b):
    half = ROPE // 2
    cq_w, ckv_w, kr_w = jnp.split(w_down, [Q_RANK, Q_RANK + KV_RANK], axis=1)
    kr_sw = jnp.concatenate([kr_w[:, half:], kr_w[:, :half]], 1)
    w_down_p = jnp.concatenate([cq_w, ckv_w, kr_w, kr_w, kr_sw, kr_sw], 1).astype(BF16)
    uq = w_uq.reshape(Q_RANK, H_MLA, NOPE + ROPE)
    uq_nope = uq[:, :, :NOPE].reshape(Q_RANK, H_MLA * NOPE)
    uq_rope = uq[:, :, NOPE:]
    uq_rope_sw = jnp.concatenate([uq_rope[..., half:], uq_rope[..., :half]], -1)
    w_uq_p = jnp.concatenate([uq_nope, uq_rope.reshape(Q_RANK, -1), uq_rope_sw.reshape(Q_RANK, -1)], 1).astype(BF16)
    wuk_t = jnp.transpose(w_uk, (1, 2, 0)).astype(BF16)
    wuv = jnp.transpose(w_uv, (1, 0, 2)).astype(BF16)
    return (w_down_p, g_q[None, :], g_kv[None, :], w_uq_p, wuk_t, wuv, w_out.astype(BF16), ln_g[None, :], ln_b[None, :])


def kernel(x_prompt, x_sample, cache_fox_k, cache_fox_v, cache_fox_logf, state_mlstm_c, state_mlstm_n, state_mlstm_m, cache_mla_ckv, cache_mla_krope, w_ab_in, b_fox_f, b_mlstm_i, b_mlstm_f, g_mlstm_norm, w_ab_out, w_mla_down, g_mla_q, w_mla_uq, g_mla_kv, w_mla_uk, w_mla_uv, w_mla_out, ln1_g, ln1_b, ln2_g, ln2_b, w_moe_group, b_moe_group, w_moe_router, b_moe_router, w_exp_gate, w_exp_up, w_exp_down):
    bp, sp, d = x_prompt.shape
    bs, ss_, _ = x_sample.shape
    xp = x_prompt.reshape(bp * sp, d)
    xs = x_sample.reshape(bs * ss_, d)
    ab_p, ab_s, c_p, c_s = [], [], [], []
    n_c = DEPTH // 2
    assert n_c >= 2
    buf_p = buf_s = None
    for l in range(DEPTH):
        j = l // 2
        route_w = (jnp.concatenate([w_moe_group[l], w_moe_router[l]], 1),
                   jnp.concatenate([b_moe_group[l], b_moe_router[l]]))
        if l % 2 == 0:
            wts = _prep_ab_weights(w_ab_in[j], b_fox_f[j], b_mlstm_i[j], b_mlstm_f[j], g_mlstm_norm[j],
                                   w_ab_out[j], ln1_g[l], ln1_b[l])
            xp1, cnt_p, st_p = ab_layer(xp, bp, sp, wts, route_w, None)
            xs1, cnt_s, st_s = ab_layer(xs, bs, ss_, wts, route_w,
                                        (cache_fox_k[j], cache_fox_v[j], cache_fox_logf[j],
                                         state_mlstm_c[j], state_mlstm_n[j], state_mlstm_m[j]))
            ab_p.append(st_p)
            ab_s.append(st_s)
        else:
            wts = _prep_mla_weights(w_mla_down[j], g_mla_q[j], w_mla_uq[j], g_mla_kv[j], w_mla_uk[j], w_mla_uv[j],
                                    w_mla_out[j], ln1_g[l], ln1_b[l])
            last = j == n_c - 1
            prev = lambda sts, i: tuple(st[i] for st in sts) if last else ()
            xp1, cnt_p, st_p = mla_layer(xp, bp, sp, wts, route_w, None, prev(c_p, 0), prev(c_p, 1))
            xs1, cnt_s, st_s = mla_layer(xs, bs, ss_, wts, route_w, (cache_mla_ckv[j], cache_mla_krope[j]),
                                         prev(c_s, 0), prev(c_s, 1))
            c_p.append(st_p)
            c_s.append(st_s)
        moe_w = (w_exp_gate[l].astype(BF16), w_exp_up[l].astype(BF16), w_exp_down[l].astype(BF16))
        xp, buf_p = hier_moe_ln(xp1, cnt_p, moe_w, ln2_g[l][None, :], ln2_b[l][None, :], buf_p)
        xs, buf_s = hier_moe_ln(xs1, cnt_s, moe_w, ln2_g[l][None, :], ln2_b[l][None, :], buf_s)

    def stack(groups, i):
        return jnp.stack([g[i] for g in groups])

    def states(ab, c, b, s):
        kv = [jnp.stack([st[i].reshape(b, s, H_FOX, DH_FOX) for st in ab]) for i in (0, 1)]
        lat = [c[-1][0].reshape(n_c, b, s, KV_RANK), c[-1][1].reshape(n_c, b, s, ROPE)]
        return tuple(kv) + tuple(stack(ab, i) for i in (2, 3, 4, 5)) + tuple(lat)

    return (xp.reshape(bp, sp, d), xs.reshape(bs, ss_, d)) + states(ab_p, c_p, bp, sp) + states(ab_s, c_s, bs, ss_)
```
